```python
import math
import jax, jax.numpy as jnp
from jax import lax
import numpy as np

D_MODEL = 1024
BATCH = 2
SEQ = 8192
DEPTH = 2
DEC_BATCH = 128
DEC_SEQ = 1
PAST_LEN = 2048
PAGE_SIZE = 128

N_A_LAYERS = (DEPTH + 1) // 2
N_C_LAYERS = DEPTH // 2

RET_HEADS = 4
RET_DK = 128
RET_DV = 128
RET_CHUNK = 128
ROPE_BASE = 10000.0
GDN_HEADS = 4
GDN_DK = 128
GDN_DV = 128
GDN_CHUNK = 64
CONV_WIDTH = 4
GDN_CONV_CH = 2 * GDN_HEADS * GDN_DK + GDN_HEADS * GDN_DV
A_IN_COLS = 2 * RET_HEADS * RET_DK + 2 * RET_HEADS * RET_DV + GDN_CONV_CH + GDN_HEADS * GDN_DV + 2 * GDN_HEADS
A_MIX = RET_HEADS * RET_DV + GDN_HEADS * GDN_DV
FOX_HEADS = 8
FOX_HD = D_MODEL // FOX_HEADS
FOX_QBLOCK = 128
FOX_FORGET_BIAS = 3.0
C_MIX = FOX_HEADS * FOX_HD
C_IN_COLS = 3 * C_MIX + FOX_HEADS
PEER_HEADS = 8
PEER_TOPK = 16
N_KEYS = 128
N_EXPERTS = N_KEYS * N_KEYS
PEER_DK = 128
PEER_BLOCK = 128
DEEPNORM_ALPHA = (2.0 * DEPTH) ** 0.25
DEEPNORM_BETA = (8.0 * DEPTH) ** -0.25
LN_EPS = 1e-5
NORM_EPS = 1e-6

kernel_name = "hybrid_retention_gdn_fox_peer_step"

F32 = jnp.float32


def split_cols(a, sizes):
    out, off = [], 0
    for s in sizes:
        out.append(a[..., off:off + s])
        off += s
    return out


def layer_norm(x, g, b):
    xf = x.astype(F32)
    mu = jnp.mean(xf, -1, keepdims=True)
    var = jnp.mean(jnp.square(xf - mu), -1, keepdims=True)
    return ((xf - mu) * lax.rsqrt(var + LN_EPS) * g.astype(F32) + b.astype(F32)).astype(x.dtype)


def head_rms_norm(o, g):
    return o * lax.rsqrt(jnp.mean(o * o, -1, keepdims=True) + NORM_EPS) * g.astype(F32)


def l2norm(x):
    return x * lax.rsqrt(jnp.sum(x * x, -1, keepdims=True) + NORM_EPS)


def rotary(x, pos):
    half = x.shape[-1] // 2
    inv = ROPE_BASE ** (-jnp.arange(half, dtype=F32) / half)
    ang = pos.astype(F32)[:, None] * inv[None, :]
    cos = jnp.cos(ang)[None, :, None, :]
    sin = jnp.sin(ang)[None, :, None, :]
    x1, x2 = x[..., :half], x[..., half:]
    return jnp.concatenate([x1 * cos - x2 * sin, x1 * sin + x2 * cos], -1)


def chunk_len(L, c):
    return c if L % c == 0 else L


def to_chunks(a, n, c):
    return jnp.moveaxis(a.reshape(a.shape[:2] + (n, c) + a.shape[3:]), 2, 0)


def from_chunks(a):
    n, b, h, c = a.shape[:4]
    return jnp.moveaxis(a, 0, 2).reshape((b, h, n * c) + a.shape[4:])


def retention_block(state, inp, log_gamma):
    q, k, v = inp
    c = q.shape[2]
    idx = jnp.arange(c, dtype=F32)
    diff = idx[:, None] - idx[None, :]
    lg = log_gamma[:, None, None]
    decay = jnp.where(diff >= 0, jnp.exp(jnp.maximum(diff, 0.0) * lg), 0.0)
    scores = jnp.einsum('bhid,bhjd->bhij', q, k) * decay
    inner = jnp.exp((idx + 1.0)[None, :] * log_gamma[:, None])
    out = (jnp.einsum('bhij,bhjv->bhiv', scores, v)
           + jnp.einsum('bhid,bhdv->bhiv', q * inner[..., None], state))
    kdec = jnp.exp((c - 1.0 - idx)[None, :] * log_gamma[:, None])
    new_state = (jnp.exp(c * log_gamma)[:, None, None] * state
                 + jnp.einsum('bhjd,bhjv->bhdv', k * kdec[..., None], v))
    return new_state, out


def gdn_block(state, inp):
    q, k, v, g, beta = inp
    c = q.shape[2]
    gc = jnp.cumsum(g, axis=-1)
    idx = jnp.arange(c)
    incl = idx[:, None] >= idx[None, :]
    strict = idx[:, None] > idx[None, :]
    diff = gc[..., :, None] - gc[..., None, :]
    decay = jnp.where(incl, jnp.exp(jnp.where(incl, diff, 0.0)), 0.0)
    kb = k * beta[..., None]
    a = jnp.where(strict, jnp.einsum('bhid,bhjd->bhij', kb, k) * decay, 0.0)
    eye = jnp.eye(c, dtype=F32)
    t = lax.linalg.triangular_solve(eye + a, jnp.broadcast_to(eye, a.shape), left_side=True,
                                    lower=True, unit_diagonal=True)
    u = jnp.einsum('bhij,bhjv->bhiv', t, v * beta[..., None])
    w = jnp.einsum('bhij,bhjd->bhid', t, kb * jnp.exp(gc)[..., None])
    v_new = u - jnp.einsum('bhid,bhdv->bhiv', w, state)
    attn = jnp.einsum('bhid,bhjd->bhij', q, k) * decay
    out = (jnp.einsum('bhid,bhdv->bhiv', q * jnp.exp(gc)[..., None], state)
           + jnp.einsum('bhij,bhjv->bhiv', attn, v_new))
    g_last = gc[..., -1:]
    new_state = (jnp.exp(g_last)[..., None] * state
                 + jnp.einsum('bhjd,bhjv->bhdv', k * jnp.exp(g_last - gc)[..., None], v_new))
    return new_state, out


def mixer_a(x, pos, ret_state, gdn_state, conv_buf, w_in, ret_norm_g, a_log, dt_bias, conv_w,
            gdn_norm_g, w_out):
    b, L, _ = x.shape
    proj = x @ w_in
    rq, rk, rv, rgate, gqkv, gz, ga, gb = split_cols(
        proj, [RET_HEADS * RET_DK, RET_HEADS * RET_DK, RET_HEADS * RET_DV, RET_HEADS * RET_DV,
               GDN_CONV_CH, GDN_HEADS * GDN_DV, GDN_HEADS, GDN_HEADS])
    rq = rotary(rq.reshape(b, L, RET_HEADS, RET_DK).astype(F32), pos)
    rk = rotary(rk.reshape(b, L, RET_HEADS, RET_DK).astype(F32), pos) * RET_DK ** -0.5
    rv = rv.reshape(b, L, RET_HEADS, RET_DV).astype(F32)
    log_gamma = jnp.log1p(-jnp.exp2(-5.0 - jnp.arange(RET_HEADS, dtype=F32)))
    c = chunk_len(L, RET_CHUNK)
    n = L // c
    xs = tuple(to_chunks(t.transpose(0, 2, 1, 3), n, c) for t in (rq, rk, rv))
    ret_new, o = lax.scan(lambda s, inp: retention_block(s, inp, log_gamma), ret_state.astype(F32), xs)
    o_r = from_chunks(o).transpose(0, 2, 1, 3)
    o_r = head_rms_norm(o_r, ret_norm_g) * jax.nn.silu(
        rgate.reshape(b, L, RET_HEADS, RET_DV).astype(F32))
    xc = jnp.concatenate([conv_buf.astype(gqkv.dtype), gqkv], axis=1)
    conv_new = xc[:, L:]
    y = xc[:, 0:L] * conv_w[0]
    for i in range(1, CONV_WIDTH):
        y = y + xc[:, i:i + L] * conv_w[i]
    y = jax.nn.silu(y.astype(F32))
    gq, gk, gv = split_cols(y, [GDN_HEADS * GDN_DK, GDN_HEADS * GDN_DK, GDN_HEADS * GDN_DV])
    gq = l2norm(gq.reshape(b, L, GDN_HEADS, GDN_DK)) * GDN_DK ** -0.5
    gk = l2norm(gk.reshape(b, L, GDN_HEADS, GDN_DK))
    gv = gv.reshape(b, L, GDN_HEADS, GDN_DV)
    g = -jnp.exp(a_log.astype(F32)) * jax.nn.softplus(ga.astype(F32) + dt_bias.astype(F32))
    beta = jax.nn.sigmoid(gb.astype(F32))
    c = chunk_len(L, GDN_CHUNK)
    n = L // c
    xs = (to_chunks(gq.transpose(0, 2, 1, 3), n, c), to_chunks(gk.transpose(0, 2, 1, 3), n, c),
          to_chunks(gv.transpose(0, 2, 1, 3), n, c), to_chunks(g.transpose(0, 2, 1), n, c),
          to_chunks(beta.transpose(0, 2, 1), n, c))
    gdn_new, o = lax.scan(gdn_block, gdn_state.astype(F32), xs)
    o_g = from_chunks(o).transpose(0, 2, 1, 3)
    o_g = head_rms_norm(o_g, gdn_norm_g) * jax.nn.silu(gz.reshape(b, L, GDN_HEADS, GDN_DV).astype(F32))
    mix = jnp.concatenate([o_r.reshape(b, L, RET_HEADS * RET_DV),
                           o_g.reshape(b, L, GDN_HEADS * GDN_DV)], -1).astype(x.dtype)
    return mix @ w_out, ret_new, gdn_new, conv_new


def fox_project(x, w_in, b_f):
    b, L, _ = x.shape
    q, k, v, f = split_cols(x @ w_in, [C_MIX, C_MIX, C_MIX, FOX_HEADS])
    q = q.reshape(b, L, FOX_HEADS, FOX_HD)
    k = k.reshape(b, L, FOX_HEADS, FOX_HD)
    v = v.reshape(b, L, FOX_HEADS, FOX_HD)
    logf = jax.nn.log_sigmoid(f.astype(F32) + b_f.astype(F32))
    return q, k, v, logf


def fox_prompt(q, k, v, logf):
    b, L = q.shape[:2]
    cum = jnp.cumsum(logf, axis=1).transpose(0, 2, 1)
    nb = L // FOX_QBLOCK
    qb = jnp.moveaxis(q.reshape(b, nb, FOX_QBLOCK, FOX_HEADS, FOX_HD), 1, 0)
    cb = jnp.moveaxis(cum.reshape(b, FOX_HEADS, nb, FOX_QBLOCK), 2, 0)
    starts = jnp.arange(nb) * FOX_QBLOCK
    kpos = jnp.arange(L)
    scale = FOX_HD ** -0.5

    def one_block(args):
        qi, ci, s0 = args
        logits = (jnp.einsum('bqhd,bkhd->bhqk', qi, k).astype(F32) * scale
                  + ci[..., :, None] - cum[:, :, None, :])
        qpos = s0 + jnp.arange(FOX_QBLOCK)
        logits = jnp.where(kpos[None, :] <= qpos[:, None], logits, -jnp.inf)
        p = jax.nn.softmax(logits, axis=-1).astype(v.dtype)
        return jnp.einsum('bhqk,bkhd->bqhd', p, v)

    o = lax.map(one_block, (qb, cb, starts))
    return jnp.moveaxis(o, 0, 1).reshape(b, L, C_MIX)


def fox_sample(q, k, v, logf, k_pool, v_pool, lf_pool, page_table):
    b, L = q.shape[:2]
    past = page_table.shape[1] * PAGE_SIZE
    k_past = k_pool[page_table].reshape(b, past, FOX_HEADS, FOX_HD).astype(q.dtype)
    v_past = v_pool[page_table].reshape(b, past, FOX_HEADS, FOX_HD).astype(v.dtype)
    lf_past = lf_pool[page_table].reshape(b, past, FOX_HEADS).astype(F32)
    cum = jnp.cumsum(jnp.concatenate([lf_past, logf], axis=1), axis=1).transpose(0, 2, 1)
    cq = cum[:, :, past:]
    scale = FOX_HD ** -0.5
    l_past = (jnp.einsum('bqhd,bkhd->bhqk', q, k_past).astype(F32) * scale
              + cq[..., :, None] - cum[:, :, None, :past])
    l_new = (jnp.einsum('bqhd,bkhd->bhqk', q, k).astype(F32) * scale
             + cq[..., :, None] - cq[..., None, :])
    causal = jnp.arange(L)[None, :] <= jnp.arange(L)[:, None]
    l_new = jnp.where(causal, l_new, -jnp.inf)
    p = jax.nn.softmax(jnp.concatenate([l_past, l_new], -1), axis=-1).astype(v.dtype)
    o = (jnp.einsum('bhqk,bkhd->bqhd', p[..., :past], v_past)
         + jnp.einsum('bhqk,bkhd->bqhd', p[..., past:], v))
    return o.reshape(b, L, C_MIX)


def peer(x, wq, sub_keys, u_tab, v_tab):
    b, L, d = x.shape
    t = b * L
    xt = x.reshape(t, d)
    q = (xt @ wq).reshape(t, PEER_HEADS, 2, PEER_DK // 2)
    s = jnp.einsum('thcd,ckd->thck', q, sub_keys).astype(F32)
    s1, i1 = lax.top_k(s[:, :, 0], PEER_TOPK)
    s2, i2 = lax.top_k(s[:, :, 1], PEER_TOPK)
    cand_s = (s1[..., :, None] + s2[..., None, :]).reshape(t, PEER_HEADS, PEER_TOPK * PEER_TOPK)
    cand_i = (i1[..., :, None] * N_KEYS + i2[..., None, :]).reshape(t, PEER_HEADS, PEER_TOPK * PEER_TOPK)
    top_s, sel = lax.top_k(cand_s, PEER_TOPK)
    idx = jnp.take_along_axis(cand_i, sel, axis=-1).reshape(t, PEER_HEADS * PEER_TOPK)
    gate = jax.nn.softmax(top_s, axis=-1).reshape(t, PEER_HEADS * PEER_TOPK)
    nblk = -(-t // PEER_BLOCK)
    pad = nblk * PEER_BLOCK - t
    xb = jnp.pad(xt, ((0, pad), (0, 0))).reshape(nblk, PEER_BLOCK, d)
    ib = jnp.pad(idx, ((0, pad), (0, 0))).reshape(nblk, PEER_BLOCK, PEER_HEADS * PEER_TOPK)
    gb = jnp.pad(gate, ((0, pad), (0, 0))).reshape(nblk, PEER_BLOCK, PEER_HEADS * PEER_TOPK)

    def expert_block(args):
        xi, ii, gi = args
        h = jnp.einsum('ted,td->te', jnp.take(u_tab, ii, axis=0), xi).astype(F32)
        a = (jax.nn.gelu(h, approximate=False) * gi).astype(xi.dtype)
        return jnp.einsum('te,ted->td', a, jnp.take(v_tab, ii, axis=0))

    y = lax.map(expert_block, (xb, ib, gb)).reshape(nblk * PEER_BLOCK, d)[:t]
    return y.reshape(b, L, d)


def setup_inputs(seed: int = 0) -> dict:
    key = jax.random.key(seed)
    ks = jax.random.split(key, 32)

    def nrm(k, shape, scale):
        return jax.random.normal(k, shape, F32) * scale

    n_pages = PAST_LEN // PAGE_SIZE
    n_used = DEC_BATCH * n_pages
    n_pool = n_used + max(1, n_used // 4)
    perm = jax.random.permutation(ks[0], n_pool)
    page_table = perm[:n_used].reshape(DEC_BATCH, n_pages).astype(jnp.int32)
    dt = jnp.exp(jax.random.uniform(ks[1], (N_A_LAYERS, GDN_HEADS), F32, math.log(1e-3), math.log(1e-1)))
    return {
        'x_prompt': nrm(ks[2], (BATCH, SEQ, D_MODEL), 1.0),
        'x_sample': nrm(ks[3], (DEC_BATCH, DEC_SEQ, D_MODEL), 1.0),
        'state_ret': nrm(ks[4], (N_A_LAYERS, DEC_BATCH, RET_HEADS, RET_DK, RET_DV), 0.1),
        'state_gdn': nrm(ks[5], (N_A_LAYERS, DEC_BATCH, GDN_HEADS, GDN_DK, GDN_DV), 0.1),
        'state_gdn_conv': nrm(ks[6], (N_A_LAYERS, DEC_BATCH, CONV_WIDTH - 1, GDN_CONV_CH), 1.0),
        'cache_fox_k': nrm(ks[7], (N_C_LAYERS, n_pool, PAGE_SIZE, FOX_HEADS, FOX_HD), 1.0),
        'cache_fox_v': nrm(ks[8], (N_C_LAYERS, n_pool, PAGE_SIZE, FOX_HEADS, FOX_HD), 1.0),
        'cache_fox_logf': jax.nn.log_sigmoid(FOX_FORGET_BIAS + nrm(ks[9], (N_C_LAYERS, n_pool, PAGE_SIZE, FOX_HEADS), 1.0)),
        'page_table': page_table,
        'w_in_a': nrm(ks[10], (N_A_LAYERS, D_MODEL, A_IN_COLS), D_MODEL ** -0.5),
        'ret_norm_g': 1.0 + nrm(ks[11], (N_A_LAYERS, RET_DV), 0.02),
        'gdn_a_log': jnp.log(jax.random.uniform(ks[12], (N_A_LAYERS, GDN_HEADS), F32, 1.0, 16.0)),
        'gdn_dt_bias': dt + jnp.log(-jnp.expm1(-dt)),
        'gdn_conv_w': nrm(ks[13], (N_A_LAYERS, CONV_WIDTH, GDN_CONV_CH), CONV_WIDTH ** -0.5),
        'gdn_norm_g': 1.0 + nrm(ks[14], (N_A_LAYERS, GDN_DV), 0.02),
        'w_out_a': nrm(ks[15], (N_A_LAYERS, A_MIX, D_MODEL), DEEPNORM_BETA * A_MIX ** -0.5),
        'w_in_c': nrm(ks[16], (N_C_LAYERS, D_MODEL, C_IN_COLS), D_MODEL ** -0.5),
        'fox_b_f': FOX_FORGET_BIAS + nrm(ks[17], (N_C_LAYERS, FOX_HEADS), 0.1),
        'w_out_c': nrm(ks[18], (N_C_LAYERS, C_MIX, D_MODEL), DEEPNORM_BETA * C_MIX ** -0.5),
        'peer_wq': nrm(ks[19], (DEPTH, D_MODEL, PEER_HEADS * PEER_DK), D_MODEL ** -0.5),
        'peer_sub_keys': nrm(ks[20], (DEPTH, 2, N_KEYS, PEER_DK // 2), (PEER_DK // 2) ** -0.5),
        'peer_u': nrm(ks[21], (DEPTH, N_EXPERTS, D_MODEL), D_MODEL ** -0.5),
        'peer_v': nrm(ks[22], (DEPTH, N_EXPERTS, D_MODEL), DEEPNORM_BETA * PEER_HEADS ** -0.5),
        'ln_g': 1.0 + nrm(ks[23], (DEPTH, 2, D_MODEL), 0.02),
        'ln_b': nrm(ks[24], (DEPTH, 2, D_MODEL), 0.02),
    }


def reference(x_prompt, x_sample, state_ret, state_gdn, state_gdn_conv, cache_fox_k, cache_fox_v,
              cache_fox_logf, page_table, w_in_a, ret_norm_g, gdn_a_log, gdn_dt_bias, gdn_conv_w,
              gdn_norm_g, w_out_a, w_in_c, fox_b_f, w_out_c, peer_wq, peer_sub_keys, peer_u, peer_v,
              ln_g, ln_b):
    xp, xs = x_prompt, x_sample
    pos_p = jnp.arange(xp.shape[1])
    pos_s = PAST_LEN + jnp.arange(xs.shape[1])
    bp, bs = xp.shape[0], xs.shape[0]
    ret_p, ret_s, gdn_p, gdn_s, conv_p, conv_s = [], [], [], [], [], []
    k_p, k_s, v_p, v_s, lf_p, lf_s = [], [], [], [], [], []
    for layer in range(DEPTH):
        i = layer // 2
        if layer % 2 == 0:
            wa = (w_in_a[i], ret_norm_g[i], gdn_a_log[i], gdn_dt_bias[i], gdn_conv_w[i], gdn_norm_g[i], w_out_a[i])
            hp, sr, sg, cv = mixer_a(xp, pos_p, jnp.zeros((bp, RET_HEADS, RET_DK, RET_DV), F32),
                                     jnp.zeros((bp, GDN_HEADS, GDN_DK, GDN_DV), F32),
                                     jnp.zeros((bp, CONV_WIDTH - 1, GDN_CONV_CH), xp.dtype), *wa)
            ret_p.append(sr); gdn_p.append(sg); conv_p.append(cv)
            hs, sr, sg, cv = mixer_a(xs, pos_s, state_ret[i], state_gdn[i], state_gdn_conv[i], *wa)
            ret_s.append(sr); gdn_s.append(sg); conv_s.append(cv)
        else:
            q, k, v, lf = fox_project(xp, w_in_c[i], fox_b_f[i])
            hp = fox_prompt(q, k, v, lf) @ w_out_c[i]
            k_p.append(k); v_p.append(v); lf_p.append(lf)
            q, k, v, lf = fox_project(xs, w_in_c[i], fox_b_f[i])
            hs = fox_sample(q, k, v, lf, cache_fox_k[i], cache_fox_v[i], cache_fox_logf[i], page_table) @ w_out_c[i]
            k_s.append(k); v_s.append(v); lf_s.append(lf)
        xp = layer_norm(DEEPNORM_ALPHA * xp + hp, ln_g[layer, 0], ln_b[layer, 0])
        xs = layer_norm(DEEPNORM_ALPHA * xs + hs, ln_g[layer, 0], ln_b[layer, 0])
        pw = (peer_wq[layer], peer_sub_keys[layer], peer_u[layer], peer_v[layer])
        xp = layer_norm(DEEPNORM_ALPHA * xp + peer(xp, *pw), ln_g[layer, 1], ln_b[layer, 1])
        xs = layer_norm(DEEPNORM_ALPHA * xs + peer(xs, *pw), ln_g[layer, 1], ln_b[layer, 1])
    return (xp, xs,
            jnp.stack(ret_p), jnp.stack(ret_s), jnp.stack(gdn_p), jnp.stack(gdn_s),
            jnp.stack(conv_p), jnp.stack(conv_s), jnp.stack(k_p), jnp.stack(k_s),
            jnp.stack(v_p), jnp.stack(v_s), jnp.stack(lf_p), jnp.stack(lf_s))
```

```python
import functools
import math

import numpy as np
import jax
import jax.numpy as jnp
from jax import lax
from jax.experimental import pallas as pl
from jax.experimental.pallas import tpu as pltpu

F32 = jnp.float32
BF16 = jnp.bfloat16
HI = lax.Precision.HIGHEST

D = 1024
PAST = 2048
PAGE = 128
RH, RDK, RDV = 4, 128, 128
GH, GDK, GDV = 4, 128, 128
CONV_W = 4
CONV_CH = 2 * GH * GDK + GH * GDV
A_COLS = 4104
A_COLS_PAD = 4224
FH, FHD = 8, 128
C_COLS_PAD = 3200
PH, PTOPK, NKEYS = 8, 16, 128
NEXP = NKEYS * NKEYS
ALPHA = 4.0 ** 0.25
LN_EPS = 1e-5
NORM_EPS = 1e-6
ROPE_BASE = 10000.0
LANES = 128
CHUNK = 128
VMEM_LIMIT = 56 * 1024 * 1024

NT = (((1,), (1,)), ((), ()))
TN = (((0,), (0,)), ((), ()))


def _cp(sem):
    return pltpu.CompilerParams(dimension_semantics=sem, vmem_limit_bytes=VMEM_LIMIT)


def _mm(a, b):
    return jnp.dot(a.astype(BF16), b.astype(BF16), preferred_element_type=F32)


def _mm_nt(a, b):
    return lax.dot_general(a.astype(BF16), b.astype(BF16), NT, preferred_element_type=F32)


def _mm_tn(a, b):
    return lax.dot_general(a.astype(BF16), b.astype(BF16), TN, preferred_element_type=F32)


def _mm_hi(a, b):
    return jnp.dot(a, b, precision=HI, preferred_element_type=F32)


def _sigmoid(x):
    return 1.0 / (1.0 + jnp.exp(-x))


def _silu(x):
    return x * _sigmoid(x)


def _softplus(x):
    return jnp.maximum(x, 0.0) + jnp.log1p(jnp.exp(-jnp.abs(x)))


def _log_sigmoid(x):
    return jnp.minimum(x, 0.0) - jnp.log1p(jnp.exp(-jnp.abs(x)))


def _layer_norm(z, g, b):
    mu = jnp.mean(z, axis=-1, keepdims=True)
    d = z - mu
    var = jnp.mean(d * d, axis=-1, keepdims=True)
    return d * lax.rsqrt(var + LN_EPS) * g + b


def _iota2(shape, axis):
    return lax.broadcasted_iota(jnp.int32, shape, axis)


def _proj_kernel(x_ref, w_ref, o_ref):
    o_ref[...] = jnp.dot(x_ref[...].astype(BF16), w_ref[...], preferred_element_type=F32)


def _proj(x, w, tm):
    t, k = x.shape
    n = w.shape[1]
    return pl.pallas_call(
        _proj_kernel,
        grid=(t // tm,),
        in_specs=[pl.BlockSpec((tm, k), lambda i: (i, 0)), pl.BlockSpec((k, n), lambda i: (0, 0))],
        out_specs=pl.BlockSpec((tm, n), lambda i: (i, 0)),
        out_shape=jax.ShapeDtypeStruct((t, n), F32),
        compiler_params=_cp(("parallel",)),
        name="proj_a",
    )(x, w)


def _fox_proj_kernel(x_ref, w_ref, bf_ref, q_ref, k_ref, v_ref, lf_ref, lft_ref):
    p = jnp.dot(x_ref[...].astype(BF16), w_ref[...], preferred_element_type=F32)
    q_ref[...] = p[:, 0:D]
    k_ref[...] = p[:, D:2 * D]
    v_ref[...] = p[:, 2 * D:3 * D]
    lf = _log_sigmoid(p[:, 3 * D:3 * D + LANES] + bf_ref[...])
    lf_ref[...] = lf[:, 0:FH]
    lft_ref[...] = lf.T[0:FH, :]


def _fox_proj(x, w, bf_row, tm):
    t = x.shape[0]
    row = lambda i: (i, 0)
    return pl.pallas_call(
        _fox_proj_kernel,
        grid=(t // tm,),
        in_specs=[pl.BlockSpec((tm, D), row), pl.BlockSpec((D, C_COLS_PAD), lambda i: (0, 0)),
                  pl.BlockSpec((1, LANES), lambda i: (0, 0))],
        out_specs=[pl.BlockSpec((tm, D), row), pl.BlockSpec((tm, D), row), pl.BlockSpec((tm, D), row),
                   pl.BlockSpec((tm, FH), row), pl.BlockSpec((FH, tm), lambda i: (0, i))],
        out_shape=[jax.ShapeDtypeStruct((t, D), F32)] * 3
        + [jax.ShapeDtypeStruct((t, FH), F32), jax.ShapeDtypeStruct((FH, t), F32)],
        compiler_params=_cp(("parallel",)),
        name="proj_c",
    )(x, w, bf_row)


def _outproj_ln_kernel(a_ref, x_ref, w_ref, g_ref, b_ref, o_ref):
    h = jnp.dot(a_ref[...].astype(BF16), w_ref[...], preferred_element_type=F32)
    o_ref[...] = _layer_norm(ALPHA * x_ref[...] + h, g_ref[...], b_ref[...])


def _outproj_ln(a, x, w, g, b, tm):
    t = x.shape[0]
    row = lambda i: (i, 0)
    fixed = lambda i: (0, 0)
    return pl.pallas_call(
        _outproj_ln_kernel,
        grid=(t // tm,),
        in_specs=[pl.BlockSpec((tm, D), row), pl.BlockSpec((tm, D), row), pl.BlockSpec((D, D), fixed),
                  pl.BlockSpec((1, D), fixed), pl.BlockSpec((1, D), fixed)],
        out_specs=pl.BlockSpec((tm, D), row),
        out_shape=jax.ShapeDtypeStruct((t, D), F32),
        compiler_params=_cp(("parallel",)),
        name="outproj_ln",
    )(a, x, w, g, b)


def _unit_lower_inverse(a, ri, ci):
    eye = (ri == ci).astype(F32)
    d = jnp.where((ri >> 4) == (ci >> 4), a, 0.0)
    d2 = _mm_hi(d, d)
    d4 = _mm_hi(d2, d2)
    d8 = _mm_hi(d4, d4)
    x = eye - d
    x = x + _mm_hi(x, d2)
    x = x + _mm_hi(x, d4)
    x = x + _mm_hi(x, d8)
    for s in (5, 6, 7):
        e = jnp.where(((ri >> s) == (ci >> s)) & ((ri >> (s - 1)) != (ci >> (s - 1))), a, 0.0)
        x = x - _mm_hi(_mm_hi(x, e), x)
    return x


def _head_gate_norm(out, g_row, gate):
    return out * lax.rsqrt(jnp.mean(out * out, axis=-1, keepdims=True) + NORM_EPS) * g_row * _silu(gate)


def _mixer_a_prompt_kernel(p_ref, cos_ref, sin_ref, cw_ref, alog_ref, dt_ref, rg_ref, gg_ref,
                           mix_ref, ret_ref, gdn_ref, conv_ref, ext_ref):
    n = pl.program_id(1)
    c = CHUNK

    @pl.when(n == 0)
    def _init():
        ret_ref[...] = jnp.zeros_like(ret_ref)
        gdn_ref[...] = jnp.zeros_like(gdn_ref)
        ext_ref[0:8, :] = jnp.zeros((8, CONV_CH), F32)

    ri = _iota2((c, c), 0)
    ci = _iota2((c, c), 1)
    rf = ri.astype(F32)
    diff = rf - ci.astype(F32)
    cosv = cos_ref[...]
    sinv = sin_ref[...]

    for h in range(RH):
        lg = math.log1p(-(2.0 ** (-5.0 - h)))
        decay = jnp.where(diff >= 0, jnp.exp(jnp.maximum(diff, 0.0) * lg), 0.0)
        inner = jnp.exp((rf + 1.0) * lg)
        kdec = jnp.exp((c - 1.0 - rf) * lg)
        rq = p_ref[0, :, h * RDK:(h + 1) * RDK]
        rk = p_ref[0, :, RH * RDK + h * RDK:RH * RDK + (h + 1) * RDK]
        rv = p_ref[0, :, 2 * RH * RDK + h * RDV:2 * RH * RDK + (h + 1) * RDV]
        rgate = p_ref[0, :, 2 * RH * RDK + RH * RDV + h * RDV:2 * RH * RDK + RH * RDV + (h + 1) * RDV]
        q = rq * cosv + pltpu.roll(rq, RDK // 2, 1) * sinv
        k = (rk * cosv + pltpu.roll(rk, RDK // 2, 1) * sinv) * (RDK ** -0.5)
        s = ret_ref[0, h]
        scores = _mm_nt(q, k) * decay
        out = _mm(scores, rv) + _mm(q * inner, s)
        ret_ref[0, h] = math.exp(c * lg) * s + _mm_tn(k * kdec, rv)
        mix_ref[0, :, h * RDV:(h + 1) * RDV] = _head_gate_norm(out, rg_ref[...], rgate)

    g_off = 2 * RH * RDK + 2 * RH * RDV
    gq_all = p_ref[0, :, g_off:g_off + CONV_CH]
    ext_ref[8:8 + c, :] = gq_all
    y = ext_ref[5:5 + c, :] * cw_ref[0:1, :]
    for i in range(1, CONV_W):
        y = y + ext_ref[5 + i:5 + i + c, :] * cw_ref[i:i + 1, :]
    conv_ref[0] = ext_ref[c + 5:c + 8, :]
    ext_ref[0:8, :] = ext_ref[c:c + 8, :]
    y = _silu(y)
    z_off = g_off + CONV_CH
    tail = p_ref[0, :, z_off + GH * GDV:z_off + GH * GDV + LANES]
    g_all = -jnp.exp(alog_ref[...]) * _softplus(tail + dt_ref[...])
    beta_all = _sigmoid(tail)
    incl = ri >= ci
    strict = ri > ci
    tril = incl.astype(F32)
    for h in range(GH):
        qh = y[:, h * GDK:(h + 1) * GDK]
        kh = y[:, GH * GDK + h * GDK:GH * GDK + (h + 1) * GDK]
        vh = y[:, 2 * GH * GDK + h * GDV:2 * GH * GDK + (h + 1) * GDV]
        qh = qh * lax.rsqrt(jnp.sum(qh * qh, axis=-1, keepdims=True) + NORM_EPS) * (GDK ** -0.5)
        kh = kh * lax.rsqrt(jnp.sum(kh * kh, axis=-1, keepdims=True) + NORM_EPS)
        gb = jnp.broadcast_to(g_all[:, h:h + 1], (c, c))
        bcol = beta_all[:, GH + h:GH + h + 1]
        gc = _mm_hi(tril, gb)
        dmat = gc - gc.T
        dec = jnp.where(incl, jnp.exp(jnp.where(incl, dmat, 0.0)), 0.0)
        kb = kh * bcol
        a = jnp.where(strict, _mm_nt(kb, kh) * dec, 0.0)
        t = _unit_lower_inverse(a, ri, ci)
        egc = jnp.exp(gc)
        u = _mm(t, vh * bcol)
        w = _mm(t, kb * egc)
        s = gdn_ref[0, h]
        v_new = u - _mm(w, s)
        attn = _mm_nt(qh, kh) * dec
        out = _mm(qh * egc, s) + _mm(attn, v_new)
        g_last = gc[c - 1:c, :]
        gdn_ref[0, h] = jnp.exp(g_last) * s + _mm_tn(kh * jnp.exp(g_last - gc), v_new)
        gz = p_ref[0, :, z_off + h * GDV:z_off + (h + 1) * GDV]
        mix_ref[0, :, RH * RDV + h * GDV:RH * RDV + (h + 1) * GDV] = _head_gate_norm(out, gg_ref[...], gz)


def _mixer_a_prompt(proj, cos2, sin2, conv_w, alog_row, dt_row, rg_row, gg_row):
    b, l, _ = proj.shape
    fixed = lambda i, n: (0, 0)
    return pl.pallas_call(
        _mixer_a_prompt_kernel,
        grid=(b, l // CHUNK),
        in_specs=[pl.BlockSpec((1, CHUNK, A_COLS_PAD), lambda i, n: (i, n, 0)),
                  pl.BlockSpec((CHUNK, LANES), lambda i, n: (n, 0)),
                  pl.BlockSpec((CHUNK, LANES), lambda i, n: (n, 0)),
                  pl.BlockSpec((CONV_W, CONV_CH), fixed),
                  pl.BlockSpec((1, LANES), fixed), pl.BlockSpec((1, LANES), fixed),
                  pl.BlockSpec((1, LANES), fixed), pl.BlockSpec((1, LANES), fixed)],
        out_specs=[pl.BlockSpec((1, CHUNK, D), lambda i, n: (i, n, 0)),
                   pl.BlockSpec((1, RH, RDK, RDV), lambda i, n: (i, 0, 0, 0)),
                   pl.BlockSpec((1, GH, GDK, GDV), lambda i, n: (i, 0, 0, 0)),
                   pl.BlockSpec((1, CONV_W - 1, CONV_CH), lambda i, n: (i, 0, 0))],
        out_shape=[jax.ShapeDtypeStruct((b, l, D), F32),
                   jax.ShapeDtypeStruct((b, RH, RDK, RDV), F32),
                   jax.ShapeDtypeStruct((b, GH, GDK, GDV), F32),
                   jax.ShapeDtypeStruct((b, CONV_W - 1, CONV_CH), F32)],
        scratch_shapes=[pltpu.VMEM((CHUNK + 8, CONV_CH), F32)],
        compiler_params=_cp(("parallel", "arbitrary")),
        name="mixer_a_prompt",
    )(proj, cos2, sin2, conv_w, alog_row, dt_row, rg_row, gg_row)


def _mixer_a_sample_kernel(p_ref, cos_ref, sin_ref, cw_ref, alog_ref, dt_ref, rg_ref, gg_ref,
                           rs_ref, gs_ref, cs_ref, mix_ref, ret_ref, gdn_ref, conv_ref):
    eye = _iota2((LANES, LANES), 0) == _iota2((LANES, LANES), 1)

    def col(v):
        return jnp.sum(jnp.where(eye, jnp.broadcast_to(v, (LANES, LANES)), 0.0), axis=1, keepdims=True)

    def vec_mat(c, s):
        return jnp.sum(c * s, axis=0, keepdims=True)

    cosv = cos_ref[...]
    sinv = sin_ref[...]
    for h in range(RH):
        gamma = 1.0 - 2.0 ** (-5.0 - h)
        rq = p_ref[0, :, h * RDK:(h + 1) * RDK]
        rk = p_ref[0, :, RH * RDK + h * RDK:RH * RDK + (h + 1) * RDK]
        rv = p_ref[0, :, 2 * RH * RDK + h * RDV:2 * RH * RDK + (h + 1) * RDV]
        rgate = p_ref[0, :, 2 * RH * RDK + RH * RDV + h * RDV:2 * RH * RDK + RH * RDV + (h + 1) * RDV]
        q = rq * cosv + pltpu.roll(rq, RDK // 2, 1) * sinv
        k = (rk * cosv + pltpu.roll(rk, RDK // 2, 1) * sinv) * (RDK ** -0.5)
        s = rs_ref[0, h]
        qk = jnp.sum(q * k, axis=-1, keepdims=True)
        out = qk * rv + gamma * vec_mat(col(q), s)
        ret_ref[0, h] = gamma * s + col(k) * rv
        mix_ref[0, :, h * RDV:(h + 1) * RDV] = _head_gate_norm(out, rg_ref[...], rgate)

    g_off = 2 * RH * RDK + 2 * RH * RDV
    gq_all = p_ref[0, :, g_off:g_off + CONV_CH]
    cb = cs_ref[0]
    y = cb[0:1, :] * cw_ref[0:1, :] + cb[1:2, :] * cw_ref[1:2, :] + cb[2:3, :] * cw_ref[2:3, :] \
        + gq_all * cw_ref[3:4, :]
    conv_ref[0, 0:2, :] = cb[1:3, :]
    conv_ref[0, 2:3, :] = gq_all
    y = _silu(y)
    z_off = g_off + CONV_CH
    tail = p_ref[0, :, z_off + GH * GDV:z_off + GH * GDV + LANES]
    g_all = -jnp.exp(alog_ref[...]) * _softplus(tail + dt_ref[...])
    beta_all = _sigmoid(tail)
    for h in range(GH):
        qh = y[:, h * GDK:(h + 1) * GDK]
        kh = y[:, GH * GDK + h * GDK:GH * GDK + (h + 1) * GDK]
        vh = y[:, 2 * GH * GDK + h * GDV:2 * GH * GDK + (h + 1) * GDV]
        qh = qh * lax.rsqrt(jnp.sum(qh * qh, axis=-1, keepdims=True) + NORM_EPS) * (GDK ** -0.5)
        kh = kh * lax.rsqrt(jnp.sum(kh * kh, axis=-1, keepdims=True) + NORM_EPS)
        eg = jnp.exp(g_all[:, h:h + 1])
        beta = beta_all[:, GH + h:GH + h + 1]
        s = gs_ref[0, h]
        kc = col(kh)
        v_new = vh * beta - vec_mat(kc * (beta * eg), s)
        qk = jnp.sum(qh * kh, axis=-1, keepdims=True)
        out = eg * vec_mat(col(qh), s) + qk * v_new
        gdn_ref[0, h] = eg * s + kc * v_new
        gz = p_ref[0, :, z_off + h * GDV:z_off + (h + 1) * GDV]
        mix_ref[0, :, RH * RDV + h * GDV:RH * RDV + (h + 1) * GDV] = _head_gate_norm(out, gg_ref[...], gz)


def _mixer_a_sample(proj, cos2, sin2, conv_w, alog_row, dt_row, rg_row, gg_row, ret_s, gdn_s, conv_s):
    b = proj.shape[0]
    fixed = lambda i: (0, 0)
    st = lambda i: (i, 0, 0, 0)
    return pl.pallas_call(
        _mixer_a_sample_kernel,
        grid=(b,),
        in_specs=[pl.BlockSpec((1, 1, A_COLS_PAD), lambda i: (i, 0, 0)),
                  pl.BlockSpec((1, LANES), fixed), pl.BlockSpec((1, LANES), fixed),
                  pl.BlockSpec((CONV_W, CONV_CH), fixed),
                  pl.BlockSpec((1, LANES), fixed), pl.BlockSpec((1, LANES), fixed),
                  pl.BlockSpec((1, LANES), fixed), pl.BlockSpec((1, LANES), fixed),
                  pl.BlockSpec((1, RH, RDK, RDV), st), pl.BlockSpec((1, GH, GDK, GDV), st),
                  pl.BlockSpec((1, CONV_W - 1, CONV_CH), lambda i: (i, 0, 0))],
        out_specs=[pl.BlockSpec((1, 1, D), lambda i: (i, 0, 0)),
                   pl.BlockSpec((1, RH, RDK, RDV), st), pl.BlockSpec((1, GH, GDK, GDV), st),
                   pl.BlockSpec((1, CONV_W - 1, CONV_CH), lambda i: (i, 0, 0))],
        out_shape=[jax.ShapeDtypeStruct((b, 1, D), F32),
                   jax.ShapeDtypeStruct((b, RH, RDK, RDV), F32),
                   jax.ShapeDtypeStruct((b, GH, GDK, GDV), F32),
                   jax.ShapeDtypeStruct((b, CONV_W - 1, CONV_CH), F32)],
        compiler_params=_cp(("parallel",)),
        name="mixer_a_sample",
    )(proj, cos2, sin2, conv_w, alog_row, dt_row, rg_row, gg_row, ret_s, gdn_s, conv_s)


def _fox_cumsum_kernel(x_ref, o_ref):
    n = x_ref.shape[1] // LANES
    upper = (_iota2((LANES, LANES), 0) <= _iota2((LANES, LANES), 1)).astype(F32)
    carry = jnp.zeros((FH, 1), F32)
    for c in range(n):
        p = _mm_hi(x_ref[:, c * LANES:(c + 1) * LANES], upper) + carry
        o_ref[0, :, c * LANES:(c + 1) * LANES] = p
        carry = p[:, LANES - 1:LANES]


def _fox_cumsum(lft, b, l):
    return pl.pallas_call(
        _fox_cumsum_kernel,
        grid=(b,),
        in_specs=[pl.BlockSpec((FH, l), lambda i: (0, i))],
        out_specs=pl.BlockSpec((1, FH, l), lambda i: (i, 0, 0)),
        out_shape=jax.ShapeDtypeStruct((b, FH, l), F32),
        compiler_params=_cp(("parallel",)),
        name="fox_cumsum",
    )(lft)


def _fox_flash_kernel(q_ref, k_ref, v_ref, cq_ref, ck_ref, o_ref, m_ref, l_ref, acc_ref, cqc_ref, *, tq, tk):
    qi = pl.program_id(2)
    kj = pl.program_id(3)

    @pl.when(kj == 0)
    def _init():
        m_ref[...] = jnp.full(m_ref.shape, -jnp.inf, F32)
        l_ref[...] = jnp.zeros(l_ref.shape, F32)
        acc_ref[...] = jnp.zeros(acc_ref.shape, F32)
        eye = _iota2((LANES, LANES), 0) == _iota2((LANES, LANES), 1)
        for s in range(tq // LANES):
            row = jnp.broadcast_to(cq_ref[0, :, s * LANES:(s + 1) * LANES], (LANES, LANES))
            cqc_ref[s * LANES:(s + 1) * LANES, :] = jnp.sum(jnp.where(eye, row, 0.0), axis=1, keepdims=True)

    @pl.when(kj * tk <= qi * tq + (tq - 1))
    def _step():
        s = _mm_nt(q_ref[0], k_ref[0]) * (FHD ** -0.5) + cqc_ref[...] - ck_ref[0]
        qpos = qi * tq + _iota2((tq, tk), 0)
        kpos = kj * tk + _iota2((tq, tk), 1)
        s = jnp.where(kpos <= qpos, s, -jnp.inf)
        m_old = m_ref[...]
        m_new = jnp.maximum(m_old, jnp.max(s, axis=1, keepdims=True))
        alpha = jnp.exp(m_old - m_new)
        p = jnp.exp(s - m_new)
        l_ref[...] = alpha * l_ref[...] + jnp.sum(p, axis=1, keepdims=True)
        acc_ref[...] = alpha * acc_ref[...] + _mm(p, v_ref[0])
        m_ref[...] = m_new

    @pl.when(kj == pl.num_programs(3) - 1)
    def _fin():
        o_ref[0] = acc_ref[...] / l_ref[...]


def _fox_flash(q, k, v, cum, tq, tk):
    b, l, _ = q.shape
    kclamp = lambda qi, kj: jnp.minimum(kj, (qi * tq + tq - 1) // tk)
    return pl.pallas_call(
        functools.partial(_fox_flash_kernel, tq=tq, tk=tk),
        grid=(b, FH, l // tq, l // tk),
        in_specs=[pl.BlockSpec((1, tq, FHD), lambda i, h, qi, kj: (i, qi, h)),
                  pl.BlockSpec((1, tk, FHD), lambda i, h, qi, kj: (i, kclamp(qi, kj), h)),
                  pl.BlockSpec((1, tk, FHD), lambda i, h, qi, kj: (i, kclamp(qi, kj), h)),
                  pl.BlockSpec((1, 1, tq), lambda i, h, qi, kj: (i * FH + h, 0, qi)),
                  pl.BlockSpec((1, 1, tk), lambda i, h, qi, kj: (i * FH + h, 0, kclamp(qi, kj)))],
        out_specs=pl.BlockSpec((1, tq, FHD), lambda i, h, qi, kj: (i, qi, h)),
        out_shape=jax.ShapeDtypeStruct((b, l, D), F32),
        scratch_shapes=[pltpu.VMEM((tq, 1), F32), pltpu.VMEM((tq, 1), F32), pltpu.VMEM((tq, FHD), F32),
                        pltpu.VMEM((tq, 1), F32)],
        compiler_params=_cp(("parallel", "parallel", "parallel", "arbitrary")),
        name="fox_flash",
    )(q, k, v, cum, cum)


def _fox_sample_kernel(pt_ref, q_ref, kn_ref, vn_ref, lfn_ref, kp_ref, vp_ref, lfp_ref, o_ref,
                       m_ref, l_ref, acc_ref, carry_ref):
    j = pl.program_id(1)
    scale = FHD ** -0.5

    @pl.when(j == 0)
    def _init():
        m_ref[...] = jnp.full(m_ref.shape, -jnp.inf, F32)
        l_ref[...] = jnp.zeros(l_ref.shape, F32)
        acc_ref[...] = jnp.zeros(acc_ref.shape, F32)
        carry_ref[...] = jnp.zeros(carry_ref.shape, F32)

    hrow = _iota2((FH, PAGE), 0)
    q = q_ref[0]
    qb = q.astype(BF16)
    tri = (_iota2((PAGE, PAGE), 0) <= _iota2((PAGE, PAGE), 1)).astype(F32)
    lf = lfp_ref[0]
    s = jnp.zeros((FH, PAGE), F32)
    cum = jnp.zeros((FH, PAGE), F32)
    for h in range(FH):
        sh = lax.dot_general(qb, kp_ref[0, :, h, :].astype(BF16), NT, preferred_element_type=F32)
        s = jnp.where(hrow == h, sh, s)
        ch = jnp.sum(jnp.broadcast_to(lf[:, h:h + 1], (PAGE, PAGE)) * tri, axis=0, keepdims=True)
        cum = jnp.where(hrow == h, ch, cum)
    cum = cum + carry_ref[...]
    carry_new = cum[:, PAGE - 1:PAGE]
    carry_ref[...] = carry_new
    logits = s * scale - cum
    m_old = m_ref[...]
    m_new = jnp.maximum(m_old, jnp.max(logits, axis=1, keepdims=True))
    alpha = jnp.exp(m_old - m_new)
    p = jnp.exp(logits - m_new)
    pb = p.astype(BF16)
    o = jnp.zeros((FH, FHD), F32)
    for h in range(FH):
        oh = jnp.dot(pb, vp_ref[0, :, h, :].astype(BF16), preferred_element_type=F32)
        o = jnp.where(hrow == h, oh, o)
    l_new = alpha * l_ref[...] + jnp.sum(p, axis=1, keepdims=True)
    acc_new = alpha * acc_ref[...] + o
    m_ref[...] = m_new
    l_ref[...] = l_new
    acc_ref[...] = acc_new

    @pl.when(j == pl.num_programs(1) - 1)
    def _fin():
        s_new = jnp.sum(q * kn_ref[0], axis=1, keepdims=True) * scale - (carry_new + lfn_ref[0])
        m2 = jnp.maximum(m_new, s_new)
        a2 = jnp.exp(m_new - m2)
        p2 = jnp.exp(s_new - m2)
        o_ref[0] = (a2 * acc_new + p2 * vn_ref[0]) / (a2 * l_new + p2)


def _fox_sample(page_table, q3, k3, v3, lf3, k_pool, v_pool, lf_pool):
    b = q3.shape[0]
    npg = page_table.shape[1]
    pt = page_table.reshape(-1)
    tok = lambda i, j, pt_ref: (i, 0, 0)
    grid_spec = pltpu.PrefetchScalarGridSpec(
        num_scalar_prefetch=1,
        grid=(b, npg),
        in_specs=[pl.BlockSpec((1, FH, FHD), tok), pl.BlockSpec((1, FH, FHD), tok), pl.BlockSpec((1, FH, FHD), tok),
                  pl.BlockSpec((1, FH, 1), tok),
                  pl.BlockSpec((1, PAGE, FH, FHD), lambda i, j, pt_ref: (pt_ref[i * npg + j], 0, 0, 0)),
                  pl.BlockSpec((1, PAGE, FH, FHD), lambda i, j, pt_ref: (pt_ref[i * npg + j], 0, 0, 0)),
                  pl.BlockSpec((1, PAGE, FH), lambda i, j, pt_ref: (pt_ref[i * npg + j], 0, 0))],
        out_specs=pl.BlockSpec((1, FH, FHD), tok),
        scratch_shapes=[pltpu.VMEM((FH, 1), F32), pltpu.VMEM((FH, 1), F32), pltpu.VMEM((FH, FHD), F32),
                        pltpu.VMEM((FH, 1), F32)],
    )
    return pl.pallas_call(
        _fox_sample_kernel,
        grid_spec=grid_spec,
        out_shape=jax.ShapeDtypeStruct((b, FH, FHD), F32),
        compiler_params=_cp(("parallel", "arbitrary")),
        name="fox_sample",
    )(pt, q3, k3, v3, lf3, k_pool, v_pool, lf_pool)


_CAND_BLOCKS = [(a, 16 if a == 0 else 8, 16 // (a + 1)) for a in range(8)]


def _top16(s, iota):
    rank = jnp.full(s.shape, float(PTOPK), F32)
    vals = jnp.zeros((PTOPK, s.shape[1]), F32)
    i16 = _iota2((PTOPK, s.shape[1]), 0)
    for r in range(PTOPK):
        m = jnp.max(s, axis=0, keepdims=True)
        first = jnp.min(jnp.where(s == m, iota, 1e9), axis=0, keepdims=True)
        sel = iota == first
        rank = jnp.where(sel, float(r), rank)
        s = jnp.where(sel, -jnp.inf, s)
        vals = jnp.where(i16 == r, m, vals)
    return rank, vals


def _route_head(s_t):
    s1 = s_t[0:NKEYS]
    s2 = s_t[NKEYS:2 * NKEYS]
    iota = _iota2((NKEYS, LANES), 0).astype(F32)
    rank1, v1 = _top16(s1, iota)
    rank2, v2 = _top16(s2, iota)
    e1 = jnp.exp(v1 - v1[0:1])
    e2 = jnp.exp(v2 - v2[0:1])
    i8 = _iota2((8, LANES), 0)
    i8f = i8.astype(F32)
    i16f = _iota2((16, LANES), 0).astype(F32)
    cand, cidx, cprob = [], [], []
    for a, rows, valid in _CAND_BLOCKS:
        c = v1[a:a + 1] + v2[0:rows]
        pr = e1[a:a + 1] * e2[0:rows]
        if rows == 16:
            ix = i16f
        else:
            ix = i8f + float(a * PTOPK)
            c = jnp.where(i8 < valid, c, -jnp.inf)
        cand.append(c)
        cidx.append(ix)
        cprob.append(pr)
    cand.append(v1[8:16] + v2[0:1])
    cidx.append((i8f + 8.0) * float(PTOPK))
    cprob.append(e1[8:16] * e2[0:1])
    cand = jnp.concatenate(cand, axis=0)
    cidx = jnp.concatenate(cidx, axis=0)
    cprob = jnp.concatenate(cprob, axis=0)
    picked = jnp.zeros(cand.shape, F32)
    for _ in range(PTOPK):
        m = jnp.max(cand, axis=0, keepdims=True)
        first = jnp.min(jnp.where(cand == m, cidx, 1e9), axis=0, keepdims=True)
        sel = cidx == first
        picked = jnp.where(sel, 1.0, picked)
        cand = jnp.where(sel, -jnp.inf, cand)
    z = jnp.sum(picked * cprob, axis=0, keepdims=True)
    n_low = jnp.zeros((8, LANES), F32)
    off = 0
    for a, rows, _ in _CAND_BLOCKS:
        cnt = jnp.sum(picked[off:off + rows], axis=0, keepdims=True)
        n_low = jnp.where(i8 == a, cnt, n_low)
        off += rows
    n16 = jnp.concatenate([n_low, picked[off:off + 8]], axis=0)
    n1d = jnp.zeros((NKEYS, LANES), F32)
    for a in range(PTOPK):
        n1d = jnp.where(rank1 == float(a), n16[a:a + 1], n1d)
    e1d = jnp.exp(s1 - v1[0:1]) / z
    e2d = jnp.exp(s2 - v2[0:1])
    return rank2, e2d, n1d, e1d


def _peer_kernel(x_ref, wq_ref, keys_ref, u_ref, vt_ref, g_ref, b_ref, o_ref,
                 xb_ref, q_ref, r2_ref, e2_ref, n1_ref, e1_ref, at_ref, acc_ref, *, tm, te):
    j = pl.program_id(1)
    nlg = tm // LANES
    nsub = te // NKEYS

    @pl.when(j == 0)
    def _route():
        xb = x_ref[...].astype(BF16)
        xb_ref[...] = xb
        q = jnp.dot(xb, wq_ref[...], preferred_element_type=F32)
        for h in range(PH):
            for g in range(nlg):
                q_ref[h, g] = q[g * LANES:(g + 1) * LANES, h * LANES:(h + 1) * LANES].astype(BF16)
        acc_ref[...] = jnp.zeros(acc_ref.shape, F32)

        def body(it, carry):
            h = it // nlg
            g = it % nlg
            s_t = lax.dot_general(keys_ref[...], q_ref[h, g], NT, preferred_element_type=F32)
            r2, e2, n1, e1 = _route_head(s_t)
            r2_ref[h, g] = r2
            e2_ref[h, g] = e2
            n1_ref[h, g] = n1
            e1_ref[h, g] = e1
            return carry

        lax.fori_loop(0, PH * nlg, body, 0)

    h_t = lax.dot_general(u_ref[...], xb_ref[...], NT, preferred_element_type=F32)
    for g in range(nlg):
        for ii in range(nsub):
            i1 = j * nsub + ii
            gate = jnp.zeros((NKEYS, LANES), F32)
            for h in range(PH):
                n_row = n1_ref[h, g, pl.ds(i1, 1), :]
                e_row = e1_ref[h, g, pl.ds(i1, 1), :]
                gate = gate + jnp.where(r2_ref[h, g] < n_row, e2_ref[h, g] * e_row, 0.0)
            hb = h_t[ii * NKEYS:(ii + 1) * NKEYS, g * LANES:(g + 1) * LANES]
            act = 0.5 * hb * (1.0 + lax.erf(hb * (2.0 ** -0.5)))
            at_ref[ii * NKEYS:(ii + 1) * NKEYS, g * LANES:(g + 1) * LANES] = (act * gate).astype(BF16)
    acc_ref[...] += jnp.dot(vt_ref[...], at_ref[...], preferred_element_type=F32)

    @pl.when(j == pl.num_programs(1) - 1)
    def _fin():
        o_ref[...] = _layer_norm(ALPHA * x_ref[...] + acc_ref[...].T, g_ref[...], b_ref[...])


def _peer(x, wq, keys, u, vt, g, b, tm, te):
    t = x.shape[0]
    nlg = tm // LANES
    fixed = lambda i, j: (0, 0)
    return pl.pallas_call(
        functools.partial(_peer_kernel, tm=tm, te=te),
        grid=(t // tm, NEXP // te),
        in_specs=[pl.BlockSpec((tm, D), lambda i, j: (i, 0)),
                  pl.BlockSpec((D, PH * LANES), fixed),
                  pl.BlockSpec((2 * NKEYS, LANES), fixed),
                  pl.BlockSpec((te, D), lambda i, j: (j, 0)),
                  pl.BlockSpec((D, te), lambda i, j: (0, j)),
                  pl.BlockSpec((1, D), fixed), pl.BlockSpec((1, D), fixed)],
        out_specs=pl.BlockSpec((tm, D), lambda i, j: (i, 0)),
        out_shape=jax.ShapeDtypeStruct((t, D), F32),
        scratch_shapes=[pltpu.VMEM((tm, D), BF16),
                        pltpu.VMEM((PH, nlg, LANES, LANES), BF16),
                        pltpu.VMEM((PH, nlg, NKEYS, LANES), F32),
                        pltpu.VMEM((PH, nlg, NKEYS, LANES), F32),
                        pltpu.VMEM((PH, nlg, NKEYS, LANES), F32),
                        pltpu.VMEM((PH, nlg, NKEYS, LANES), F32),
                        pltpu.VMEM((te, tm), BF16),
                        pltpu.VMEM((D, tm), F32)],
        compiler_params=_cp(("parallel", "arbitrary")),
        name="peer",
    )(x, wq, keys, u, vt, g, b)


def _rope_tables(pos):
    half = RDK // 2
    inv = ROPE_BASE ** (-jnp.arange(half, dtype=F32) / half)
    ang = pos.astype(F32)[:, None] * inv[None, :]
    cos, sin = jnp.cos(ang), jnp.sin(ang)
    return jnp.concatenate([cos, cos], -1), jnp.concatenate([-sin, sin], -1)


def _pad_lanes(v, width=LANES, offset=0):
    return jnp.zeros((1, width), F32).at[0, offset:offset + v.shape[0]].set(v.astype(F32))


def _peer_params(wq, sub_keys, u_tab, v_tab):
    half = sub_keys.shape[-1]
    keys = jnp.zeros((2 * NKEYS, LANES), F32)
    keys = keys.at[0:NKEYS, 0:half].set(sub_keys[0]).at[NKEYS:, half:2 * half].set(sub_keys[1])
    return wq.astype(BF16), keys.astype(BF16), u_tab.astype(BF16), v_tab.T.astype(BF16)


def _tile(t, pref):
    return pref if t % pref == 0 else LANES


def kernel(x_prompt, x_sample, state_ret, state_gdn, state_gdn_conv, cache_fox_k, cache_fox_v, cache_fox_logf,
           page_table, w_in_a, ret_norm_g, gdn_a_log, gdn_dt_bias, gdn_conv_w, gdn_norm_g, w_out_a, w_in_c,
           fox_b_f, w_out_c, peer_wq, peer_sub_keys, peer_u, peer_v, ln_g, ln_b):
    bp, lp, _ = x_prompt.shape
    bs = x_sample.shape[0]
    tp = bp * lp
    xp = x_prompt.reshape(tp, D)
    xs = x_sample.reshape(bs, D)
    tmp, tms = _tile(tp, 512), _tile(bs, 512)
    ln = lambda layer, k: (ln_g[layer, k].reshape(1, D), ln_b[layer, k].reshape(1, D))

    w_in = jnp.pad(w_in_a[0], ((0, 0), (0, A_COLS_PAD - A_COLS))).astype(BF16)
    cos_p, sin_p = _rope_tables(jnp.arange(lp))
    cos_s, sin_s = _rope_tables(PAST + jnp.arange(1))
    alog, dtb = _pad_lanes(gdn_a_log[0]), _pad_lanes(gdn_dt_bias[0])
    rg, gg = ret_norm_g[0].reshape(1, RDV), gdn_norm_g[0].reshape(1, GDV)
    proj_p = _proj(xp, w_in, tmp).reshape(bp, lp, A_COLS_PAD)
    proj_s = _proj(xs, w_in, tms).reshape(bs, 1, A_COLS_PAD)
    mix_p, ret_p, gdn_p, conv_p = _mixer_a_prompt(proj_p, cos_p, sin_p, gdn_conv_w[0], alog, dtb, rg, gg)
    mix_s, ret_s, gdn_s, conv_s = _mixer_a_sample(proj_s, cos_s, sin_s, gdn_conv_w[0], alog, dtb, rg, gg,
                                                  state_ret[0], state_gdn[0], state_gdn_conv[0])
    w_out = w_out_a[0].astype(BF16)
    g0, b0 = ln(0, 0)
    xp = _outproj_ln(mix_p.reshape(tp, D), xp, w_out, g0, b0, tmp)
    xs = _outproj_ln(mix_s.reshape(bs, D), xs, w_out, g0, b0, tms)
    pw = _peer_params(peer_wq[0], peer_sub_keys[0], peer_u[0], peer_v[0])
    g1, b1 = ln(0, 1)
    xp = _peer(xp, *pw, g1, b1, tmp, 512)
    xs = _peer(xs, *pw, g1, b1, tms, 512)

    w_c = jnp.pad(w_in_c[0], ((0, 0), (0, C_COLS_PAD - w_in_c.shape[-1]))).astype(BF16)
    bf_row = _pad_lanes(fox_b_f[0])
    qp, kp, vp, lfp, lftp = _fox_proj(xp, w_c, bf_row, tmp)
    qs, ks, vs, lfs, _ = _fox_proj(xs, w_c, bf_row, tms)
    cum = _fox_cumsum(lftp, bp, lp).reshape(bp * FH, 1, lp)
    tq = _tile(lp, 512)
    op = _fox_flash(qp.reshape(bp, lp, D), kp.reshape(bp, lp, D), vp.reshape(bp, lp, D), cum, tq, tq)
    os_ = _fox_sample(page_table, qs.reshape(bs, FH, FHD), ks.reshape(bs, FH, FHD), vs.reshape(bs, FH, FHD),
                      lfs.reshape(bs, FH, 1), cache_fox_k[0], cache_fox_v[0], cache_fox_logf[0])
    w_oc = w_out_c[0].astype(BF16)
    g2, b2 = ln(1, 0)
    xp = _outproj_ln(op.reshape(tp, D), xp, w_oc, g2, b2, tmp)
    xs = _outproj_ln(os_.reshape(bs, D), xs, w_oc, g2, b2, tms)
    pw = _peer_params(peer_wq[1], peer_sub_keys[1], peer_u[1], peer_v[1])
    g3, b3 = ln(1, 1)
    xp = _peer(xp, *pw, g3, b3, tmp, 512)
    xs = _peer(xs, *pw, g3, b3, tms, 512)

    return (xp.reshape(bp, lp, D), xs.reshape(bs, 1, D),
            ret_p[None], ret_s[None], gdn_p[None], gdn_s[None], conv_p[None], conv_s[None],
            kp.reshape(1, bp, lp, FH, FHD), ks.reshape(1, bs, 1, FH, FHD),
            vp.reshape(1, bp, lp, FH, FHD), vs.reshape(1, bs, 1, FH, FHD),
            lfp.reshape(1, bp, lp, FH), lfs.reshape(1, bs, 1, FH))
```

```python
import functools
import math

import jax
import jax.numpy as jnp
from jax import lax
from jax.experimental import pallas as pl
from jax.experimental.pallas import tpu as pltpu

F32 = jnp.float32
BF16 = jnp.bfloat16
HI = lax.Precision.HIGHEST

D = 1024
PAST = 2048
PAGE = 128
RH, RDK, RDV = 4, 128, 128
GH, GDK, GDV = 4, 128, 128
CONV_W = 4
CONV_CH = 2 * GH * GDK + GH * GDV
A_COLS = 4104
A_COLS_PAD = 4224
FH, FHD = 8, 128
C_COLS_PAD = 3200
PH, PTOPK, NKEYS = 8, 16, 128
NEXP = NKEYS * NKEYS
ALPHA = 4.0 ** 0.25
LN_EPS = 1e-5
NORM_EPS = 1e-6
ROPE_BASE = 10000.0
LANES = 128
CHUNK = 128
PAGES_PER_STEP = 4
VMEM_LIMIT = 56 * 1024 * 1024

NT = (((1,), (1,)), ((), ()))
TN = (((0,), (0,)), ((), ()))


def _cp(sem):
    return pltpu.CompilerParams(dimension_semantics=sem, vmem_limit_bytes=VMEM_LIMIT)


def _mm(a, b):
    return jnp.dot(a.astype(BF16), b.astype(BF16), preferred_element_type=F32)


def _mm_nt(a, b):
    return lax.dot_general(a.astype(BF16), b.astype(BF16), NT, preferred_element_type=F32)


def _mm_tn(a, b):
    return lax.dot_general(a.astype(BF16), b.astype(BF16), TN, preferred_element_type=F32)


def _mm_hi(a, b):
    return jnp.dot(a, b, precision=HI, preferred_element_type=F32)


def _split2(a):
    hi = a.astype(BF16)
    return hi, (a - hi.astype(F32)).astype(BF16)


def _mm_s(a, b):
    return (jnp.dot(a[0], b[0], preferred_element_type=F32) + jnp.dot(a[0], b[1], preferred_element_type=F32)
            + jnp.dot(a[1], b[0], preferred_element_type=F32))


def _sigmoid(x):
    return 1.0 / (1.0 + jnp.exp(-x))


def _silu(x):
    return x * _sigmoid(x)


def _softplus(x):
    return jnp.maximum(x, 0.0) + jnp.log1p(jnp.exp(-jnp.abs(x)))


def _log_sigmoid(x):
    return jnp.minimum(x, 0.0) - jnp.log1p(jnp.exp(-jnp.abs(x)))


def _layer_norm(z, g, b):
    mu = jnp.mean(z, axis=-1, keepdims=True)
    d = z - mu
    var = jnp.mean(d * d, axis=-1, keepdims=True)
    return d * lax.rsqrt(var + LN_EPS) * g + b


def _iota2(shape, axis):
    return lax.broadcasted_iota(jnp.int32, shape, axis)


def _proj_kernel(x_ref, w_ref, o_ref):
    o_ref[...] = jnp.dot(x_ref[...].astype(BF16), w_ref[...], preferred_element_type=F32)


def _proj(x, w, tm):
    t, k = x.shape
    n = w.shape[1]
    return pl.pallas_call(
        _proj_kernel,
        grid=(t // tm,),
        in_specs=[pl.BlockSpec((tm, k), lambda i: (i, 0)), pl.BlockSpec((k, n), lambda i: (0, 0))],
        out_specs=pl.BlockSpec((tm, n), lambda i: (i, 0)),
        out_shape=jax.ShapeDtypeStruct((t, n), F32),
        compiler_params=_cp(("parallel",)),
        name="proj_a",
    )(x, w)


def _fox_proj_kernel(x_ref, w_ref, bf_ref, k_ref, v_ref, lf_ref, lft_ref, *q_refs, attn_copies):
    p = jnp.dot(x_ref[...].astype(BF16), w_ref[...], preferred_element_type=F32)
    q = p[:, 0:D]
    k = p[:, D:2 * D]
    v = p[:, 2 * D:3 * D]
    k_ref[...] = k
    v_ref[...] = v
    lf = _log_sigmoid(p[:, 3 * D:3 * D + LANES] + bf_ref[...])
    lf_ref[...] = lf[:, 0:FH]
    lft_ref[...] = lf.T[0:FH, :]
    if attn_copies:
        qb_ref, kb_ref, vb_ref = q_refs
        qb_ref[...] = (q * (FHD ** -0.5)).astype(BF16)
        kb_ref[...] = k.astype(BF16)
        vb_ref[...] = v.astype(BF16)
    else:
        q_refs[0][...] = q


def _fox_proj(x, w, bf_row, tm, attn_copies):
    t = x.shape[0]
    row = lambda i: (i, 0)
    out_specs = [pl.BlockSpec((tm, D), row), pl.BlockSpec((tm, D), row),
                 pl.BlockSpec((tm, FH), row), pl.BlockSpec((FH, tm), lambda i: (0, i))]
    out_shape = [jax.ShapeDtypeStruct((t, D), F32), jax.ShapeDtypeStruct((t, D), F32),
                 jax.ShapeDtypeStruct((t, FH), F32), jax.ShapeDtypeStruct((FH, t), F32)]
    n_q = 3 if attn_copies else 1
    out_specs += [pl.BlockSpec((tm, D), row)] * n_q
    out_shape += [jax.ShapeDtypeStruct((t, D), BF16 if attn_copies else F32)] * n_q
    return pl.pallas_call(
        functools.partial(_fox_proj_kernel, attn_copies=attn_copies),
        grid=(t // tm,),
        in_specs=[pl.BlockSpec((tm, D), row), pl.BlockSpec((D, C_COLS_PAD), lambda i: (0, 0)),
                  pl.BlockSpec((1, LANES), lambda i: (0, 0))],
        out_specs=out_specs,
        out_shape=out_shape,
        compiler_params=_cp(("parallel",)),
        name="proj_c",
    )(x, w, bf_row)


def _outproj_ln_kernel(a_ref, x_ref, w_ref, g_ref, b_ref, o_ref):
    h = jnp.dot(a_ref[...].astype(BF16), w_ref[...], preferred_element_type=F32)
    o_ref[...] = _layer_norm(ALPHA * x_ref[...] + h, g_ref[...], b_ref[...])


def _outproj_ln(a, x, w, g, b, tm):
    t = x.shape[0]
    row = lambda i: (i, 0)
    fixed = lambda i: (0, 0)
    return pl.pallas_call(
        _outproj_ln_kernel,
        grid=(t // tm,),
        in_specs=[pl.BlockSpec((tm, D), row), pl.BlockSpec((tm, D), row), pl.BlockSpec((D, D), fixed),
                  pl.BlockSpec((1, D), fixed), pl.BlockSpec((1, D), fixed)],
        out_specs=pl.BlockSpec((tm, D), row),
        out_shape=jax.ShapeDtypeStruct((t, D), F32),
        compiler_params=_cp(("parallel",)),
        name="outproj_ln",
    )(a, x, w, g, b)


def _unit_lower_inverses(a_list, ri, ci):
    eye = (ri == ci).astype(F32)
    blk = (ri >> 4) == (ci >> 4)
    d = [_split2(jnp.where(blk, a, 0.0)) for a in a_list]
    d2f = [_mm_s(x, x) for x in d]
    d2 = [_split2(x) for x in d2f]
    d4f = [_mm_s(x, x) for x in d2]
    d4 = [_split2(x) for x in d4f]
    d8 = [_split2(_mm_s(x, x)) for x in d4]
    xs = [eye - jnp.where(blk, a, 0.0) for a in a_list]
    for pw in (d2, d4, d8):
        xs = [x + _mm_s(_split2(x), p) for x, p in zip(xs, pw)]
    for s in (5, 6, 7):
        msk = ((ri >> s) == (ci >> s)) & ((ri >> (s - 1)) != (ci >> (s - 1)))
        es = [_split2(jnp.where(msk, a, 0.0)) for a in a_list]
        xsp = [_split2(x) for x in xs]
        xe = [_split2(_mm_s(x, e)) for x, e in zip(xsp, es)]
        xs = [x - _mm_s(y, xp) for x, y, xp in zip(xs, xe, xsp)]
    return xs


def _head_gate_norm(out, g_row, gate):
    return out * lax.rsqrt(jnp.mean(out * out, axis=-1, keepdims=True) + NORM_EPS) * g_row * _silu(gate)


def _mixer_a_prompt_kernel(p_ref, cos_ref, sin_ref, cw_ref, alog_ref, dt_ref, rg_ref, gg_ref,
                           mix_ref, ret_ref, gdn_ref, conv_ref, ext_ref, *, nb):
    n = pl.program_id(0)
    c = CHUNK

    @pl.when(n == 0)
    def _init():
        ret_ref[...] = jnp.zeros_like(ret_ref)
        gdn_ref[...] = jnp.zeros_like(gdn_ref)
        ext_ref[:, 0:8, :] = jnp.zeros((nb, 8, CONV_CH), F32)

    ri = _iota2((c, c), 0)
    ci = _iota2((c, c), 1)
    rf = ri.astype(F32)
    diff = rf - ci.astype(F32)
    cosv = cos_ref[...]
    sinv = sin_ref[...]

    for h in range(RH):
        lg = math.log1p(-(2.0 ** (-5.0 - h)))
        decay = jnp.where(diff >= 0, jnp.exp(jnp.maximum(diff, 0.0) * lg), 0.0)
        inner = jnp.exp((rf + 1.0) * lg)
        kdec = jnp.exp((c - 1.0 - rf) * lg)
        for b in range(nb):
            rq = p_ref[b, :, h * RDK:(h + 1) * RDK]
            rk = p_ref[b, :, RH * RDK + h * RDK:RH * RDK + (h + 1) * RDK]
            rv = p_ref[b, :, 2 * RH * RDK + h * RDV:2 * RH * RDK + (h + 1) * RDV]
            rgate = p_ref[b, :, 2 * RH * RDK + RH * RDV + h * RDV:2 * RH * RDK + RH * RDV + (h + 1) * RDV]
            q = rq * cosv + pltpu.roll(rq, RDK // 2, 1) * sinv
            k = (rk * cosv + pltpu.roll(rk, RDK // 2, 1) * sinv) * (RDK ** -0.5)
            s = ret_ref[b, h]
            scores = _mm_nt(q, k) * decay
            out = _mm(scores, rv) + _mm(q * inner, s)
            ret_ref[b, h] = math.exp(c * lg) * s + _mm_tn(k * kdec, rv)
            mix_ref[b, :, h * RDV:(h + 1) * RDV] = _head_gate_norm(out, rg_ref[...], rgate)

    g_off = 2 * RH * RDK + 2 * RH * RDV
    z_off = g_off + CONV_CH
    incl = ri >= ci
    strict = ri > ci
    tril = incl.astype(BF16)
    units = []
    for b in range(nb):
        ext_ref[b, 8:8 + c, :] = p_ref[b, :, g_off:g_off + CONV_CH]
        y = ext_ref[b, 5:5 + c, :] * cw_ref[0:1, :]
        for i in range(1, CONV_W):
            y = y + ext_ref[b, 5 + i:5 + i + c, :] * cw_ref[i:i + 1, :]
        conv_ref[b] = ext_ref[b, c + 5:c + 8, :]
        ext_ref[b, 0:8, :] = ext_ref[b, c:c + 8, :]
        y = _silu(y)
        tail = p_ref[b, :, z_off + GH * GDV:z_off + GH * GDV + LANES]
        g_all = -jnp.exp(alog_ref[...]) * _softplus(tail + dt_ref[...])
        beta_all = _sigmoid(tail)
        for h in range(GH):
            qh = y[:, h * GDK:(h + 1) * GDK]
            kh = y[:, GH * GDK + h * GDK:GH * GDK + (h + 1) * GDK]
            vh = y[:, 2 * GH * GDK + h * GDV:2 * GH * GDK + (h + 1) * GDV]
            qh = qh * lax.rsqrt(jnp.sum(qh * qh, axis=-1, keepdims=True) + NORM_EPS) * (GDK ** -0.5)
            kh = kh * lax.rsqrt(jnp.sum(kh * kh, axis=-1, keepdims=True) + NORM_EPS)
            gb = jnp.broadcast_to(g_all[:, h:h + 1], (c, c))
            bcol = beta_all[:, GH + h:GH + h + 1]
            units.append((b, h, qh, kh, vh, gb, bcol))

    gcs = []
    for (_, _, _, _, _, gb, _) in units:
        g1 = gb.astype(BF16)
        r1 = gb - g1.astype(F32)
        g2 = r1.astype(BF16)
        g3 = (r1 - g2.astype(F32)).astype(BF16)
        gcs.append(jnp.dot(tril, g1, preferred_element_type=F32) + jnp.dot(tril, g2, preferred_element_type=F32)
                   + jnp.dot(tril, g3, preferred_element_type=F32))
    decs = []
    for gc in gcs:
        dmat = gc - gc.T
        decs.append(jnp.where(incl, jnp.exp(jnp.where(incl, dmat, 0.0)), 0.0))
    kbs = [u[3] * u[6] for u in units]
    a_list = [jnp.where(strict, _mm_nt(kb, u[3]) * dec, 0.0) for kb, u, dec in zip(kbs, units, decs)]
    ts = [t.astype(BF16) for t in _unit_lower_inverses(a_list, ri, ci)]
    egcs = [jnp.exp(gc) for gc in gcs]
    us = [jnp.dot(t, (u[4] * u[6]).astype(BF16), preferred_element_type=F32) for t, u in zip(ts, units)]
    ws = [jnp.dot(t, (kb * egc).astype(BF16), preferred_element_type=F32) for t, kb, egc in zip(ts, kbs, egcs)]
    attns = [_mm_nt(u[2], u[3]) * dec for u, dec in zip(units, decs)]
    ss = [gdn_ref[u[0], u[1]] for u in units]
    v_news = [uu - _mm(w, s) for uu, w, s in zip(us, ws, ss)]
    outs = [_mm(u[2] * egc, s) + _mm(attn, vn) for u, egc, s, attn, vn in zip(units, egcs, ss, attns, v_news)]
    for u, gc, s, vn, out in zip(units, gcs, ss, v_news, outs):
        b, h = u[0], u[1]
        g_last = gc[c - 1:c, :]
        gdn_ref[b, h] = jnp.exp(g_last) * s + _mm_tn(u[3] * jnp.exp(g_last - gc), vn)
        gz = p_ref[b, :, z_off + h * GDV:z_off + (h + 1) * GDV]
        mix_ref[b, :, RH * RDV + h * GDV:RH * RDV + (h + 1) * GDV] = _head_gate_norm(out, gg_ref[...], gz)


def _mixer_a_prompt(proj, cos2, sin2, conv_w, alog_row, dt_row, rg_row, gg_row):
    b, l, _ = proj.shape
    fixed = lambda n: (0, 0)
    return pl.pallas_call(
        functools.partial(_mixer_a_prompt_kernel, nb=b),
        grid=(l // CHUNK,),
        in_specs=[pl.BlockSpec((b, CHUNK, A_COLS_PAD), lambda n: (0, n, 0)),
                  pl.BlockSpec((CHUNK, LANES), lambda n: (n, 0)),
                  pl.BlockSpec((CHUNK, LANES), lambda n: (n, 0)),
                  pl.BlockSpec((CONV_W, CONV_CH), fixed),
                  pl.BlockSpec((1, LANES), fixed), pl.BlockSpec((1, LANES), fixed),
                  pl.BlockSpec((1, LANES), fixed), pl.BlockSpec((1, LANES), fixed)],
        out_specs=[pl.BlockSpec((b, CHUNK, D), lambda n: (0, n, 0)),
                   pl.BlockSpec((b, RH, RDK, RDV), lambda n: (0, 0, 0, 0)),
                   pl.BlockSpec((b, GH, GDK, GDV), lambda n: (0, 0, 0, 0)),
                   pl.BlockSpec((b, CONV_W - 1, CONV_CH), lambda n: (0, 0, 0))],
        out_shape=[jax.ShapeDtypeStruct((b, l, D), F32),
                   jax.ShapeDtypeStruct((b, RH, RDK, RDV), F32),
                   jax.ShapeDtypeStruct((b, GH, GDK, GDV), F32),
                   jax.ShapeDtypeStruct((b, CONV_W - 1, CONV_CH), F32)],
        scratch_shapes=[pltpu.VMEM((b, CHUNK + 8, CONV_CH), F32)],
        compiler_params=_cp(("arbitrary",)),
        name="mixer_a_prompt",
    )(proj, cos2, sin2, conv_w, alog_row, dt_row, rg_row, gg_row)


def _mixer_a_sample_kernel(p_ref, cos_ref, sin_ref, cw_ref, alog_ref, dt_ref, rg_ref, gg_ref,
                           rs_ref, gs_ref, cs_ref, mix_ref, ret_ref, gdn_ref, conv_ref):
    eye = _iota2((LANES, LANES), 0) == _iota2((LANES, LANES), 1)

    def col(v):
        return jnp.sum(jnp.where(eye, jnp.broadcast_to(v, (LANES, LANES)), 0.0), axis=1, keepdims=True)

    def vec_mat(c, s):
        return jnp.sum(c * s, axis=0, keepdims=True)

    cosv = cos_ref[...]
    sinv = sin_ref[...]
    for h in range(RH):
        gamma = 1.0 - 2.0 ** (-5.0 - h)
        rq = p_ref[0, :, h * RDK:(h + 1) * RDK]
        rk = p_ref[0, :, RH * RDK + h * RDK:RH * RDK + (h + 1) * RDK]
        rv = p_ref[0, :, 2 * RH * RDK + h * RDV:2 * RH * RDK + (h + 1) * RDV]
        rgate = p_ref[0, :, 2 * RH * RDK + RH * RDV + h * RDV:2 * RH * RDK + RH * RDV + (h + 1) * RDV]
        q = rq * cosv + pltpu.roll(rq, RDK // 2, 1) * sinv
        k = (rk * cosv + pltpu.roll(rk, RDK // 2, 1) * sinv) * (RDK ** -0.5)
        s = rs_ref[0, h]
        qk = jnp.sum(q * k, axis=-1, keepdims=True)
        out = qk * rv + gamma * vec_mat(col(q), s)
        ret_ref[0, h] = gamma * s + col(k) * rv
        mix_ref[0, :, h * RDV:(h + 1) * RDV] = _head_gate_norm(out, rg_ref[...], rgate)

    g_off = 2 * RH * RDK + 2 * RH * RDV
    gq_all = p_ref[0, :, g_off:g_off + CONV_CH]
    cb = cs_ref[0]
    y = cb[0:1, :] * cw_ref[0:1, :] + cb[1:2, :] * cw_ref[1:2, :] + cb[2:3, :] * cw_ref[2:3, :] \
        + gq_all * cw_ref[3:4, :]
    conv_ref[0, 0:2, :] = cb[1:3, :]
    conv_ref[0, 2:3, :] = gq_all
    y = _silu(y)
    z_off = g_off + CONV_CH
    tail = p_ref[0, :, z_off + GH * GDV:z_off + GH * GDV + LANES]
    g_all = -jnp.exp(alog_ref[...]) * _softplus(tail + dt_ref[...])
    beta_all = _sigmoid(tail)
    for h in range(GH):
        qh = y[:, h * GDK:(h + 1) * GDK]
        kh = y[:, GH * GDK + h * GDK:GH * GDK + (h + 1) * GDK]
        vh = y[:, 2 * GH * GDK + h * GDV:2 * GH * GDK + (h + 1) * GDV]
        qh = qh * lax.rsqrt(jnp.sum(qh * qh, axis=-1, keepdims=True) + NORM_EPS) * (GDK ** -0.5)
        kh = kh * lax.rsqrt(jnp.sum(kh * kh, axis=-1, keepdims=True) + NORM_EPS)
        eg = jnp.exp(g_all[:, h:h + 1])
        beta = beta_all[:, GH + h:GH + h + 1]
        s = gs_ref[0, h]
        kc = col(kh)
        v_new = vh * beta - vec_mat(kc * (beta * eg), s)
        qk = jnp.sum(qh * kh, axis=-1, keepdims=True)
        out = eg * vec_mat(col(qh), s) + qk * v_new
        gdn_ref[0, h] = eg * s + kc * v_new
        gz = p_ref[0, :, z_off + h * GDV:z_off + (h + 1) * GDV]
        mix_ref[0, :, RH * RDV + h * GDV:RH * RDV + (h + 1) * GDV] = _head_gate_norm(out, gg_ref[...], gz)


def _mixer_a_sample(proj, cos2, sin2, conv_w, alog_row, dt_row, rg_row, gg_row, ret_s, gdn_s, conv_s):
    b = proj.shape[0]
    fixed = lambda i: (0, 0)
    st = lambda i: (i, 0, 0, 0)
    return pl.pallas_call(
        _mixer_a_sample_kernel,
        grid=(b,),
        in_specs=[pl.BlockSpec((1, 1, A_COLS_PAD), lambda i: (i, 0, 0)),
                  pl.BlockSpec((1, LANES), fixed), pl.BlockSpec((1, LANES), fixed),
                  pl.BlockSpec((CONV_W, CONV_CH), fixed),
                  pl.BlockSpec((1, LANES), fixed), pl.BlockSpec((1, LANES), fixed),
                  pl.BlockSpec((1, LANES), fixed), pl.BlockSpec((1, LANES), fixed),
                  pl.BlockSpec((1, RH, RDK, RDV), st), pl.BlockSpec((1, GH, GDK, GDV), st),
                  pl.BlockSpec((1, CONV_W - 1, CONV_CH), lambda i: (i, 0, 0))],
        out_specs=[pl.BlockSpec((1, 1, D), lambda i: (i, 0, 0)),
                   pl.BlockSpec((1, RH, RDK, RDV), st), pl.BlockSpec((1, GH, GDK, GDV), st),
                   pl.BlockSpec((1, CONV_W - 1, CONV_CH), lambda i: (i, 0, 0))],
        out_shape=[jax.ShapeDtypeStruct((b, 1, D), F32),
                   jax.ShapeDtypeStruct((b, RH, RDK, RDV), F32),
                   jax.ShapeDtypeStruct((b, GH, GDK, GDV), F32),
                   jax.ShapeDtypeStruct((b, CONV_W - 1, CONV_CH), F32)],
        compiler_params=_cp(("parallel",)),
        name="mixer_a_sample",
    )(proj, cos2, sin2, conv_w, alog_row, dt_row, rg_row, gg_row, ret_s, gdn_s, conv_s)


def _fox_cumsum_kernel(x_ref, o_ref):
    n = x_ref.shape[1] // LANES
    upper = (_iota2((LANES, LANES), 0) <= _iota2((LANES, LANES), 1)).astype(F32)
    carry = jnp.zeros((FH, 1), F32)
    for c in range(n):
        p = _mm_hi(x_ref[:, c * LANES:(c + 1) * LANES], upper) + carry
        o_ref[0, :, c * LANES:(c + 1) * LANES] = p
        carry = p[:, LANES - 1:LANES]


def _fox_cumsum(lft, b, l):
    return pl.pallas_call(
        _fox_cumsum_kernel,
        grid=(b,),
        in_specs=[pl.BlockSpec((FH, l), lambda i: (0, i))],
        out_specs=pl.BlockSpec((1, FH, l), lambda i: (i, 0, 0)),
        out_shape=jax.ShapeDtypeStruct((b, FH, l), F32),
        compiler_params=_cp(("parallel",)),
        name="fox_cumsum",
    )(lft)


def _fox_flash_kernel(q_ref, k_ref, v_ref, ck_ref, o_ref, *, tq, tk):
    qi = pl.program_id(2)
    ratio = tq // tk
    q = q_ref[0]

    def block(j, carry, masked):
        m_old, l_old, acc = carry
        k = k_ref[0, pl.ds(pl.multiple_of(j * tk, tk), tk), :]
        v = v_ref[0, pl.ds(pl.multiple_of(j * tk, tk), tk), :]
        s = lax.dot_general(q, k, NT, preferred_element_type=F32) - ck_ref[0, pl.ds(j, 1), :]
        if masked:
            qpos = qi * tq + _iota2((tq, tk), 0)
            kpos = j * tk + _iota2((tq, tk), 1)
            s = jnp.where(kpos <= qpos, s, -jnp.inf)
        m_new = jnp.maximum(m_old, jnp.max(s, axis=1, keepdims=True))
        alpha = jnp.exp(m_old - m_new)
        p = jnp.exp(s - m_new)
        l_new = alpha * l_old + jnp.sum(p, axis=1, keepdims=True)
        acc = alpha * acc + jnp.dot(p.astype(BF16), v, preferred_element_type=F32)
        return m_new, l_new, acc

    init = (jnp.full((tq, 1), -jnp.inf, F32), jnp.zeros((tq, 1), F32), jnp.zeros((tq, FHD), F32))
    carry = lax.fori_loop(0, qi * ratio, lambda j, c: block(j, c, False), init)
    for d in range(ratio):
        carry = block(qi * ratio + d, carry, True)
    o_ref[0] = carry[2] / carry[1]


def _fox_flash(qb, kb, vb, cum, tq, tk):
    b, l, _ = qb.shape
    return pl.pallas_call(
        functools.partial(_fox_flash_kernel, tq=tq, tk=tk),
        grid=(b, FH, l // tq),
        in_specs=[pl.BlockSpec((1, tq, FHD), lambda i, h, qi: (i, qi, h)),
                  pl.BlockSpec((1, l, FHD), lambda i, h, qi: (i, 0, h)),
                  pl.BlockSpec((1, l, FHD), lambda i, h, qi: (i, 0, h)),
                  pl.BlockSpec((1, l // tk, tk), lambda i, h, qi: (i * FH + h, 0, 0))],
        out_specs=pl.BlockSpec((1, tq, FHD), lambda i, h, qi: (i, qi, h)),
        out_shape=jax.ShapeDtypeStruct((b, l, D), F32),
        compiler_params=_cp(("parallel", "parallel", "arbitrary")),
        name="fox_flash",
    )(qb, kb, vb, cum)


def _fox_sample_kernel(pt_ref, q_ref, kn_ref, vn_ref, lfn_ref, *rest):
    npp = PAGES_PER_STEP
    k_refs, v_refs, lf_refs = rest[0:npp], rest[npp:2 * npp], rest[2 * npp:3 * npp]
    o_ref, m_ref, l_ref, acc_ref, carry_ref = rest[3 * npp:]
    j = pl.program_id(1)
    scale = FHD ** -0.5
    rows = PAGE * FH

    @pl.when(j == 0)
    def _init():
        m_ref[...] = jnp.full(m_ref.shape, -jnp.inf, F32)
        l_ref[...] = jnp.zeros(l_ref.shape, F32)
        acc_ref[...] = jnp.zeros(acc_ref.shape, F32)
        carry_ref[...] = jnp.zeros(carry_ref.shape, F32)

    q = q_ref[0]
    qb = (q * scale).astype(BF16)
    own = (_iota2((FH, rows), 1) & (FH - 1)) == _iota2((FH, rows), 0)
    upper = (_iota2((LANES, LANES), 0) <= _iota2((LANES, LANES), 1)).astype(F32)
    m_run, l_run, acc, carry = m_ref[...], l_ref[...], acc_ref[...], carry_ref[...]
    for pg in range(npp):
        kp = k_refs[pg][0].reshape(rows, FHD).astype(BF16)
        vp = v_refs[pg][0].reshape(rows, FHD).astype(BF16)
        s = lax.dot_general(qb, kp, NT, preferred_element_type=F32)
        lf = jnp.where(own, jnp.broadcast_to(lf_refs[pg][0], (FH, rows)), 0.0)
        stacked = jnp.concatenate([lf[:, c * LANES:(c + 1) * LANES] for c in range(rows // LANES)], axis=0)
        pref = _mm_hi(stacked, upper)
        cums = []
        for c in range(rows // LANES):
            pc = pref[c * FH:(c + 1) * FH, :] + carry
            cums.append(pc)
            carry = pc[:, LANES - 1:LANES]
        cum = jnp.concatenate(cums, axis=1)
        logits = jnp.where(own, s - cum, -jnp.inf)
        m_new = jnp.maximum(m_run, jnp.max(logits, axis=1, keepdims=True))
        alpha = jnp.exp(m_run - m_new)
        p = jnp.exp(logits - m_new)
        l_run = alpha * l_run + jnp.sum(p, axis=1, keepdims=True)
        acc = alpha * acc + jnp.dot(p.astype(BF16), vp, preferred_element_type=F32)
        m_run = m_new
    m_ref[...] = m_run
    l_ref[...] = l_run
    acc_ref[...] = acc
    carry_ref[...] = carry

    @pl.when(j == pl.num_programs(1) - 1)
    def _fin():
        s_new = jnp.sum(q * kn_ref[0], axis=1, keepdims=True) * scale - (carry + lfn_ref[0])
        m2 = jnp.maximum(m_run, s_new)
        a2 = jnp.exp(m_run - m2)
        p2 = jnp.exp(s_new - m2)
        o_ref[0] = (a2 * acc + p2 * vn_ref[0]) / (a2 * l_run + p2)


def _fox_sample(page_table, q3, k3, v3, lf3, k_pool, v_pool, lf_pool):
    b = q3.shape[0]
    npg = page_table.shape[1]
    npp = PAGES_PER_STEP
    assert npg % npp == 0
    pt = page_table.reshape(-1)
    tok = lambda i, j, pt_ref: (i, 0, 0)

    def page(r, nd):
        return lambda i, j, pt_ref: (pt_ref[i * npg + j * npp + r],) + (0,) * nd

    grid_spec = pltpu.PrefetchScalarGridSpec(
        num_scalar_prefetch=1,
        grid=(b, npg // npp),
        in_specs=[pl.BlockSpec((1, FH, FHD), tok), pl.BlockSpec((1, FH, FHD), tok), pl.BlockSpec((1, FH, FHD), tok),
                  pl.BlockSpec((1, FH, 1), tok)]
        + [pl.BlockSpec((1, PAGE, FH, FHD), page(r, 3)) for r in range(npp)]
        + [pl.BlockSpec((1, PAGE, FH, FHD), page(r, 3)) for r in range(npp)]
        + [pl.BlockSpec((1, 1, PAGE * FH), page(r, 2)) for r in range(npp)],
        out_specs=pl.BlockSpec((1, FH, FHD), tok),
        scratch_shapes=[pltpu.VMEM((FH, 1), F32), pltpu.VMEM((FH, 1), F32), pltpu.VMEM((FH, FHD), F32),
                        pltpu.VMEM((FH, 1), F32)],
    )
    lf_flat = lf_pool.reshape(lf_pool.shape[0], 1, PAGE * FH)
    return pl.pallas_call(
        _fox_sample_kernel,
        grid_spec=grid_spec,
        out_shape=jax.ShapeDtypeStruct((b, FH, FHD), F32),
        compiler_params=_cp(("parallel", "arbitrary")),
        name="fox_sample",
    )(pt, q3, k3, v3, lf3, *([k_pool] * npp), *([v_pool] * npp), *([lf_flat] * npp))


_CAND_BLOCKS = [(a, 16 if a == 0 else 8, 16 // (a + 1)) for a in range(8)]


def _top16(s, iota, exact):
    rank = jnp.full(s.shape, float(PTOPK), F32)
    vals = jnp.zeros((PTOPK, s.shape[1]), F32)
    i16 = _iota2((PTOPK, s.shape[1]), 0)
    for r in range(PTOPK):
        m = jnp.max(s, axis=0, keepdims=True)
        sel = s == m
        if exact:
            sel = iota == jnp.min(jnp.where(sel, iota, 1e9), axis=0, keepdims=True)
        rank = jnp.where(sel, float(r), rank)
        s = jnp.where(sel, -jnp.inf, s)
        vals = jnp.where(i16 == r, m, vals)
    taken = jnp.sum(jnp.where(rank < float(PTOPK), 1.0, 0.0), axis=0, keepdims=True)
    return rank, vals, taken


def _pick16(cand, cidx, exact):
    picked = jnp.zeros(cand.shape, F32)
    for _ in range(PTOPK):
        m = jnp.max(cand, axis=0, keepdims=True)
        sel = cand == m
        if exact:
            sel = cidx == jnp.min(jnp.where(sel, cidx, 1e9), axis=0, keepdims=True)
        picked = jnp.where(sel, 1.0, picked)
        cand = jnp.where(sel, -jnp.inf, cand)
    return picked


def _not_16(count):
    return jnp.max(jnp.abs(count - float(PTOPK))) > 0.5


def _route_head(s_t):
    s1 = s_t[0:NKEYS]
    s2 = s_t[NKEYS:2 * NKEYS]
    iota = _iota2((NKEYS, LANES), 0).astype(F32)
    rank1, v1, t1 = _top16(s1, iota, False)
    rank2, v2, t2 = _top16(s2, iota, False)
    rank1, v1, rank2, v2 = lax.cond(
        _not_16(t1) | _not_16(t2),
        lambda: _top16(s1, iota, True)[0:2] + _top16(s2, iota, True)[0:2],
        lambda: (rank1, v1, rank2, v2))
    e1 = jnp.exp(v1 - v1[0:1])
    e2 = jnp.exp(v2 - v2[0:1])
    i8 = _iota2((8, LANES), 0)
    i8f = i8.astype(F32)
    i16f = _iota2((16, LANES), 0).astype(F32)
    cand, cidx, cprob = [], [], []
    for a, rows, valid in _CAND_BLOCKS:
        c = v1[a:a + 1] + v2[0:rows]
        pr = e1[a:a + 1] * e2[0:rows]
        if rows == 16:
            ix = i16f
        else:
            ix = i8f + float(a * PTOPK)
            c = jnp.where(i8 < valid, c, -jnp.inf)
        cand.append(c)
        cidx.append(ix)
        cprob.append(pr)
    cand.append(v1[8:16] + v2[0:1])
    cidx.append((i8f + 8.0) * float(PTOPK))
    cprob.append(e1[8:16] * e2[0:1])
    cand = jnp.concatenate(cand, axis=0)
    cidx = jnp.concatenate(cidx, axis=0)
    cprob = jnp.concatenate(cprob, axis=0)
    picked = _pick16(cand, cidx, False)
    picked = lax.cond(_not_16(jnp.sum(picked, axis=0, keepdims=True)),
                      lambda: _pick16(cand, cidx, True), lambda: picked)
    z = jnp.sum(picked * cprob, axis=0, keepdims=True)
    n_low = jnp.zeros((8, LANES), F32)
    off = 0
    for a, rows, _ in _CAND_BLOCKS:
        cnt = jnp.sum(picked[off:off + rows], axis=0, keepdims=True)
        n_low = jnp.where(i8 == a, cnt, n_low)
        off += rows
    n16 = jnp.concatenate([n_low, picked[off:off + 8]], axis=0)
    n1d = jnp.zeros((NKEYS, LANES), F32)
    for a in range(PTOPK):
        n1d = jnp.where(rank1 == float(a), n16[a:a + 1], n1d)
    e1d = jnp.exp(s1 - v1[0:1]) / z
    e2d = jnp.exp(s2 - v2[0:1])
    return rank2, e2d, n1d, e1d


def _peer_kernel(x_ref, wq_ref, keys_ref, u_ref, vt_ref, g_ref, b_ref, o_ref,
                 xb_ref, q_ref, r2_ref, e2_ref, n1_ref, e1_ref, at_ref, acc_ref, *, tm, te):
    j = pl.program_id(1)
    nlg = tm // LANES
    nsub = te // NKEYS

    @pl.when(j == 0)
    def _route():
        xb = x_ref[...].astype(BF16)
        xb_ref[...] = xb
        q = jnp.dot(xb, wq_ref[...], preferred_element_type=F32)
        for h in range(PH):
            for g in range(nlg):
                q_ref[h, g] = q[g * LANES:(g + 1) * LANES, h * LANES:(h + 1) * LANES].astype(BF16)
        acc_ref[...] = jnp.zeros(acc_ref.shape, F32)

        def body(it, carry):
            h = it // nlg
            g = it % nlg
            s_t = lax.dot_general(keys_ref[...], q_ref[h, g], NT, preferred_element_type=F32)
            r2, e2, n1, e1 = _route_head(s_t)
            r2_ref[h, g] = r2.astype(BF16)
            e2_ref[h, g] = e2.astype(BF16)
            n1_ref[h, g] = n1
            e1_ref[h, g] = e1
            return carry

        lax.fori_loop(0, PH * nlg, body, 0)

    h_t = lax.dot_general(u_ref[...], xb_ref[...], NT, preferred_element_type=F32)
    zero = jnp.zeros((), BF16)
    pair = 2
    for g in range(nlg):
        for pp in range(nsub // pair):
            gates = [jnp.zeros((NKEYS, LANES), BF16) for _ in range(pair)]
            for h in range(PH):
                r2 = r2_ref[h, g]
                e2 = e2_ref[h, g]
                for ii in range(pair):
                    i1 = j * nsub + pp * pair + ii
                    n_row = n1_ref[h, g, pl.ds(i1, 1), :].astype(BF16)
                    e_row = e1_ref[h, g, pl.ds(i1, 1), :].astype(BF16)
                    gates[ii] = gates[ii] + jnp.where(r2 < n_row, e2, zero) * e_row
            for ii in range(pair):
                r0 = (pp * pair + ii) * NKEYS
                hb = h_t[r0:r0 + NKEYS, g * LANES:(g + 1) * LANES]
                act = 0.5 * hb * (1.0 + lax.erf(hb * (2.0 ** -0.5)))
                at_ref[r0:r0 + NKEYS, g * LANES:(g + 1) * LANES] = act.astype(BF16) * gates[ii]
    acc_ref[...] += jnp.dot(vt_ref[...], at_ref[...], preferred_element_type=F32)

    @pl.when(j == pl.num_programs(1) - 1)
    def _fin():
        o_ref[...] = _layer_norm(ALPHA * x_ref[...] + acc_ref[...].T, g_ref[...], b_ref[...])


def _peer(x, wq, keys, u, vt, g, b, tm, te):
    t = x.shape[0]
    nlg = tm // LANES
    fixed = lambda i, j: (0, 0)
    return pl.pallas_call(
        functools.partial(_peer_kernel, tm=tm, te=te),
        grid=(t // tm, NEXP // te),
        in_specs=[pl.BlockSpec((tm, D), lambda i, j: (i, 0)),
                  pl.BlockSpec((D, PH * LANES), fixed),
                  pl.BlockSpec((2 * NKEYS, LANES), fixed),
                  pl.BlockSpec((te, D), lambda i, j: (j, 0)),
                  pl.BlockSpec((D, te), lambda i, j: (0, j)),
                  pl.BlockSpec((1, D), fixed), pl.BlockSpec((1, D), fixed)],
        out_specs=pl.BlockSpec((tm, D), lambda i, j: (i, 0)),
        out_shape=jax.ShapeDtypeStruct((t, D), F32),
        scratch_shapes=[pltpu.VMEM((tm, D), BF16),
                        pltpu.VMEM((PH, nlg, LANES, LANES), BF16),
                        pltpu.VMEM((PH, nlg, NKEYS, LANES), BF16),
                        pltpu.VMEM((PH, nlg, NKEYS, LANES), BF16),
                        pltpu.VMEM((PH, nlg, NKEYS, LANES), F32),
                        pltpu.VMEM((PH, nlg, NKEYS, LANES), F32),
                        pltpu.VMEM((te, tm), BF16),
                        pltpu.VMEM((D, tm), F32)],
        compiler_params=_cp(("parallel", "arbitrary")),
        name="peer",
    )(x, wq, keys, u, vt, g, b)


def _rope_tables(pos):
    half = RDK // 2
    inv = ROPE_BASE ** (-jnp.arange(half, dtype=F32) / half)
    ang = pos.astype(F32)[:, None] * inv[None, :]
    cos, sin = jnp.cos(ang), jnp.sin(ang)
    return jnp.concatenate([cos, cos], -1), jnp.concatenate([-sin, sin], -1)


def _pad_lanes(v, width=LANES, offset=0):
    return jnp.zeros((1, width), F32).at[0, offset:offset + v.shape[0]].set(v.astype(F32))


def _peer_params(wq, sub_keys, u_tab, v_tab):
    half = sub_keys.shape[-1]
    keys = jnp.zeros((2 * NKEYS, LANES), F32)
    keys = keys.at[0:NKEYS, 0:half].set(sub_keys[0]).at[NKEYS:, half:2 * half].set(sub_keys[1])
    return wq.astype(BF16), keys.astype(BF16), u_tab.astype(BF16), v_tab.T.astype(BF16)


def _tile(t, pref):
    return pref if t % pref == 0 else LANES


def kernel(x_prompt, x_sample, state_ret, state_gdn, state_gdn_conv, cache_fox_k, cache_fox_v, cache_fox_logf,
           page_table, w_in_a, ret_norm_g, gdn_a_log, gdn_dt_bias, gdn_conv_w, gdn_norm_g, w_out_a, w_in_c,
           fox_b_f, w_out_c, peer_wq, peer_sub_keys, peer_u, peer_v, ln_g, ln_b):
    bp, lp, _ = x_prompt.shape
    bs = x_sample.shape[0]
    tp = bp * lp
    xp = x_prompt.reshape(tp, D)
    xs = x_sample.reshape(bs, D)
    tmp, tms = _tile(tp, 512), _tile(bs, 512)
    te = 1024
    ln = lambda layer, k: (ln_g[layer, k].reshape(1, D), ln_b[layer, k].reshape(1, D))

    w_in = jnp.pad(w_in_a[0], ((0, 0), (0, A_COLS_PAD - A_COLS))).astype(BF16)
    cos_p, sin_p = _rope_tables(jnp.arange(lp))
    cos_s, sin_s = _rope_tables(PAST + jnp.arange(1))
    alog, dtb = _pad_lanes(gdn_a_log[0]), _pad_lanes(gdn_dt_bias[0])
    rg, gg = ret_norm_g[0].reshape(1, RDV), gdn_norm_g[0].reshape(1, GDV)
    proj_p = _proj(xp, w_in, tmp).reshape(bp, lp, A_COLS_PAD)
    proj_s = _proj(xs, w_in, tms).reshape(bs, 1, A_COLS_PAD)
    mix_p, ret_p, gdn_p, conv_p = _mixer_a_prompt(proj_p, cos_p, sin_p, gdn_conv_w[0], alog, dtb, rg, gg)
    mix_s, ret_s, gdn_s, conv_s = _mixer_a_sample(proj_s, cos_s, sin_s, gdn_conv_w[0], alog, dtb, rg, gg,
                                                  state_ret[0], state_gdn[0], state_gdn_conv[0])
    w_out = w_out_a[0].astype(BF16)
    g0, b0 = ln(0, 0)
    xp = _outproj_ln(mix_p.reshape(tp, D), xp, w_out, g0, b0, tmp)
    xs = _outproj_ln(mix_s.reshape(bs, D), xs, w_out, g0, b0, tms)
    pw = _peer_params(peer_wq[0], peer_sub_keys[0], peer_u[0], peer_v[0])
    g1, b1 = ln(0, 1)
    xp = _peer(xp, *pw, g1, b1, tmp, te)
    xs = _peer(xs, *pw, g1, b1, tms, te)

    w_c = jnp.pad(w_in_c[0], ((0, 0), (0, C_COLS_PAD - w_in_c.shape[-1]))).astype(BF16)
    bf_row = _pad_lanes(fox_b_f[0])
    kp, vp, lfp, lftp, qbp, kbp, vbp = _fox_proj(xp, w_c, bf_row, tmp, True)
    ks, vs, lfs, _, qs = _fox_proj(xs, w_c, bf_row, tms, False)
    tq, tk = _tile(lp, 1024), _tile(lp, 512)
    cum = _fox_cumsum(lftp, bp, lp).reshape(bp * FH, lp // tk, tk)
    op = _fox_flash(qbp.reshape(bp, lp, D), kbp.reshape(bp, lp, D), vbp.reshape(bp, lp, D), cum, tq, tk)
    os_ = _fox_sample(page_table, qs.reshape(bs, FH, FHD), ks.reshape(bs, FH, FHD), vs.reshape(bs, FH, FHD),
                      lfs.reshape(bs, FH, 1), cache_fox_k[0], cache_fox_v[0], cache_fox_logf[0])
    w_oc = w_out_c[0].astype(BF16)
    g2, b2 = ln(1, 0)
    xp = _outproj_ln(op.reshape(tp, D), xp, w_oc, g2, b2, tmp)
    xs = _outproj_ln(os_.reshape(bs, D), xs, w_oc, g2, b2, tms)
    pw = _peer_params(peer_wq[1], peer_sub_keys[1], peer_u[1], peer_v[1])
    g3, b3 = ln(1, 1)
    xp = _peer(xp, *pw, g3, b3, tmp, te)
    xs = _peer(xs, *pw, g3, b3, tms, te)

    return (xp.reshape(bp, lp, D), xs.reshape(bs, 1, D),
            ret_p[None], ret_s[None], gdn_p[None], gdn_s[None], conv_p[None], conv_s[None],
            kp.reshape(1, bp, lp, FH, FHD), ks.reshape(1, bs, 1, FH, FHD),
            vp.reshape(1, bp, lp, FH, FHD), vs.reshape(1, bs, 1, FH, FHD),
            lfp.reshape(1, bp, lp, FH), lfs.reshape(1, bs, 1, FH))
```

```python
import functools
import math

import jax
import jax.numpy as jnp
from jax import lax
from jax.experimental import pallas as pl
from jax.experimental.pallas import tpu as pltpu

F32 = jnp.float32
BF16 = jnp.bfloat16

D = 1024
PAST = 2048
PAGE = 128
RH, RDK, RDV = 4, 128, 128
GH, GDK, GDV = 4, 128, 128
CONV_W = 4
CONV_CH = 2 * GH * GDK + GH * GDV
A_COLS = 4104
A_COLS_PAD = 4224
FH, FHD = 8, 128
C_COLS_PAD = 3200
PH, PTOPK, NKEYS = 8, 16, 128
NEXP = NKEYS * NKEYS
ALPHA = 4.0 ** 0.25
LN_EPS = 1e-5
NORM_EPS = 1e-6
ROPE_BASE = 10000.0
LANES = 128
CHUNK = 128
PAGES_PER_STEP = 4
VMEM_LIMIT = 56 * 1024 * 1024

NT = (((1,), (1,)), ((), ()))
TN = (((0,), (0,)), ((), ()))


def _cp(sem):
    return pltpu.CompilerParams(dimension_semantics=sem, vmem_limit_bytes=VMEM_LIMIT)


def _mm(a, b):
    return jnp.dot(a.astype(BF16), b.astype(BF16), preferred_element_type=F32)


def _mm_nt(a, b):
    return lax.dot_general(a.astype(BF16), b.astype(BF16), NT, preferred_element_type=F32)


def _mm_tn(a, b):
    return lax.dot_general(a.astype(BF16), b.astype(BF16), TN, preferred_element_type=F32)


def _split3(x):
    x1 = x.astype(BF16)
    r1 = x - x1.astype(F32)
    x2 = r1.astype(BF16)
    return x1, x2, (r1 - x2.astype(F32)).astype(BF16)


def _mm_01_left(m01, x):
    return sum(jnp.dot(m01, t, preferred_element_type=F32) for t in _split3(x))


def _mm_01_right(x, m01):
    return sum(jnp.dot(t, m01, preferred_element_type=F32) for t in _split3(x))


def _split2(a):
    hi = a.astype(BF16)
    return hi, (a - hi.astype(F32)).astype(BF16)


def _mm_s(a, b):
    return (jnp.dot(a[0], b[0], preferred_element_type=F32) + jnp.dot(a[0], b[1], preferred_element_type=F32)
            + jnp.dot(a[1], b[0], preferred_element_type=F32))


def _sigmoid(x):
    return 1.0 / (1.0 + jnp.exp(-x))


def _silu(x):
    return x * _sigmoid(x)


def _softplus(x):
    return jnp.maximum(x, 0.0) + jnp.log1p(jnp.exp(-jnp.abs(x)))


def _log_sigmoid(x):
    return jnp.minimum(x, 0.0) - jnp.log1p(jnp.exp(-jnp.abs(x)))


def _layer_norm(z, g, b):
    mu = jnp.mean(z, axis=-1, keepdims=True)
    d = z - mu
    var = jnp.mean(d * d, axis=-1, keepdims=True)
    return d * lax.rsqrt(var + LN_EPS) * g + b


def _iota2(shape, axis):
    return lax.broadcasted_iota(jnp.int32, shape, axis)


def _proj_kernel(x_ref, w_ref, o_ref):
    o_ref[...] = jnp.dot(x_ref[...].astype(BF16), w_ref[...], preferred_element_type=F32)


def _proj(x, w, tm):
    t, k = x.shape
    n = w.shape[1]
    return pl.pallas_call(
        _proj_kernel,
        grid=(t // tm,),
        in_specs=[pl.BlockSpec((tm, k), lambda i: (i, 0)), pl.BlockSpec((k, n), lambda i: (0, 0))],
        out_specs=pl.BlockSpec((tm, n), lambda i: (i, 0)),
        out_shape=jax.ShapeDtypeStruct((t, n), F32),
        compiler_params=_cp(("parallel",)),
        name="proj_a",
    )(x, w)


def _fox_proj_kernel(x_ref, w_ref, bf_ref, k_ref, v_ref, lf_ref, lft_ref, *q_refs, attn_copies):
    p = jnp.dot(x_ref[...].astype(BF16), w_ref[...], preferred_element_type=F32)
    q = p[:, 0:D]
    k = p[:, D:2 * D]
    v = p[:, 2 * D:3 * D]
    k_ref[...] = k
    v_ref[...] = v
    lf = _log_sigmoid(p[:, 3 * D:3 * D + LANES] + bf_ref[...])
    lf_ref[...] = lf[:, 0:FH]
    lft_ref[...] = lf.T[0:FH, :]
    if attn_copies:
        qb_ref, kb_ref, vb_ref = q_refs
        qb_ref[...] = (q * (FHD ** -0.5)).astype(BF16)
        kb_ref[...] = k.astype(BF16)
        vb_ref[...] = v.astype(BF16)
    else:
        q_refs[0][...] = q


def _fox_proj(x, w, bf_row, tm, attn_copies):
    t = x.shape[0]
    row = lambda i: (i, 0)
    out_specs = [pl.BlockSpec((tm, D), row), pl.BlockSpec((tm, D), row),
                 pl.BlockSpec((tm, FH), row), pl.BlockSpec((FH, tm), lambda i: (0, i))]
    out_shape = [jax.ShapeDtypeStruct((t, D), F32), jax.ShapeDtypeStruct((t, D), F32),
                 jax.ShapeDtypeStruct((t, FH), F32), jax.ShapeDtypeStruct((FH, t), F32)]
    n_q = 3 if attn_copies else 1
    out_specs += [pl.BlockSpec((tm, D), row)] * n_q
    out_shape += [jax.ShapeDtypeStruct((t, D), BF16 if attn_copies else F32)] * n_q
    return pl.pallas_call(
        functools.partial(_fox_proj_kernel, attn_copies=attn_copies),
        grid=(t // tm,),
        in_specs=[pl.BlockSpec((tm, D), row), pl.BlockSpec((D, C_COLS_PAD), lambda i: (0, 0)),
                  pl.BlockSpec((1, LANES), lambda i: (0, 0))],
        out_specs=out_specs,
        out_shape=out_shape,
        compiler_params=_cp(("parallel",)),
        name="proj_c",
    )(x, w, bf_row)


def _outproj_ln_kernel(a_ref, x_ref, w_ref, g_ref, b_ref, o_ref):
    h = jnp.dot(a_ref[...].astype(BF16), w_ref[...], preferred_element_type=F32)
    o_ref[...] = _layer_norm(ALPHA * x_ref[...] + h, g_ref[...], b_ref[...])


def _outproj_ln(a, x, w, g, b, tm):
    t = x.shape[0]
    row = lambda i: (i, 0)
    fixed = lambda i: (0, 0)
    return pl.pallas_call(
        _outproj_ln_kernel,
        grid=(t // tm,),
        in_specs=[pl.BlockSpec((tm, D), row), pl.BlockSpec((tm, D), row), pl.BlockSpec((D, D), fixed),
                  pl.BlockSpec((1, D), fixed), pl.BlockSpec((1, D), fixed)],
        out_specs=pl.BlockSpec((tm, D), row),
        out_shape=jax.ShapeDtypeStruct((t, D), F32),
        compiler_params=_cp(("parallel",)),
        name="outproj_ln",
    )(a, x, w, g, b)


def _unit_lower_inverses(a_list, ri, ci):
    eye = (ri == ci).astype(F32)
    blk = (ri >> 4) == (ci >> 4)
    d = [_split2(jnp.where(blk, a, 0.0)) for a in a_list]
    d2f = [_mm_s(x, x) for x in d]
    d2 = [_split2(x) for x in d2f]
    d4f = [_mm_s(x, x) for x in d2]
    d4 = [_split2(x) for x in d4f]
    d8 = [_split2(_mm_s(x, x)) for x in d4]
    xs = [eye - jnp.where(blk, a, 0.0) for a in a_list]
    for pw in (d2, d4, d8):
        xs = [x + _mm_s(_split2(x), p) for x, p in zip(xs, pw)]
    for s in (5, 6, 7):
        msk = ((ri >> s) == (ci >> s)) & ((ri >> (s - 1)) != (ci >> (s - 1)))
        es = [_split2(jnp.where(msk, a, 0.0)) for a in a_list]
        xsp = [_split2(x) for x in xs]
        xe = [_split2(_mm_s(x, e)) for x, e in zip(xsp, es)]
        xs = [x - _mm_s(y, xp) for x, y, xp in zip(xs, xe, xsp)]
    return xs


def _head_gate_norm(out, g_row, gate):
    return out * lax.rsqrt(jnp.mean(out * out, axis=-1, keepdims=True) + NORM_EPS) * g_row * _silu(gate)


def _mixer_a_prompt_kernel(p_ref, cos_ref, sin_ref, cw_ref, alog_ref, dt_ref, rg_ref, gg_ref,
                           mix_ref, ret_ref, gdn_ref, conv_ref, ext_ref, *, nb):
    n = pl.program_id(0)
    c = CHUNK

    @pl.when(n == 0)
    def _init():
        ret_ref[...] = jnp.zeros_like(ret_ref)
        gdn_ref[...] = jnp.zeros_like(gdn_ref)
        ext_ref[:, 0:8, :] = jnp.zeros((nb, 8, CONV_CH), F32)

    ri = _iota2((c, c), 0)
    ci = _iota2((c, c), 1)
    rf = ri.astype(F32)
    diff = rf - ci.astype(F32)
    cosv = cos_ref[...]
    sinv = sin_ref[...]

    for h in range(RH):
        lg = math.log1p(-(2.0 ** (-5.0 - h)))
        decay = jnp.where(diff >= 0, jnp.exp(jnp.maximum(diff, 0.0) * lg), 0.0)
        inner = jnp.exp((rf + 1.0) * lg)
        kdec = jnp.exp((c - 1.0 - rf) * lg)
        for b in range(nb):
            rq = p_ref[b, :, h * RDK:(h + 1) * RDK]
            rk = p_ref[b, :, RH * RDK + h * RDK:RH * RDK + (h + 1) * RDK]
            rv = p_ref[b, :, 2 * RH * RDK + h * RDV:2 * RH * RDK + (h + 1) * RDV]
            rgate = p_ref[b, :, 2 * RH * RDK + RH * RDV + h * RDV:2 * RH * RDK + RH * RDV + (h + 1) * RDV]
            q = rq * cosv + pltpu.roll(rq, RDK // 2, 1) * sinv
            k = (rk * cosv + pltpu.roll(rk, RDK // 2, 1) * sinv) * (RDK ** -0.5)
            s = ret_ref[b, h]
            scores = _mm_nt(q, k) * decay
            out = _mm(scores, rv) + _mm(q * inner, s)
            ret_ref[b, h] = math.exp(c * lg) * s + _mm_tn(k * kdec, rv)
            mix_ref[b, :, h * RDV:(h + 1) * RDV] = _head_gate_norm(out, rg_ref[...], rgate)

    g_off = 2 * RH * RDK + 2 * RH * RDV
    z_off = g_off + CONV_CH
    incl = ri >= ci
    strict = ri > ci
    tril = incl.astype(BF16)
    units = []
    for b in range(nb):
        ext_ref[b, 8:8 + c, :] = p_ref[b, :, g_off:g_off + CONV_CH]
        y = ext_ref[b, 5:5 + c, :] * cw_ref[0:1, :]
        for i in range(1, CONV_W):
            y = y + ext_ref[b, 5 + i:5 + i + c, :] * cw_ref[i:i + 1, :]
        conv_ref[b] = ext_ref[b, c + 5:c + 8, :]
        ext_ref[b, 0:8, :] = ext_ref[b, c:c + 8, :]
        y = _silu(y)
        tail = p_ref[b, :, z_off + GH * GDV:z_off + GH * GDV + LANES]
        g_all = -jnp.exp(alog_ref[...]) * _softplus(tail + dt_ref[...])
        beta_all = _sigmoid(tail)
        for h in range(GH):
            qh = y[:, h * GDK:(h + 1) * GDK]
            kh = y[:, GH * GDK + h * GDK:GH * GDK + (h + 1) * GDK]
            vh = y[:, 2 * GH * GDK + h * GDV:2 * GH * GDK + (h + 1) * GDV]
            qh = qh * lax.rsqrt(jnp.sum(qh * qh, axis=-1, keepdims=True) + NORM_EPS) * (GDK ** -0.5)
            kh = kh * lax.rsqrt(jnp.sum(kh * kh, axis=-1, keepdims=True) + NORM_EPS)
            gb = jnp.broadcast_to(g_all[:, h:h + 1], (c, c))
            bcol = beta_all[:, GH + h:GH + h + 1]
            units.append((b, h, qh, kh, vh, gb, bcol))

    gcs = [_mm_01_left(tril, u[5]) for u in units]
    decs = []
    for gc in gcs:
        dmat = gc - gc.T
        decs.append(jnp.where(incl, jnp.exp(jnp.where(incl, dmat, 0.0)), 0.0))
    kbs = [u[3] * u[6] for u in units]
    a_list = [jnp.where(strict, _mm_nt(kb, u[3]) * dec, 0.0) for kb, u, dec in zip(kbs, units, decs)]
    ts = [t.astype(BF16) for t in _unit_lower_inverses(a_list, ri, ci)]
    egcs = [jnp.exp(gc) for gc in gcs]
    us = [jnp.dot(t, (u[4] * u[6]).astype(BF16), preferred_element_type=F32) for t, u in zip(ts, units)]
    ws = [jnp.dot(t, (kb * egc).astype(BF16), preferred_element_type=F32) for t, kb, egc in zip(ts, kbs, egcs)]
    attns = [_mm_nt(u[2], u[3]) * dec for u, dec in zip(units, decs)]
    ss = [gdn_ref[u[0], u[1]] for u in units]
    v_news = [uu - _mm(w, s) for uu, w, s in zip(us, ws, ss)]
    outs = [_mm(u[2] * egc, s) + _mm(attn, vn) for u, egc, s, attn, vn in zip(units, egcs, ss, attns, v_news)]
    for u, gc, s, vn, out in zip(units, gcs, ss, v_news, outs):
        b, h = u[0], u[1]
        g_last = gc[c - 1:c, :]
        gdn_ref[b, h] = jnp.exp(g_last) * s + _mm_tn(u[3] * jnp.exp(g_last - gc), vn)
        gz = p_ref[b, :, z_off + h * GDV:z_off + (h + 1) * GDV]
        mix_ref[b, :, RH * RDV + h * GDV:RH * RDV + (h + 1) * GDV] = _head_gate_norm(out, gg_ref[...], gz)


def _mixer_a_prompt(proj, cos2, sin2, conv_w, alog_row, dt_row, rg_row, gg_row):
    b, l, _ = proj.shape
    fixed = lambda n: (0, 0)
    return pl.pallas_call(
        functools.partial(_mixer_a_prompt_kernel, nb=b),
        grid=(l // CHUNK,),
        in_specs=[pl.BlockSpec((b, CHUNK, A_COLS_PAD), lambda n: (0, n, 0)),
                  pl.BlockSpec((CHUNK, LANES), lambda n: (n, 0)),
                  pl.BlockSpec((CHUNK, LANES), lambda n: (n, 0)),
                  pl.BlockSpec((CONV_W, CONV_CH), fixed),
                  pl.BlockSpec((1, LANES), fixed), pl.BlockSpec((1, LANES), fixed),
                  pl.BlockSpec((1, LANES), fixed), pl.BlockSpec((1, LANES), fixed)],
        out_specs=[pl.BlockSpec((b, CHUNK, D), lambda n: (0, n, 0)),
                   pl.BlockSpec((b, RH, RDK, RDV), lambda n: (0, 0, 0, 0)),
                   pl.BlockSpec((b, GH, GDK, GDV), lambda n: (0, 0, 0, 0)),
                   pl.BlockSpec((b, CONV_W - 1, CONV_CH), lambda n: (0, 0, 0))],
        out_shape=[jax.ShapeDtypeStruct((b, l, D), F32),
                   jax.ShapeDtypeStruct((b, RH, RDK, RDV), F32),
                   jax.ShapeDtypeStruct((b, GH, GDK, GDV), F32),
                   jax.ShapeDtypeStruct((b, CONV_W - 1, CONV_CH), F32)],
        scratch_shapes=[pltpu.VMEM((b, CHUNK + 8, CONV_CH), F32)],
        compiler_params=_cp(("arbitrary",)),
        name="mixer_a_prompt",
    )(proj, cos2, sin2, conv_w, alog_row, dt_row, rg_row, gg_row)


def _mixer_a_sample_kernel(p_ref, cos_ref, sin_ref, cw_ref, alog_ref, dt_ref, rg_ref, gg_ref,
                           rs_ref, gs_ref, cs_ref, mix_ref, ret_ref, gdn_ref, conv_ref):
    eye = _iota2((LANES, LANES), 0) == _iota2((LANES, LANES), 1)

    def col(v):
        return jnp.sum(jnp.where(eye, jnp.broadcast_to(v, (LANES, LANES)), 0.0), axis=1, keepdims=True)

    def vec_mat(c, s):
        return jnp.sum(c * s, axis=0, keepdims=True)

    cosv = cos_ref[...]
    sinv = sin_ref[...]
    for h in range(RH):
        gamma = 1.0 - 2.0 ** (-5.0 - h)
        rq = p_ref[0, :, h * RDK:(h + 1) * RDK]
        rk = p_ref[0, :, RH * RDK + h * RDK:RH * RDK + (h + 1) * RDK]
        rv = p_ref[0, :, 2 * RH * RDK + h * RDV:2 * RH * RDK + (h + 1) * RDV]
        rgate = p_ref[0, :, 2 * RH * RDK + RH * RDV + h * RDV:2 * RH * RDK + RH * RDV + (h + 1) * RDV]
        q = rq * cosv + pltpu.roll(rq, RDK // 2, 1) * sinv
        k = (rk * cosv + pltpu.roll(rk, RDK // 2, 1) * sinv) * (RDK ** -0.5)
        s = rs_ref[0, h]
        qk = jnp.sum(q * k, axis=-1, keepdims=True)
        out = qk * rv + gamma * vec_mat(col(q), s)
        ret_ref[0, h] = gamma * s + col(k) * rv
        mix_ref[0, :, h * RDV:(h + 1) * RDV] = _head_gate_norm(out, rg_ref[...], rgate)

    g_off = 2 * RH * RDK + 2 * RH * RDV
    gq_all = p_ref[0, :, g_off:g_off + CONV_CH]
    cb = cs_ref[0]
    y = cb[0:1, :] * cw_ref[0:1, :] + cb[1:2, :] * cw_ref[1:2, :] + cb[2:3, :] * cw_ref[2:3, :] \
        + gq_all * cw_ref[3:4, :]
    conv_ref[0, 0:2, :] = cb[1:3, :]
    conv_ref[0, 2:3, :] = gq_all
    y = _silu(y)
    z_off = g_off + CONV_CH
    tail = p_ref[0, :, z_off + GH * GDV:z_off + GH * GDV + LANES]
    g_all = -jnp.exp(alog_ref[...]) * _softplus(tail + dt_ref[...])
    beta_all = _sigmoid(tail)
    for h in range(GH):
        qh = y[:, h * GDK:(h + 1) * GDK]
        kh = y[:, GH * GDK + h * GDK:GH * GDK + (h + 1) * GDK]
        vh = y[:, 2 * GH * GDK + h * GDV:2 * GH * GDK + (h + 1) * GDV]
        qh = qh * lax.rsqrt(jnp.sum(qh * qh, axis=-1, keepdims=True) + NORM_EPS) * (GDK ** -0.5)
        kh = kh * lax.rsqrt(jnp.sum(kh * kh, axis=-1, keepdims=True) + NORM_EPS)
        eg = jnp.exp(g_all[:, h:h + 1])
        beta = beta_all[:, GH + h:GH + h + 1]
        s = gs_ref[0, h]
        kc = col(kh)
        v_new = vh * beta - vec_mat(kc * (beta * eg), s)
        qk = jnp.sum(qh * kh, axis=-1, keepdims=True)
        out = eg * vec_mat(col(qh), s) + qk * v_new
        gdn_ref[0, h] = eg * s + kc * v_new
        gz = p_ref[0, :, z_off + h * GDV:z_off + (h + 1) * GDV]
        mix_ref[0, :, RH * RDV + h * GDV:RH * RDV + (h + 1) * GDV] = _head_gate_norm(out, gg_ref[...], gz)


def _mixer_a_sample(proj, cos2, sin2, conv_w, alog_row, dt_row, rg_row, gg_row, ret_s, gdn_s, conv_s):
    b = proj.shape[0]
    fixed = lambda i: (0, 0)
    st = lambda i: (i, 0, 0, 0)
    return pl.pallas_call(
        _mixer_a_sample_kernel,
        grid=(b,),
        in_specs=[pl.BlockSpec((1, 1, A_COLS_PAD), lambda i: (i, 0, 0)),
                  pl.BlockSpec((1, LANES), fixed), pl.BlockSpec((1, LANES), fixed),
                  pl.BlockSpec((CONV_W, CONV_CH), fixed),
                  pl.BlockSpec((1, LANES), fixed), pl.BlockSpec((1, LANES), fixed),
                  pl.BlockSpec((1, LANES), fixed), pl.BlockSpec((1, LANES), fixed),
                  pl.BlockSpec((1, RH, RDK, RDV), st), pl.BlockSpec((1, GH, GDK, GDV), st),
                  pl.BlockSpec((1, CONV_W - 1, CONV_CH), lambda i: (i, 0, 0))],
        out_specs=[pl.BlockSpec((1, 1, D), lambda i: (i, 0, 0)),
                   pl.BlockSpec((1, RH, RDK, RDV), st), pl.BlockSpec((1, GH, GDK, GDV), st),
                   pl.BlockSpec((1, CONV_W - 1, CONV_CH), lambda i: (i, 0, 0))],
        out_shape=[jax.ShapeDtypeStruct((b, 1, D), F32),
                   jax.ShapeDtypeStruct((b, RH, RDK, RDV), F32),
                   jax.ShapeDtypeStruct((b, GH, GDK, GDV), F32),
                   jax.ShapeDtypeStruct((b, CONV_W - 1, CONV_CH), F32)],
        compiler_params=_cp(("parallel",)),
        name="mixer_a_sample",
    )(proj, cos2, sin2, conv_w, alog_row, dt_row, rg_row, gg_row, ret_s, gdn_s, conv_s)


def _fox_cumsum_kernel(x_ref, o_ref):
    n = x_ref.shape[1] // LANES
    upper = (_iota2((LANES, LANES), 0) <= _iota2((LANES, LANES), 1)).astype(BF16)
    carry = jnp.zeros((FH, 1), F32)
    for c in range(n):
        p = _mm_01_right(x_ref[:, c * LANES:(c + 1) * LANES], upper) + carry
        o_ref[0, :, c * LANES:(c + 1) * LANES] = p
        carry = p[:, LANES - 1:LANES]


def _fox_cumsum(lft, b, l):
    return pl.pallas_call(
        _fox_cumsum_kernel,
        grid=(b,),
        in_specs=[pl.BlockSpec((FH, l), lambda i: (0, i))],
        out_specs=pl.BlockSpec((1, FH, l), lambda i: (i, 0, 0)),
        out_shape=jax.ShapeDtypeStruct((b, FH, l), F32),
        compiler_params=_cp(("parallel",)),
        name="fox_cumsum",
    )(lft)


def _fox_flash_kernel(q_ref, k_ref, v_ref, ck_ref, o_ref, *, tq, tk):
    qi = pl.program_id(2)
    ratio = tq // tk
    q = q_ref[0]

    def block(j, carry, masked):
        m_old, l_old, acc = carry
        k = k_ref[0, pl.ds(pl.multiple_of(j * tk, tk), tk), :]
        v = v_ref[0, pl.ds(pl.multiple_of(j * tk, tk), tk), :]
        s = lax.dot_general(q, k, NT, preferred_element_type=F32) - ck_ref[0, pl.ds(j, 1), :]
        if masked:
            qpos = qi * tq + _iota2((tq, tk), 0)
            kpos = j * tk + _iota2((tq, tk), 1)
            s = jnp.where(kpos <= qpos, s, -jnp.inf)
        m_new = jnp.maximum(m_old, jnp.max(s, axis=1, keepdims=True))
        alpha = jnp.exp(m_old - m_new)
        p = jnp.exp(s - m_new)
        l_new = alpha * l_old + jnp.sum(p, axis=1, keepdims=True)
        acc = alpha * acc + jnp.dot(p.astype(BF16), v, preferred_element_type=F32)
        return m_new, l_new, acc

    init = (jnp.full((tq, 1), -jnp.inf, F32), jnp.zeros((tq, 1), F32), jnp.zeros((tq, FHD), F32))
    carry = lax.fori_loop(0, qi * ratio, lambda j, c: block(j, c, False), init)
    for d in range(ratio):
        carry = block(qi * ratio + d, carry, True)
    o_ref[0] = carry[2] / carry[1]


def _fox_flash(qb, kb, vb, cum, tq, tk):
    b, l, _ = qb.shape
    return pl.pallas_call(
        functools.partial(_fox_flash_kernel, tq=tq, tk=tk),
        grid=(b, FH, l // tq),
        in_specs=[pl.BlockSpec((1, tq, FHD), lambda i, h, qi: (i, qi, h)),
                  pl.BlockSpec((1, l, FHD), lambda i, h, qi: (i, 0, h)),
                  pl.BlockSpec((1, l, FHD), lambda i, h, qi: (i, 0, h)),
                  pl.BlockSpec((1, l // tk, tk), lambda i, h, qi: (i * FH + h, 0, 0))],
        out_specs=pl.BlockSpec((1, tq, FHD), lambda i, h, qi: (i, qi, h)),
        out_shape=jax.ShapeDtypeStruct((b, l, D), F32),
        compiler_params=_cp(("parallel", "parallel", "arbitrary")),
        name="fox_flash",
    )(qb, kb, vb, cum)


def _fox_sample_kernel(pt_ref, q_ref, kn_ref, vn_ref, lfn_ref, *rest):
    npp = PAGES_PER_STEP
    k_refs, v_refs, lf_refs = rest[0:npp], rest[npp:2 * npp], rest[2 * npp:3 * npp]
    o_ref, m_ref, l_ref, acc_ref, carry_ref = rest[3 * npp:]
    j = pl.program_id(1)
    scale = FHD ** -0.5
    rows = PAGE * FH

    @pl.when(j == 0)
    def _init():
        m_ref[...] = jnp.full(m_ref.shape, -jnp.inf, F32)
        l_ref[...] = jnp.zeros(l_ref.shape, F32)
        acc_ref[...] = jnp.zeros(acc_ref.shape, F32)
        carry_ref[...] = jnp.zeros(carry_ref.shape, F32)

    q = q_ref[0]
    qb = (q * scale).astype(BF16)
    own = (_iota2((FH, rows), 1) & (FH - 1)) == _iota2((FH, rows), 0)
    nch = rows // LANES
    ci = _iota2((LANES, 2 * LANES), 1)
    pre_tot = ((_iota2((LANES, 2 * LANES), 0) <= ci) | (ci >= LANES)).astype(BF16)
    stacked, s_pages = [], []
    for pg in range(npp):
        kp = k_refs[pg][0].reshape(rows, FHD).astype(BF16)
        s_pages.append(lax.dot_general(qb, kp, NT, preferred_element_type=F32))
        lf = jnp.where(own, jnp.broadcast_to(lf_refs[pg][0], (FH, rows)), 0.0)
        stacked += [lf[:, c * LANES:(c + 1) * LANES] for c in range(nch)]
    pt = _mm_01_right(jnp.concatenate(stacked, axis=0), pre_tot)
    carry = carry_ref[...]
    logits = []
    for pg in range(npp):
        cums = []
        for c in range(nch):
            r0 = (pg * nch + c) * FH
            cums.append(pt[r0:r0 + FH, 0:LANES] + carry)
            carry = carry + pt[r0:r0 + FH, LANES:2 * LANES]
        logits.append(jnp.where(own, s_pages[pg] - jnp.concatenate(cums, axis=1), -jnp.inf))
    carry_ref[...] = carry
    m_old = m_ref[...]
    m_new = m_old
    for lg in logits:
        m_new = jnp.maximum(m_new, jnp.max(lg, axis=1, keepdims=True))
    alpha = jnp.exp(m_old - m_new)
    l_new = alpha * l_ref[...]
    acc = alpha * acc_ref[...]
    for pg in range(npp):
        p = jnp.exp(logits[pg] - m_new)
        l_new = l_new + jnp.sum(p, axis=1, keepdims=True)
        vp = v_refs[pg][0].reshape(rows, FHD).astype(BF16)
        acc = acc + jnp.dot(p.astype(BF16), vp, preferred_element_type=F32)
    m_ref[...] = m_new
    l_ref[...] = l_new
    acc_ref[...] = acc

    @pl.when(j == pl.num_programs(1) - 1)
    def _fin():
        s_new = jnp.sum(q * kn_ref[0], axis=1, keepdims=True) * scale - (carry[:, 0:1] + lfn_ref[0])
        m2 = jnp.maximum(m_new, s_new)
        a2 = jnp.exp(m_new - m2)
        p2 = jnp.exp(s_new - m2)
        o_ref[0] = (a2 * acc + p2 * vn_ref[0]) / (a2 * l_new + p2)


def _fox_sample(page_table, q3, k3, v3, lf3, k_pool, v_pool, lf_pool):
    b = q3.shape[0]
    npg = page_table.shape[1]
    npp = PAGES_PER_STEP
    assert npg % npp == 0
    pt = page_table.reshape(-1)
    tok = lambda i, j, pt_ref: (i, 0, 0)

    def page(r, nd):
        return lambda i, j, pt_ref: (pt_ref[i * npg + j * npp + r],) + (0,) * nd

    grid_spec = pltpu.PrefetchScalarGridSpec(
        num_scalar_prefetch=1,
        grid=(b, npg // npp),
        in_specs=[pl.BlockSpec((1, FH, FHD), tok), pl.BlockSpec((1, FH, FHD), tok), pl.BlockSpec((1, FH, FHD), tok),
                  pl.BlockSpec((1, FH, 1), tok)]
        + [pl.BlockSpec((1, PAGE, FH, FHD), page(r, 3)) for r in range(npp)]
        + [pl.BlockSpec((1, PAGE, FH, FHD), page(r, 3)) for r in range(npp)]
        + [pl.BlockSpec((1, 1, PAGE * FH), page(r, 2)) for r in range(npp)],
        out_specs=pl.BlockSpec((1, FH, FHD), tok),
        scratch_shapes=[pltpu.VMEM((FH, 1), F32), pltpu.VMEM((FH, 1), F32), pltpu.VMEM((FH, FHD), F32),
                        pltpu.VMEM((FH, LANES), F32)],
    )
    lf_flat =lf_pool.reshape(lf_pool.shape[0], 1, PAGE * FH)
    return pl.pallas_call(
        _fox_sample_kernel,
        grid_spec=grid_spec,
        out_shape=jax.ShapeDtypeStruct((b, FH, FHD), F32),
        compiler_params=_cp(("parallel", "arbitrary")),
        name="fox_sample",
    )(pt, q3, k3, v3, lf3, *([k_pool] * npp), *([v_pool] * npp), *([lf_flat] * npp))


_CAND_BLOCKS = [(a, 16 if a == 0 else 8, 16 // (a + 1)) for a in range(8)]


_MARK0 = -(2.0 ** 127)
_MARK_STEP = 2.0 ** 120
_MARK_LIMIT = -(2.0 ** 126)


def _top16(s, iota):
    rank = jnp.full(s.shape, float(PTOPK), F32)
    vals = jnp.zeros((PTOPK, s.shape[1]), F32)
    i16 = _iota2((PTOPK, s.shape[1]), 0)
    for r in range(PTOPK):
        m = jnp.max(s, axis=0, keepdims=True)
        sel = iota == jnp.min(jnp.where(s == m, iota, 1e9), axis=0, keepdims=True)
        rank = jnp.where(sel, float(r), rank)
        s = jnp.where(sel, -jnp.inf, s)
        vals = jnp.where(i16 == r, m, vals)
    return rank, vals


def _top16_fast(s):
    lowest = jnp.min(s)
    vals = jnp.zeros((PTOPK, s.shape[1]), F32)
    i16 = _iota2((PTOPK, s.shape[1]), 0)
    for r in range(PTOPK):
        m = jnp.max(s, axis=0, keepdims=True)
        s = jnp.where(s == m, _MARK0 + r * _MARK_STEP, s)
        vals = jnp.where(i16 == r, m, vals)
    marked = s < _MARK_LIMIT
    rank = jnp.where(marked, (s - _MARK0) * (1.0 / _MARK_STEP), float(PTOPK))
    taken = jnp.sum(jnp.where(marked, 1.0, 0.0), axis=0, keepdims=True)
    return rank, vals, _not_16(taken) | (lowest <= _MARK_LIMIT)


def _pick16(cand, cidx, exact):
    picked = jnp.zeros(cand.shape, F32)
    for _ in range(PTOPK):
        m = jnp.max(cand, axis=0, keepdims=True)
        sel = cand == m
        if exact:
            sel = cidx == jnp.min(jnp.where(sel, cidx, 1e9), axis=0, keepdims=True)
        picked = jnp.where(sel, 1.0, picked)
        cand = jnp.where(sel, -jnp.inf, cand)
    return picked


def _not_16(count):
    return jnp.max(jnp.abs(count - float(PTOPK))) > 0.5


def _route_head(s_t):
    s1 = s_t[0:NKEYS]
    s2 = s_t[NKEYS:2 * NKEYS]
    iota = _iota2((NKEYS, LANES), 0).astype(F32)
    rank1, v1, redo1 = _top16_fast(s1)
    rank2, v2, redo2 = _top16_fast(s2)
    rank1, v1, rank2, v2 = lax.cond(redo1 | redo2, lambda: _top16(s1, iota) + _top16(s2, iota),
                                    lambda: (rank1, v1, rank2, v2))
    e1 = jnp.exp(v1 - v1[0:1])
    e2 = jnp.exp(v2 - v2[0:1])
    i8 = _iota2((8, LANES), 0)
    i8f = i8.astype(F32)
    i16f = _iota2((16, LANES), 0).astype(F32)
    cand, cidx, cprob = [], [], []
    for a, rows, valid in _CAND_BLOCKS:
        c = v1[a:a + 1] + v2[0:rows]
        pr = e1[a:a + 1] * e2[0:rows]
        if rows == 16:
            ix = i16f
        else:
            ix = i8f + float(a * PTOPK)
            c = jnp.where(i8 < valid, c, -jnp.inf)
        cand.append(c)
        cidx.append(ix)
        cprob.append(pr)
    cand.append(v1[8:16] + v2[0:1])
    cidx.append((i8f + 8.0) * float(PTOPK))
    cprob.append(e1[8:16] * e2[0:1])
    cand = jnp.concatenate(cand, axis=0)
    cidx = jnp.concatenate(cidx, axis=0)
    cprob = jnp.concatenate(cprob, axis=0)
    picked = _pick16(cand, cidx, False)
    picked = lax.cond(_not_16(jnp.sum(picked, axis=0, keepdims=True)),
                      lambda: _pick16(cand, cidx, True), lambda: picked)
    z = jnp.sum(picked * cprob, axis=0, keepdims=True)
    n_low = jnp.zeros((8, LANES), F32)
    off = 0
    for a, rows, _ in _CAND_BLOCKS:
        cnt = jnp.sum(picked[off:off + rows], axis=0, keepdims=True)
        n_low = jnp.where(i8 == a, cnt, n_low)
        off += rows
    n16 = jnp.concatenate([n_low, picked[off:off + 8]], axis=0)
    n1d = jnp.zeros((NKEYS, LANES), F32)
    for a in range(PTOPK):
        n1d = jnp.where(rank1 == float(a), n16[a:a + 1], n1d)
    e1d = jnp.exp(s1 - v1[0:1]) / z
    e2d = jnp.exp(s2 - v2[0:1])
    return rank2, e2d, n1d, e1d


def _peer_kernel(x_ref, wq_ref, keys_ref, u_ref, vt_ref, g_ref, b_ref, o_ref,
                 xb_ref, q_ref, r2_ref, e2_ref, n1_ref, e1_ref, at_ref, acc_ref, *, tm, te):
    j = pl.program_id(1)
    nlg = tm // LANES
    nsub = te // NKEYS

    @pl.when(j == 0)
    def _route():
        xb = x_ref[...].astype(BF16)
        xb_ref[...] = xb
        q = jnp.dot(xb, wq_ref[...], preferred_element_type=F32)
        for h in range(PH):
            for g in range(nlg):
                q_ref[h, g] = q[g * LANES:(g + 1) * LANES, h * LANES:(h + 1) * LANES].astype(BF16)
        acc_ref[...] = jnp.zeros(acc_ref.shape, F32)

        def body(it, carry):
            h = it // nlg
            g = it % nlg
            s_t = lax.dot_general(keys_ref[...], q_ref[h, g], NT, preferred_element_type=F32)
            r2, e2, n1, e1 = _route_head(s_t)
            r2_ref[h, g] = r2.astype(BF16)
            e2_ref[h, g] = e2.astype(BF16)
            n1_ref[h, g] = n1
            e1_ref[h, g] = e1
            return carry

        lax.fori_loop(0, PH * nlg, body, 0)

    h_t = lax.dot_general(u_ref[...], xb_ref[...], NT, preferred_element_type=F32)
    zero = jnp.zeros((), BF16)
    pair = 2
    for g in range(nlg):
        for pp in range(nsub // pair):
            gates = [jnp.zeros((NKEYS, LANES), BF16) for _ in range(pair)]
            for h in range(PH):
                r2 = r2_ref[h, g]
                e2 = e2_ref[h, g]
                for ii in range(pair):
                    i1 = j * nsub + pp * pair + ii
                    n_row = n1_ref[h, g, pl.ds(i1, 1), :].astype(BF16)
                    e_row = e1_ref[h, g, pl.ds(i1, 1), :].astype(BF16)
                    gates[ii] = gates[ii] + jnp.where(r2 < n_row, e2, zero) * e_row
            for ii in range(pair):
                r0 = (pp * pair + ii) * NKEYS
                hb = h_t[r0:r0 + NKEYS, g * LANES:(g + 1) * LANES]
                act = 0.5 * hb * (1.0 + lax.erf(hb * (2.0 ** -0.5)))
                at_ref[r0:r0 + NKEYS, g * LANES:(g + 1) * LANES] = act.astype(BF16) * gates[ii]
    acc_ref[...] += jnp.dot(vt_ref[...], at_ref[...], preferred_element_type=F32)

    @pl.when(j == pl.num_programs(1) - 1)
    def _fin():
        o_ref[...] = _layer_norm(ALPHA * x_ref[...] + acc_ref[...].T, g_ref[...], b_ref[...])


def _peer(x, wq, keys, u, vt, g, b, tm, te):
    t = x.shape[0]
    nlg = tm // LANES
    fixed = lambda i, j: (0, 0)
    return pl.pallas_call(
        functools.partial(_peer_kernel, tm=tm, te=te),
        grid=(t // tm, NEXP // te),
        in_specs=[pl.BlockSpec((tm, D), lambda i, j: (i, 0)),
                  pl.BlockSpec((D, PH * LANES), fixed),
                  pl.BlockSpec((2 * NKEYS, LANES), fixed),
                  pl.BlockSpec((te, D), lambda i, j: (j, 0)),
                  pl.BlockSpec((D, te), lambda i, j: (0, j)),
                  pl.BlockSpec((1, D), fixed), pl.BlockSpec((1, D), fixed)],
        out_specs=pl.BlockSpec((tm, D), lambda i, j: (i, 0)),
        out_shape=jax.ShapeDtypeStruct((t, D), F32),
        scratch_shapes=[pltpu.VMEM((tm, D), BF16),
                        pltpu.VMEM((PH, nlg, LANES, LANES), BF16),
                        pltpu.VMEM((PH, nlg, NKEYS, LANES), BF16),
                        pltpu.VMEM((PH, nlg, NKEYS, LANES), BF16),
                        pltpu.VMEM((PH, nlg, NKEYS, LANES), F32),
                        pltpu.VMEM((PH, nlg, NKEYS, LANES), F32),
                        pltpu.VMEM((te, tm), BF16),
                        pltpu.VMEM((D, tm), F32)],
        compiler_params=_cp(("parallel", "arbitrary")),
        name="peer",
    )(x, wq, keys, u, vt, g, b)


def _rope_tables(pos):
    half = RDK // 2
    inv = ROPE_BASE ** (-jnp.arange(half, dtype=F32) / half)
    ang = pos.astype(F32)[:, None] * inv[None, :]
    cos, sin = jnp.cos(ang), jnp.sin(ang)
    return jnp.concatenate([cos, cos], -1), jnp.concatenate([-sin, sin], -1)


def _pad_lanes(v, width=LANES, offset=0):
    return jnp.zeros((1, width), F32).at[0, offset:offset + v.shape[0]].set(v.astype(F32))


def _peer_params(wq, sub_keys, u_tab, v_tab):
    half = sub_keys.shape[-1]
    keys = jnp.zeros((2 * NKEYS, LANES), F32)
    keys = keys.at[0:NKEYS, 0:half].set(sub_keys[0]).at[NKEYS:, half:2 * half].set(sub_keys[1])
    return wq.astype(BF16), keys.astype(BF16), u_tab.astype(BF16), v_tab.T.astype(BF16)


def _tile(t, pref):
    return pref if t % pref == 0 else LANES


def kernel(x_prompt, x_sample, state_ret, state_gdn, state_gdn_conv, cache_fox_k, cache_fox_v, cache_fox_logf,
           page_table, w_in_a, ret_norm_g, gdn_a_log, gdn_dt_bias, gdn_conv_w, gdn_norm_g, w_out_a, w_in_c,
           fox_b_f, w_out_c, peer_wq, peer_sub_keys, peer_u, peer_v, ln_g, ln_b):
    bp, lp, _ = x_prompt.shape
    bs = x_sample.shape[0]
    tp = bp * lp
    xp = x_prompt.reshape(tp, D)
    xs = x_sample.reshape(bs, D)
    tmp, tms = _tile(tp, 512), _tile(bs, 512)
    te = 1024
    ln = lambda layer, k: (ln_g[layer, k].reshape(1, D), ln_b[layer, k].reshape(1, D))

    w_in = jnp.pad(w_in_a[0], ((0, 0), (0, A_COLS_PAD - A_COLS))).astype(BF16)
    cos_p, sin_p = _rope_tables(jnp.arange(lp))
    cos_s, sin_s = _rope_tables(PAST + jnp.arange(1))
    alog, dtb = _pad_lanes(gdn_a_log[0]), _pad_lanes(gdn_dt_bias[0])
    rg, gg = ret_norm_g[0].reshape(1, RDV), gdn_norm_g[0].reshape(1, GDV)
    proj_p = _proj(xp, w_in, tmp).reshape(bp, lp, A_COLS_PAD)
    proj_s = _proj(xs, w_in, tms).reshape(bs, 1, A_COLS_PAD)
    mix_p, ret_p, gdn_p, conv_p = _mixer_a_prompt(proj_p, cos_p, sin_p, gdn_conv_w[0], alog, dtb, rg, gg)
    mix_s, ret_s, gdn_s, conv_s = _mixer_a_sample(proj_s, cos_s, sin_s, gdn_conv_w[0], alog, dtb, rg, gg,
                                                  state_ret[0], state_gdn[0], state_gdn_conv[0])
    w_out = w_out_a[0].astype(BF16)
    g0, b0 = ln(0, 0)
    xp = _outproj_ln(mix_p.reshape(tp, D), xp, w_out, g0, b0, tmp)
    xs = _outproj_ln(mix_s.reshape(bs, D), xs, w_out, g0, b0, tms)
    pw = _peer_params(peer_wq[0], peer_sub_keys[0], peer_u[0], peer_v[0])
    g1, b1 = ln(0, 1)
    xp = _peer(xp, *pw, g1, b1, tmp, te)
    xs = _peer(xs, *pw, g1, b1, tms, te)

    w_c = jnp.pad(w_in_c[0], ((0, 0), (0, C_COLS_PAD - w_in_c.shape[-1]))).astype(BF16)
    bf_row = _pad_lanes(fox_b_f[0])
    kp, vp, lfp, lftp, qbp, kbp, vbp = _fox_proj(xp, w_c, bf_row, tmp, True)
    ks, vs, lfs, _, qs = _fox_proj(xs, w_c, bf_row, tms, False)
    tq, tk = _tile(lp, 1024), _tile(lp, 512)
    cum = _fox_cumsum(lftp, bp, lp).reshape(bp * FH, lp // tk, tk)
    op = _fox_flash(qbp.reshape(bp, lp, D), kbp.reshape(bp, lp, D), vbp.reshape(bp, lp, D), cum, tq, tk)
    os_ = _fox_sample(page_table, qs.reshape(bs, FH, FHD), ks.reshape(bs, FH, FHD), vs.reshape(bs, FH, FHD),
                      lfs.reshape(bs, FH, 1), cache_fox_k[0], cache_fox_v[0], cache_fox_logf[0])
    w_oc = w_out_c[0].astype(BF16)
    g2, b2 = ln(1, 0)
    xp = _outproj_ln(op.reshape(tp, D), xp, w_oc, g2, b2, tmp)
    xs = _outproj_ln(os_.reshape(bs, D), xs, w_oc, g2, b2, tms)
    pw = _peer_params(peer_wq[1], peer_sub_keys[1], peer_u[1], peer_v[1])
    g3, b3 = ln(1, 1)
    xp = _peer(xp, *pw, g3, b3, tmp, te)
    xs = _peer(xs, *pw, g3, b3, tms, te)

    return (xp.reshape(bp, lp, D), xs.reshape(bs, 1, D),
            ret_p[None], ret_s[None], gdn_p[None], gdn_s[None], conv_p[None], conv_s[None],
            kp.reshape(1, bp, lp, FH, FHD), ks.reshape(1, bs, 1, FH, FHD),
            vp.reshape(1, bp, lp, FH, FHD), vs.reshape(1, bs, 1, FH, FHD),
            lfp.reshape(1, bp, lp, FH), lfs.reshape(1, bs, 1, FH))
```

```python
import functools
import math

import jax
import jax.numpy as jnp
from jax import lax
from jax.experimental import pallas as pl
from jax.experimental.pallas import tpu as pltpu

F32 = jnp.float32
BF16 = jnp.bfloat16

D = 1024
PAST = 2048
PAGE = 128
RH, RDK, RDV = 4, 128, 128
GH, GDK, GDV = 4, 128, 128
CONV_W = 4
CONV_CH = 2 * GH * GDK + GH * GDV
A_COLS = 4104
A_COLS_PAD = 4224
FH, FHD = 8, 128
C_COLS_PAD = 3200
PH, PTOPK, NKEYS = 8, 16, 128
NEXP = NKEYS * NKEYS
ALPHA = 4.0 ** 0.25
LN_EPS = 1e-5
NORM_EPS = 1e-6
ROPE_BASE = 10000.0
LANES = 128
CHUNK = 128
PAGES_PER_STEP = 4
VMEM_LIMIT = 56 * 1024 * 1024

NT = (((1,), (1,)), ((), ()))
TN = (((0,), (0,)), ((), ()))


def _cp(sem):
    return pltpu.CompilerParams(dimension_semantics=sem, vmem_limit_bytes=VMEM_LIMIT)


def _mm(a, b):
    return jnp.dot(a.astype(BF16), b.astype(BF16), preferred_element_type=F32)


def _mm_nt(a, b):
    return lax.dot_general(a.astype(BF16), b.astype(BF16), NT, preferred_element_type=F32)


def _mm_tn(a, b):
    return lax.dot_general(a.astype(BF16), b.astype(BF16), TN, preferred_element_type=F32)


def _split3(x):
    x1 = x.astype(BF16)
    r1 = x - x1.astype(F32)
    x2 = r1.astype(BF16)
    return x1, x2, (r1 - x2.astype(F32)).astype(BF16)


def _mm_01_left(m01, x):
    return sum(jnp.dot(m01, t, preferred_element_type=F32) for t in _split3(x))


def _mm_01_right(x, m01):
    return sum(jnp.dot(t, m01, preferred_element_type=F32) for t in _split3(x))


def _split2(a):
    hi = a.astype(BF16)
    return hi, (a - hi.astype(F32)).astype(BF16)


def _mm_s(a, b):
    return (jnp.dot(a[0], b[0], preferred_element_type=F32) + jnp.dot(a[0], b[1], preferred_element_type=F32)
            + jnp.dot(a[1], b[0], preferred_element_type=F32))


def _sigmoid(x):
    return 1.0 / (1.0 + jnp.exp(-x))


def _silu(x):
    return x * _sigmoid(x)


def _softplus(x):
    return jnp.maximum(x, 0.0) + jnp.log1p(jnp.exp(-jnp.abs(x)))


def _log_sigmoid(x):
    return jnp.minimum(x, 0.0) - jnp.log1p(jnp.exp(-jnp.abs(x)))


def _layer_norm(z, g, b):
    mu = jnp.mean(z, axis=-1, keepdims=True)
    d = z - mu
    var = jnp.mean(d * d, axis=-1, keepdims=True)
    return d * lax.rsqrt(var + LN_EPS) * g + b


def _iota2(shape, axis):
    return lax.broadcasted_iota(jnp.int32, shape, axis)


def _proj_kernel(x_ref, w_ref, o_ref):
    o_ref[...] = jnp.dot(x_ref[...].astype(BF16), w_ref[...], preferred_element_type=F32)


def _proj(x, w, tm):
    t, k = x.shape
    n = w.shape[1]
    return pl.pallas_call(
        _proj_kernel,
        grid=(t // tm,),
        in_specs=[pl.BlockSpec((tm, k), lambda i: (i, 0)), pl.BlockSpec((k, n), lambda i: (0, 0))],
        out_specs=pl.BlockSpec((tm, n), lambda i: (i, 0)),
        out_shape=jax.ShapeDtypeStruct((t, n), F32),
        compiler_params=_cp(("parallel",)),
        name="proj_a",
    )(x, w)


def _fox_proj_kernel(x_ref, w_ref, bf_ref, k_ref, v_ref, lf_ref, lft_ref, *q_refs, attn_copies):
    p = jnp.dot(x_ref[...].astype(BF16), w_ref[...], preferred_element_type=F32)
    q = p[:, 0:D]
    k = p[:, D:2 * D]
    v = p[:, 2 * D:3 * D]
    k_ref[...] = k
    v_ref[...] = v
    lf = _log_sigmoid(p[:, 3 * D:3 * D + LANES] + bf_ref[...])
    lf_ref[...] = lf[:, 0:FH]
    lft_ref[...] = lf.T[0:FH, :]
    if attn_copies:
        qb_ref, kb_ref, vb_ref = q_refs
        qb_ref[...] = (q * (FHD ** -0.5)).astype(BF16)
        kb_ref[...] = k.astype(BF16)
        vb_ref[...] = v.astype(BF16)
    else:
        q_refs[0][...] = q


def _fox_proj(x, w, bf_row, tm, attn_copies):
    t = x.shape[0]
    row = lambda i: (i, 0)
    out_specs = [pl.BlockSpec((tm, D), row), pl.BlockSpec((tm, D), row),
                 pl.BlockSpec((tm, FH), row), pl.BlockSpec((FH, tm), lambda i: (0, i))]
    out_shape = [jax.ShapeDtypeStruct((t, D), F32), jax.ShapeDtypeStruct((t, D), F32),
                 jax.ShapeDtypeStruct((t, FH), F32), jax.ShapeDtypeStruct((FH, t), F32)]
    n_q = 3 if attn_copies else 1
    out_specs += [pl.BlockSpec((tm, D), row)] * n_q
    out_shape += [jax.ShapeDtypeStruct((t, D), BF16 if attn_copies else F32)] * n_q
    return pl.pallas_call(
        functools.partial(_fox_proj_kernel, attn_copies=attn_copies),
        grid=(t // tm,),
        in_specs=[pl.BlockSpec((tm, D), row), pl.BlockSpec((D, C_COLS_PAD), lambda i: (0, 0)),
                  pl.BlockSpec((1, LANES), lambda i: (0, 0))],
        out_specs=out_specs,
        out_shape=out_shape,
        compiler_params=_cp(("parallel",)),
        name="proj_c",
    )(x, w, bf_row)


def _outproj_ln_kernel(a_ref, x_ref, w_ref, g_ref, b_ref, o_ref):
    h = jnp.dot(a_ref[...].astype(BF16), w_ref[...], preferred_element_type=F32)
    o_ref[...] = _layer_norm(ALPHA * x_ref[...] + h, g_ref[...], b_ref[...])


def _outproj_ln(a, x, w, g, b, tm):
    t = x.shape[0]
    row = lambda i: (i, 0)
    fixed = lambda i: (0, 0)
    return pl.pallas_call(
        _outproj_ln_kernel,
        grid=(t // tm,),
        in_specs=[pl.BlockSpec((tm, D), row), pl.BlockSpec((tm, D), row), pl.BlockSpec((D, D), fixed),
                  pl.BlockSpec((1, D), fixed), pl.BlockSpec((1, D), fixed)],
        out_specs=pl.BlockSpec((tm, D), row),
        out_shape=jax.ShapeDtypeStruct((t, D), F32),
        compiler_params=_cp(("parallel",)),
        name="outproj_ln",
    )(a, x, w, g, b)


def _unit_lower_inverses(a_list, ri, ci):
    eye = (ri == ci).astype(F32)
    blk = (ri >> 4) == (ci >> 4)
    d = [_split2(jnp.where(blk, a, 0.0)) for a in a_list]
    d2f = [_mm_s(x, x) for x in d]
    d2 = [_split2(x) for x in d2f]
    d4f = [_mm_s(x, x) for x in d2]
    d4 = [_split2(x) for x in d4f]
    d8 = [_split2(_mm_s(x, x)) for x in d4]
    xs = [eye - jnp.where(blk, a, 0.0) for a in a_list]
    for pw in (d2, d4, d8):
        xs = [x + _mm_s(_split2(x), p) for x, p in zip(xs, pw)]
    for s in (5, 6, 7):
        msk = ((ri >> s) == (ci >> s)) & ((ri >> (s - 1)) != (ci >> (s - 1)))
        es = [_split2(jnp.where(msk, a, 0.0)) for a in a_list]
        xsp = [_split2(x) for x in xs]
        xe = [_split2(_mm_s(x, e)) for x, e in zip(xsp, es)]
        xs = [x - _mm_s(y, xp) for x, y, xp in zip(xs, xe, xsp)]
    return xs


def _head_gate_norm(out, g_row, gate):
    return out * lax.rsqrt(jnp.mean(out * out, axis=-1, keepdims=True) + NORM_EPS) * g_row * _silu(gate)


def _mixer_a_prompt_kernel(p_ref, cos_ref, sin_ref, cw_ref, alog_ref, dt_ref, rg_ref, gg_ref,
                           mix_ref, ret_ref, gdn_ref, conv_ref, ext_ref, *, nb):
    n = pl.program_id(0)
    c = CHUNK

    @pl.when(n == 0)
    def _init():
        ret_ref[...] = jnp.zeros_like(ret_ref)
        gdn_ref[...] = jnp.zeros_like(gdn_ref)
        ext_ref[:, 0:8, :] = jnp.zeros((nb, 8, CONV_CH), F32)

    ri = _iota2((c, c), 0)
    ci = _iota2((c, c), 1)
    rf = ri.astype(F32)
    diff = rf - ci.astype(F32)
    cosv = cos_ref[...]
    sinv = sin_ref[...]

    for h in range(RH):
        lg = math.log1p(-(2.0 ** (-5.0 - h)))
        decay = jnp.where(diff >= 0, jnp.exp(jnp.maximum(diff, 0.0) * lg), 0.0)
        inner = jnp.exp((rf + 1.0) * lg)
        kdec = jnp.exp((c - 1.0 - rf) * lg)
        for b in range(nb):
            rq = p_ref[b, :, h * RDK:(h + 1) * RDK]
            rk = p_ref[b, :, RH * RDK + h * RDK:RH * RDK + (h + 1) * RDK]
            rv = p_ref[b, :, 2 * RH * RDK + h * RDV:2 * RH * RDK + (h + 1) * RDV]
            rgate = p_ref[b, :, 2 * RH * RDK + RH * RDV + h * RDV:2 * RH * RDK + RH * RDV + (h + 1) * RDV]
            q = rq * cosv + pltpu.roll(rq, RDK // 2, 1) * sinv
            k = (rk * cosv + pltpu.roll(rk, RDK // 2, 1) * sinv) * (RDK ** -0.5)
            s = ret_ref[b, h]
            scores = _mm_nt(q, k) * decay
            out = _mm(scores, rv) + _mm(q * inner, s)
            ret_ref[b, h] = math.exp(c * lg) * s + _mm_tn(k * kdec, rv)
            mix_ref[b, :, h * RDV:(h + 1) * RDV] = _head_gate_norm(out, rg_ref[...], rgate)

    g_off = 2 * RH * RDK + 2 * RH * RDV
    z_off = g_off + CONV_CH
    incl = ri >= ci
    strict = ri > ci
    tril = incl.astype(BF16)
    units = []
    for b in range(nb):
        ext_ref[b, 8:8 + c, :] = p_ref[b, :, g_off:g_off + CONV_CH]
        y = ext_ref[b, 5:5 + c, :] * cw_ref[0:1, :]
        for i in range(1, CONV_W):
            y = y + ext_ref[b, 5 + i:5 + i + c, :] * cw_ref[i:i + 1, :]
        conv_ref[b] = ext_ref[b, c + 5:c + 8, :]
        ext_ref[b, 0:8, :] = ext_ref[b, c:c + 8, :]
        y = _silu(y)
        tail = p_ref[b, :, z_off + GH * GDV:z_off + GH * GDV + LANES]
        g_all = -jnp.exp(alog_ref[...]) * _softplus(tail + dt_ref[...])
        beta_all = _sigmoid(tail)
        for h in range(GH):
            qh = y[:, h * GDK:(h + 1) * GDK]
            kh = y[:, GH * GDK + h * GDK:GH * GDK + (h + 1) * GDK]
            vh = y[:, 2 * GH * GDK + h * GDV:2 * GH * GDK + (h + 1) * GDV]
            qh = qh * lax.rsqrt(jnp.sum(qh * qh, axis=-1, keepdims=True) + NORM_EPS) * (GDK ** -0.5)
            kh = kh * lax.rsqrt(jnp.sum(kh * kh, axis=-1, keepdims=True) + NORM_EPS)
            gb = jnp.broadcast_to(g_all[:, h:h + 1], (c, c))
            bcol = beta_all[:, GH + h:GH + h + 1]
            units.append((b, h, qh, kh, vh, gb, bcol))

    gcs = [_mm_01_left(tril, u[5]) for u in units]
    decs = []
    for gc in gcs:
        dmat = gc - gc.T
        decs.append(jnp.where(incl, jnp.exp(jnp.where(incl, dmat, 0.0)), 0.0))
    kbs = [u[3] * u[6] for u in units]
    a_list = [jnp.where(strict, _mm_nt(kb, u[3]) * dec, 0.0) for kb, u, dec in zip(kbs, units, decs)]
    ts = [t.astype(BF16) for t in _unit_lower_inverses(a_list, ri, ci)]
    egcs = [jnp.exp(gc) for gc in gcs]
    us = [jnp.dot(t, (u[4] * u[6]).astype(BF16), preferred_element_type=F32) for t, u in zip(ts, units)]
    ws = [jnp.dot(t, (kb * egc).astype(BF16), preferred_element_type=F32) for t, kb, egc in zip(ts, kbs, egcs)]
    attns = [_mm_nt(u[2], u[3]) * dec for u, dec in zip(units, decs)]
    ss = [gdn_ref[u[0], u[1]] for u in units]
    v_news = [uu - _mm(w, s) for uu, w, s in zip(us, ws, ss)]
    outs = [_mm(u[2] * egc, s) + _mm(attn, vn) for u, egc, s, attn, vn in zip(units, egcs, ss, attns, v_news)]
    for u, gc, s, vn, out in zip(units, gcs, ss, v_news, outs):
        b, h = u[0], u[1]
        g_last = gc[c - 1:c, :]
        gdn_ref[b, h] = jnp.exp(g_last) * s + _mm_tn(u[3] * jnp.exp(g_last - gc), vn)
        gz = p_ref[b, :, z_off + h * GDV:z_off + (h + 1) * GDV]
        mix_ref[b, :, RH * RDV + h * GDV:RH * RDV + (h + 1) * GDV] = _head_gate_norm(out, gg_ref[...], gz)


def _mixer_a_prompt(proj, cos2, sin2, conv_w, alog_row, dt_row, rg_row, gg_row):
    b, l, _ = proj.shape
    fixed = lambda n: (0, 0)
    return pl.pallas_call(
        functools.partial(_mixer_a_prompt_kernel, nb=b),
        grid=(l // CHUNK,),
        in_specs=[pl.BlockSpec((b, CHUNK, A_COLS_PAD), lambda n: (0, n, 0)),
                  pl.BlockSpec((CHUNK, LANES), lambda n: (n, 0)),
                  pl.BlockSpec((CHUNK, LANES), lambda n: (n, 0)),
                  pl.BlockSpec((CONV_W, CONV_CH), fixed),
                  pl.BlockSpec((1, LANES), fixed), pl.BlockSpec((1, LANES), fixed),
                  pl.BlockSpec((1, LANES), fixed), pl.BlockSpec((1, LANES), fixed)],
        out_specs=[pl.BlockSpec((b, CHUNK, D), lambda n: (0, n, 0)),
                   pl.BlockSpec((b, RH, RDK, RDV), lambda n: (0, 0, 0, 0)),
                   pl.BlockSpec((b, GH, GDK, GDV), lambda n: (0, 0, 0, 0)),
                   pl.BlockSpec((b, CONV_W - 1, CONV_CH), lambda n: (0, 0, 0))],
        out_shape=[jax.ShapeDtypeStruct((b, l, D), F32),
                   jax.ShapeDtypeStruct((b, RH, RDK, RDV), F32),
                   jax.ShapeDtypeStruct((b, GH, GDK, GDV), F32),
                   jax.ShapeDtypeStruct((b, CONV_W - 1, CONV_CH), F32)],
        scratch_shapes=[pltpu.VMEM((b, CHUNK + 8, CONV_CH), F32)],
        compiler_params=_cp(("arbitrary",)),
        name="mixer_a_prompt",
    )(proj, cos2, sin2, conv_w, alog_row, dt_row, rg_row, gg_row)


def _mixer_a_sample_kernel(p_ref, cos_ref, sin_ref, cw_ref, alog_ref, dt_ref, rg_ref, gg_ref,
                           rs_ref, gs_ref, cs_ref, mix_ref, ret_ref, gdn_ref, conv_ref):
    eye = _iota2((LANES, LANES), 0) == _iota2((LANES, LANES), 1)

    def col(v):
        return jnp.sum(jnp.where(eye, jnp.broadcast_to(v, (LANES, LANES)), 0.0), axis=1, keepdims=True)

    def vec_mat(c, s):
        return jnp.sum(c * s, axis=0, keepdims=True)

    cosv = cos_ref[...]
    sinv = sin_ref[...]
    for h in range(RH):
        gamma = 1.0 - 2.0 ** (-5.0 - h)
        rq = p_ref[0, :, h * RDK:(h + 1) * RDK]
        rk = p_ref[0, :, RH * RDK + h * RDK:RH * RDK + (h + 1) * RDK]
        rv = p_ref[0, :, 2 * RH * RDK + h * RDV:2 * RH * RDK + (h + 1) * RDV]
        rgate = p_ref[0, :, 2 * RH * RDK + RH * RDV + h * RDV:2 * RH * RDK + RH * RDV + (h + 1) * RDV]
        q = rq * cosv + pltpu.roll(rq, RDK // 2, 1) * sinv
        k = (rk * cosv + pltpu.roll(rk, RDK // 2, 1) * sinv) * (RDK ** -0.5)
        s = rs_ref[0, h]
        qk = jnp.sum(q * k, axis=-1, keepdims=True)
        out = qk * rv + gamma * vec_mat(col(q), s)
        ret_ref[0, h] = gamma * s + col(k) * rv
        mix_ref[0, :, h * RDV:(h + 1) * RDV] = _head_gate_norm(out, rg_ref[...], rgate)

    g_off = 2 * RH * RDK + 2 * RH * RDV
    gq_all = p_ref[0, :, g_off:g_off + CONV_CH]
    cb = cs_ref[0]
    y = cb[0:1, :] * cw_ref[0:1, :] + cb[1:2, :] * cw_ref[1:2, :] + cb[2:3, :] * cw_ref[2:3, :] \
        + gq_all * cw_ref[3:4, :]
    conv_ref[0, 0:2, :] = cb[1:3, :]
    conv_ref[0, 2:3, :] = gq_all
    y = _silu(y)
    z_off = g_off + CONV_CH
    tail = p_ref[0, :, z_off + GH * GDV:z_off + GH * GDV + LANES]
    g_all = -jnp.exp(alog_ref[...]) * _softplus(tail + dt_ref[...])
    beta_all = _sigmoid(tail)
    for h in range(GH):
        qh = y[:, h * GDK:(h + 1) * GDK]
        kh = y[:, GH * GDK + h * GDK:GH * GDK + (h + 1) * GDK]
        vh = y[:, 2 * GH * GDK + h * GDV:2 * GH * GDK + (h + 1) * GDV]
        qh = qh * lax.rsqrt(jnp.sum(qh * qh, axis=-1, keepdims=True) + NORM_EPS) * (GDK ** -0.5)
        kh = kh * lax.rsqrt(jnp.sum(kh * kh, axis=-1, keepdims=True) + NORM_EPS)
        eg = jnp.exp(g_all[:, h:h + 1])
        beta = beta_all[:, GH + h:GH + h + 1]
        s = gs_ref[0, h]
        kc = col(kh)
        v_new = vh * beta - vec_mat(kc * (beta * eg), s)
        qk = jnp.sum(qh * kh, axis=-1, keepdims=True)
        out = eg * vec_mat(col(qh), s) + qk * v_new
        gdn_ref[0, h] = eg * s + kc * v_new
        gz = p_ref[0, :, z_off + h * GDV:z_off + (h + 1) * GDV]
        mix_ref[0, :, RH * RDV + h * GDV:RH * RDV + (h + 1) * GDV] = _head_gate_norm(out, gg_ref[...], gz)


def _mixer_a_sample(proj, cos2, sin2, conv_w, alog_row, dt_row, rg_row, gg_row, ret_s, gdn_s, conv_s):
    b = proj.shape[0]
    fixed = lambda i: (0, 0)
    st = lambda i: (i, 0, 0, 0)
    return pl.pallas_call(
        _mixer_a_sample_kernel,
        grid=(b,),
        in_specs=[pl.BlockSpec((1, 1, A_COLS_PAD), lambda i: (i, 0, 0)),
                  pl.BlockSpec((1, LANES), fixed), pl.BlockSpec((1, LANES), fixed),
                  pl.BlockSpec((CONV_W, CONV_CH), fixed),
                  pl.BlockSpec((1, LANES), fixed), pl.BlockSpec((1, LANES), fixed),
                  pl.BlockSpec((1, LANES), fixed), pl.BlockSpec((1, LANES), fixed),
                  pl.BlockSpec((1, RH, RDK, RDV), st), pl.BlockSpec((1, GH, GDK, GDV), st),
                  pl.BlockSpec((1, CONV_W - 1, CONV_CH), lambda i: (i, 0, 0))],
        out_specs=[pl.BlockSpec((1, 1, D), lambda i: (i, 0, 0)),
                   pl.BlockSpec((1, RH, RDK, RDV), st), pl.BlockSpec((1, GH, GDK, GDV), st),
                   pl.BlockSpec((1, CONV_W - 1, CONV_CH), lambda i: (i, 0, 0))],
        out_shape=[jax.ShapeDtypeStruct((b, 1, D), F32),
                   jax.ShapeDtypeStruct((b, RH, RDK, RDV), F32),
                   jax.ShapeDtypeStruct((b, GH, GDK, GDV), F32),
                   jax.ShapeDtypeStruct((b, CONV_W - 1, CONV_CH), F32)],
        compiler_params=_cp(("parallel",)),
        name="mixer_a_sample",
    )(proj, cos2, sin2, conv_w, alog_row, dt_row, rg_row, gg_row, ret_s, gdn_s, conv_s)


def _fox_cumsum_kernel(x_ref, o_ref):
    n = x_ref.shape[1] // LANES
    upper = (_iota2((LANES, LANES), 0) <= _iota2((LANES, LANES), 1)).astype(BF16)
    carry = jnp.zeros((FH, 1), F32)
    for c in range(n):
        p = _mm_01_right(x_ref[:, c * LANES:(c + 1) * LANES], upper) + carry
        o_ref[0, :, c * LANES:(c + 1) * LANES] = p
        carry = p[:, LANES - 1:LANES]


def _fox_cumsum(lft, b, l):
    return pl.pallas_call(
        _fox_cumsum_kernel,
        grid=(b,),
        in_specs=[pl.BlockSpec((FH, l), lambda i: (0, i))],
        out_specs=pl.BlockSpec((1, FH, l), lambda i: (i, 0, 0)),
        out_shape=jax.ShapeDtypeStruct((b, FH, l), F32),
        compiler_params=_cp(("parallel",)),
        name="fox_cumsum",
    )(lft)


def _fox_flash_kernel(q_ref, k_ref, v_ref, ck_ref, o_ref, *, tq, tk):
    qi = pl.program_id(2)
    ratio = tq // tk
    q = q_ref[0]

    def block(j, carry, masked):
        m_old, l_old, acc = carry
        k = k_ref[0, pl.ds(pl.multiple_of(j * tk, tk), tk), :]
        v = v_ref[0, pl.ds(pl.multiple_of(j * tk, tk), tk), :]
        s = lax.dot_general(q, k, NT, preferred_element_type=F32) - ck_ref[0, pl.ds(j, 1), :]
        if masked:
            qpos = qi * tq + _iota2((tq, tk), 0)
            kpos = j * tk + _iota2((tq, tk), 1)
            s = jnp.where(kpos <= qpos, s, -jnp.inf)
        m_new = jnp.maximum(m_old, jnp.max(s, axis=1, keepdims=True))
        alpha = jnp.exp(m_old - m_new)
        p = jnp.exp(s - m_new)
        l_new = alpha * l_old + jnp.sum(p, axis=1, keepdims=True)
        acc = alpha * acc + jnp.dot(p.astype(BF16), v, preferred_element_type=F32)
        return m_new, l_new, acc

    init = (jnp.full((tq, 1), -jnp.inf, F32), jnp.zeros((tq, 1), F32), jnp.zeros((tq, FHD), F32))
    carry = lax.fori_loop(0, qi * ratio, lambda j, c: block(j, c, False), init)
    for d in range(ratio):
        carry = block(qi * ratio + d, carry, True)
    o_ref[0] = carry[2] / carry[1]


def _fox_flash(qb, kb, vb, cum, tq, tk):
    b, l, _ = qb.shape
    return pl.pallas_call(
        functools.partial(_fox_flash_kernel, tq=tq, tk=tk),
        grid=(b, FH, l // tq),
        in_specs=[pl.BlockSpec((1, tq, FHD), lambda i, h, qi: (i, qi, h)),
                  pl.BlockSpec((1, l, FHD), lambda i, h, qi: (i, 0, h)),
                  pl.BlockSpec((1, l, FHD), lambda i, h, qi: (i, 0, h)),
                  pl.BlockSpec((1, l // tk, tk), lambda i, h, qi: (i * FH + h, 0, 0))],
        out_specs=pl.BlockSpec((1, tq, FHD), lambda i, h, qi: (i, qi, h)),
        out_shape=jax.ShapeDtypeStruct((b, l, D), F32),
        compiler_params=_cp(("parallel", "parallel", "arbitrary")),
        name="fox_flash",
    )(qb, kb, vb, cum)


def _fox_sample_kernel(pt_ref, q_ref, kn_ref, vn_ref, lfn_ref, *rest):
    npp = PAGES_PER_STEP
    k_refs, v_refs, lf_refs = rest[0:npp], rest[npp:2 * npp], rest[2 * npp:3 * npp]
    o_ref, m_ref, l_ref, acc_ref, carry_ref = rest[3 * npp:]
    j = pl.program_id(1)
    scale = FHD ** -0.5
    rows = PAGE * FH

    @pl.when(j == 0)
    def _init():
        m_ref[...] = jnp.full(m_ref.shape, -jnp.inf, F32)
        l_ref[...] = jnp.zeros(l_ref.shape, F32)
        acc_ref[...] = jnp.zeros(acc_ref.shape, F32)
        carry_ref[...] = jnp.zeros(carry_ref.shape, F32)

    q = q_ref[0]
    qb = (q * scale).astype(BF16)
    own = (_iota2((FH, rows), 1) & (FH - 1)) == _iota2((FH, rows), 0)
    nch = rows // LANES
    ci = _iota2((LANES, 2 * LANES), 1)
    pre_tot = ((_iota2((LANES, 2 * LANES), 0) <= ci) | (ci >= LANES)).astype(BF16)
    stacked, s_pages = [], []
    for pg in range(npp):
        kp = k_refs[pg][0].reshape(rows, FHD).astype(BF16)
        s_pages.append(lax.dot_general(qb, kp, NT, preferred_element_type=F32))
        lf = jnp.where(own, jnp.broadcast_to(lf_refs[pg][0], (FH, rows)), 0.0)
        stacked += [lf[:, c * LANES:(c + 1) * LANES] for c in range(nch)]
    pt = _mm_01_right(jnp.concatenate(stacked, axis=0), pre_tot)
    carry = carry_ref[...]
    logits = []
    for pg in range(npp):
        cums = []
        for c in range(nch):
            r0 = (pg * nch + c) * FH
            cums.append(pt[r0:r0 + FH, 0:LANES] + carry)
            carry = carry + pt[r0:r0 + FH, LANES:2 * LANES]
        logits.append(jnp.where(own, s_pages[pg] - jnp.concatenate(cums, axis=1), -jnp.inf))
    carry_ref[...] = carry
    m_old = m_ref[...]
    m_new = m_old
    for lg in logits:
        m_new = jnp.maximum(m_new, jnp.max(lg, axis=1, keepdims=True))
    alpha = jnp.exp(m_old - m_new)
    l_new = alpha * l_ref[...]
    acc = alpha * acc_ref[...]
    for pg in range(npp):
        p = jnp.exp(logits[pg] - m_new)
        l_new = l_new + jnp.sum(p, axis=1, keepdims=True)
        vp = v_refs[pg][0].reshape(rows, FHD).astype(BF16)
        acc = acc + jnp.dot(p.astype(BF16), vp, preferred_element_type=F32)
    m_ref[...] = m_new
    l_ref[...] = l_new
    acc_ref[...] = acc

    @pl.when(j == pl.num_programs(1) - 1)
    def _fin():
        s_new = jnp.sum(q * kn_ref[0], axis=1, keepdims=True) * scale - (carry[:, 0:1] + lfn_ref[0])
        m2 = jnp.maximum(m_new, s_new)
        a2 = jnp.exp(m_new - m2)
        p2 = jnp.exp(s_new - m2)
        o_ref[0] = (a2 * acc + p2 * vn_ref[0]) / (a2 * l_new + p2)


def _fox_sample(page_table, q3, k3, v3, lf3, k_pool, v_pool, lf_pool):
    b = q3.shape[0]
    npg = page_table.shape[1]
    npp = PAGES_PER_STEP
    assert npg % npp == 0
    pt = page_table.reshape(-1)
    tok = lambda i, j, pt_ref: (i, 0, 0)

    def page(r, nd):
        return lambda i, j, pt_ref: (pt_ref[i * npg + j * npp + r],) + (0,) * nd

    grid_spec = pltpu.PrefetchScalarGridSpec(
        num_scalar_prefetch=1,
        grid=(b, npg // npp),
        in_specs=[pl.BlockSpec((1, FH, FHD), tok), pl.BlockSpec((1, FH, FHD), tok), pl.BlockSpec((1, FH, FHD), tok),
                  pl.BlockSpec((1, FH, 1), tok)]
        + [pl.BlockSpec((1, PAGE, FH, FHD), page(r, 3)) for r in range(npp)]
        + [pl.BlockSpec((1, PAGE, FH, FHD), page(r, 3)) for r in range(npp)]
        + [pl.BlockSpec((1, 1, PAGE * FH), page(r, 2)) for r in range(npp)],
        out_specs=pl.BlockSpec((1, FH, FHD), tok),
        scratch_shapes=[pltpu.VMEM((FH, 1), F32), pltpu.VMEM((FH, 1), F32), pltpu.VMEM((FH, FHD), F32),
                        pltpu.VMEM((FH, LANES), F32)],
    )
    lf_flat =lf_pool.reshape(lf_pool.shape[0], 1, PAGE * FH)
    return pl.pallas_call(
        _fox_sample_kernel,
        grid_spec=grid_spec,
        out_shape=jax.ShapeDtypeStruct((b, FH, FHD), F32),
        compiler_params=_cp(("parallel", "arbitrary")),
        name="fox_sample",
    )(pt, q3, k3, v3, lf3, *([k_pool] * npp), *([v_pool] * npp), *([lf_flat] * npp))


_CAND_BLOCKS = [(a, 16 if a == 0 else 8, 16 // (a + 1)) for a in range(8)]


_MARK0 = -(2.0 ** 127)
_MARK_STEP = 2.0 ** 120
_MARK_LIMIT = -(2.0 ** 126)


def _top16(s, iota):
    rank = jnp.full(s.shape, float(PTOPK), F32)
    vals = jnp.zeros((PTOPK, s.shape[1]), F32)
    i16 = _iota2((PTOPK, s.shape[1]), 0)
    for r in range(PTOPK):
        m = jnp.max(s, axis=0, keepdims=True)
        sel = iota == jnp.min(jnp.where(s == m, iota, 1e9), axis=0, keepdims=True)
        rank = jnp.where(sel, float(r), rank)
        s = jnp.where(sel, -jnp.inf, s)
        vals = jnp.where(i16 == r, m, vals)
    return rank, vals


def _top16_fast(s):
    lowest = jnp.min(s)
    vals = jnp.zeros((PTOPK, s.shape[1]), F32)
    i16 = _iota2((PTOPK, s.shape[1]), 0)
    for r in range(PTOPK):
        m = jnp.max(s, axis=0, keepdims=True)
        s = jnp.where(s == m, _MARK0 + r * _MARK_STEP, s)
        vals = jnp.where(i16 == r, m, vals)
    marked = s < _MARK_LIMIT
    rank = jnp.where(marked, (s - _MARK0) * (1.0 / _MARK_STEP), float(PTOPK))
    taken = jnp.sum(jnp.where(marked, 1.0, 0.0), axis=0, keepdims=True)
    return rank, vals, _not_16(taken) | (lowest <= _MARK_LIMIT)


def _pick16(cand, cidx, exact):
    picked = jnp.zeros(cand.shape, F32)
    for _ in range(PTOPK):
        m = jnp.max(cand, axis=0, keepdims=True)
        sel = cand == m
        if exact:
            sel = cidx == jnp.min(jnp.where(sel, cidx, 1e9), axis=0, keepdims=True)
        picked = jnp.where(sel, 1.0, picked)
        cand = jnp.where(sel, -jnp.inf, cand)
    return picked


def _not_16(count):
    return jnp.max(jnp.abs(count - float(PTOPK))) > 0.5


def _route_head(s_t):
    s1 = s_t[0:NKEYS]
    s2 = s_t[NKEYS:2 * NKEYS]
    w = s_t.shape[1]
    iota = _iota2((NKEYS, w), 0).astype(F32)
    rank1, v1, redo1 = _top16_fast(s1)
    rank2, v2, redo2 = _top16_fast(s2)
    rank1, v1, rank2, v2 = lax.cond(redo1 | redo2, lambda: _top16(s1, iota) + _top16(s2, iota),
                                    lambda: (rank1, v1, rank2, v2))
    e1 = jnp.exp(v1 - v1[0:1])
    e2 = jnp.exp(v2 - v2[0:1])
    i8 = _iota2((8, w), 0)
    i8f = i8.astype(F32)
    i16f = _iota2((16, w), 0).astype(F32)
    cand, cidx, cprob = [], [], []
    for a, rows, valid in _CAND_BLOCKS:
        c = v1[a:a + 1] + v2[0:rows]
        pr = e1[a:a + 1] * e2[0:rows]
        if rows == 16:
            ix = i16f
        else:
            ix = i8f + float(a * PTOPK)
            c = jnp.where(i8 < valid, c, -jnp.inf)
        cand.append(c)
        cidx.append(ix)
        cprob.append(pr)
    cand.append(v1[8:16] + v2[0:1])
    cidx.append((i8f + 8.0) * float(PTOPK))
    cprob.append(e1[8:16] * e2[0:1])
    cand = jnp.concatenate(cand, axis=0)
    cidx = jnp.concatenate(cidx, axis=0)
    cprob = jnp.concatenate(cprob, axis=0)
    picked = _pick16(cand, cidx, False)
    picked = lax.cond(_not_16(jnp.sum(picked, axis=0, keepdims=True)),
                      lambda: _pick16(cand, cidx, True), lambda: picked)
    z = jnp.sum(picked * cprob, axis=0, keepdims=True)
    n_low = jnp.zeros((8, w), F32)
    off = 0
    for a, rows, _ in _CAND_BLOCKS:
        cnt = jnp.sum(picked[off:off + rows], axis=0, keepdims=True)
        n_low = jnp.where(i8 == a, cnt, n_low)
        off += rows
    n16 = jnp.concatenate([n_low, picked[off:off + 8]], axis=0)
    n1d = jnp.zeros((NKEYS, w), F32)
    for a in range(PTOPK):
        n1d = jnp.where(rank1 == float(a), n16[a:a + 1], n1d)
    e1d = jnp.exp(s1 - v1[0:1]) * (0.5 / z)
    e2d = jnp.exp(s2 - v2[0:1])
    return rank2, e2d, n1d, e1d


def _peer_kernel(x_ref, wq_ref, keys_ref, u0_ref, un_ref, vt_ref, g_ref, b_ref, o_ref,
                 xb_ref, q_ref, r2_ref, e2_ref, n1_ref, e1_ref, ha_ref, hb_ref, at_ref, acc_ref, *, tm, te, rw):
    j = pl.program_id(1)
    nlg = tm // LANES
    nsub = te // NKEYS

    @pl.when(j == 0)
    def _route():
        xb = x_ref[...].astype(BF16)
        xb_ref[...] = xb
        q = jnp.dot(xb, wq_ref[...], preferred_element_type=F32)
        for h in range(PH):
            q_ref[h] = q[:, h * LANES:(h + 1) * LANES].astype(BF16)
        acc_ref[...] = jnp.zeros(acc_ref.shape, F32)
        ha_ref[...] = lax.dot_general(u0_ref[...], xb, NT, preferred_element_type=F32)

        def body(h, carry):
            for w0 in range(0, tm, rw):
                s_t = lax.dot_general(keys_ref[...], q_ref[h, w0:w0 + rw, :], NT,
                                      preferred_element_type=F32)
                r2, e2, n1, e1 = _route_head(s_t)
                r2_ref[h, :, w0:w0 + rw] = r2.astype(BF16)
                e2_ref[h, :, w0:w0 + rw] = e2.astype(BF16)
                for g in range(rw // LANES):
                    n1_ref[h, w0 // LANES + g] = n1[:, g * LANES:(g + 1) * LANES]
                    e1_ref[h, w0 // LANES + g] = e1[:, g * LANES:(g + 1) * LANES]
            return carry

        lax.fori_loop(0, PH, body, 0)

    def step(h_cur_ref, h_next_ref):
        h_next_ref[...] = lax.dot_general(un_ref[...], xb_ref[...], NT, preferred_element_type=F32)
        zero = jnp.zeros((), BF16)
        pair = 2
        for pp in range(nsub // pair):
            for g in range(nlg):
                lanes = slice(g * LANES, (g + 1) * LANES)
                gates = [jnp.zeros((NKEYS, LANES), BF16) for _ in range(pair)]
                for h in range(PH):
                    r2 = r2_ref[h, :, lanes]
                    e2 = e2_ref[h, :, lanes]
                    for ii in range(pair):
                        i1 = j * nsub + pp * pair + ii
                        n_row = n1_ref[h, g, pl.ds(i1, 1), :].astype(BF16)
                        e_row = e1_ref[h, g, pl.ds(i1, 1), :].astype(BF16)
                        gates[ii] = gates[ii] + jnp.where(r2 < n_row, e2, zero) * e_row
                for ii in range(pair):
                    r0 = (pp * pair + ii) * NKEYS
                    hb = h_cur_ref[r0:r0 + NKEYS, lanes]
                    act = hb * (1.0 + lax.erf(hb * (2.0 ** -0.5)))
                    at_ref[r0:r0 + NKEYS, lanes] = act.astype(BF16) * gates[ii]
        acc_ref[...] += jnp.dot(vt_ref[...], at_ref[...], preferred_element_type=F32)

    @pl.when(j % 2 == 0)
    def _even():
        step(ha_ref, hb_ref)

    @pl.when(j % 2 == 1)
    def _odd():
        step(hb_ref, ha_ref)

    @pl.when(j == pl.num_programs(1) - 1)
    def _fin():
        o_ref[...] = _layer_norm(ALPHA * x_ref[...] + acc_ref[...].T, g_ref[...], b_ref[...])


def _peer(x, wq, keys, u, vt, g, b, tm, te):
    t = x.shape[0]
    nj = NEXP // te
    fixed = lambda i, j: (0, 0)
    return pl.pallas_call(
        functools.partial(_peer_kernel, tm=tm, te=te, rw=min(tm, 2 * LANES)),
        grid=(t // tm, nj),
        in_specs=[pl.BlockSpec((tm, D), lambda i, j: (i, 0)),
                  pl.BlockSpec((D, PH * LANES), fixed),
                  pl.BlockSpec((2 * NKEYS, LANES), fixed),
                  pl.BlockSpec((te, D), fixed),
                  pl.BlockSpec((te, D), lambda i, j: (jnp.minimum(j + 1, nj - 1), 0)),
                  pl.BlockSpec((D, te), lambda i, j: (0, j)),
                  pl.BlockSpec((1, D), fixed), pl.BlockSpec((1, D), fixed)],
        out_specs=pl.BlockSpec((tm, D), lambda i, j: (i, 0)),
        out_shape=jax.ShapeDtypeStruct((t, D), F32),
        scratch_shapes=[pltpu.VMEM((tm, D), BF16),
                        pltpu.VMEM((PH, tm, LANES), BF16),
                        pltpu.VMEM((PH, NKEYS, tm), BF16),
                        pltpu.VMEM((PH, NKEYS, tm), BF16),
                        pltpu.VMEM((PH, tm // LANES, NKEYS, LANES), F32),
                        pltpu.VMEM((PH, tm // LANES, NKEYS, LANES), F32),
                        pltpu.VMEM((te, tm), F32),
                        pltpu.VMEM((te, tm), F32),
                        pltpu.VMEM((te, tm), BF16),
                        pltpu.VMEM((D, tm), F32)],
        compiler_params=_cp(("parallel", "arbitrary")),
        name="peer",
    )(x, wq, keys, u, u, vt, g, b)


def _rope_tables(pos):
    half = RDK // 2
    inv = ROPE_BASE ** (-jnp.arange(half, dtype=F32) / half)
    ang = pos.astype(F32)[:, None] * inv[None, :]
    cos, sin = jnp.cos(ang), jnp.sin(ang)
    return jnp.concatenate([cos, cos], -1), jnp.concatenate([-sin, sin], -1)


def _pad_lanes(v, width=LANES, offset=0):
    return jnp.zeros((1, width), F32).at[0, offset:offset + v.shape[0]].set(v.astype(F32))


def _peer_params(wq, sub_keys, u_tab, v_tab):
    half = sub_keys.shape[-1]
    keys = jnp.zeros((2 * NKEYS, LANES), F32)
    keys = keys.at[0:NKEYS, 0:half].set(sub_keys[0]).at[NKEYS:, half:2 * half].set(sub_keys[1])
    return wq.astype(BF16), keys.astype(BF16), u_tab.astype(BF16), v_tab.T.astype(BF16)


def _tile(t, pref):
    return pref if t % pref == 0 else LANES


def kernel(x_prompt, x_sample, state_ret, state_gdn, state_gdn_conv, cache_fox_k, cache_fox_v, cache_fox_logf,
           page_table, w_in_a, ret_norm_g, gdn_a_log, gdn_dt_bias, gdn_conv_w, gdn_norm_g, w_out_a, w_in_c,
           fox_b_f, w_out_c, peer_wq, peer_sub_keys, peer_u, peer_v, ln_g, ln_b):
    bp, lp, _ = x_prompt.shape
    bs = x_sample.shape[0]
    tp = bp * lp
    xp = x_prompt.reshape(tp, D)
    xs = x_sample.reshape(bs, D)
    tmp, tms = _tile(tp, 512), _tile(bs, 512)
    te = 1024
    ln = lambda layer, k: (ln_g[layer, k].reshape(1, D), ln_b[layer, k].reshape(1, D))

    w_in = jnp.pad(w_in_a[0], ((0, 0), (0, A_COLS_PAD - A_COLS))).astype(BF16)
    cos_p, sin_p = _rope_tables(jnp.arange(lp))
    cos_s, sin_s = _rope_tables(PAST + jnp.arange(1))
    alog, dtb = _pad_lanes(gdn_a_log[0]), _pad_lanes(gdn_dt_bias[0])
    rg, gg = ret_norm_g[0].reshape(1, RDV), gdn_norm_g[0].reshape(1, GDV)
    proj_p = _proj(xp, w_in, tmp).reshape(bp, lp, A_COLS_PAD)
    proj_s = _proj(xs, w_in, tms).reshape(bs, 1, A_COLS_PAD)
    mix_p, ret_p, gdn_p, conv_p = _mixer_a_prompt(proj_p, cos_p, sin_p, gdn_conv_w[0], alog, dtb, rg, gg)
    mix_s, ret_s, gdn_s, conv_s = _mixer_a_sample(proj_s, cos_s, sin_s, gdn_conv_w[0], alog, dtb, rg, gg,
                                                  state_ret[0], state_gdn[0], state_gdn_conv[0])
    w_out = w_out_a[0].astype(BF16)
    g0, b0 = ln(0, 0)
    xp = _outproj_ln(mix_p.reshape(tp, D), xp, w_out, g0, b0, tmp)
    xs = _outproj_ln(mix_s.reshape(bs, D), xs, w_out, g0, b0, tms)
    pw = _peer_params(peer_wq[0], peer_sub_keys[0], peer_u[0], peer_v[0])
    g1, b1 = ln(0, 1)
    xp = _peer(xp, *pw, g1, b1, tmp, te)
    xs = _peer(xs, *pw, g1, b1, tms, te)

    w_c = jnp.pad(w_in_c[0], ((0, 0), (0, C_COLS_PAD - w_in_c.shape[-1]))).astype(BF16)
    bf_row = _pad_lanes(fox_b_f[0])
    kp, vp, lfp, lftp, qbp, kbp, vbp = _fox_proj(xp, w_c, bf_row, tmp, True)
    ks, vs, lfs, _, qs = _fox_proj(xs, w_c, bf_row, tms, False)
    tq, tk = _tile(lp, 1024), _tile(lp, 512)
    cum = _fox_cumsum(lftp, bp, lp).reshape(bp * FH, lp // tk, tk)
    op = _fox_flash(qbp.reshape(bp, lp, D), kbp.reshape(bp, lp, D), vbp.reshape(bp, lp, D), cum, tq, tk)
    os_ = _fox_sample(page_table, qs.reshape(bs, FH, FHD), ks.reshape(bs, FH, FHD), vs.reshape(bs, FH, FHD),
                      lfs.reshape(bs, FH, 1), cache_fox_k[0], cache_fox_v[0], cache_fox_logf[0])
    w_oc = w_out_c[0].astype(BF16)
    g2, b2 = ln(1, 0)
    xp = _outproj_ln(op.reshape(tp, D), xp, w_oc, g2, b2, tmp)
    xs = _outproj_ln(os_.reshape(bs, D), xs, w_oc, g2, b2, tms)
    pw = _peer_params(peer_wq[1], peer_sub_keys[1], peer_u[1], peer_v[1])
    g3, b3 = ln(1, 1)
    xp = _peer(xp, *pw, g3, b3, tmp, te)
    xs = _peer(xs, *pw, g3, b3, tms, te)

    return (xp.reshape(bp, lp, D), xs.reshape(bs, 1, D),
            ret_p[None], ret_s[None], gdn_p[None], gdn_s[None], conv_p[None], conv_s[None],
            kp.reshape(1, bp, lp, FH, FHD), ks.reshape(1, bs, 1, FH, FHD),
            vp.reshape(1, bp, lp, FH, FHD), vs.reshape(1, bs, 1, FH, FHD),
            lfp.reshape(1, bp, lp, FH), lfs.reshape(1, bs, 1, FH))
```

```python
import functools
import math

import jax
import jax.numpy as jnp
from jax import lax
from jax.experimental import pallas as pl
from jax.experimental.pallas import tpu as pltpu

F32 = jnp.float32
BF16 = jnp.bfloat16

D = 1024
PAST = 2048
PAGE = 128
RH, RDK, RDV = 4, 128, 128
GH, GDK, GDV = 4, 128, 128
CONV_W = 4
CONV_CH = 2 * GH * GDK + GH * GDV
A_COLS = 4104
A_COLS_PAD = 4224
FH, FHD = 8, 128
C_COLS_PAD = 3200
PH, PTOPK, NKEYS = 8, 16, 128
NEXP = NKEYS * NKEYS
ALPHA = 4.0 ** 0.25
LN_EPS = 1e-5
NORM_EPS = 1e-6
ROPE_BASE = 10000.0
LANES = 128
CHUNK = 128
PAGES_PER_STEP = 4
GATE_DT = BF16
VMEM_LIMIT = 56 * 1024 * 1024

NT = (((1,), (1,)), ((), ()))
TN = (((0,), (0,)), ((), ()))


def _cp(sem):
    return pltpu.CompilerParams(dimension_semantics=sem, vmem_limit_bytes=VMEM_LIMIT)


def _mm(a, b):
    return jnp.dot(a.astype(BF16), b.astype(BF16), preferred_element_type=F32)


def _mm_nt(a, b):
    return lax.dot_general(a.astype(BF16), b.astype(BF16), NT, preferred_element_type=F32)


def _mm_tn(a, b):
    return lax.dot_general(a.astype(BF16), b.astype(BF16), TN, preferred_element_type=F32)


def _split3(x):
    x1 = x.astype(BF16)
    r1 = x - x1.astype(F32)
    x2 = r1.astype(BF16)
    return x1, x2, (r1 - x2.astype(F32)).astype(BF16)


def _mm_01_left(m01, x):
    return sum(jnp.dot(m01, t, preferred_element_type=F32) for t in _split3(x))


def _mm_01_right(x, m01):
    return sum(jnp.dot(t, m01, preferred_element_type=F32) for t in _split3(x))


def _split2(a):
    hi = a.astype(BF16)
    return hi, (a - hi.astype(F32)).astype(BF16)


def _mm_s(a, b):
    return (jnp.dot(a[0], b[0], preferred_element_type=F32) + jnp.dot(a[0], b[1], preferred_element_type=F32)
            + jnp.dot(a[1], b[0], preferred_element_type=F32))


def _sigmoid(x):
    return 1.0 / (1.0 + jnp.exp(-x))


def _silu(x):
    return x * _sigmoid(x)


def _softplus(x):
    return jnp.maximum(x, 0.0) + jnp.log1p(jnp.exp(-jnp.abs(x)))


def _log_sigmoid(x):
    return jnp.minimum(x, 0.0) - jnp.log1p(jnp.exp(-jnp.abs(x)))


def _layer_norm(z, g, b):
    mu = jnp.mean(z, axis=-1, keepdims=True)
    d = z - mu
    var = jnp.mean(d * d, axis=-1, keepdims=True)
    return d * lax.rsqrt(var + LN_EPS) * g + b


def _iota2(shape, axis):
    return lax.broadcasted_iota(jnp.int32, shape, axis)


def _proj_kernel(x_ref, w_ref, o_ref):
    o_ref[...] = jnp.dot(x_ref[...].astype(BF16), w_ref[...], preferred_element_type=F32)


def _proj(x, w, tm):
    t, k = x.shape
    n = w.shape[1]
    return pl.pallas_call(
        _proj_kernel,
        grid=(t // tm,),
        in_specs=[pl.BlockSpec((tm, k), lambda i: (i, 0)), pl.BlockSpec((k, n), lambda i: (0, 0))],
        out_specs=pl.BlockSpec((tm, n), lambda i: (i, 0)),
        out_shape=jax.ShapeDtypeStruct((t, n), F32),
        compiler_params=_cp(("parallel",)),
        name="proj_a",
    )(x, w)


def _fox_proj_kernel(x_ref, w_ref, bf_ref, k_ref, v_ref, lf_ref, lft_ref, *q_refs, attn_copies):
    p = jnp.dot(x_ref[...].astype(BF16), w_ref[...], preferred_element_type=F32)
    q = p[:, 0:D]
    k = p[:, D:2 * D]
    v = p[:, 2 * D:3 * D]
    for h in range(FH):
        k_ref[:, h, :] = k[:, h * FHD:(h + 1) * FHD]
        v_ref[:, h, :] = v[:, h * FHD:(h + 1) * FHD]
    lf = _log_sigmoid(p[:, 3 * D:3 * D + LANES] + bf_ref[...])
    lf_ref[...] = lf[:, 0:FH]
    lft_ref[...] = lf.T[0:FH, :]
    if attn_copies:
        qb_ref, kb_ref, vb_ref = q_refs
        qb_ref[...] = (q * (FHD ** -0.5)).astype(BF16)
        kb_ref[...] = k.astype(BF16)
        vb_ref[...] = v.astype(BF16)
    else:
        q_refs[0][...] = q


def _fox_proj(x, w, bf_row, tm, attn_copies):
    t = x.shape[0]
    row = lambda i: (i, 0)
    row3 = lambda i: (i, 0, 0)
    out_specs = [pl.BlockSpec((tm, FH, FHD), row3), pl.BlockSpec((tm, FH, FHD), row3),
                 pl.BlockSpec((tm, FH), row), pl.BlockSpec((FH, tm), lambda i: (0, i))]
    out_shape = [jax.ShapeDtypeStruct((t, FH, FHD), F32), jax.ShapeDtypeStruct((t, FH, FHD), F32),
                 jax.ShapeDtypeStruct((t, FH), F32), jax.ShapeDtypeStruct((FH, t), F32)]
    n_q = 3 if attn_copies else 1
    out_specs += [pl.BlockSpec((tm, D), row)] * n_q
    out_shape += [jax.ShapeDtypeStruct((t, D), BF16 if attn_copies else F32)] * n_q
    return pl.pallas_call(
        functools.partial(_fox_proj_kernel, attn_copies=attn_copies),
        grid=(t // tm,),
        in_specs=[pl.BlockSpec((tm, D), row), pl.BlockSpec((D, C_COLS_PAD), lambda i: (0, 0)),
                  pl.BlockSpec((1, LANES), lambda i: (0, 0))],
        out_specs=out_specs,
        out_shape=out_shape,
        compiler_params=_cp(("parallel",)),
        name="proj_c",
    )(x, w, bf_row)


def _outproj_ln_kernel(a_ref, x_ref, w_ref, g_ref, b_ref, o_ref):
    h = jnp.dot(a_ref[...].astype(BF16), w_ref[...], preferred_element_type=F32)
    o_ref[...] = _layer_norm(ALPHA * x_ref[...] + h, g_ref[...], b_ref[...])


def _outproj_ln(a, x, w, g, b, tm):
    t = x.shape[0]
    row = lambda i: (i, 0)
    fixed = lambda i: (0, 0)
    return pl.pallas_call(
        _outproj_ln_kernel,
        grid=(t // tm,),
        in_specs=[pl.BlockSpec((tm, D), row), pl.BlockSpec((tm, D), row), pl.BlockSpec((D, D), fixed),
                  pl.BlockSpec((1, D), fixed), pl.BlockSpec((1, D), fixed)],
        out_specs=pl.BlockSpec((tm, D), row),
        out_shape=jax.ShapeDtypeStruct((t, D), F32),
        compiler_params=_cp(("parallel",)),
        name="outproj_ln",
    )(a, x, w, g, b)


def _unit_lower_inverses(a_list, ri, ci):
    eye = (ri == ci).astype(F32)
    blk = (ri >> 4) == (ci >> 4)
    d = [_split2(jnp.where(blk, a, 0.0)) for a in a_list]
    d2f = [_mm_s(x, x) for x in d]
    d2 = [_split2(x) for x in d2f]
    d4f = [_mm_s(x, x) for x in d2]
    d4 = [_split2(x) for x in d4f]
    d8 = [_split2(_mm_s(x, x)) for x in d4]
    xs = [eye - jnp.where(blk, a, 0.0) for a in a_list]
    for pw in (d2, d4, d8):
        xs = [x + _mm_s(_split2(x), p) for x, p in zip(xs, pw)]
    for s in (5, 6, 7):
        msk = ((ri >> s) == (ci >> s)) & ((ri >> (s - 1)) != (ci >> (s - 1)))
        es = [_split2(jnp.where(msk, a, 0.0)) for a in a_list]
        xsp = [_split2(x) for x in xs]
        xe = [_split2(_mm_s(x, e)) for x, e in zip(xsp, es)]
        xs = [x - _mm_s(y, xp) for x, y, xp in zip(xs, xe, xsp)]
    return xs


def _head_gate_norm(out, g_row, gate):
    return out * lax.rsqrt(jnp.mean(out * out, axis=-1, keepdims=True) + NORM_EPS) * g_row * _silu(gate)


def _mixer_a_prompt_kernel(p_ref, cos_ref, sin_ref, cw_ref, alog_ref, dt_ref, rg_ref, gg_ref,
                           mix_ref, ret_ref, gdn_ref, conv_ref, ext_ref, *, nb):
    n = pl.program_id(0)
    c = CHUNK

    @pl.when(n == 0)
    def _init():
        ret_ref[...] = jnp.zeros_like(ret_ref)
        gdn_ref[...] = jnp.zeros_like(gdn_ref)
        ext_ref[:, 0:8, :] = jnp.zeros((nb, 8, CONV_CH), F32)

    ri = _iota2((c, c), 0)
    ci = _iota2((c, c), 1)
    rf = ri.astype(F32)
    diff = rf - ci.astype(F32)
    cosv = cos_ref[...]
    sinv = sin_ref[...]

    for h in range(RH):
        lg = math.log1p(-(2.0 ** (-5.0 - h)))
        decay = jnp.where(diff >= 0, jnp.exp(jnp.maximum(diff, 0.0) * lg), 0.0)
        inner = jnp.exp((rf + 1.0) * lg)
        kdec = jnp.exp((c - 1.0 - rf) * lg)
        for b in range(nb):
            rq = p_ref[b, :, h * RDK:(h + 1) * RDK]
            rk = p_ref[b, :, RH * RDK + h * RDK:RH * RDK + (h + 1) * RDK]
            rv = p_ref[b, :, 2 * RH * RDK + h * RDV:2 * RH * RDK + (h + 1) * RDV]
            rgate = p_ref[b, :, 2 * RH * RDK + RH * RDV + h * RDV:2 * RH * RDK + RH * RDV + (h + 1) * RDV]
            q = rq * cosv + pltpu.roll(rq, RDK // 2, 1) * sinv
            k = (rk * cosv + pltpu.roll(rk, RDK // 2, 1) * sinv) * (RDK ** -0.5)
            s = ret_ref[b, h]
            scores = _mm_nt(q, k) * decay
            out = _mm(scores, rv) + _mm(q * inner, s)
            ret_ref[b, h] = math.exp(c * lg) * s + _mm_tn(k * kdec, rv)
            mix_ref[b, :, h * RDV:(h + 1) * RDV] = _head_gate_norm(out, rg_ref[...], rgate)

    g_off = 2 * RH * RDK + 2 * RH * RDV
    z_off = g_off + CONV_CH
    incl = ri >= ci
    strict = ri > ci
    tril = incl.astype(BF16)
    units = []
    for b in range(nb):
        ext_ref[b, 8:8 + c, :] = p_ref[b, :, g_off:g_off + CONV_CH]
        y = ext_ref[b, 5:5 + c, :] * cw_ref[0:1, :]
        for i in range(1, CONV_W):
            y = y + ext_ref[b, 5 + i:5 + i + c, :] * cw_ref[i:i + 1, :]
        conv_ref[b] = ext_ref[b, c + 5:c + 8, :]
        ext_ref[b, 0:8, :] = ext_ref[b, c:c + 8, :]
        y = _silu(y)
        tail = p_ref[b, :, z_off + GH * GDV:z_off + GH * GDV + LANES]
        g_all = -jnp.exp(alog_ref[...]) * _softplus(tail + dt_ref[...])
        beta_all = _sigmoid(tail)
        for h in range(GH):
            qh = y[:, h * GDK:(h + 1) * GDK]
            kh = y[:, GH * GDK + h * GDK:GH * GDK + (h + 1) * GDK]
            vh = y[:, 2 * GH * GDK + h * GDV:2 * GH * GDK + (h + 1) * GDV]
            qh = qh * lax.rsqrt(jnp.sum(qh * qh, axis=-1, keepdims=True) + NORM_EPS) * (GDK ** -0.5)
            kh = kh * lax.rsqrt(jnp.sum(kh * kh, axis=-1, keepdims=True) + NORM_EPS)
            gb = jnp.broadcast_to(g_all[:, h:h + 1], (c, c))
            bcol = beta_all[:, GH + h:GH + h + 1]
            units.append((b, h, qh, kh, vh, gb, bcol))

    gcs = [_mm_01_left(tril, u[5]) for u in units]
    decs = []
    for gc in gcs:
        dmat = gc - gc.T
        decs.append(jnp.where(incl, jnp.exp(jnp.where(incl, dmat, 0.0)), 0.0))
    kbs = [u[3] * u[6] for u in units]
    a_list = [jnp.where(strict, _mm_nt(kb, u[3]) * dec, 0.0) for kb, u, dec in zip(kbs, units, decs)]
    ts = [t.astype(BF16) for t in _unit_lower_inverses(a_list, ri, ci)]
    egcs = [jnp.exp(gc) for gc in gcs]
    us = [jnp.dot(t, (u[4] * u[6]).astype(BF16), preferred_element_type=F32) for t, u in zip(ts, units)]
    ws = [jnp.dot(t, (kb * egc).astype(BF16), preferred_element_type=F32) for t, kb, egc in zip(ts, kbs, egcs)]
    attns = [_mm_nt(u[2], u[3]) * dec for u, dec in zip(units, decs)]
    ss = [gdn_ref[u[0], u[1]] for u in units]
    v_news = [uu - _mm(w, s) for uu, w, s in zip(us, ws, ss)]
    outs = [_mm(u[2] * egc, s) + _mm(attn, vn) for u, egc, s, attn, vn in zip(units, egcs, ss, attns, v_news)]
    for u, gc, s, vn, out in zip(units, gcs, ss, v_news, outs):
        b, h = u[0], u[1]
        g_last = gc[c - 1:c, :]
        gdn_ref[b, h] = jnp.exp(g_last) * s + _mm_tn(u[3] * jnp.exp(g_last - gc), vn)
        gz = p_ref[b, :, z_off + h * GDV:z_off + (h + 1) * GDV]
        mix_ref[b, :, RH * RDV + h * GDV:RH * RDV + (h + 1) * GDV] = _head_gate_norm(out, gg_ref[...], gz)


def _mixer_a_prompt(proj, cos2, sin2, conv_w, alog_row, dt_row, rg_row, gg_row):
    b, l, _ = proj.shape
    fixed = lambda n: (0, 0)
    return pl.pallas_call(
        functools.partial(_mixer_a_prompt_kernel, nb=b),
        grid=(l // CHUNK,),
        in_specs=[pl.BlockSpec((b, CHUNK, A_COLS_PAD), lambda n: (0, n, 0)),
                  pl.BlockSpec((CHUNK, LANES), lambda n: (n, 0)),
                  pl.BlockSpec((CHUNK, LANES), lambda n: (n, 0)),
                  pl.BlockSpec((CONV_W, CONV_CH), fixed),
                  pl.BlockSpec((1, LANES), fixed), pl.BlockSpec((1, LANES), fixed),
                  pl.BlockSpec((1, LANES), fixed), pl.BlockSpec((1, LANES), fixed)],
        out_specs=[pl.BlockSpec((b, CHUNK, D), lambda n: (0, n, 0)),
                   pl.BlockSpec((b, RH, RDK, RDV), lambda n: (0, 0, 0, 0)),
                   pl.BlockSpec((b, GH, GDK, GDV), lambda n: (0, 0, 0, 0)),
                   pl.BlockSpec((b, CONV_W - 1, CONV_CH), lambda n: (0, 0, 0))],
        out_shape=[jax.ShapeDtypeStruct((b, l, D), F32),
                   jax.ShapeDtypeStruct((b, RH, RDK, RDV), F32),
                   jax.ShapeDtypeStruct((b, GH, GDK, GDV), F32),
                   jax.ShapeDtypeStruct((b, CONV_W - 1, CONV_CH), F32)],
        scratch_shapes=[pltpu.VMEM((b, CHUNK + 8, CONV_CH), F32)],
        compiler_params=_cp(("arbitrary",)),
        name="mixer_a_prompt",
    )(proj, cos2, sin2, conv_w, alog_row, dt_row, rg_row, gg_row)


def _mixer_a_sample_kernel(p_ref, cos_ref, sin_ref, cw_ref, alog_ref, dt_ref, rg_ref, gg_ref,
                           rs_ref, gs_ref, cs_ref, mix_ref, ret_ref, gdn_ref, conv_ref):
    eye = _iota2((LANES, LANES), 0) == _iota2((LANES, LANES), 1)

    def col(v):
        return jnp.sum(jnp.where(eye, jnp.broadcast_to(v, (LANES, LANES)), 0.0), axis=1, keepdims=True)

    def vec_mat(c, s):
        return jnp.sum(c * s, axis=0, keepdims=True)

    cosv = cos_ref[...]
    sinv = sin_ref[...]
    for h in range(RH):
        gamma = 1.0 - 2.0 ** (-5.0 - h)
        rq = p_ref[0, :, h * RDK:(h + 1) * RDK]
        rk = p_ref[0, :, RH * RDK + h * RDK:RH * RDK + (h + 1) * RDK]
        rv = p_ref[0, :, 2 * RH * RDK + h * RDV:2 * RH * RDK + (h + 1) * RDV]
        rgate = p_ref[0, :, 2 * RH * RDK + RH * RDV + h * RDV:2 * RH * RDK + RH * RDV + (h + 1) * RDV]
        q = rq * cosv + pltpu.roll(rq, RDK // 2, 1) * sinv
        k = (rk * cosv + pltpu.roll(rk, RDK // 2, 1) * sinv) * (RDK ** -0.5)
        s = rs_ref[0, h]
        qk = jnp.sum(q * k, axis=-1, keepdims=True)
        out = qk * rv + gamma * vec_mat(col(q), s)
        ret_ref[0, h] = gamma * s + col(k) * rv
        mix_ref[0, :, h * RDV:(h + 1) * RDV] = _head_gate_norm(out, rg_ref[...], rgate)

    g_off = 2 * RH * RDK + 2 * RH * RDV
    gq_all = p_ref[0, :, g_off:g_off + CONV_CH]
    cb = cs_ref[0]
    y = cb[0:1, :] * cw_ref[0:1, :] + cb[1:2, :] * cw_ref[1:2, :] + cb[2:3, :] * cw_ref[2:3, :] \
        + gq_all * cw_ref[3:4, :]
    conv_ref[0, 0:2, :] = cb[1:3, :]
    conv_ref[0, 2:3, :] = gq_all
    y = _silu(y)
    z_off = g_off + CONV_CH
    tail = p_ref[0, :, z_off + GH * GDV:z_off + GH * GDV + LANES]
    g_all = -jnp.exp(alog_ref[...]) * _softplus(tail + dt_ref[...])
    beta_all = _sigmoid(tail)
    for h in range(GH):
        qh = y[:, h * GDK:(h + 1) * GDK]
        kh = y[:, GH * GDK + h * GDK:GH * GDK + (h + 1) * GDK]
        vh = y[:, 2 * GH * GDK + h * GDV:2 * GH * GDK + (h + 1) * GDV]
        qh = qh * lax.rsqrt(jnp.sum(qh * qh, axis=-1, keepdims=True) + NORM_EPS) * (GDK ** -0.5)
        kh = kh * lax.rsqrt(jnp.sum(kh * kh, axis=-1, keepdims=True) + NORM_EPS)
        eg = jnp.exp(g_all[:, h:h + 1])
        beta = beta_all[:, GH + h:GH + h + 1]
        s = gs_ref[0, h]
        kc = col(kh)
        v_new = vh * beta - vec_mat(kc * (beta * eg), s)
        qk = jnp.sum(qh * kh, axis=-1, keepdims=True)
        out = eg * vec_mat(col(qh), s) + qk * v_new
        gdn_ref[0, h] = eg * s + kc * v_new
        gz = p_ref[0, :, z_off + h * GDV:z_off + (h + 1) * GDV]
        mix_ref[0, :, RH * RDV + h * GDV:RH * RDV + (h + 1) * GDV] = _head_gate_norm(out, gg_ref[...], gz)


def _mixer_a_sample(proj, cos2, sin2, conv_w, alog_row, dt_row, rg_row, gg_row, ret_s, gdn_s, conv_s):
    b = proj.shape[0]
    fixed = lambda i: (0, 0)
    st = lambda i: (i, 0, 0, 0)
    return pl.pallas_call(
        _mixer_a_sample_kernel,
        grid=(b,),
        in_specs=[pl.BlockSpec((1, 1, A_COLS_PAD), lambda i: (i, 0, 0)),
                  pl.BlockSpec((1, LANES), fixed), pl.BlockSpec((1, LANES), fixed),
                  pl.BlockSpec((CONV_W, CONV_CH), fixed),
                  pl.BlockSpec((1, LANES), fixed), pl.BlockSpec((1, LANES), fixed),
                  pl.BlockSpec((1, LANES), fixed), pl.BlockSpec((1, LANES), fixed),
                  pl.BlockSpec((1, RH, RDK, RDV), st), pl.BlockSpec((1, GH, GDK, GDV), st),
                  pl.BlockSpec((1, CONV_W - 1, CONV_CH), lambda i: (i, 0, 0))],
        out_specs=[pl.BlockSpec((1, 1, D), lambda i: (i, 0, 0)),
                   pl.BlockSpec((1, RH, RDK, RDV), st), pl.BlockSpec((1, GH, GDK, GDV), st),
                   pl.BlockSpec((1, CONV_W - 1, CONV_CH), lambda i: (i, 0, 0))],
        out_shape=[jax.ShapeDtypeStruct((b, 1, D), F32),
                   jax.ShapeDtypeStruct((b, RH, RDK, RDV), F32),
                   jax.ShapeDtypeStruct((b, GH, GDK, GDV), F32),
                   jax.ShapeDtypeStruct((b, CONV_W - 1, CONV_CH), F32)],
        compiler_params=_cp(("parallel",)),
        name="mixer_a_sample",
    )(proj, cos2, sin2, conv_w, alog_row, dt_row, rg_row, gg_row, ret_s, gdn_s, conv_s)


def _fox_cumsum_kernel(x_ref, o_ref):
    n = x_ref.shape[1] // LANES
    upper = (_iota2((LANES, LANES), 0) <= _iota2((LANES, LANES), 1)).astype(BF16)
    carry = jnp.zeros((FH, 1), F32)
    for c in range(n):
        p = _mm_01_right(x_ref[:, c * LANES:(c + 1) * LANES], upper) + carry
        o_ref[0, :, c * LANES:(c + 1) * LANES] = p
        carry = p[:, LANES - 1:LANES]


def _fox_cumsum(lft, b, l):
    return pl.pallas_call(
        _fox_cumsum_kernel,
        grid=(b,),
        in_specs=[pl.BlockSpec((FH, l), lambda i: (0, i))],
        out_specs=pl.BlockSpec((1, FH, l), lambda i: (i, 0, 0)),
        out_shape=jax.ShapeDtypeStruct((b, FH, l), F32),
        compiler_params=_cp(("parallel",)),
        name="fox_cumsum",
    )(lft)


def _fox_flash_kernel(q_ref, k_ref, v_ref, ck_ref, o_ref, sa_ref, sb_ref, *, tq, tk):
    assert tq == 2 * tk
    qi = pl.program_id(2)
    q = q_ref[0]

    def scores(j, s_ref):
        k = k_ref[0, pl.ds(pl.multiple_of(j * tk, tk), tk), :]
        s_ref[...] = lax.dot_general(q, k, NT, preferred_element_type=F32) - ck_ref[0, pl.ds(j, 1), :]

    def update(j, s_ref, carry, masked):
        m_old, l_old, acc = carry
        s = s_ref[...]
        if masked:
            qpos = qi * tq + _iota2((tq, tk), 0)
            kpos = j * tk + _iota2((tq, tk), 1)
            s = jnp.where(kpos <= qpos, s, -jnp.inf)
        v = v_ref[0, pl.ds(pl.multiple_of(j * tk, tk), tk), :]
        m_new = jnp.maximum(m_old, jnp.max(s, axis=1, keepdims=True))
        alpha = jnp.exp(m_old - m_new)
        p = jnp.exp(s - m_new)
        l_new = alpha * l_old + jnp.sum(p, axis=1, keepdims=True)
        acc = alpha * acc + jnp.dot(p.astype(BF16), v, preferred_element_type=F32)
        return m_new, l_new, acc

    def two_blocks(i, carry):
        scores(2 * i + 1, sb_ref)
        carry = update(2 * i, sa_ref, carry, False)
        scores(2 * i + 2, sa_ref)
        return update(2 * i + 1, sb_ref, carry, False)

    scores(0, sa_ref)
    init = (jnp.full((tq, 1), -jnp.inf, F32), jnp.zeros((tq, 1), F32), jnp.zeros((tq, FHD), F32))
    carry = lax.fori_loop(0, qi, two_blocks, init)
    scores(2 * qi + 1, sb_ref)
    carry = update(2 * qi, sa_ref, carry, True)
    carry = update(2 * qi + 1, sb_ref, carry, True)
    o_ref[0] = carry[2] / carry[1]


def _fox_flash(qb, kb, vb, cum, tq, tk):
    b, l, _ = qb.shape
    return pl.pallas_call(
        functools.partial(_fox_flash_kernel, tq=tq, tk=tk),
        grid=(b, FH, l // tq),
        in_specs=[pl.BlockSpec((1, tq, FHD), lambda i, h, qi: (i, qi, h)),
                  pl.BlockSpec((1, l, FHD), lambda i, h, qi: (i, 0, h)),
                  pl.BlockSpec((1, l, FHD), lambda i, h, qi: (i, 0, h)),
                  pl.BlockSpec((1, l // tk, tk), lambda i, h, qi: (i * FH + h, 0, 0))],
        out_specs=pl.BlockSpec((1, tq, FHD), lambda i, h, qi: (i, qi, h)),
        out_shape=jax.ShapeDtypeStruct((b, l, D), F32),
        scratch_shapes=[pltpu.VMEM((tq, tk), F32), pltpu.VMEM((tq, tk), F32)],
        compiler_params=_cp(("parallel", "parallel", "arbitrary")),
        name="fox_flash",
    )(qb, kb, vb, cum)


def _fox_sample_kernel(pt_ref, q_ref, kn_ref, vn_ref, lfn_ref, *rest):
    npp = PAGES_PER_STEP
    k_refs, v_refs, lf_refs = rest[0:npp], rest[npp:2 * npp], rest[2 * npp:3 * npp]
    o_ref, m_ref, l_ref, acc_ref, carry_ref = rest[3 * npp:]
    j = pl.program_id(1)
    scale = FHD ** -0.5
    rows = PAGE * FH

    @pl.when(j == 0)
    def _init():
        m_ref[...] = jnp.full(m_ref.shape, -jnp.inf, F32)
        l_ref[...] = jnp.zeros(l_ref.shape, F32)
        acc_ref[...] = jnp.zeros(acc_ref.shape, F32)
        carry_ref[...] = jnp.zeros(carry_ref.shape, F32)

    q = q_ref[0]
    qb = (q * scale).astype(BF16)
    own = (_iota2((FH, rows), 1) & (FH - 1)) == _iota2((FH, rows), 0)
    nch = rows // LANES
    ci = _iota2((LANES, 2 * LANES), 1)
    pre_tot = ((_iota2((LANES, 2 * LANES), 0) <= ci) | (ci >= LANES)).astype(BF16)
    stacked, s_pages = [], []
    for pg in range(npp):
        kp = k_refs[pg][0].reshape(rows, FHD).astype(BF16)
        s_pages.append(lax.dot_general(qb, kp, NT, preferred_element_type=F32))
        lf = jnp.where(own, jnp.broadcast_to(lf_refs[pg][0], (FH, rows)), 0.0)
        stacked += [lf[:, c * LANES:(c + 1) * LANES] for c in range(nch)]
    pt = _mm_01_right(jnp.concatenate(stacked, axis=0), pre_tot)
    carry = carry_ref[...]
    logits = []
    for pg in range(npp):
        cums = []
        for c in range(nch):
            r0 = (pg * nch + c) * FH
            cums.append(pt[r0:r0 + FH, 0:LANES] + carry)
            carry = carry + pt[r0:r0 + FH, LANES:2 * LANES]
        logits.append(jnp.where(own, s_pages[pg] - jnp.concatenate(cums, axis=1), -jnp.inf))
    carry_ref[...] = carry
    m_old = m_ref[...]
    m_new = m_old
    for lg in logits:
        m_new = jnp.maximum(m_new, jnp.max(lg, axis=1, keepdims=True))
    alpha = jnp.exp(m_old - m_new)
    l_new = alpha * l_ref[...]
    acc = alpha * acc_ref[...]
    for pg in range(npp):
        p = jnp.exp(logits[pg] - m_new)
        l_new = l_new + jnp.sum(p, axis=1, keepdims=True)
        vp = v_refs[pg][0].reshape(rows, FHD).astype(BF16)
        acc = acc + jnp.dot(p.astype(BF16), vp, preferred_element_type=F32)
    m_ref[...] = m_new
    l_ref[...] = l_new
    acc_ref[...] = acc

    @pl.when(j == pl.num_programs(1) - 1)
    def _fin():
        s_new = jnp.sum(q * kn_ref[0], axis=1, keepdims=True) * scale - (carry[:, 0:1] + lfn_ref[0])
        m2 = jnp.maximum(m_new, s_new)
        a2 = jnp.exp(m_new - m2)
        p2 = jnp.exp(s_new - m2)
        o_ref[0] = (a2 * acc + p2 * vn_ref[0]) / (a2 * l_new + p2)


def _fox_sample(page_table, q3, k3, v3, lf3, k_pool, v_pool, lf_pool):
    b = q3.shape[0]
    npg = page_table.shape[1]
    npp = PAGES_PER_STEP
    assert npg % npp == 0
    pt = page_table.reshape(-1)
    tok = lambda i, j, pt_ref: (i, 0, 0)

    def page(r, nd):
        return lambda i, j, pt_ref: (pt_ref[i * npg + j * npp + r],) + (0,) * nd

    grid_spec = pltpu.PrefetchScalarGridSpec(
        num_scalar_prefetch=1,
        grid=(b, npg // npp),
        in_specs=[pl.BlockSpec((1, FH, FHD), tok), pl.BlockSpec((1, FH, FHD), tok), pl.BlockSpec((1, FH, FHD), tok),
                  pl.BlockSpec((1, FH, 1), tok)]
        + [pl.BlockSpec((1, PAGE, FH, FHD), page(r, 3)) for r in range(npp)]
        + [pl.BlockSpec((1, PAGE, FH, FHD), page(r, 3)) for r in range(npp)]
        + [pl.BlockSpec((1, 1, PAGE * FH), page(r, 2)) for r in range(npp)],
        out_specs=pl.BlockSpec((1, FH, FHD), tok),
        scratch_shapes=[pltpu.VMEM((FH, 1), F32), pltpu.VMEM((FH, 1), F32), pltpu.VMEM((FH, FHD), F32),
                        pltpu.VMEM((FH, LANES), F32)],
    )
    lf_flat =lf_pool.reshape(lf_pool.shape[0], 1, PAGE * FH)
    return pl.pallas_call(
        _fox_sample_kernel,
        grid_spec=grid_spec,
        out_shape=jax.ShapeDtypeStruct((b, FH, FHD), F32),
        compiler_params=_cp(("parallel", "arbitrary")),
        name="fox_sample",
    )(pt, q3, k3, v3, lf3, *([k_pool] * npp), *([v_pool] * npp), *([lf_flat] * npp))


_CAND_BLOCKS = [(a, 16 if a == 0 else 8, 16 // (a + 1)) for a in range(8)]


_MARK0 = -(2.0 ** 127)
_MARK_STEP = 2.0 ** 120
_MARK_LIMIT = -(2.0 ** 126)


def _top16(s, iota):
    rank = jnp.full(s.shape, float(PTOPK), F32)
    vals = jnp.zeros((PTOPK, s.shape[1]), F32)
    i16 = _iota2((PTOPK, s.shape[1]), 0)
    for r in range(PTOPK):
        m = jnp.max(s, axis=0, keepdims=True)
        sel = iota == jnp.min(jnp.where(s == m, iota, 1e9), axis=0, keepdims=True)
        rank = jnp.where(sel, float(r), rank)
        s = jnp.where(sel, -jnp.inf, s)
        vals = jnp.where(i16 == r, m, vals)
    return rank, vals


def _top16_fast(s):
    lowest = jnp.min(s)
    vals = jnp.zeros((PTOPK, s.shape[1]), F32)
    i16 = _iota2((PTOPK, s.shape[1]), 0)
    for r in range(PTOPK):
        m = jnp.max(s, axis=0, keepdims=True)
        s = jnp.where(s == m, _MARK0 + r * _MARK_STEP, s)
        vals = jnp.where(i16 == r, m, vals)
    marked = s < _MARK_LIMIT
    rank = jnp.where(marked, (s - _MARK0) * (1.0 / _MARK_STEP), float(PTOPK))
    taken = jnp.sum(jnp.where(marked, 1.0, 0.0), axis=0, keepdims=True)
    return rank, vals, _not_16(taken) | (lowest <= _MARK_LIMIT)


def _pick16(cand, cidx, exact):
    picked = jnp.zeros(cand.shape, F32)
    for _ in range(PTOPK):
        m = jnp.max(cand, axis=0, keepdims=True)
        sel = cand == m
        if exact:
            sel = cidx == jnp.min(jnp.where(sel, cidx, 1e9), axis=0, keepdims=True)
        picked = jnp.where(sel, 1.0, picked)
        cand = jnp.where(sel, -jnp.inf, cand)
    return picked


def _not_16(count):
    return jnp.max(jnp.abs(count - float(PTOPK))) > 0.5


def _route_head(s_t):
    s1 = s_t[0:NKEYS]
    s2 = s_t[NKEYS:2 * NKEYS]
    w = s_t.shape[1]
    iota = _iota2((NKEYS, w), 0).astype(F32)
    rank1, v1, redo1 = _top16_fast(s1)
    rank2, v2, redo2 = _top16_fast(s2)
    rank1, v1, rank2, v2 = lax.cond(redo1 | redo2, lambda: _top16(s1, iota) + _top16(s2, iota),
                                    lambda: (rank1, v1, rank2, v2))
    e1 = jnp.exp(v1 - v1[0:1])
    e2 = jnp.exp(v2 - v2[0:1])
    i8 = _iota2((8, w), 0)
    i8f = i8.astype(F32)
    i16f = _iota2((16, w), 0).astype(F32)
    cand, cidx, cprob = [], [], []
    for a, rows, valid in _CAND_BLOCKS:
        c = v1[a:a + 1] + v2[0:rows]
        pr = e1[a:a + 1] * e2[0:rows]
        if rows == 16:
            ix = i16f
        else:
            ix = i8f + float(a * PTOPK)
            c = jnp.where(i8 < valid, c, -jnp.inf)
        cand.append(c)
        cidx.append(ix)
        cprob.append(pr)
    cand.append(v1[8:16] + v2[0:1])
    cidx.append((i8f + 8.0) * float(PTOPK))
    cprob.append(e1[8:16] * e2[0:1])
    cand = jnp.concatenate(cand, axis=0)
    cidx = jnp.concatenate(cidx, axis=0)
    cprob = jnp.concatenate(cprob, axis=0)
    picked = _pick16(cand, cidx, False)
    picked = lax.cond(_not_16(jnp.sum(picked, axis=0, keepdims=True)),
                      lambda: _pick16(cand, cidx, True), lambda: picked)
    z = jnp.sum(picked * cprob, axis=0, keepdims=True)
    n_low = jnp.zeros((8, w), F32)
    off = 0
    for a, rows, _ in _CAND_BLOCKS:
        cnt = jnp.sum(picked[off:off + rows], axis=0, keepdims=True)
        n_low = jnp.where(i8 == a, cnt, n_low)
        off += rows
    n16 = jnp.concatenate([n_low, picked[off:off + 8]], axis=0)
    n1d = jnp.zeros((NKEYS, w), F32)
    for a in range(PTOPK):
        n1d = jnp.where(rank1 == float(a), n16[a:a + 1], n1d)
    e1d = jnp.exp(s1 - v1[0:1]) * (0.5 / z)
    e2d = jnp.exp(s2 - v2[0:1])
    return rank2, e2d, n1d, e1d


def _peer_kernel(x_ref, wq_ref, keys_ref, u0_ref, un_ref, vt_ref, g_ref, b_ref, o_ref,
                 xb_ref, q_ref, r2_ref, e2_ref, n1_ref, e1_ref, ha_ref, hb_ref, at_ref, acc_ref, *, tm, te, rw):
    j = pl.program_id(1)
    nlg = tm // LANES
    nsub = te // NKEYS

    @pl.when(j == 0)
    def _route():
        xb = x_ref[...].astype(BF16)
        xb_ref[...] = xb
        q = jnp.dot(xb, wq_ref[...], preferred_element_type=F32)
        for h in range(PH):
            q_ref[h] = q[:, h * LANES:(h + 1) * LANES].astype(BF16)
        acc_ref[...] = jnp.zeros(acc_ref.shape, F32)
        ha_ref[:, 0:tm] = lax.dot_general(u0_ref[...], xb, NT, preferred_element_type=F32)

        def body(h, carry):
            for w0 in range(0, tm, rw):
                s_t = lax.dot_general(keys_ref[...], q_ref[h, w0:w0 + rw, :], NT,
                                      preferred_element_type=F32)
                r2, e2, n1, e1 = _route_head(s_t)
                for g in range(rw // LANES):
                    lg = slice(g * LANES, (g + 1) * LANES)
                    r2_ref[h, w0 // LANES + g] = r2[:, lg].astype(GATE_DT)
                    e2_ref[h, w0 // LANES + g] = e2[:, lg].astype(GATE_DT)
                    n1_ref[h, w0 // LANES + g] = n1[:, lg]
                    e1_ref[h, w0 // LANES + g] = e1[:, lg]
            return carry

        lax.fori_loop(0, PH, body, 0)

    def step(h_cur_ref, h_next_ref):
        h_next_ref[:, 0:tm] = lax.dot_general(un_ref[...], xb_ref[...], NT, preferred_element_type=F32)
        zero = jnp.zeros((), GATE_DT)
        pair = 2
        for pp in range(nsub // pair):
            for g in range(nlg):
                lanes = slice(g * LANES, (g + 1) * LANES)
                gates = [jnp.zeros((NKEYS, LANES), GATE_DT) for _ in range(pair)]
                for h in range(PH):
                    r2 = r2_ref[h, g]
                    e2 = e2_ref[h, g]
                    for ii in range(pair):
                        i1 = j * nsub + pp * pair + ii
                        n_row = n1_ref[h, g, pl.ds(i1, 1), :].astype(GATE_DT)
                        e_row = e1_ref[h, g, pl.ds(i1, 1), :].astype(GATE_DT)
                        gates[ii] = gates[ii] + jnp.where(r2 < n_row, e2, zero) * e_row
                for ii in range(pair):
                    r0 = (pp * pair + ii) * NKEYS
                    hb = h_cur_ref[r0:r0 + NKEYS, lanes]
                    act = hb * (1.0 + lax.erf(hb * (2.0 ** -0.5)))
                    at_ref[r0:r0 + NKEYS, lanes] = (act.astype(GATE_DT) * gates[ii]).astype(BF16)
        acc_ref[...] += jnp.dot(vt_ref[0], at_ref[:, 0:tm], preferred_element_type=F32)

    @pl.when(j % 2 == 0)
    def _even():
        step(ha_ref, hb_ref)

    @pl.when(j % 2 == 1)
    def _odd():
        step(hb_ref, ha_ref)

    @pl.when(j == pl.num_programs(1) - 1)
    def _fin():
        o_ref[...] = _layer_norm(ALPHA * x_ref[...] + acc_ref[...].T, g_ref[...], b_ref[...])


def _peer(x, wq, keys, u, vt, g, b, tm, te):
    t = x.shape[0]
    nj = NEXP // te
    fixed = lambda i, j: (0, 0)
    row_pad = LANES if (tm // LANES) % 4 == 0 else 0
    return pl.pallas_call(
        functools.partial(_peer_kernel, tm=tm, te=te, rw=tm),
        grid=(t // tm, nj),
        in_specs=[pl.BlockSpec((tm, D), lambda i, j: (i, 0)),
                  pl.BlockSpec((D, PH * LANES), fixed),
                  pl.BlockSpec((2 * NKEYS, LANES), fixed),
                  pl.BlockSpec((te, D), fixed),
                  pl.BlockSpec((te, D), lambda i, j: (jnp.minimum(j + 1, nj - 1), 0)),
                  pl.BlockSpec((1, D, te), lambda i, j: (j, 0, 0)),
                  pl.BlockSpec((1, D), fixed), pl.BlockSpec((1, D), fixed)],
        out_specs=pl.BlockSpec((tm, D), lambda i, j: (i, 0)),
        out_shape=jax.ShapeDtypeStruct((t, D), F32),
        scratch_shapes=[pltpu.VMEM((tm, D), BF16),
                        pltpu.VMEM((PH, tm, LANES), BF16),
                        pltpu.VMEM((PH, tm // LANES, NKEYS, LANES), GATE_DT),
                        pltpu.VMEM((PH, tm // LANES, NKEYS, LANES), GATE_DT),
                        pltpu.VMEM((PH, tm // LANES, NKEYS, LANES), F32),
                        pltpu.VMEM((PH, tm // LANES, NKEYS, LANES), F32),
                        pltpu.VMEM((te, tm + row_pad), F32),
                        pltpu.VMEM((te, tm + row_pad), F32),
                        pltpu.VMEM((te, tm + row_pad), BF16),
                        pltpu.VMEM((D, tm), F32)],
        compiler_params=_cp(("parallel", "arbitrary")),
        name="peer",
    )(x, wq, keys, u, u, vt, g, b)


def _rope_tables(pos):
    half = RDK // 2
    inv = ROPE_BASE ** (-jnp.arange(half, dtype=F32) / half)
    ang = pos.astype(F32)[:, None] * inv[None, :]
    cos, sin = jnp.cos(ang), jnp.sin(ang)
    return jnp.concatenate([cos, cos], -1), jnp.concatenate([-sin, sin], -1)


def _pad_lanes(v, width=LANES, offset=0):
    return jnp.zeros((1, width), F32).at[0, offset:offset + v.shape[0]].set(v.astype(F32))


def _peer_params(wq, sub_keys, u_tab, v_tab, te):
    half = sub_keys.shape[-1]
    keys = jnp.zeros((2 * NKEYS, LANES), F32)
    keys = keys.at[0:NKEYS, 0:half].set(sub_keys[0]).at[NKEYS:, half:2 * half].set(sub_keys[1])
    vt = v_tab.astype(BF16).reshape(NEXP // te, te, D).transpose(0, 2, 1)
    return wq.astype(BF16), keys.astype(BF16), u_tab.astype(BF16), vt


def _tile(t, pref):
    return pref if t % pref == 0 else LANES


def kernel(x_prompt, x_sample, state_ret, state_gdn, state_gdn_conv, cache_fox_k, cache_fox_v, cache_fox_logf,
           page_table, w_in_a, ret_norm_g, gdn_a_log, gdn_dt_bias, gdn_conv_w, gdn_norm_g, w_out_a, w_in_c,
           fox_b_f, w_out_c, peer_wq, peer_sub_keys, peer_u, peer_v, ln_g, ln_b):
    bp, lp, _ = x_prompt.shape
    bs = x_sample.shape[0]
    tp = bp * lp
    xp = x_prompt.reshape(tp, D)
    xs = x_sample.reshape(bs, D)
    tmp, tms = _tile(tp, 512), _tile(bs, 512)
    te = 1024
    ln = lambda layer, k: (ln_g[layer, k].reshape(1, D), ln_b[layer, k].reshape(1, D))

    w_in = jnp.pad(w_in_a[0], ((0, 0), (0, A_COLS_PAD - A_COLS))).astype(BF16)
    cos_p, sin_p = _rope_tables(jnp.arange(lp))
    cos_s, sin_s = _rope_tables(PAST + jnp.arange(1))
    alog, dtb = _pad_lanes(gdn_a_log[0]), _pad_lanes(gdn_dt_bias[0])
    rg, gg = ret_norm_g[0].reshape(1, RDV), gdn_norm_g[0].reshape(1, GDV)
    proj_p = _proj(xp, w_in, tmp).reshape(bp, lp, A_COLS_PAD)
    proj_s = _proj(xs, w_in, tms).reshape(bs, 1, A_COLS_PAD)
    mix_p, ret_p, gdn_p, conv_p = _mixer_a_prompt(proj_p, cos_p, sin_p, gdn_conv_w[0], alog, dtb, rg, gg)
    mix_s, ret_s, gdn_s, conv_s = _mixer_a_sample(proj_s, cos_s, sin_s, gdn_conv_w[0], alog, dtb, rg, gg,
                                                  state_ret[0], state_gdn[0], state_gdn_conv[0])
    w_out = w_out_a[0].astype(BF16)
    g0, b0 = ln(0, 0)
    xp = _outproj_ln(mix_p.reshape(tp, D), xp, w_out, g0, b0, tmp)
    xs = _outproj_ln(mix_s.reshape(bs, D), xs, w_out, g0, b0, tms)
    pw = _peer_params(peer_wq[0], peer_sub_keys[0], peer_u[0], peer_v[0], te)
    g1, b1 = ln(0, 1)
    xp = _peer(xp, *pw, g1, b1, tmp, te)
    xs = _peer(xs, *pw, g1, b1, tms, te)

    w_c = jnp.pad(w_in_c[0], ((0, 0), (0, C_COLS_PAD - w_in_c.shape[-1]))).astype(BF16)
    bf_row = _pad_lanes(fox_b_f[0])
    kp, vp, lfp, lftp, qbp, kbp, vbp = _fox_proj(xp, w_c, bf_row, tmp, True)
    ks, vs, lfs, _, qs = _fox_proj(xs, w_c, bf_row, tms, False)
    tq, tk = _tile(lp, 1024), _tile(lp, 512)
    cum = _fox_cumsum(lftp, bp, lp).reshape(bp * FH, lp // tk, tk)
    op = _fox_flash(qbp.reshape(bp, lp, D), kbp.reshape(bp, lp, D), vbp.reshape(bp, lp, D), cum, tq, tk)
    os_ = _fox_sample(page_table, qs.reshape(bs, FH, FHD), ks.reshape(bs, FH, FHD), vs.reshape(bs, FH, FHD),
                      lfs.reshape(bs, FH, 1), cache_fox_k[0], cache_fox_v[0], cache_fox_logf[0])
    w_oc = w_out_c[0].astype(BF16)
    g2, b2 = ln(1, 0)
    xp = _outproj_ln(op.reshape(tp, D), xp, w_oc, g2, b2, tmp)
    xs = _outproj_ln(os_.reshape(bs, D), xs, w_oc, g2, b2, tms)
    pw = _peer_params(peer_wq[1], peer_sub_keys[1], peer_u[1], peer_v[1], te)
    g3, b3 = ln(1, 1)
    xp = _peer(xp, *pw, g3, b3, tmp, te)
    xs = _peer(xs, *pw, g3, b3, tms, te)

    return (xp.reshape(bp, lp, D), xs.reshape(bs, 1, D),
            ret_p[None], ret_s[None], gdn_p[None], gdn_s[None], conv_p[None], conv_s[None],
            kp.reshape(1, bp, lp, FH, FHD), ks.reshape(1, bs, 1, FH, FHD),
            vp.reshape(1, bp, lp, FH, FHD), vs.reshape(1, bs, 1, FH, FHD),
            lfp.reshape(1, bp, lp, FH), lfs.reshape(1, bs, 1, FH))
```

```python
import functools
import math

import jax
import jax.numpy as jnp
from jax import lax
from jax.experimental import pallas as pl
from jax.experimental.pallas import tpu as pltpu

F32 = jnp.float32
BF16 = jnp.bfloat16

D = 1024
PAST = 2048
PAGE = 128
RH, RDK, RDV = 4, 128, 128
GH, GDK, GDV = 4, 128, 128
CONV_W = 4
CONV_CH = 2 * GH * GDK + GH * GDV
A_COLS = 4104
A_COLS_PAD = 4224
FH, FHD = 8, 128
C_COLS_PAD = 3200
PH, PTOPK, NKEYS = 8, 16, 128
NEXP = NKEYS * NKEYS
ALPHA = 4.0 ** 0.25
LN_EPS = 1e-5
NORM_EPS = 1e-6
ROPE_BASE = 10000.0
LANES = 128
CHUNK = 128
PAGES_PER_STEP = 8
GATE_DT = BF16
VMEM_LIMIT = 56 * 1024 * 1024

NT = (((1,), (1,)), ((), ()))
TN = (((0,), (0,)), ((), ()))


def _cp(sem):
    return pltpu.CompilerParams(dimension_semantics=sem, vmem_limit_bytes=VMEM_LIMIT)


def _mm(a, b):
    return jnp.dot(a.astype(BF16), b.astype(BF16), preferred_element_type=F32)


def _mm_nt(a, b):
    return lax.dot_general(a.astype(BF16), b.astype(BF16), NT, preferred_element_type=F32)


def _mm_tn(a, b):
    return lax.dot_general(a.astype(BF16), b.astype(BF16), TN, preferred_element_type=F32)


def _split3(x):
    x1 = x.astype(BF16)
    r1 = x - x1.astype(F32)
    x2 = r1.astype(BF16)
    return x1, x2, (r1 - x2.astype(F32)).astype(BF16)


def _mm_01_left(m01, x):
    return sum(jnp.dot(m01, t, preferred_element_type=F32) for t in _split3(x))


def _mm_01_right(x, m01):
    return sum(jnp.dot(t, m01, preferred_element_type=F32) for t in _split3(x))


def _split2(a):
    hi = a.astype(BF16)
    return hi, (a - hi.astype(F32)).astype(BF16)


def _mm_s(a, b):
    return (jnp.dot(a[0], b[0], preferred_element_type=F32) + jnp.dot(a[0], b[1], preferred_element_type=F32)
            + jnp.dot(a[1], b[0], preferred_element_type=F32))


def _sigmoid(x):
    return 1.0 / (1.0 + jnp.exp(-x))


def _silu(x):
    return x * _sigmoid(x)


def _softplus(x):
    return jnp.maximum(x, 0.0) + jnp.log1p(jnp.exp(-jnp.abs(x)))


def _log_sigmoid(x):
    return jnp.minimum(x, 0.0) - jnp.log1p(jnp.exp(-jnp.abs(x)))


def _layer_norm(z, g, b):
    mu = jnp.mean(z, axis=-1, keepdims=True)
    d = z - mu
    var = jnp.mean(d * d, axis=-1, keepdims=True)
    return d * lax.rsqrt(var + LN_EPS) * g + b


def _iota2(shape, axis):
    return lax.broadcasted_iota(jnp.int32, shape, axis)


def _proj_kernel(x_ref, w_ref, o_ref):
    o_ref[...] = jnp.dot(x_ref[...].astype(BF16), w_ref[...], preferred_element_type=F32)


def _proj(x, w, tm):
    t, k = x.shape
    n = w.shape[1]
    return pl.pallas_call(
        _proj_kernel,
        grid=(t // tm,),
        in_specs=[pl.BlockSpec((tm, k), lambda i: (i, 0)), pl.BlockSpec((k, n), lambda i: (0, 0))],
        out_specs=pl.BlockSpec((tm, n), lambda i: (i, 0)),
        out_shape=jax.ShapeDtypeStruct((t, n), F32),
        compiler_params=_cp(("parallel",)),
        name="proj_a",
    )(x, w)


def _fox_proj_kernel(x_ref, w_ref, bf_ref, k_ref, v_ref, lf_ref, lft_ref, *q_refs, attn_copies):
    p = jnp.dot(x_ref[...].astype(BF16), w_ref[...], preferred_element_type=F32)
    q = p[:, 0:D]
    k = p[:, D:2 * D]
    v = p[:, 2 * D:3 * D]
    k_ref[...] = k
    v_ref[...] = v
    lf = _log_sigmoid(p[:, 3 * D:3 * D + LANES] + bf_ref[...])
    lf_ref[...] = lf[:, 0:FH]
    lft_ref[...] = lf.T[0:FH, :]
    if attn_copies:
        qb_ref, kb_ref, vb_ref = q_refs
        qb_ref[...] = (q * (FHD ** -0.5)).astype(BF16)
        kb_ref[...] = k.astype(BF16)
        vb_ref[...] = v.astype(BF16)
    else:
        q_refs[0][...] = q


def _fox_proj(x, w, bf_row, tm, attn_copies):
    t = x.shape[0]
    row = lambda i: (i, 0)
    out_specs = [pl.BlockSpec((tm, D), row), pl.BlockSpec((tm, D), row),
                 pl.BlockSpec((tm, FH), row), pl.BlockSpec((FH, tm), lambda i: (0, i))]
    out_shape = [jax.ShapeDtypeStruct((t, D), F32), jax.ShapeDtypeStruct((t, D), F32),
                 jax.ShapeDtypeStruct((t, FH), F32), jax.ShapeDtypeStruct((FH, t), F32)]
    n_q = 3 if attn_copies else 1
    out_specs += [pl.BlockSpec((tm, D), row)] * n_q
    out_shape += [jax.ShapeDtypeStruct((t, D), BF16 if attn_copies else F32)] * n_q
    return pl.pallas_call(
        functools.partial(_fox_proj_kernel, attn_copies=attn_copies),
        grid=(t // tm,),
        in_specs=[pl.BlockSpec((tm, D), row), pl.BlockSpec((D, C_COLS_PAD), lambda i: (0, 0)),
                  pl.BlockSpec((1, LANES), lambda i: (0, 0))],
        out_specs=out_specs,
        out_shape=out_shape,
        compiler_params=_cp(("parallel",)),
        name="proj_c",
    )(x, w, bf_row)


def _outproj_ln_kernel(a_ref, x_ref, w_ref, g_ref, b_ref, o_ref):
    h = jnp.dot(a_ref[...].astype(BF16), w_ref[...], preferred_element_type=F32)
    o_ref[...] = _layer_norm(ALPHA * x_ref[...] + h, g_ref[...], b_ref[...])


def _outproj_ln(a, x, w, g, b, tm):
    t = x.shape[0]
    row = lambda i: (i, 0)
    fixed = lambda i: (0, 0)
    return pl.pallas_call(
        _outproj_ln_kernel,
        grid=(t // tm,),
        in_specs=[pl.BlockSpec((tm, D), row), pl.BlockSpec((tm, D), row), pl.BlockSpec((D, D), fixed),
                  pl.BlockSpec((1, D), fixed), pl.BlockSpec((1, D), fixed)],
        out_specs=pl.BlockSpec((tm, D), row),
        out_shape=jax.ShapeDtypeStruct((t, D), F32),
        compiler_params=_cp(("parallel",)),
        name="outproj_ln",
    )(a, x, w, g, b)


def _unit_lower_inverses(a_list, ri, ci):
    eye = (ri == ci).astype(F32)
    blk = (ri >> 4) == (ci >> 4)
    d = [_split2(jnp.where(blk, a, 0.0)) for a in a_list]
    d2f = [_mm_s(x, x) for x in d]
    d2 = [_split2(x) for x in d2f]
    d4f = [_mm_s(x, x) for x in d2]
    d4 = [_split2(x) for x in d4f]
    d8 = [_split2(_mm_s(x, x)) for x in d4]
    xs = [eye - jnp.where(blk, a, 0.0) for a in a_list]
    for pw in (d2, d4, d8):
        xs = [x + _mm_s(_split2(x), p) for x, p in zip(xs, pw)]
    for s in (5, 6, 7):
        msk = ((ri >> s) == (ci >> s)) & ((ri >> (s - 1)) != (ci >> (s - 1)))
        es = [_split2(jnp.where(msk, a, 0.0)) for a in a_list]
        xsp = [_split2(x) for x in xs]
        xe = [_split2(_mm_s(x, e)) for x, e in zip(xsp, es)]
        xs = [x - _mm_s(y, xp) for x, y, xp in zip(xs, xe, xsp)]
    return xs


def _head_gate_norm(out, g_row, gate):
    return out * lax.rsqrt(jnp.mean(out * out, axis=-1, keepdims=True) + NORM_EPS) * g_row * _silu(gate)


def _mixer_a_prompt_kernel(p_ref, cos_ref, sin_ref, cw_ref, alog_ref, dt_ref, rg_ref, gg_ref,
                           mix_ref, ret_ref, gdn_ref, conv_ref, ext_ref, *, nb):
    n = pl.program_id(0)
    c = CHUNK

    @pl.when(n == 0)
    def _init():
        ret_ref[...] = jnp.zeros_like(ret_ref)
        gdn_ref[...] = jnp.zeros_like(gdn_ref)
        ext_ref[:, 0:8, :] = jnp.zeros((nb, 8, CONV_CH), F32)

    ri = _iota2((c, c), 0)
    ci = _iota2((c, c), 1)
    rf = ri.astype(F32)
    diff = rf - ci.astype(F32)
    cosv = cos_ref[...]
    sinv = sin_ref[...]

    for h in range(RH):
        lg = math.log1p(-(2.0 ** (-5.0 - h)))
        decay = jnp.where(diff >= 0, jnp.exp(jnp.maximum(diff, 0.0) * lg), 0.0)
        inner = jnp.exp((rf + 1.0) * lg)
        kdec = jnp.exp((c - 1.0 - rf) * lg)
        for b in range(nb):
            rq = p_ref[b, :, h * RDK:(h + 1) * RDK]
            rk = p_ref[b, :, RH * RDK + h * RDK:RH * RDK + (h + 1) * RDK]
            rv = p_ref[b, :, 2 * RH * RDK + h * RDV:2 * RH * RDK + (h + 1) * RDV]
            rgate = p_ref[b, :, 2 * RH * RDK + RH * RDV + h * RDV:2 * RH * RDK + RH * RDV + (h + 1) * RDV]
            q = rq * cosv + pltpu.roll(rq, RDK // 2, 1) * sinv
            k = (rk * cosv + pltpu.roll(rk, RDK // 2, 1) * sinv) * (RDK ** -0.5)
            s = ret_ref[b, h]
            scores = _mm_nt(q, k) * decay
            out = _mm(scores, rv) + _mm(q * inner, s)
            ret_ref[b, h] = math.exp(c * lg) * s + _mm_tn(k * kdec, rv)
            mix_ref[b, :, h * RDV:(h + 1) * RDV] = _head_gate_norm(out, rg_ref[...], rgate)

    g_off = 2 * RH * RDK + 2 * RH * RDV
    z_off = g_off + CONV_CH
    incl = ri >= ci
    strict = ri > ci
    tril = incl.astype(BF16)
    units = []
    for b in range(nb):
        ext_ref[b, 8:8 + c, :] = p_ref[b, :, g_off:g_off + CONV_CH]
        y = ext_ref[b, 5:5 + c, :] * cw_ref[0:1, :]
        for i in range(1, CONV_W):
            y = y + ext_ref[b, 5 + i:5 + i + c, :] * cw_ref[i:i + 1, :]
        conv_ref[b] = ext_ref[b, c + 5:c + 8, :]
        ext_ref[b, 0:8, :] = ext_ref[b, c:c + 8, :]
        y = _silu(y)
        tail = p_ref[b, :, z_off + GH * GDV:z_off + GH * GDV + LANES]
        g_all = -jnp.exp(alog_ref[...]) * _softplus(tail + dt_ref[...])
        beta_all = _sigmoid(tail)
        for h in range(GH):
            qh = y[:, h * GDK:(h + 1) * GDK]
            kh = y[:, GH * GDK + h * GDK:GH * GDK + (h + 1) * GDK]
            vh = y[:, 2 * GH * GDK + h * GDV:2 * GH * GDK + (h + 1) * GDV]
            qh = qh * lax.rsqrt(jnp.sum(qh * qh, axis=-1, keepdims=True) + NORM_EPS) * (GDK ** -0.5)
            kh = kh * lax.rsqrt(jnp.sum(kh * kh, axis=-1, keepdims=True) + NORM_EPS)
            gb = jnp.broadcast_to(g_all[:, h:h + 1], (c, c))
            bcol = beta_all[:, GH + h:GH + h + 1]
            units.append((b, h, qh, kh, vh, gb, bcol))

    gcs = [_mm_01_left(tril, u[5]) for u in units]
    decs = []
    for gc in gcs:
        dmat = gc - gc.T
        decs.append(jnp.where(incl, jnp.exp(jnp.where(incl, dmat, 0.0)), 0.0))
    kbs = [u[3] * u[6] for u in units]
    a_list = [jnp.where(strict, _mm_nt(kb, u[3]) * dec, 0.0) for kb, u, dec in zip(kbs, units, decs)]
    ts = [t.astype(BF16) for t in _unit_lower_inverses(a_list, ri, ci)]
    egcs = [jnp.exp(gc) for gc in gcs]
    us = [jnp.dot(t, (u[4] * u[6]).astype(BF16), preferred_element_type=F32) for t, u in zip(ts, units)]
    ws = [jnp.dot(t, (kb * egc).astype(BF16), preferred_element_type=F32) for t, kb, egc in zip(ts, kbs, egcs)]
    attns = [_mm_nt(u[2], u[3]) * dec for u, dec in zip(units, decs)]
    ss = [gdn_ref[u[0], u[1]] for u in units]
    v_news = [uu - _mm(w, s) for uu, w, s in zip(us, ws, ss)]
    outs = [_mm(u[2] * egc, s) + _mm(attn, vn) for u, egc, s, attn, vn in zip(units, egcs, ss, attns, v_news)]
    for u, gc, s, vn, out in zip(units, gcs, ss, v_news, outs):
        b, h = u[0], u[1]
        g_last = gc[c - 1:c, :]
        gdn_ref[b, h] = jnp.exp(g_last) * s + _mm_tn(u[3] * jnp.exp(g_last - gc), vn)
        gz = p_ref[b, :, z_off + h * GDV:z_off + (h + 1) * GDV]
        mix_ref[b, :, RH * RDV + h * GDV:RH * RDV + (h + 1) * GDV] = _head_gate_norm(out, gg_ref[...], gz)


def _mixer_a_prompt(proj, cos2, sin2, conv_w, alog_row, dt_row, rg_row, gg_row):
    b, l, _ = proj.shape
    fixed = lambda n: (0, 0)
    return pl.pallas_call(
        functools.partial(_mixer_a_prompt_kernel, nb=b),
        grid=(l // CHUNK,),
        in_specs=[pl.BlockSpec((b, CHUNK, A_COLS_PAD), lambda n: (0, n, 0)),
                  pl.BlockSpec((CHUNK, LANES), lambda n: (n, 0)),
                  pl.BlockSpec((CHUNK, LANES), lambda n: (n, 0)),
                  pl.BlockSpec((CONV_W, CONV_CH), fixed),
                  pl.BlockSpec((1, LANES), fixed), pl.BlockSpec((1, LANES), fixed),
                  pl.BlockSpec((1, LANES), fixed), pl.BlockSpec((1, LANES), fixed)],
        out_specs=[pl.BlockSpec((b, CHUNK, D), lambda n: (0, n, 0)),
                   pl.BlockSpec((b, RH, RDK, RDV), lambda n: (0, 0, 0, 0)),
                   pl.BlockSpec((b, GH, GDK, GDV), lambda n: (0, 0, 0, 0)),
                   pl.BlockSpec((b, CONV_W - 1, CONV_CH), lambda n: (0, 0, 0))],
        out_shape=[jax.ShapeDtypeStruct((b, l, D), F32),
                   jax.ShapeDtypeStruct((b, RH, RDK, RDV), F32),
                   jax.ShapeDtypeStruct((b, GH, GDK, GDV), F32),
                   jax.ShapeDtypeStruct((b, CONV_W - 1, CONV_CH), F32)],
        scratch_shapes=[pltpu.VMEM((b, CHUNK + 8, CONV_CH), F32)],
        compiler_params=_cp(("arbitrary",)),
        name="mixer_a_prompt",
    )(proj, cos2, sin2, conv_w, alog_row, dt_row, rg_row, gg_row)


def _mixer_a_sample_kernel(p_ref, cos_ref, sin_ref, cw_ref, alog_ref, dt_ref, rg_ref, gg_ref,
                           rs_ref, gs_ref, cs_ref, mix_ref, ret_ref, gdn_ref, conv_ref):
    eye = _iota2((LANES, LANES), 0) == _iota2((LANES, LANES), 1)

    def col(v):
        return jnp.sum(jnp.where(eye, jnp.broadcast_to(v, (LANES, LANES)), 0.0), axis=1, keepdims=True)

    def vec_mat(c, s):
        return jnp.sum(c * s, axis=0, keepdims=True)

    cosv = cos_ref[...]
    sinv = sin_ref[...]
    for h in range(RH):
        gamma = 1.0 - 2.0 ** (-5.0 - h)
        rq = p_ref[0, :, h * RDK:(h + 1) * RDK]
        rk = p_ref[0, :, RH * RDK + h * RDK:RH * RDK + (h + 1) * RDK]
        rv = p_ref[0, :, 2 * RH * RDK + h * RDV:2 * RH * RDK + (h + 1) * RDV]
        rgate = p_ref[0, :, 2 * RH * RDK + RH * RDV + h * RDV:2 * RH * RDK + RH * RDV + (h + 1) * RDV]
        q = rq * cosv + pltpu.roll(rq, RDK // 2, 1) * sinv
        k = (rk * cosv + pltpu.roll(rk, RDK // 2, 1) * sinv) * (RDK ** -0.5)
        s = rs_ref[0, h]
        qk = jnp.sum(q * k, axis=-1, keepdims=True)
        out = qk * rv + gamma * vec_mat(col(q), s)
        ret_ref[0, h] = gamma * s + col(k) * rv
        mix_ref[0, :, h * RDV:(h + 1) * RDV] = _head_gate_norm(out, rg_ref[...], rgate)

    g_off = 2 * RH * RDK + 2 * RH * RDV
    gq_all = p_ref[0, :, g_off:g_off + CONV_CH]
    cb = cs_ref[0]
    y = cb[0:1, :] * cw_ref[0:1, :] + cb[1:2, :] * cw_ref[1:2, :] + cb[2:3, :] * cw_ref[2:3, :] \
        + gq_all * cw_ref[3:4, :]
    conv_ref[0, 0:2, :] = cb[1:3, :]
    conv_ref[0, 2:3, :] = gq_all
    y = _silu(y)
    z_off = g_off + CONV_CH
    tail = p_ref[0, :, z_off + GH * GDV:z_off + GH * GDV + LANES]
    g_all = -jnp.exp(alog_ref[...]) * _softplus(tail + dt_ref[...])
    beta_all = _sigmoid(tail)
    for h in range(GH):
        qh = y[:, h * GDK:(h + 1) * GDK]
        kh = y[:, GH * GDK + h * GDK:GH * GDK + (h + 1) * GDK]
        vh = y[:, 2 * GH * GDK + h * GDV:2 * GH * GDK + (h + 1) * GDV]
        qh = qh * lax.rsqrt(jnp.sum(qh * qh, axis=-1, keepdims=True) + NORM_EPS) * (GDK ** -0.5)
        kh = kh * lax.rsqrt(jnp.sum(kh * kh, axis=-1, keepdims=True) + NORM_EPS)
        eg = jnp.exp(g_all[:, h:h + 1])
        beta = beta_all[:, GH + h:GH + h + 1]
        s = gs_ref[0, h]
        kc = col(kh)
        v_new = vh * beta - vec_mat(kc * (beta * eg), s)
        qk = jnp.sum(qh * kh, axis=-1, keepdims=True)
        out = eg * vec_mat(col(qh), s) + qk * v_new
        gdn_ref[0, h] = eg * s + kc * v_new
        gz = p_ref[0, :, z_off + h * GDV:z_off + (h + 1) * GDV]
        mix_ref[0, :, RH * RDV + h * GDV:RH * RDV + (h + 1) * GDV] = _head_gate_norm(out, gg_ref[...], gz)


def _mixer_a_sample(proj, cos2, sin2, conv_w, alog_row, dt_row, rg_row, gg_row, ret_s, gdn_s, conv_s):
    b = proj.shape[0]
    fixed = lambda i: (0, 0)
    st = lambda i: (i, 0, 0, 0)
    return pl.pallas_call(
        _mixer_a_sample_kernel,
        grid=(b,),
        in_specs=[pl.BlockSpec((1, 1, A_COLS_PAD), lambda i: (i, 0, 0)),
                  pl.BlockSpec((1, LANES), fixed), pl.BlockSpec((1, LANES), fixed),
                  pl.BlockSpec((CONV_W, CONV_CH), fixed),
                  pl.BlockSpec((1, LANES), fixed), pl.BlockSpec((1, LANES), fixed),
                  pl.BlockSpec((1, LANES), fixed), pl.BlockSpec((1, LANES), fixed),
                  pl.BlockSpec((1, RH, RDK, RDV), st), pl.BlockSpec((1, GH, GDK, GDV), st),
                  pl.BlockSpec((1, CONV_W - 1, CONV_CH), lambda i: (i, 0, 0))],
        out_specs=[pl.BlockSpec((1, 1, D), lambda i: (i, 0, 0)),
                   pl.BlockSpec((1, RH, RDK, RDV), st), pl.BlockSpec((1, GH, GDK, GDV), st),
                   pl.BlockSpec((1, CONV_W - 1, CONV_CH), lambda i: (i, 0, 0))],
        out_shape=[jax.ShapeDtypeStruct((b, 1, D), F32),
                   jax.ShapeDtypeStruct((b, RH, RDK, RDV), F32),
                   jax.ShapeDtypeStruct((b, GH, GDK, GDV), F32),
                   jax.ShapeDtypeStruct((b, CONV_W - 1, CONV_CH), F32)],
        compiler_params=_cp(("parallel",)),
        name="mixer_a_sample",
    )(proj, cos2, sin2, conv_w, alog_row, dt_row, rg_row, gg_row, ret_s, gdn_s, conv_s)


def _fox_cumsum_kernel(x_ref, o_ref):
    n = x_ref.shape[1] // LANES
    upper = (_iota2((LANES, LANES), 0) <= _iota2((LANES, LANES), 1)).astype(BF16)
    carry = jnp.zeros((FH, 1), F32)
    for c in range(n):
        p = _mm_01_right(x_ref[:, c * LANES:(c + 1) * LANES], upper) + carry
        o_ref[0, :, c * LANES:(c + 1) * LANES] = p
        carry = p[:, LANES - 1:LANES]


def _fox_cumsum(lft, b, l):
    return pl.pallas_call(
        _fox_cumsum_kernel,
        grid=(b,),
        in_specs=[pl.BlockSpec((FH, l), lambda i: (0, i))],
        out_specs=pl.BlockSpec((1, FH, l), lambda i: (i, 0, 0)),
        out_shape=jax.ShapeDtypeStruct((b, FH, l), F32),
        compiler_params=_cp(("parallel",)),
        name="fox_cumsum",
    )(lft)


def _fox_flash_kernel(q_ref, k_ref, v_ref, ck_ref, o_ref, sa_ref, sb_ref, *, tq, tk):
    assert tq == 2 * tk
    qi = pl.program_id(2)
    q = q_ref[0]

    def scores(j, s_ref):
        k = k_ref[0, pl.ds(pl.multiple_of(j * tk, tk), tk), :]
        s_ref[...] = lax.dot_general(q, k, NT, preferred_element_type=F32) - ck_ref[0, pl.ds(j, 1), :]

    def update(j, s_ref, carry, masked):
        m_old, l_old, acc = carry
        s = s_ref[...]
        if masked:
            qpos = qi * tq + _iota2((tq, tk), 0)
            kpos = j * tk + _iota2((tq, tk), 1)
            s = jnp.where(kpos <= qpos, s, -jnp.inf)
        v = v_ref[0, pl.ds(pl.multiple_of(j * tk, tk), tk), :]
        m_new = jnp.maximum(m_old, jnp.max(s, axis=1, keepdims=True))
        alpha = jnp.exp(m_old - m_new)
        p = jnp.exp(s - m_new)
        l_new = alpha * l_old + jnp.sum(p, axis=1, keepdims=True)
        acc = alpha * acc + jnp.dot(p.astype(BF16), v, preferred_element_type=F32)
        return m_new, l_new, acc

    def two_blocks(i, carry):
        scores(2 * i + 1, sb_ref)
        carry = update(2 * i, sa_ref, carry, False)
        scores(2 * i + 2, sa_ref)
        return update(2 * i + 1, sb_ref, carry, False)

    scores(0, sa_ref)
    init = (jnp.full((tq, 1), -jnp.inf, F32), jnp.zeros((tq, 1), F32), jnp.zeros((tq, FHD), F32))
    carry = lax.fori_loop(0, qi, two_blocks, init)
    scores(2 * qi + 1, sb_ref)
    carry = update(2 * qi, sa_ref, carry, True)
    carry = update(2 * qi + 1, sb_ref, carry, True)
    o_ref[0] = carry[2] / carry[1]


def _fox_flash(qb, kb, vb, cum, tq, tk):
    b, l, _ = qb.shape
    return pl.pallas_call(
        functools.partial(_fox_flash_kernel, tq=tq, tk=tk),
        grid=(b, FH, l // tq),
        in_specs=[pl.BlockSpec((1, tq, FHD), lambda i, h, qi: (i, qi, h)),
                  pl.BlockSpec((1, l, FHD), lambda i, h, qi: (i, 0, h)),
                  pl.BlockSpec((1, l, FHD), lambda i, h, qi: (i, 0, h)),
                  pl.BlockSpec((1, l // tk, tk), lambda i, h, qi: (i * FH + h, 0, 0))],
        out_specs=pl.BlockSpec((1, tq, FHD), lambda i, h, qi: (i, qi, h)),
        out_shape=jax.ShapeDtypeStruct((b, l, D), F32),
        scratch_shapes=[pltpu.VMEM((tq, tk), F32), pltpu.VMEM((tq, tk), F32)],
        compiler_params=_cp(("parallel", "parallel", "arbitrary")),
        name="fox_flash",
    )(qb, kb, vb, cum)


def _fox_sample_kernel(pt_ref, q_ref, kn_ref, vn_ref, lfn_ref, *rest):
    npp = PAGES_PER_STEP
    k_refs, v_refs, lf_refs = rest[0:npp], rest[npp:2 * npp], rest[2 * npp:3 * npp]
    o_ref, m_ref, l_ref, acc_ref, carry_ref = rest[3 * npp:]
    j = pl.program_id(1)
    scale = FHD ** -0.5
    rows = PAGE * FH

    @pl.when(j == 0)
    def _init():
        m_ref[...] = jnp.full(m_ref.shape, -jnp.inf, F32)
        l_ref[...] = jnp.zeros(l_ref.shape, F32)
        acc_ref[...] = jnp.zeros(acc_ref.shape, F32)
        carry_ref[...] = jnp.zeros(carry_ref.shape, F32)

    q = q_ref[0]
    qb = (q * scale).astype(BF16)
    own = (_iota2((FH, rows), 1) & (FH - 1)) == _iota2((FH, rows), 0)
    nch = rows // LANES
    ci = _iota2((LANES, 2 * LANES), 1)
    pre_tot = ((_iota2((LANES, 2 * LANES), 0) <= ci) | (ci >= LANES)).astype(BF16)
    stacked, s_pages = [], []
    for pg in range(npp):
        kp = k_refs[pg][0].reshape(rows, FHD).astype(BF16)
        s_pages.append(lax.dot_general(qb, kp, NT, preferred_element_type=F32))
        lf = jnp.where(own, jnp.broadcast_to(lf_refs[pg][0], (FH, rows)), 0.0)
        stacked += [lf[:, c * LANES:(c + 1) * LANES] for c in range(nch)]
    pt = _mm_01_right(jnp.concatenate(stacked, axis=0), pre_tot)
    carry = carry_ref[...]
    logits = []
    for pg in range(npp):
        cums = []
        for c in range(nch):
            r0 = (pg * nch + c) * FH
            cums.append(pt[r0:r0 + FH, 0:LANES] + carry)
            carry = carry + pt[r0:r0 + FH, LANES:2 * LANES]
        logits.append(jnp.where(own, s_pages[pg] - jnp.concatenate(cums, axis=1), -jnp.inf))
    carry_ref[...] = carry
    m_old = m_ref[...]
    m_new = m_old
    for lg in logits:
        m_new = jnp.maximum(m_new, jnp.max(lg, axis=1, keepdims=True))
    alpha = jnp.exp(m_old - m_new)
    l_new = alpha * l_ref[...]
    acc = alpha * acc_ref[...]
    for pg in range(npp):
        p = jnp.exp(logits[pg] - m_new)
        l_new = l_new + jnp.sum(p, axis=1, keepdims=True)
        vp = v_refs[pg][0].reshape(rows, FHD).astype(BF16)
        acc = acc + jnp.dot(p.astype(BF16), vp, preferred_element_type=F32)
    m_ref[...] = m_new
    l_ref[...] = l_new
    acc_ref[...] = acc

    @pl.when(j == pl.num_programs(1) - 1)
    def _fin():
        s_new = jnp.sum(q * kn_ref[0], axis=1, keepdims=True) * scale - (carry[:, 0:1] + lfn_ref[0])
        m2 = jnp.maximum(m_new, s_new)
        a2 = jnp.exp(m_new - m2)
        p2 = jnp.exp(s_new - m2)
        o_ref[0] = (a2 * acc + p2 * vn_ref[0]) / (a2 * l_new + p2)


def _fox_sample(page_table, q3, k3, v3, lf3, k_pool, v_pool, lf_pool):
    b = q3.shape[0]
    npg = page_table.shape[1]
    npp = PAGES_PER_STEP
    assert npg % npp == 0
    pt = page_table.reshape(-1)
    tok = lambda i, j, pt_ref: (i, 0, 0)

    def page(r, nd):
        return lambda i, j, pt_ref: (pt_ref[i * npg + j * npp + r],) + (0,) * nd

    grid_spec = pltpu.PrefetchScalarGridSpec(
        num_scalar_prefetch=1,
        grid=(b, npg // npp),
        in_specs=[pl.BlockSpec((1, FH, FHD), tok), pl.BlockSpec((1, FH, FHD), tok), pl.BlockSpec((1, FH, FHD), tok),
                  pl.BlockSpec((1, FH, 1), tok)]
        + [pl.BlockSpec((1, PAGE, FH, FHD), page(r, 3)) for r in range(npp)]
        + [pl.BlockSpec((1, PAGE, FH, FHD), page(r, 3)) for r in range(npp)]
        + [pl.BlockSpec((1, 1, PAGE * FH), page(r, 2)) for r in range(npp)],
        out_specs=pl.BlockSpec((1, FH, FHD), tok),
        scratch_shapes=[pltpu.VMEM((FH, 1), F32), pltpu.VMEM((FH, 1), F32), pltpu.VMEM((FH, FHD), F32),
                        pltpu.VMEM((FH, LANES), F32)],
    )
    lf_flat =lf_pool.reshape(lf_pool.shape[0], 1, PAGE * FH)
    return pl.pallas_call(
        _fox_sample_kernel,
        grid_spec=grid_spec,
        out_shape=jax.ShapeDtypeStruct((b, FH, FHD), F32),
        compiler_params=_cp(("parallel", "arbitrary")),
        name="fox_sample",
    )(pt, q3, k3, v3, lf3, *([k_pool] * npp), *([v_pool] * npp), *([lf_flat] * npp))


_CAND_BLOCKS = [(a, 16 if a == 0 else 8, 16 // (a + 1)) for a in range(8)]


_MARK0 = -(2.0 ** 127)
_MARK_STEP = 2.0 ** 120
_MARK_LIMIT = -(2.0 ** 126)


def _top16(s, iota):
    rank = jnp.full(s.shape, float(PTOPK), F32)
    vals = jnp.zeros((PTOPK, s.shape[1]), F32)
    i16 = _iota2((PTOPK, s.shape[1]), 0)
    for r in range(PTOPK):
        m = jnp.max(s, axis=0, keepdims=True)
        sel = iota == jnp.min(jnp.where(s == m, iota, 1e9), axis=0, keepdims=True)
        rank = jnp.where(sel, float(r), rank)
        s = jnp.where(sel, -jnp.inf, s)
        vals = jnp.where(i16 == r, m, vals)
    return rank, vals


def _top16_fast(s):
    lowest = jnp.min(s)
    vals = jnp.zeros((PTOPK, s.shape[1]), F32)
    i16 = _iota2((PTOPK, s.shape[1]), 0)
    for r in range(PTOPK):
        m = jnp.max(s, axis=0, keepdims=True)
        s = jnp.where(s == m, _MARK0 + r * _MARK_STEP, s)
        vals = jnp.where(i16 == r, m, vals)
    marked = s < _MARK_LIMIT
    rank = jnp.where(marked, (s - _MARK0) * (1.0 / _MARK_STEP), float(PTOPK))
    taken = jnp.sum(jnp.where(marked, 1.0, 0.0), axis=0, keepdims=True)
    return rank, vals, _not_16(taken) | (lowest <= _MARK_LIMIT)


def _pick16(cand, cidx, exact):
    picked = jnp.zeros(cand.shape, F32)
    for _ in range(PTOPK):
        m = jnp.max(cand, axis=0, keepdims=True)
        sel = cand == m
        if exact:
            sel = cidx == jnp.min(jnp.where(sel, cidx, 1e9), axis=0, keepdims=True)
        picked = jnp.where(sel, 1.0, picked)
        cand = jnp.where(sel, -jnp.inf, cand)
    return picked


def _not_16(count):
    return jnp.max(jnp.abs(count - float(PTOPK))) > 0.5


def _route_head(s_t):
    s1 = s_t[0:NKEYS]
    s2 = s_t[NKEYS:2 * NKEYS]
    w = s_t.shape[1]
    iota = _iota2((NKEYS, w), 0).astype(F32)
    rank1, v1, redo1 = _top16_fast(s1)
    rank2, v2, redo2 = _top16_fast(s2)
    rank1, v1, rank2, v2 = lax.cond(redo1 | redo2, lambda: _top16(s1, iota) + _top16(s2, iota),
                                    lambda: (rank1, v1, rank2, v2))
    e1 = jnp.exp(v1 - v1[0:1])
    e2 = jnp.exp(v2 - v2[0:1])
    i8 = _iota2((8, w), 0)
    i8f = i8.astype(F32)
    i16f = _iota2((16, w), 0).astype(F32)
    cand, cidx, cprob = [], [], []
    for a, rows, valid in _CAND_BLOCKS:
        c = v1[a:a + 1] + v2[0:rows]
        pr = e1[a:a + 1] * e2[0:rows]
        if rows == 16:
            ix = i16f
        else:
            ix = i8f + float(a * PTOPK)
            c = jnp.where(i8 < valid, c, -jnp.inf)
        cand.append(c)
        cidx.append(ix)
        cprob.append(pr)
    cand.append(v1[8:16] + v2[0:1])
    cidx.append((i8f + 8.0) * float(PTOPK))
    cprob.append(e1[8:16] * e2[0:1])
    cand = jnp.concatenate(cand, axis=0)
    cidx = jnp.concatenate(cidx, axis=0)
    cprob = jnp.concatenate(cprob, axis=0)
    picked = _pick16(cand, cidx, False)
    picked = lax.cond(_not_16(jnp.sum(picked, axis=0, keepdims=True)),
                      lambda: _pick16(cand, cidx, True), lambda: picked)
    z = jnp.sum(picked * cprob, axis=0, keepdims=True)
    n_low = jnp.zeros((8, w), F32)
    off = 0
    for a, rows, _ in _CAND_BLOCKS:
        cnt = jnp.sum(picked[off:off + rows], axis=0, keepdims=True)
        n_low = jnp.where(i8 == a, cnt, n_low)
        off += rows
    n16 = jnp.concatenate([n_low, picked[off:off + 8]], axis=0)
    n1d = jnp.zeros((NKEYS, w), F32)
    for a in range(PTOPK):
        n1d = jnp.where(rank1 == float(a), n16[a:a + 1], n1d)
    e1d = jnp.exp(s1 - v1[0:1]) * (0.5 / z)
    e2d = jnp.exp(s2 - v2[0:1])
    return rank2, e2d, n1d, e1d


def _peer_kernel(x_ref, wq_ref, keys_ref, u0_ref, un_ref, vt_ref, g_ref, b_ref, o_ref,
                 xb_ref, q_ref, r2_ref, e2_ref, n1_ref, e1_ref, ha_ref, hb_ref, at_ref, acc_ref, *, tm, te, rw):
    j = pl.program_id(1)
    nlg = tm // LANES
    nsub = te // NKEYS

    @pl.when(j == 0)
    def _route():
        xb = x_ref[...].astype(BF16)
        xb_ref[...] = xb
        q = jnp.dot(xb, wq_ref[...], preferred_element_type=F32)
        for h in range(PH):
            q_ref[h] = q[:, h * LANES:(h + 1) * LANES].astype(BF16)
        acc_ref[...] = jnp.zeros(acc_ref.shape, F32)
        ha_ref[:, 0:tm] = lax.dot_general(u0_ref[...], xb, NT, preferred_element_type=F32)

        def body(h, carry):
            for w0 in range(0, tm, rw):
                s_t = lax.dot_general(keys_ref[...], q_ref[h, w0:w0 + rw, :], NT,
                                      preferred_element_type=F32)
                r2, e2, n1, e1 = _route_head(s_t)
                for g in range(rw // LANES):
                    lg = slice(g * LANES, (g + 1) * LANES)
                    r2_ref[h, w0 // LANES + g] = r2[:, lg].astype(GATE_DT)
                    e2_ref[h, w0 // LANES + g] = e2[:, lg].astype(GATE_DT)
                    n1_ref[h, w0 // LANES + g] = n1[:, lg]
                    e1_ref[h, w0 // LANES + g] = e1[:, lg]
            return carry

        lax.fori_loop(0, PH, body, 0)

    def step(h_cur_ref, h_next_ref):
        h_next_ref[:, 0:tm] = lax.dot_general(un_ref[...], xb_ref[...], NT, preferred_element_type=F32)
        zero = jnp.zeros((), GATE_DT)
        pair = 2
        for pp in range(nsub // pair):
            for g in range(nlg):
                lanes = slice(g * LANES, (g + 1) * LANES)
                gates = [jnp.zeros((NKEYS, LANES), GATE_DT) for _ in range(pair)]
                for h in range(PH):
                    r2 = r2_ref[h, g]
                    e2 = e2_ref[h, g]
                    for ii in range(pair):
                        i1 = j * nsub + pp * pair + ii
                        n_row = n1_ref[h, g, pl.ds(i1, 1), :].astype(GATE_DT)
                        e_row = e1_ref[h, g, pl.ds(i1, 1), :].astype(GATE_DT)
                        gates[ii] = gates[ii] + jnp.where(r2 < n_row, e2, zero) * e_row
                for ii in range(pair):
                    r0 = (pp * pair + ii) * NKEYS
                    hb = h_cur_ref[r0:r0 + NKEYS, lanes]
                    act = hb * (1.0 + lax.erf(hb * (2.0 ** -0.5)))
                    at_ref[r0:r0 + NKEYS, lanes] = (act.astype(GATE_DT) * gates[ii]).astype(BF16)
        acc_ref[...] += jnp.dot(vt_ref[0], at_ref[:, 0:tm], preferred_element_type=F32)

    @pl.when(j % 2 == 0)
    def _even():
        step(ha_ref, hb_ref)

    @pl.when(j % 2 == 1)
    def _odd():
        step(hb_ref, ha_ref)

    @pl.when(j == pl.num_programs(1) - 1)
    def _fin():
        o_ref[...] = _layer_norm(ALPHA * x_ref[...] + acc_ref[...].T, g_ref[...], b_ref[...])


def _peer(x, wq, keys, u, vt, g, b, tm, te):
    t = x.shape[0]
    nj = NEXP // te
    fixed = lambda i, j: (0, 0)
    row_pad = LANES if (tm // LANES) % 4 == 0 else 0
    return pl.pallas_call(
        functools.partial(_peer_kernel, tm=tm, te=te, rw=tm),
        grid=(t // tm, nj),
        in_specs=[pl.BlockSpec((tm, D), lambda i, j: (i, 0)),
                  pl.BlockSpec((D, PH * LANES), fixed),
                  pl.BlockSpec((2 * NKEYS, LANES), fixed),
                  pl.BlockSpec((te, D), fixed),
                  pl.BlockSpec((te, D), lambda i, j: (jnp.minimum(j + 1, nj - 1), 0)),
                  pl.BlockSpec((1, D, te), lambda i, j: (j, 0, 0)),
                  pl.BlockSpec((1, D), fixed), pl.BlockSpec((1, D), fixed)],
        out_specs=pl.BlockSpec((tm, D), lambda i, j: (i, 0)),
        out_shape=jax.ShapeDtypeStruct((t, D), F32),
        scratch_shapes=[pltpu.VMEM((tm, D), BF16),
                        pltpu.VMEM((PH, tm, LANES), BF16),
                        pltpu.VMEM((PH, tm // LANES, NKEYS, LANES), GATE_DT),
                        pltpu.VMEM((PH, tm // LANES, NKEYS, LANES), GATE_DT),
                        pltpu.VMEM((PH, tm // LANES, NKEYS, LANES), F32),
                        pltpu.VMEM((PH, tm // LANES, NKEYS, LANES), F32),
                        pltpu.VMEM((te, tm + row_pad), F32),
                        pltpu.VMEM((te, tm + row_pad), F32),
                        pltpu.VMEM((te, tm + row_pad), BF16),
                        pltpu.VMEM((D, tm), F32)],
        compiler_params=_cp(("parallel", "arbitrary")),
        name="peer",
    )(x, wq, keys, u, u, vt, g, b)


def _rope_tables(pos):
    half = RDK // 2
    inv = ROPE_BASE ** (-jnp.arange(half, dtype=F32) / half)
    ang = pos.astype(F32)[:, None] * inv[None, :]
    cos, sin = jnp.cos(ang), jnp.sin(ang)
    return jnp.concatenate([cos, cos], -1), jnp.concatenate([-sin, sin], -1)


def _pad_lanes(v, width=LANES, offset=0):
    return jnp.zeros((1, width), F32).at[0, offset:offset + v.shape[0]].set(v.astype(F32))


def _peer_params(wq, sub_keys, u_tab, v_tab, te):
    half = sub_keys.shape[-1]
    keys = jnp.zeros((2 * NKEYS, LANES), F32)
    keys = keys.at[0:NKEYS, 0:half].set(sub_keys[0]).at[NKEYS:, half:2 * half].set(sub_keys[1])
    vt = v_tab.astype(BF16).reshape(NEXP // te, te, D).transpose(0, 2, 1)
    return wq.astype(BF16), keys.astype(BF16), u_tab.astype(BF16), vt


def _tile(t, pref):
    return pref if t % pref == 0 else LANES


def kernel(x_prompt, x_sample, state_ret, state_gdn, state_gdn_conv, cache_fox_k, cache_fox_v, cache_fox_logf,
           page_table, w_in_a, ret_norm_g, gdn_a_log, gdn_dt_bias, gdn_conv_w, gdn_norm_g, w_out_a, w_in_c,
           fox_b_f, w_out_c, peer_wq, peer_sub_keys, peer_u, peer_v, ln_g, ln_b):
    bp, lp, _ = x_prompt.shape
    bs = x_sample.shape[0]
    tp = bp * lp
    xp = x_prompt.reshape(tp, D)
    xs = x_sample.reshape(bs, D)
    tmp, tms = _tile(tp, 512), _tile(bs, 512)
    te = 1024
    ln = lambda layer, k: (ln_g[layer, k].reshape(1, D), ln_b[layer, k].reshape(1, D))

    w_in = jnp.pad(w_in_a[0], ((0, 0), (0, A_COLS_PAD - A_COLS))).astype(BF16)
    cos_p, sin_p = _rope_tables(jnp.arange(lp))
    cos_s, sin_s = _rope_tables(PAST + jnp.arange(1))
    alog, dtb = _pad_lanes(gdn_a_log[0]), _pad_lanes(gdn_dt_bias[0])
    rg, gg = ret_norm_g[0].reshape(1, RDV), gdn_norm_g[0].reshape(1, GDV)
    proj_p = _proj(xp, w_in, tmp).reshape(bp, lp, A_COLS_PAD)
    proj_s = _proj(xs, w_in, tms).reshape(bs, 1, A_COLS_PAD)
    mix_p, ret_p, gdn_p, conv_p = _mixer_a_prompt(proj_p, cos_p, sin_p, gdn_conv_w[0], alog, dtb, rg, gg)
    mix_s, ret_s, gdn_s, conv_s = _mixer_a_sample(proj_s, cos_s, sin_s, gdn_conv_w[0], alog, dtb, rg, gg,
                                                  state_ret[0], state_gdn[0], state_gdn_conv[0])
    w_out = w_out_a[0].astype(BF16)
    g0, b0 = ln(0, 0)
    xp = _outproj_ln(mix_p.reshape(tp, D), xp, w_out, g0, b0, tmp)
    xs = _outproj_ln(mix_s.reshape(bs, D), xs, w_out, g0, b0, tms)
    pw = _peer_params(peer_wq[0], peer_sub_keys[0], peer_u[0], peer_v[0], te)
    g1, b1 = ln(0, 1)
    xp = _peer(xp, *pw, g1, b1, tmp, te)
    xs = _peer(xs, *pw, g1, b1, tms, te)

    w_c = jnp.pad(w_in_c[0], ((0, 0), (0, C_COLS_PAD - w_in_c.shape[-1]))).astype(BF16)
    bf_row = _pad_lanes(fox_b_f[0])
    kp, vp, lfp, lftp, qbp, kbp, vbp = _fox_proj(xp, w_c, bf_row, tmp, True)
    ks, vs, lfs, _, qs = _fox_proj(xs, w_c, bf_row, tms, False)
    tq, tk = _tile(lp, 1024), _tile(lp, 512)
    cum = _fox_cumsum(lftp, bp, lp).reshape(bp * FH, lp // tk, tk)
    op = _fox_flash(qbp.reshape(bp, lp, D), kbp.reshape(bp, lp, D), vbp.reshape(bp, lp, D), cum, tq, tk)
    os_ = _fox_sample(page_table, qs.reshape(bs, FH, FHD), ks.reshape(bs, FH, FHD), vs.reshape(bs, FH, FHD),
                      lfs.reshape(bs, FH, 1), cache_fox_k[0], cache_fox_v[0], cache_fox_logf[0])
    w_oc = w_out_c[0].astype(BF16)
    g2, b2 = ln(1, 0)
    xp = _outproj_ln(op.reshape(tp, D), xp, w_oc, g2, b2, tmp)
    xs = _outproj_ln(os_.reshape(bs, D), xs, w_oc, g2, b2, tms)
    pw = _peer_params(peer_wq[1], peer_sub_keys[1], peer_u[1], peer_v[1], te)
    g3, b3 = ln(1, 1)
    xp = _peer(xp, *pw, g3, b3, tmp, te)
    xs = _peer(xs, *pw, g3, b3, tms, te)

    return (xp.reshape(bp, lp, D), xs.reshape(bs, 1, D),
            ret_p[None], ret_s[None], gdn_p[None], gdn_s[None], conv_p[None], conv_s[None],
            kp.reshape(1, bp, lp, FH, FHD), ks.reshape(1, bs, 1, FH, FHD),
            vp.reshape(1, bp, lp, FH, FHD), vs.reshape(1, bs, 1, FH, FHD),
            lfp.reshape(1, bp, lp, FH), lfs.reshape(1, bs, 1, FH))
```

```python
import functools
import math

import jax
import jax.numpy as jnp
from jax import lax
from jax.experimental import pallas as pl
from jax.experimental.pallas import tpu as pltpu

F32 = jnp.float32
BF16 = jnp.bfloat16

D = 1024
PAST = 2048
PAGE = 128
RH, RDK, RDV = 4, 128, 128
GH, GDK, GDV = 4, 128, 128
CONV_W = 4
CONV_CH = 2 * GH * GDK + GH * GDV
A_COLS = 4104
A_COLS_PAD = 4224
FH, FHD = 8, 128
C_COLS_PAD = 3200
PH, PTOPK, NKEYS = 8, 16, 128
NEXP = NKEYS * NKEYS
ALPHA = 4.0 ** 0.25
LN_EPS = 1e-5
NORM_EPS = 1e-6
ROPE_BASE = 10000.0
LANES = 128
CHUNK = 128
PAGES_PER_STEP = 8
GATE_DT = BF16
VMEM_LIMIT = 56 * 1024 * 1024

NT = (((1,), (1,)), ((), ()))
TN = (((0,), (0,)), ((), ()))


def _cp(sem):
    return pltpu.CompilerParams(dimension_semantics=sem, vmem_limit_bytes=VMEM_LIMIT)


def _mm(a, b):
    return jnp.dot(a.astype(BF16), b.astype(BF16), preferred_element_type=F32)


def _mm_nt(a, b):
    return lax.dot_general(a.astype(BF16), b.astype(BF16), NT, preferred_element_type=F32)


def _mm_tn(a, b):
    return lax.dot_general(a.astype(BF16), b.astype(BF16), TN, preferred_element_type=F32)


def _split3(x):
    x1 = x.astype(BF16)
    r1 = x - x1.astype(F32)
    x2 = r1.astype(BF16)
    return x1, x2, (r1 - x2.astype(F32)).astype(BF16)


def _mm_01_left(m01, x):
    return sum(jnp.dot(m01, t, preferred_element_type=F32) for t in _split3(x))


def _mm_01_right(x, m01):
    return sum(jnp.dot(t, m01, preferred_element_type=F32) for t in _split3(x))


def _split2(a):
    hi = a.astype(BF16)
    return hi, (a - hi.astype(F32)).astype(BF16)


def _mm_s(a, b):
    return (jnp.dot(a[0], b[0], preferred_element_type=F32) + jnp.dot(a[0], b[1], preferred_element_type=F32)
            + jnp.dot(a[1], b[0], preferred_element_type=F32))


def _sigmoid(x):
    return 1.0 / (1.0 + jnp.exp(-x))


def _silu(x):
    return x * _sigmoid(x)


def _softplus(x):
    return jnp.maximum(x, 0.0) + jnp.log1p(jnp.exp(-jnp.abs(x)))


def _log_sigmoid(x):
    return jnp.minimum(x, 0.0) - jnp.log1p(jnp.exp(-jnp.abs(x)))


def _layer_norm(z, g, b):
    mu = jnp.mean(z, axis=-1, keepdims=True)
    d = z - mu
    var = jnp.mean(d * d, axis=-1, keepdims=True)
    return d * lax.rsqrt(var + LN_EPS) * g + b


def _iota2(shape, axis):
    return lax.broadcasted_iota(jnp.int32, shape, axis)


def _proj_kernel(x_ref, w_ref, o_ref):
    o_ref[...] = jnp.dot(x_ref[...].astype(BF16), w_ref[...], preferred_element_type=F32)


def _proj(x, w, tm):
    t, k = x.shape
    n = w.shape[1]
    return pl.pallas_call(
        _proj_kernel,
        grid=(t // tm,),
        in_specs=[pl.BlockSpec((tm, k), lambda i: (i, 0)), pl.BlockSpec((k, n), lambda i: (0, 0))],
        out_specs=pl.BlockSpec((tm, n), lambda i: (i, 0)),
        out_shape=jax.ShapeDtypeStruct((t, n), F32),
        compiler_params=_cp(("parallel",)),
        name="proj_a",
    )(x, w)


def _fox_proj_kernel(x_ref, w_ref, bf_ref, k_ref, v_ref, lf_ref, lft_ref, *q_refs, attn_copies):
    p = jnp.dot(x_ref[...].astype(BF16), w_ref[...], preferred_element_type=F32)
    q = p[:, 0:D]
    k = p[:, D:2 * D]
    v = p[:, 2 * D:3 * D]
    k_ref[...] = k
    v_ref[...] = v
    lf = _log_sigmoid(p[:, 3 * D:3 * D + LANES] + bf_ref[...])
    lf_ref[...] = lf[:, 0:FH]
    lft_ref[...] = lf.T[0:FH, :]
    if attn_copies:
        qt_ref, kb_ref, vt_ref = q_refs
        kb_ref[...] = k.astype(BF16)
        qs = q * (FHD ** -0.5)
        for h in range(FH):
            qt_ref[0, h] = qs[:, h * FHD:(h + 1) * FHD].T.astype(BF16)
            vt_ref[0, h] = v[:, h * FHD:(h + 1) * FHD].T.astype(BF16)
    else:
        q_refs[0][...] = q


def _fox_proj(x, w, bf_row, tm, attn_copies):
    t = x.shape[0]
    row = lambda i: (i, 0)
    out_specs = [pl.BlockSpec((tm, D), row), pl.BlockSpec((tm, D), row),
                 pl.BlockSpec((tm, FH), row), pl.BlockSpec((FH, tm), lambda i: (0, i))]
    out_shape = [jax.ShapeDtypeStruct((t, D), F32), jax.ShapeDtypeStruct((t, D), F32),
                 jax.ShapeDtypeStruct((t, FH), F32), jax.ShapeDtypeStruct((FH, t), F32)]
    if attn_copies:
        tile = pl.BlockSpec((1, FH, FHD, tm), lambda i: (i, 0, 0, 0))
        tiles = jax.ShapeDtypeStruct((t // tm, FH, FHD, tm), BF16)
        out_specs += [tile, pl.BlockSpec((tm, D), row), tile]
        out_shape += [tiles, jax.ShapeDtypeStruct((t, D), BF16), tiles]
    else:
        out_specs += [pl.BlockSpec((tm, D), row)]
        out_shape += [jax.ShapeDtypeStruct((t, D), F32)]
    return pl.pallas_call(
        functools.partial(_fox_proj_kernel, attn_copies=attn_copies),
        grid=(t // tm,),
        in_specs=[pl.BlockSpec((tm, D), row), pl.BlockSpec((D, C_COLS_PAD), lambda i: (0, 0)),
                  pl.BlockSpec((1, LANES), lambda i: (0, 0))],
        out_specs=out_specs,
        out_shape=out_shape,
        compiler_params=_cp(("parallel",)),
        name="proj_c",
    )(x, w, bf_row)


def _outproj_ln_kernel(a_ref, x_ref, w_ref, g_ref, b_ref, o_ref):
    h = jnp.dot(a_ref[...].astype(BF16), w_ref[...], preferred_element_type=F32)
    o_ref[...] = _layer_norm(ALPHA * x_ref[...] + h, g_ref[...], b_ref[...])


def _outproj_ln(a, x, w, g, b, tm):
    t = x.shape[0]
    row = lambda i: (i, 0)
    fixed = lambda i: (0, 0)
    return pl.pallas_call(
        _outproj_ln_kernel,
        grid=(t // tm,),
        in_specs=[pl.BlockSpec((tm, D), row), pl.BlockSpec((tm, D), row), pl.BlockSpec((D, D), fixed),
                  pl.BlockSpec((1, D), fixed), pl.BlockSpec((1, D), fixed)],
        out_specs=pl.BlockSpec((tm, D), row),
        out_shape=jax.ShapeDtypeStruct((t, D), F32),
        compiler_params=_cp(("parallel",)),
        name="outproj_ln",
    )(a, x, w, g, b)


def _unit_lower_inverses(a_list, ri, ci):
    eye = (ri == ci).astype(F32)
    blk = (ri >> 4) == (ci >> 4)
    d = [_split2(jnp.where(blk, a, 0.0)) for a in a_list]
    d2f = [_mm_s(x, x) for x in d]
    d2 = [_split2(x) for x in d2f]
    d4f = [_mm_s(x, x) for x in d2]
    d4 = [_split2(x) for x in d4f]
    d8 = [_split2(_mm_s(x, x)) for x in d4]
    xs = [eye - jnp.where(blk, a, 0.0) for a in a_list]
    for pw in (d2, d4, d8):
        xs = [x + _mm_s(_split2(x), p) for x, p in zip(xs, pw)]
    for s in (5, 6, 7):
        msk = ((ri >> s) == (ci >> s)) & ((ri >> (s - 1)) != (ci >> (s - 1)))
        es = [_split2(jnp.where(msk, a, 0.0)) for a in a_list]
        xsp = [_split2(x) for x in xs]
        xe = [_split2(_mm_s(x, e)) for x, e in zip(xsp, es)]
        xs = [x - _mm_s(y, xp) for x, y, xp in zip(xs, xe, xsp)]
    return xs


def _head_gate_norm(out, g_row, gate):
    return out * lax.rsqrt(jnp.mean(out * out, axis=-1, keepdims=True) + NORM_EPS) * g_row * _silu(gate)


def _mixer_a_prompt_kernel(p_ref, cos_ref, sin_ref, cw_ref, alog_ref, dt_ref, rg_ref, gg_ref,
                           mix_ref, ret_ref, gdn_ref, conv_ref, ext_ref, *, nb):
    n = pl.program_id(0)
    c = CHUNK

    @pl.when(n == 0)
    def _init():
        ret_ref[...] = jnp.zeros_like(ret_ref)
        gdn_ref[...] = jnp.zeros_like(gdn_ref)
        ext_ref[:, 0:8, :] = jnp.zeros((nb, 8, CONV_CH), F32)

    ri = _iota2((c, c), 0)
    ci = _iota2((c, c), 1)
    rf = ri.astype(F32)
    diff = rf - ci.astype(F32)
    cosv = cos_ref[...]
    sinv = sin_ref[...]

    for h in range(RH):
        lg = math.log1p(-(2.0 ** (-5.0 - h)))
        decay = jnp.where(diff >= 0, jnp.exp(jnp.maximum(diff, 0.0) * lg), 0.0)
        inner = jnp.exp((rf + 1.0) * lg)
        kdec = jnp.exp((c - 1.0 - rf) * lg)
        for b in range(nb):
            rq = p_ref[b, :, h * RDK:(h + 1) * RDK]
            rk = p_ref[b, :, RH * RDK + h * RDK:RH * RDK + (h + 1) * RDK]
            rv = p_ref[b, :, 2 * RH * RDK + h * RDV:2 * RH * RDK + (h + 1) * RDV]
            rgate = p_ref[b, :, 2 * RH * RDK + RH * RDV + h * RDV:2 * RH * RDK + RH * RDV + (h + 1) * RDV]
            q = rq * cosv + pltpu.roll(rq, RDK // 2, 1) * sinv
            k = (rk * cosv + pltpu.roll(rk, RDK // 2, 1) * sinv) * (RDK ** -0.5)
            s = ret_ref[b, h]
            scores = _mm_nt(q, k) * decay
            out = _mm(scores, rv) + _mm(q * inner, s)
            ret_ref[b, h] = math.exp(c * lg) * s + _mm_tn(k * kdec, rv)
            mix_ref[b, :, h * RDV:(h + 1) * RDV] = _head_gate_norm(out, rg_ref[...], rgate)

    g_off = 2 * RH * RDK + 2 * RH * RDV
    z_off = g_off + CONV_CH
    incl = ri >= ci
    strict = ri > ci
    tril = incl.astype(BF16)
    units = []
    for b in range(nb):
        ext_ref[b, 8:8 + c, :] = p_ref[b, :, g_off:g_off + CONV_CH]
        y = ext_ref[b, 5:5 + c, :] * cw_ref[0:1, :]
        for i in range(1, CONV_W):
            y = y + ext_ref[b, 5 + i:5 + i + c, :] * cw_ref[i:i + 1, :]
        conv_ref[b] = ext_ref[b, c + 5:c + 8, :]
        ext_ref[b, 0:8, :] = ext_ref[b, c:c + 8, :]
        y = _silu(y)
        tail = p_ref[b, :, z_off + GH * GDV:z_off + GH * GDV + LANES]
        g_all = -jnp.exp(alog_ref[...]) * _softplus(tail + dt_ref[...])
        beta_all = _sigmoid(tail)
        for h in range(GH):
            qh = y[:, h * GDK:(h + 1) * GDK]
            kh = y[:, GH * GDK + h * GDK:GH * GDK + (h + 1) * GDK]
            vh = y[:, 2 * GH * GDK + h * GDV:2 * GH * GDK + (h + 1) * GDV]
            qh = qh * lax.rsqrt(jnp.sum(qh * qh, axis=-1, keepdims=True) + NORM_EPS) * (GDK ** -0.5)
            kh = kh * lax.rsqrt(jnp.sum(kh * kh, axis=-1, keepdims=True) + NORM_EPS)
            gb = jnp.broadcast_to(g_all[:, h:h + 1], (c, c))
            bcol = beta_all[:, GH + h:GH + h + 1]
            units.append((b, h, qh, kh, vh, gb, bcol))

    gcs = [_mm_01_left(tril, u[5]) for u in units]
    decs = []
    for gc in gcs:
        dmat = gc - gc.T
        decs.append(jnp.where(incl, jnp.exp(jnp.where(incl, dmat, 0.0)), 0.0))
    kbs = [u[3] * u[6] for u in units]
    a_list = [jnp.where(strict, _mm_nt(kb, u[3]) * dec, 0.0) for kb, u, dec in zip(kbs, units, decs)]
    ts = [t.astype(BF16) for t in _unit_lower_inverses(a_list, ri, ci)]
    egcs = [jnp.exp(gc) for gc in gcs]
    us = [jnp.dot(t, (u[4] * u[6]).astype(BF16), preferred_element_type=F32) for t, u in zip(ts, units)]
    ws = [jnp.dot(t, (kb * egc).astype(BF16), preferred_element_type=F32) for t, kb, egc in zip(ts, kbs, egcs)]
    attns = [_mm_nt(u[2], u[3]) * dec for u, dec in zip(units, decs)]
    ss = [gdn_ref[u[0], u[1]] for u in units]
    v_news = [uu - _mm(w, s) for uu, w, s in zip(us, ws, ss)]
    outs = [_mm(u[2] * egc, s) + _mm(attn, vn) for u, egc, s, attn, vn in zip(units, egcs, ss, attns, v_news)]
    for u, gc, s, vn, out in zip(units, gcs, ss, v_news, outs):
        b, h = u[0], u[1]
        g_last = gc[c - 1:c, :]
        gdn_ref[b, h] = jnp.exp(g_last) * s + _mm_tn(u[3] * jnp.exp(g_last - gc), vn)
        gz = p_ref[b, :, z_off + h * GDV:z_off + (h + 1) * GDV]
        mix_ref[b, :, RH * RDV + h * GDV:RH * RDV + (h + 1) * GDV] = _head_gate_norm(out, gg_ref[...], gz)


def _mixer_a_prompt(proj, cos2, sin2, conv_w, alog_row, dt_row, rg_row, gg_row):
    b, l, _ = proj.shape
    fixed = lambda n: (0, 0)
    return pl.pallas_call(
        functools.partial(_mixer_a_prompt_kernel, nb=b),
        grid=(l // CHUNK,),
        in_specs=[pl.BlockSpec((b, CHUNK, A_COLS_PAD), lambda n: (0, n, 0)),
                  pl.BlockSpec((CHUNK, LANES), lambda n: (n, 0)),
                  pl.BlockSpec((CHUNK, LANES), lambda n: (n, 0)),
                  pl.BlockSpec((CONV_W, CONV_CH), fixed),
                  pl.BlockSpec((1, LANES), fixed), pl.BlockSpec((1, LANES), fixed),
                  pl.BlockSpec((1, LANES), fixed), pl.BlockSpec((1, LANES), fixed)],
        out_specs=[pl.BlockSpec((b, CHUNK, D), lambda n: (0, n, 0)),
                   pl.BlockSpec((b, RH, RDK, RDV), lambda n: (0, 0, 0, 0)),
                   pl.BlockSpec((b, GH, GDK, GDV), lambda n: (0, 0, 0, 0)),
                   pl.BlockSpec((b, CONV_W - 1, CONV_CH), lambda n: (0, 0, 0))],
        out_shape=[jax.ShapeDtypeStruct((b, l, D), F32),
                   jax.ShapeDtypeStruct((b, RH, RDK, RDV), F32),
                   jax.ShapeDtypeStruct((b, GH, GDK, GDV), F32),
                   jax.ShapeDtypeStruct((b, CONV_W - 1, CONV_CH), F32)],
        scratch_shapes=[pltpu.VMEM((b, CHUNK + 8, CONV_CH), F32)],
        compiler_params=_cp(("arbitrary",)),
        name="mixer_a_prompt",
    )(proj, cos2, sin2, conv_w, alog_row, dt_row, rg_row, gg_row)


def _mixer_a_sample_kernel(p_ref, cos_ref, sin_ref, cw_ref, alog_ref, dt_ref, rg_ref, gg_ref,
                           rs_ref, gs_ref, cs_ref, mix_ref, ret_ref, gdn_ref, conv_ref):
    eye = _iota2((LANES, LANES), 0) == _iota2((LANES, LANES), 1)

    def col(v):
        return jnp.sum(jnp.where(eye, jnp.broadcast_to(v, (LANES, LANES)), 0.0), axis=1, keepdims=True)

    def vec_mat(c, s):
        return jnp.sum(c * s, axis=0, keepdims=True)

    cosv = cos_ref[...]
    sinv = sin_ref[...]
    for h in range(RH):
        gamma = 1.0 - 2.0 ** (-5.0 - h)
        rq = p_ref[0, :, h * RDK:(h + 1) * RDK]
        rk = p_ref[0, :, RH * RDK + h * RDK:RH * RDK + (h + 1) * RDK]
        rv = p_ref[0, :, 2 * RH * RDK + h * RDV:2 * RH * RDK + (h + 1) * RDV]
        rgate = p_ref[0, :, 2 * RH * RDK + RH * RDV + h * RDV:2 * RH * RDK + RH * RDV + (h + 1) * RDV]
        q = rq * cosv + pltpu.roll(rq, RDK // 2, 1) * sinv
        k = (rk * cosv + pltpu.roll(rk, RDK // 2, 1) * sinv) * (RDK ** -0.5)
        s = rs_ref[0, h]
        qk = jnp.sum(q * k, axis=-1, keepdims=True)
        out = qk * rv + gamma * vec_mat(col(q), s)
        ret_ref[0, h] = gamma * s + col(k) * rv
        mix_ref[0, :, h * RDV:(h + 1) * RDV] = _head_gate_norm(out, rg_ref[...], rgate)

    g_off = 2 * RH * RDK + 2 * RH * RDV
    gq_all = p_ref[0, :, g_off:g_off + CONV_CH]
    cb = cs_ref[0]
    y = cb[0:1, :] * cw_ref[0:1, :] + cb[1:2, :] * cw_ref[1:2, :] + cb[2:3, :] * cw_ref[2:3, :] \
        + gq_all * cw_ref[3:4, :]
    conv_ref[0, 0:2, :] = cb[1:3, :]
    conv_ref[0, 2:3, :] = gq_all
    y = _silu(y)
    z_off = g_off + CONV_CH
    tail = p_ref[0, :, z_off + GH * GDV:z_off + GH * GDV + LANES]
    g_all = -jnp.exp(alog_ref[...]) * _softplus(tail + dt_ref[...])
    beta_all = _sigmoid(tail)
    for h in range(GH):
        qh = y[:, h * GDK:(h + 1) * GDK]
        kh = y[:, GH * GDK + h * GDK:GH * GDK + (h + 1) * GDK]
        vh = y[:, 2 * GH * GDK + h * GDV:2 * GH * GDK + (h + 1) * GDV]
        qh = qh * lax.rsqrt(jnp.sum(qh * qh, axis=-1, keepdims=True) + NORM_EPS) * (GDK ** -0.5)
        kh = kh * lax.rsqrt(jnp.sum(kh * kh, axis=-1, keepdims=True) + NORM_EPS)
        eg = jnp.exp(g_all[:, h:h + 1])
        beta = beta_all[:, GH + h:GH + h + 1]
        s = gs_ref[0, h]
        kc = col(kh)
        v_new = vh * beta - vec_mat(kc * (beta * eg), s)
        qk = jnp.sum(qh * kh, axis=-1, keepdims=True)
        out = eg * vec_mat(col(qh), s) + qk * v_new
        gdn_ref[0, h] = eg * s + kc * v_new
        gz = p_ref[0, :, z_off + h * GDV:z_off + (h + 1) * GDV]
        mix_ref[0, :, RH * RDV + h * GDV:RH * RDV + (h + 1) * GDV] = _head_gate_norm(out, gg_ref[...], gz)


def _mixer_a_sample(proj, cos2, sin2, conv_w, alog_row, dt_row, rg_row, gg_row, ret_s, gdn_s, conv_s):
    b = proj.shape[0]
    fixed = lambda i: (0, 0)
    st = lambda i: (i, 0, 0, 0)
    return pl.pallas_call(
        _mixer_a_sample_kernel,
        grid=(b,),
        in_specs=[pl.BlockSpec((1, 1, A_COLS_PAD), lambda i: (i, 0, 0)),
                  pl.BlockSpec((1, LANES), fixed), pl.BlockSpec((1, LANES), fixed),
                  pl.BlockSpec((CONV_W, CONV_CH), fixed),
                  pl.BlockSpec((1, LANES), fixed), pl.BlockSpec((1, LANES), fixed),
                  pl.BlockSpec((1, LANES), fixed), pl.BlockSpec((1, LANES), fixed),
                  pl.BlockSpec((1, RH, RDK, RDV), st), pl.BlockSpec((1, GH, GDK, GDV), st),
                  pl.BlockSpec((1, CONV_W - 1, CONV_CH), lambda i: (i, 0, 0))],
        out_specs=[pl.BlockSpec((1, 1, D), lambda i: (i, 0, 0)),
                   pl.BlockSpec((1, RH, RDK, RDV), st), pl.BlockSpec((1, GH, GDK, GDV), st),
                   pl.BlockSpec((1, CONV_W - 1, CONV_CH), lambda i: (i, 0, 0))],
        out_shape=[jax.ShapeDtypeStruct((b, 1, D), F32),
                   jax.ShapeDtypeStruct((b, RH, RDK, RDV), F32),
                   jax.ShapeDtypeStruct((b, GH, GDK, GDV), F32),
                   jax.ShapeDtypeStruct((b, CONV_W - 1, CONV_CH), F32)],
        compiler_params=_cp(("parallel",)),
        name="mixer_a_sample",
    )(proj, cos2, sin2, conv_w, alog_row, dt_row, rg_row, gg_row, ret_s, gdn_s, conv_s)


def _fox_cumsum_kernel(x_ref, o_ref):
    n = x_ref.shape[1] // LANES
    upper = (_iota2((LANES, LANES), 0) <= _iota2((LANES, LANES), 1)).astype(BF16)
    carry = jnp.zeros((FH, 1), F32)
    for c in range(n):
        p = _mm_01_right(x_ref[:, c * LANES:(c + 1) * LANES], upper) + carry
        o_ref[0, :, c * LANES:(c + 1) * LANES] = p
        carry = p[:, LANES - 1:LANES]


def _fox_cumsum(lft, b, l):
    return pl.pallas_call(
        _fox_cumsum_kernel,
        grid=(b,),
        in_specs=[pl.BlockSpec((FH, l), lambda i: (0, i))],
        out_specs=pl.BlockSpec((1, FH, l), lambda i: (i, 0, 0)),
        out_shape=jax.ShapeDtypeStruct((b, FH, l), F32),
        compiler_params=_cp(("parallel",)),
        name="fox_cumsum",
    )(lft)


def _fox_flash_kernel(q_ref, k_ref, v_ref, ck_ref, o_ref, ka_ref, sa_ref, sb_ref, *, tq, tk):
    assert tq == 2 * tk and tk % LANES == 0
    qi = pl.program_id(2)
    nk = k_ref.shape[1] // tk

    @pl.when(qi == 0)
    def _augment_keys():
        eye = _iota2((LANES, LANES), 0) == _iota2((LANES, LANES), 1)
        lane = _iota2((LANES, LANES), 1)
        for j in range(nk):
            ka_ref[j * tk:(j + 1) * tk, 0:FHD] = k_ref[0, j * tk:(j + 1) * tk, :]
            for c in range(tk // LANES):
                row = jnp.broadcast_to(ck_ref[0, j:j + 1, c * LANES:(c + 1) * LANES], (LANES, LANES))
                col = jnp.sum(jnp.where(eye, row, 0.0), axis=1, keepdims=True)
                c1, c2, c3 = (t.astype(F32) for t in _split3(col))
                blk = jnp.where(lane == 0, c1, jnp.where(lane == 1, c2, jnp.where(lane == 2, c3, 0.0)))
                r0 = j * tk + c * LANES
                ka_ref[r0:r0 + LANES, FHD:2 * FHD] = blk.astype(BF16)

    sub = _iota2((FHD, tq), 0)
    minus_ones = jnp.where(sub < 3, -1.0, 0.0).astype(BF16)
    qa = jnp.concatenate([q_ref[0, 0], q_ref[1, 0]], axis=1)
    qa = jnp.concatenate([qa, minus_ones], axis=0)

    def scores(j, s_ref):
        ka = ka_ref[pl.ds(pl.multiple_of(j * tk, tk), tk), :]
        s_ref[...] = jnp.dot(ka, qa, preferred_element_type=F32)

    def update(j, s_ref, carry, masked):
        m_old, l_old, acc = carry
        s = s_ref[...]
        if masked:
            kpos = j * tk + _iota2((tk, tq), 0)
            qpos = qi * tq + _iota2((tk, tq), 1)
            s = jnp.where(kpos <= qpos, s, -jnp.inf)
        m_new = jnp.maximum(m_old, jnp.max(s, axis=0, keepdims=True))
        alpha = jnp.exp(m_old - m_new)
        p = jnp.exp(s - m_new)
        l_new = alpha * l_old + jnp.sum(p, axis=0, keepdims=True)
        acc = alpha * acc + jnp.dot(v_ref[j, 0], p.astype(BF16), preferred_element_type=F32)
        return m_new, l_new, acc

    def two_blocks(i, carry):
        scores(2 * i + 1, sb_ref)
        carry = update(2 * i, sa_ref, carry, False)
        scores(2 * i + 2, sa_ref)
        return update(2 * i + 1, sb_ref, carry, False)

    scores(0, sa_ref)
    init = (jnp.full((1, tq), -jnp.inf, F32), jnp.zeros((1, tq), F32), jnp.zeros((FHD, tq), F32))
    carry = lax.fori_loop(0, qi, two_blocks, init)
    scores(2 * qi + 1, sb_ref)
    carry = update(2 * qi, sa_ref, carry, True)
    carry = update(2 * qi + 1, sb_ref, carry, True)
    o_ref[0] = (carry[2] / carry[1]).T


def _fox_flash(qt, kb, vt, cum, tq, tk):
    b, l, _ = kb.shape
    nk = l // tk
    return pl.pallas_call(
        functools.partial(_fox_flash_kernel, tq=tq, tk=tk),
        grid=(b, FH, l // tq),
        in_specs=[pl.BlockSpec((tq // tk, 1, FHD, tk), lambda i, h, qi: (i * (l // tq) + qi, h, 0, 0)),
                  pl.BlockSpec((1, l, FHD), lambda i, h, qi: (i, 0, h)),
                  pl.BlockSpec((nk, 1, FHD, tk), lambda i, h, qi: (i, h, 0, 0)),
                  pl.BlockSpec((1, nk, tk), lambda i, h, qi: (i * FH + h, 0, 0))],
        out_specs=pl.BlockSpec((1, tq, FHD), lambda i, h, qi: (i, qi, h)),
        out_shape=jax.ShapeDtypeStruct((b, l, D), F32),
        scratch_shapes=[pltpu.VMEM((l, 2 * FHD), BF16), pltpu.VMEM((tk, tq), F32), pltpu.VMEM((tk, tq), F32)],
        compiler_params=_cp(("arbitrary", "arbitrary", "arbitrary")),
        name="fox_flash",
    )(qt, kb, vt, cum)


def _fox_sample_kernel(pt_ref, q_ref, kn_ref, vn_ref, lfn_ref, *rest):
    npp = PAGES_PER_STEP
    k_refs, v_refs, lf_refs = rest[0:npp], rest[npp:2 * npp], rest[2 * npp:3 * npp]
    o_ref, m_ref, l_ref, acc_ref, carry_ref = rest[3 * npp:]
    j = pl.program_id(1)
    scale = FHD ** -0.5
    rows = PAGE * FH

    @pl.when(j == 0)
    def _init():
        m_ref[...] = jnp.full(m_ref.shape, -jnp.inf, F32)
        l_ref[...] = jnp.zeros(l_ref.shape, F32)
        acc_ref[...] = jnp.zeros(acc_ref.shape, F32)
        carry_ref[...] = jnp.zeros(carry_ref.shape, F32)

    q = q_ref[0]
    qb = (q * scale).astype(BF16)
    own = (_iota2((FH, rows), 1) & (FH - 1)) == _iota2((FH, rows), 0)
    nch = rows // LANES
    ci = _iota2((LANES, 2 * LANES), 1)
    pre_tot = ((_iota2((LANES, 2 * LANES), 0) <= ci) | (ci >= LANES)).astype(BF16)
    stacked, s_pages = [], []
    for pg in range(npp):
        kp = k_refs[pg][0].reshape(rows, FHD).astype(BF16)
        s_pages.append(lax.dot_general(qb, kp, NT, preferred_element_type=F32))
        lf = jnp.where(own, jnp.broadcast_to(lf_refs[pg][0], (FH, rows)), 0.0)
        stacked += [lf[:, c * LANES:(c + 1) * LANES] for c in range(nch)]
    pt = _mm_01_right(jnp.concatenate(stacked, axis=0), pre_tot)
    carry = carry_ref[...]
    logits = []
    for pg in range(npp):
        cums = []
        for c in range(nch):
            r0 = (pg * nch + c) * FH
            cums.append(pt[r0:r0 + FH, 0:LANES] + carry)
            carry = carry + pt[r0:r0 + FH, LANES:2 * LANES]
        logits.append(jnp.where(own, s_pages[pg] - jnp.concatenate(cums, axis=1), -jnp.inf))
    carry_ref[...] = carry
    m_old = m_ref[...]
    m_new = m_old
    for lg in logits:
        m_new = jnp.maximum(m_new, jnp.max(lg, axis=1, keepdims=True))
    alpha = jnp.exp(m_old - m_new)
    l_new = alpha * l_ref[...]
    acc = alpha * acc_ref[...]
    for pg in range(npp):
        p = jnp.exp(logits[pg] - m_new)
        l_new = l_new + jnp.sum(p, axis=1, keepdims=True)
        vp = v_refs[pg][0].reshape(rows, FHD).astype(BF16)
        acc = acc + jnp.dot(p.astype(BF16), vp, preferred_element_type=F32)
    m_ref[...] = m_new
    l_ref[...] = l_new
    acc_ref[...] = acc

    @pl.when(j == pl.num_programs(1) - 1)
    def _fin():
        s_new = jnp.sum(q * kn_ref[0], axis=1, keepdims=True) * scale - (carry[:, 0:1] + lfn_ref[0])
        m2 = jnp.maximum(m_new, s_new)
        a2 = jnp.exp(m_new - m2)
        p2 = jnp.exp(s_new - m2)
        o_ref[0] = (a2 * acc + p2 * vn_ref[0]) / (a2 * l_new + p2)


def _fox_sample(page_table, q3, k3, v3, lf3, k_pool, v_pool, lf_pool):
    b = q3.shape[0]
    npg = page_table.shape[1]
    npp = PAGES_PER_STEP
    assert npg % npp == 0
    pt = page_table.reshape(-1)
    tok = lambda i, j, pt_ref: (i, 0, 0)

    def page(r, nd):
        return lambda i, j, pt_ref: (pt_ref[i * npg + j * npp + r],) + (0,) * nd

    grid_spec = pltpu.PrefetchScalarGridSpec(
        num_scalar_prefetch=1,
        grid=(b, npg // npp),
        in_specs=[pl.BlockSpec((1, FH, FHD), tok), pl.BlockSpec((1, FH, FHD), tok), pl.BlockSpec((1, FH, FHD), tok),
                  pl.BlockSpec((1, FH, 1), tok)]
        + [pl.BlockSpec((1, PAGE, FH, FHD), page(r, 3)) for r in range(npp)]
        + [pl.BlockSpec((1, PAGE, FH, FHD), page(r, 3)) for r in range(npp)]
        + [pl.BlockSpec((1, 1, PAGE * FH), page(r, 2)) for r in range(npp)],
        out_specs=pl.BlockSpec((1, FH, FHD), tok),
        scratch_shapes=[pltpu.VMEM((FH, 1), F32), pltpu.VMEM((FH, 1), F32), pltpu.VMEM((FH, FHD), F32),
                        pltpu.VMEM((FH, LANES), F32)],
    )
    lf_flat =lf_pool.reshape(lf_pool.shape[0], 1, PAGE * FH)
    return pl.pallas_call(
        _fox_sample_kernel,
        grid_spec=grid_spec,
        out_shape=jax.ShapeDtypeStruct((b, FH, FHD), F32),
        compiler_params=_cp(("parallel", "arbitrary")),
        name="fox_sample",
    )(pt, q3, k3, v3, lf3, *([k_pool] * npp), *([v_pool] * npp), *([lf_flat] * npp))


_CAND_BLOCKS = [(a, 16 if a == 0 else 8, 16 // (a + 1)) for a in range(8)]


_MARK0 = -(2.0 ** 127)
_MARK_STEP = 2.0 ** 120
_MARK_LIMIT = -(2.0 ** 126)


def _top16(s, iota):
    rank = jnp.full(s.shape, float(PTOPK), F32)
    vals = jnp.zeros((PTOPK, s.shape[1]), F32)
    i16 = _iota2((PTOPK, s.shape[1]), 0)
    for r in range(PTOPK):
        m = jnp.max(s, axis=0, keepdims=True)
        sel = iota == jnp.min(jnp.where(s == m, iota, 1e9), axis=0, keepdims=True)
        rank = jnp.where(sel, float(r), rank)
        s = jnp.where(sel, -jnp.inf, s)
        vals = jnp.where(i16 == r, m, vals)
    return rank, vals


def _top16_fast(s):
    lowest = jnp.min(s)
    vals = jnp.zeros((PTOPK, s.shape[1]), F32)
    i16 = _iota2((PTOPK, s.shape[1]), 0)
    for r in range(PTOPK):
        m = jnp.max(s, axis=0, keepdims=True)
        s = jnp.where(s == m, _MARK0 + r * _MARK_STEP, s)
        vals = jnp.where(i16 == r, m, vals)
    marked = s < _MARK_LIMIT
    rank = jnp.where(marked, (s - _MARK0) * (1.0 / _MARK_STEP), float(PTOPK))
    taken = jnp.sum(jnp.where(marked, 1.0, 0.0), axis=0, keepdims=True)
    return rank, vals, _not_16(taken) | (lowest <= _MARK_LIMIT)


def _pick16(cand, cidx, exact):
    picked = jnp.zeros(cand.shape, F32)
    for _ in range(PTOPK):
        m = jnp.max(cand, axis=0, keepdims=True)
        sel = cand == m
        if exact:
            sel = cidx == jnp.min(jnp.where(sel, cidx, 1e9), axis=0, keepdims=True)
        picked = jnp.where(sel, 1.0, picked)
        cand = jnp.where(sel, -jnp.inf, cand)
    return picked


def _not_16(count):
    return jnp.max(jnp.abs(count - float(PTOPK))) > 0.5


def _route_head(s_t):
    s1 = s_t[0:NKEYS]
    s2 = s_t[NKEYS:2 * NKEYS]
    w = s_t.shape[1]
    iota = _iota2((NKEYS, w), 0).astype(F32)
    rank1, v1, redo1 = _top16_fast(s1)
    rank2, v2, redo2 = _top16_fast(s2)
    rank1, v1, rank2, v2 = lax.cond(redo1 | redo2, lambda: _top16(s1, iota) + _top16(s2, iota),
                                    lambda: (rank1, v1, rank2, v2))
    e1 = jnp.exp(v1 - v1[0:1])
    e2 = jnp.exp(v2 - v2[0:1])
    i8 = _iota2((8, w), 0)
    i8f = i8.astype(F32)
    i16f = _iota2((16, w), 0).astype(F32)
    cand, cidx, cprob = [], [], []
    for a, rows, valid in _CAND_BLOCKS:
        c = v1[a:a + 1] + v2[0:rows]
        pr = e1[a:a + 1] * e2[0:rows]
        if rows == 16:
            ix = i16f
        else:
            ix = i8f + float(a * PTOPK)
            c = jnp.where(i8 < valid, c, -jnp.inf)
        cand.append(c)
        cidx.append(ix)
        cprob.append(pr)
    cand.append(v1[8:16] + v2[0:1])
    cidx.append((i8f + 8.0) * float(PTOPK))
    cprob.append(e1[8:16] * e2[0:1])
    cand = jnp.concatenate(cand, axis=0)
    cidx = jnp.concatenate(cidx, axis=0)
    cprob = jnp.concatenate(cprob, axis=0)
    picked = _pick16(cand, cidx, False)
    picked = lax.cond(_not_16(jnp.sum(picked, axis=0, keepdims=True)),
                      lambda: _pick16(cand, cidx, True), lambda: picked)
    z = jnp.sum(picked * cprob, axis=0, keepdims=True)
    n_low = jnp.zeros((8, w), F32)
    off = 0
    for a, rows, _ in _CAND_BLOCKS:
        cnt = jnp.sum(picked[off:off + rows], axis=0, keepdims=True)
        n_low = jnp.where(i8 == a, cnt, n_low)
        off += rows
    n16 = jnp.concatenate([n_low, picked[off:off + 8]], axis=0)
    n1d = jnp.zeros((NKEYS, w), F32)
    for a in range(PTOPK):
        n1d = jnp.where(rank1 == float(a), n16[a:a + 1], n1d)
    e1d = jnp.exp(s1 - v1[0:1]) * (0.5 / z)
    e2d = jnp.exp(s2 - v2[0:1])
    return rank2, e2d, n1d, e1d


def _peer_kernel(x_ref, wq_ref, keys_ref, u0_ref, un_ref, vt_ref, g_ref, b_ref, o_ref,
                 xb_ref, q_ref, r2_ref, e2_ref, n1_ref, e1_ref, ha_ref, hb_ref, at_ref, acc_ref, *, tm, te, rw):
    j = pl.program_id(1)
    nlg = tm // LANES
    nsub = te // NKEYS

    @pl.when(j == 0)
    def _route():
        xb = x_ref[...].astype(BF16)
        xb_ref[...] = xb
        q = jnp.dot(xb, wq_ref[...], preferred_element_type=F32)
        for h in range(PH):
            q_ref[h] = q[:, h * LANES:(h + 1) * LANES].astype(BF16)
        acc_ref[...] = jnp.zeros(acc_ref.shape, F32)
        ha_ref[:, 0:tm] = lax.dot_general(u0_ref[...], xb, NT, preferred_element_type=F32)

        def body(h, carry):
            for w0 in range(0, tm, rw):
                s_t = lax.dot_general(keys_ref[...], q_ref[h, w0:w0 + rw, :], NT,
                                      preferred_element_type=F32)
                r2, e2, n1, e1 = _route_head(s_t)
                for g in range(rw // LANES):
                    lg = slice(g * LANES, (g + 1) * LANES)
                    r2_ref[h, w0 // LANES + g] = r2[:, lg].astype(GATE_DT)
                    e2_ref[h, w0 // LANES + g] = e2[:, lg].astype(GATE_DT)
                    n1_ref[h, w0 // LANES + g] = n1[:, lg]
                    e1_ref[h, w0 // LANES + g] = e1[:, lg]
            return carry

        lax.fori_loop(0, PH, body, 0)

    def step(h_cur_ref, h_next_ref):
        h_next_ref[:, 0:tm] = lax.dot_general(un_ref[...], xb_ref[...], NT, preferred_element_type=F32)
        zero = jnp.zeros((), GATE_DT)
        pair = 2
        for pp in range(nsub // pair):
            for g in range(nlg):
                lanes = slice(g * LANES, (g + 1) * LANES)
                gates = [jnp.zeros((NKEYS, LANES), GATE_DT) for _ in range(pair)]
                for h in range(PH):
                    r2 = r2_ref[h, g]
                    e2 = e2_ref[h, g]
                    for ii in range(pair):
                        i1 = j * nsub + pp * pair + ii
                        n_row = n1_ref[h, g, pl.ds(i1, 1), :].astype(GATE_DT)
                        e_row = e1_ref[h, g, pl.ds(i1, 1), :].astype(GATE_DT)
                        gates[ii] = gates[ii] + jnp.where(r2 < n_row, e2, zero) * e_row
                for ii in range(pair):
                    r0 = (pp * pair + ii) * NKEYS
                    hb = h_cur_ref[r0:r0 + NKEYS, lanes]
                    act = hb * (1.0 + lax.erf(hb * (2.0 ** -0.5)))
                    at_ref[r0:r0 + NKEYS, lanes] = (act.astype(GATE_DT) * gates[ii]).astype(BF16)
        acc_ref[...] += jnp.dot(vt_ref[0], at_ref[:, 0:tm], preferred_element_type=F32)

    @pl.when(j % 2 == 0)
    def _even():
        step(ha_ref, hb_ref)

    @pl.when(j % 2 == 1)
    def _odd():
        step(hb_ref, ha_ref)

    @pl.when(j == pl.num_programs(1) - 1)
    def _fin():
        o_ref[...] = _layer_norm(ALPHA * x_ref[...] + acc_ref[...].T, g_ref[...], b_ref[...])


def _peer(x, wq, keys, u, vt, g, b, tm, te):
    t = x.shape[0]
    nj = NEXP // te
    fixed = lambda i, j: (0, 0)
    row_pad = LANES if (tm // LANES) % 4 == 0 else 0
    return pl.pallas_call(
        functools.partial(_peer_kernel, tm=tm, te=te, rw=tm),
        grid=(t // tm, nj),
        in_specs=[pl.BlockSpec((tm, D), lambda i, j: (i, 0)),
                  pl.BlockSpec((D, PH * LANES), fixed),
                  pl.BlockSpec((2 * NKEYS, LANES), fixed),
                  pl.BlockSpec((te, D), fixed),
                  pl.BlockSpec((te, D), lambda i, j: (jnp.minimum(j + 1, nj - 1), 0)),
                  pl.BlockSpec((1, D, te), lambda i, j: (j, 0, 0)),
                  pl.BlockSpec((1, D), fixed), pl.BlockSpec((1, D), fixed)],
        out_specs=pl.BlockSpec((tm, D), lambda i, j: (i, 0)),
        out_shape=jax.ShapeDtypeStruct((t, D), F32),
        scratch_shapes=[pltpu.VMEM((tm, D), BF16),
                        pltpu.VMEM((PH, tm, LANES), BF16),
                        pltpu.VMEM((PH, tm // LANES, NKEYS, LANES), GATE_DT),
                        pltpu.VMEM((PH, tm // LANES, NKEYS, LANES), GATE_DT),
                        pltpu.VMEM((PH, tm // LANES, NKEYS, LANES), F32),
                        pltpu.VMEM((PH, tm // LANES, NKEYS, LANES), F32),
                        pltpu.VMEM((te, tm + row_pad), F32),
                        pltpu.VMEM((te, tm + row_pad), F32),
                        pltpu.VMEM((te, tm + row_pad), BF16),
                        pltpu.VMEM((D, tm), F32)],
        compiler_params=_cp(("parallel", "arbitrary")),
        name="peer",
    )(x, wq, keys, u, u, vt, g, b)


def _rope_tables(pos):
    half = RDK // 2
    inv = ROPE_BASE ** (-jnp.arange(half, dtype=F32) / half)
    ang = pos.astype(F32)[:, None] * inv[None, :]
    cos, sin = jnp.cos(ang), jnp.sin(ang)
    return jnp.concatenate([cos, cos], -1), jnp.concatenate([-sin, sin], -1)


def _pad_lanes(v, width=LANES, offset=0):
    return jnp.zeros((1, width), F32).at[0, offset:offset + v.shape[0]].set(v.astype(F32))


def _peer_params(wq, sub_keys, u_tab, v_tab, te):
    half = sub_keys.shape[-1]
    keys = jnp.zeros((2 * NKEYS, LANES), F32)
    keys = keys.at[0:NKEYS, 0:half].set(sub_keys[0]).at[NKEYS:, half:2 * half].set(sub_keys[1])
    vt = v_tab.astype(BF16).reshape(NEXP // te, te, D).transpose(0, 2, 1)
    return wq.astype(BF16), keys.astype(BF16), u_tab.astype(BF16), vt


def _tile(t, pref):
    return pref if t % pref == 0 else LANES


def kernel(x_prompt, x_sample, state_ret, state_gdn, state_gdn_conv, cache_fox_k, cache_fox_v, cache_fox_logf,
           page_table, w_in_a, ret_norm_g, gdn_a_log, gdn_dt_bias, gdn_conv_w, gdn_norm_g, w_out_a, w_in_c,
           fox_b_f, w_out_c, peer_wq, peer_sub_keys, peer_u, peer_v, ln_g, ln_b):
    bp, lp, _ = x_prompt.shape
    bs = x_sample.shape[0]
    tp = bp * lp
    xp = x_prompt.reshape(tp, D)
    xs = x_sample.reshape(bs, D)
    tmp, tms = _tile(tp, 512), _tile(bs, 512)
    te = 1024
    ln = lambda layer, k: (ln_g[layer, k].reshape(1, D), ln_b[layer, k].reshape(1, D))

    w_in = jnp.pad(w_in_a[0], ((0, 0), (0, A_COLS_PAD - A_COLS))).astype(BF16)
    cos_p, sin_p = _rope_tables(jnp.arange(lp))
    cos_s, sin_s = _rope_tables(PAST + jnp.arange(1))
    alog, dtb = _pad_lanes(gdn_a_log[0]), _pad_lanes(gdn_dt_bias[0])
    rg, gg = ret_norm_g[0].reshape(1, RDV), gdn_norm_g[0].reshape(1, GDV)
    proj_p = _proj(xp, w_in, tmp).reshape(bp, lp, A_COLS_PAD)
    proj_s = _proj(xs, w_in, tms).reshape(bs, 1, A_COLS_PAD)
    mix_p, ret_p, gdn_p, conv_p = _mixer_a_prompt(proj_p, cos_p, sin_p, gdn_conv_w[0], alog, dtb, rg, gg)
    mix_s, ret_s, gdn_s, conv_s = _mixer_a_sample(proj_s, cos_s, sin_s, gdn_conv_w[0], alog, dtb, rg, gg,
                                                  state_ret[0], state_gdn[0], state_gdn_conv[0])
    w_out = w_out_a[0].astype(BF16)
    g0, b0 = ln(0, 0)
    xp = _outproj_ln(mix_p.reshape(tp, D), xp, w_out, g0, b0, tmp)
    xs = _outproj_ln(mix_s.reshape(bs, D), xs, w_out, g0, b0, tms)
    pw = _peer_params(peer_wq[0], peer_sub_keys[0], peer_u[0], peer_v[0], te)
    g1, b1 = ln(0, 1)
    xp = _peer(xp, *pw, g1, b1, tmp, te)
    xs = _peer(xs, *pw, g1, b1, tms, te)

    w_c = jnp.pad(w_in_c[0], ((0, 0), (0, C_COLS_PAD - w_in_c.shape[-1]))).astype(BF16)
    bf_row = _pad_lanes(fox_b_f[0])
    kp, vp, lfp, lftp, qtp, kbp, vtp = _fox_proj(xp, w_c, bf_row, tmp, True)
    ks, vs, lfs, _, qs = _fox_proj(xs, w_c, bf_row, tms, False)
    tk = tmp
    tq = 2 * tk
    cum = _fox_cumsum(lftp, bp, lp).reshape(bp * FH, lp // tk, tk)
    op = _fox_flash(qtp, kbp.reshape(bp, lp, D), vtp, cum, tq, tk)
    os_ = _fox_sample(page_table, qs.reshape(bs, FH, FHD), ks.reshape(bs, FH, FHD), vs.reshape(bs, FH, FHD),
                      lfs.reshape(bs, FH, 1), cache_fox_k[0], cache_fox_v[0], cache_fox_logf[0])
    w_oc = w_out_c[0].astype(BF16)
    g2, b2 = ln(1, 0)
    xp = _outproj_ln(op.reshape(tp, D), xp, w_oc, g2, b2, tmp)
    xs = _outproj_ln(os_.reshape(bs, D), xs, w_oc, g2, b2, tms)
    pw = _peer_params(peer_wq[1], peer_sub_keys[1], peer_u[1], peer_v[1], te)
    g3, b3 = ln(1, 1)
    xp = _peer(xp, *pw, g3, b3, tmp, te)
    xs = _peer(xs, *pw, g3, b3, tms, te)

    return (xp.reshape(bp, lp, D), xs.reshape(bs, 1, D),
            ret_p[None], ret_s[None], gdn_p[None], gdn_s[None], conv_p[None], conv_s[None],
            kp.reshape(1, bp, lp, FH, FHD), ks.reshape(1, bs, 1, FH, FHD),
            vp.reshape(1, bp, lp, FH, FHD), vs.reshape(1, bs, 1, FH, FHD),
            lfp.reshape(1, bp, lp, FH), lfs.reshape(1, bs, 1, FH))
```

```python
import functools
import math

import jax
import jax.numpy as jnp
from jax import lax
from jax.experimental import pallas as pl
from jax.experimental.pallas import tpu as pltpu

F32 = jnp.float32
BF16 = jnp.bfloat16

D = 1024
PAST = 2048
PAGE = 128
RH, RDK, RDV = 4, 128, 128
GH, GDK, GDV = 4, 128, 128
CONV_W = 4
CONV_CH = 2 * GH * GDK + GH * GDV
A_COLS = 4104
A_COLS_PAD = 4224
FH, FHD = 8, 128
C_COLS_PAD = 3200
PH, PTOPK, NKEYS = 8, 16, 128
NEXP = NKEYS * NKEYS
ALPHA = 4.0 ** 0.25
LN_EPS = 1e-5
NORM_EPS = 1e-6
ROPE_BASE = 10000.0
LANES = 128
CHUNK = 128
PAGES_PER_STEP = 8
GATE_DT = BF16
VMEM_LIMIT = 56 * 1024 * 1024

NT = (((1,), (1,)), ((), ()))
TN = (((0,), (0,)), ((), ()))


def _cp(sem):
    return pltpu.CompilerParams(dimension_semantics=sem, vmem_limit_bytes=VMEM_LIMIT)


def _mm(a, b):
    return jnp.dot(a.astype(BF16), b.astype(BF16), preferred_element_type=F32)


def _mm_nt(a, b):
    return lax.dot_general(a.astype(BF16), b.astype(BF16), NT, preferred_element_type=F32)


def _mm_tn(a, b):
    return lax.dot_general(a.astype(BF16), b.astype(BF16), TN, preferred_element_type=F32)


def _split3(x):
    x1 = x.astype(BF16)
    r1 = x - x1.astype(F32)
    x2 = r1.astype(BF16)
    return x1, x2, (r1 - x2.astype(F32)).astype(BF16)


def _mm_01_left(m01, x):
    return sum(jnp.dot(m01, t, preferred_element_type=F32) for t in _split3(x))


def _mm_01_right(x, m01):
    return sum(jnp.dot(t, m01, preferred_element_type=F32) for t in _split3(x))


def _split2(a):
    hi = a.astype(BF16)
    return hi, (a - hi.astype(F32)).astype(BF16)


def _mm_s(a, b):
    return (jnp.dot(a[0], b[0], preferred_element_type=F32) + jnp.dot(a[0], b[1], preferred_element_type=F32)
            + jnp.dot(a[1], b[0], preferred_element_type=F32))


def _sigmoid(x):
    return 1.0 / (1.0 + jnp.exp(-x))


def _silu(x):
    return x * _sigmoid(x)


def _softplus(x):
    return jnp.maximum(x, 0.0) + jnp.log1p(jnp.exp(-jnp.abs(x)))


def _log_sigmoid(x):
    return jnp.minimum(x, 0.0) - jnp.log1p(jnp.exp(-jnp.abs(x)))


def _layer_norm(z, g, b):
    mu = jnp.mean(z, axis=-1, keepdims=True)
    d = z - mu
    var = jnp.mean(d * d, axis=-1, keepdims=True)
    return d * lax.rsqrt(var + LN_EPS) * g + b


def _iota2(shape, axis):
    return lax.broadcasted_iota(jnp.int32, shape, axis)


def _proj_kernel(x_ref, w_ref, o_ref):
    o_ref[...] = jnp.dot(x_ref[...].astype(BF16), w_ref[...], preferred_element_type=F32)


def _proj(x, w, tm):
    t, k = x.shape
    n = w.shape[1]
    return pl.pallas_call(
        _proj_kernel,
        grid=(t // tm,),
        in_specs=[pl.BlockSpec((tm, k), lambda i: (i, 0)), pl.BlockSpec((k, n), lambda i: (0, 0))],
        out_specs=pl.BlockSpec((tm, n), lambda i: (i, 0)),
        out_shape=jax.ShapeDtypeStruct((t, n), F32),
        compiler_params=_cp(("parallel",)),
        name="proj_a",
    )(x, w)


def _fox_proj_kernel(x_ref, w_ref, bf_ref, k_ref, v_ref, lf_ref, lft_ref, *q_refs, attn_copies):
    p = jnp.dot(x_ref[...].astype(BF16), w_ref[...], preferred_element_type=F32)
    q = p[:, 0:D]
    k = p[:, D:2 * D]
    v = p[:, 2 * D:3 * D]
    k_ref[...] = k
    v_ref[...] = v
    lf = _log_sigmoid(p[:, 3 * D:3 * D + LANES] + bf_ref[...])
    lf_ref[...] = lf[:, 0:FH]
    lft_ref[...] = lf.T[0:FH, :]
    if attn_copies:
        qt_ref, kb_ref, vt_ref = q_refs
        kb_ref[...] = k.astype(BF16)
        qs = q * (FHD ** -0.5)
        for h in range(FH):
            qt_ref[0, h] = qs[:, h * FHD:(h + 1) * FHD].T.astype(BF16)
            vt_ref[0, h] = v[:, h * FHD:(h + 1) * FHD].T.astype(BF16)
    else:
        q_refs[0][...] = q


def _fox_proj(x, w, bf_row, tm, attn_copies):
    t = x.shape[0]
    row = lambda i: (i, 0)
    out_specs = [pl.BlockSpec((tm, D), row), pl.BlockSpec((tm, D), row),
                 pl.BlockSpec((tm, FH), row), pl.BlockSpec((FH, tm), lambda i: (0, i))]
    out_shape = [jax.ShapeDtypeStruct((t, D), F32), jax.ShapeDtypeStruct((t, D), F32),
                 jax.ShapeDtypeStruct((t, FH), F32), jax.ShapeDtypeStruct((FH, t), F32)]
    if attn_copies:
        tile = pl.BlockSpec((1, FH, FHD, tm), lambda i: (i, 0, 0, 0))
        tiles = jax.ShapeDtypeStruct((t // tm, FH, FHD, tm), BF16)
        out_specs += [tile, pl.BlockSpec((tm, D), row), tile]
        out_shape += [tiles, jax.ShapeDtypeStruct((t, D), BF16), tiles]
    else:
        out_specs += [pl.BlockSpec((tm, D), row)]
        out_shape += [jax.ShapeDtypeStruct((t, D), F32)]
    return pl.pallas_call(
        functools.partial(_fox_proj_kernel, attn_copies=attn_copies),
        grid=(t // tm,),
        in_specs=[pl.BlockSpec((tm, D), row), pl.BlockSpec((D, C_COLS_PAD), lambda i: (0, 0)),
                  pl.BlockSpec((1, LANES), lambda i: (0, 0))],
        out_specs=out_specs,
        out_shape=out_shape,
        compiler_params=_cp(("parallel",)),
        name="proj_c",
    )(x, w, bf_row)


def _outproj_ln_kernel(a_ref, x_ref, w_ref, g_ref, b_ref, o_ref):
    h = jnp.dot(a_ref[...].astype(BF16), w_ref[...], preferred_element_type=F32)
    o_ref[...] = _layer_norm(ALPHA * x_ref[...] + h, g_ref[...], b_ref[...])


def _outproj_ln(a, x, w, g, b, tm):
    t = x.shape[0]
    row = lambda i: (i, 0)
    fixed = lambda i: (0, 0)
    return pl.pallas_call(
        _outproj_ln_kernel,
        grid=(t // tm,),
        in_specs=[pl.BlockSpec((tm, D), row), pl.BlockSpec((tm, D), row), pl.BlockSpec((D, D), fixed),
                  pl.BlockSpec((1, D), fixed), pl.BlockSpec((1, D), fixed)],
        out_specs=pl.BlockSpec((tm, D), row),
        out_shape=jax.ShapeDtypeStruct((t, D), F32),
        compiler_params=_cp(("parallel",)),
        name="outproj_ln",
    )(a, x, w, g, b)


def _unit_lower_inverses(a_list, ri, ci):
    eye = (ri == ci).astype(F32)
    blk = (ri >> 4) == (ci >> 4)
    d = [_split2(jnp.where(blk, a, 0.0)) for a in a_list]
    d2f = [_mm_s(x, x) for x in d]
    d2 = [_split2(x) for x in d2f]
    d4f = [_mm_s(x, x) for x in d2]
    d4 = [_split2(x) for x in d4f]
    d8 = [_split2(_mm_s(x, x)) for x in d4]
    xs = [eye - jnp.where(blk, a, 0.0) for a in a_list]
    for pw in (d2, d4, d8):
        xs = [x + _mm_s(_split2(x), p) for x, p in zip(xs, pw)]
    for s in (5, 6, 7):
        msk = ((ri >> s) == (ci >> s)) & ((ri >> (s - 1)) != (ci >> (s - 1)))
        es = [_split2(jnp.where(msk, a, 0.0)) for a in a_list]
        xsp = [_split2(x) for x in xs]
        xe = [_split2(_mm_s(x, e)) for x, e in zip(xsp, es)]
        xs = [x - _mm_s(y, xp) for x, y, xp in zip(xs, xe, xsp)]
    return xs


def _head_gate_norm(out, g_row, gate):
    return out * lax.rsqrt(jnp.mean(out * out, axis=-1, keepdims=True) + NORM_EPS) * g_row * _silu(gate)


def _mixer_a_prompt_kernel(p_ref, cos_ref, sin_ref, cw_ref, alog_ref, dt_ref, rg_ref, gg_ref,
                           mix_ref, ret_ref, gdn_ref, conv_ref, ext_ref, *, nb):
    n = pl.program_id(0)
    c = CHUNK

    @pl.when(n == 0)
    def _init():
        ret_ref[...] = jnp.zeros_like(ret_ref)
        gdn_ref[...] = jnp.zeros_like(gdn_ref)
        ext_ref[:, 0:8, :] = jnp.zeros((nb, 8, CONV_CH), F32)

    ri = _iota2((c, c), 0)
    ci = _iota2((c, c), 1)
    rf = ri.astype(F32)
    diff = rf - ci.astype(F32)
    cosv = cos_ref[...]
    sinv = sin_ref[...]

    lgs = [math.log1p(-(2.0 ** (-5.0 - h))) for h in range(RH)]
    decay = [jnp.where(diff >= 0, jnp.exp(jnp.maximum(diff, 0.0) * lg), 0.0) for lg in lgs]
    inner = [jnp.exp((rf + 1.0) * lg) for lg in lgs]
    kdec = [jnp.exp((c - 1.0 - rf) * lg) for lg in lgs]
    r_units = [(b, h) for b in range(nb) for h in range(RH)]
    r_q, r_k = [], []
    for b, h in r_units:
        rq = p_ref[b, :, h * RDK:(h + 1) * RDK]
        rk = p_ref[b, :, RH * RDK + h * RDK:RH * RDK + (h + 1) * RDK]
        r_q.append(rq * cosv + pltpu.roll(rq, RDK // 2, 1) * sinv)
        r_k.append((rk * cosv + pltpu.roll(rk, RDK // 2, 1) * sinv) * (RDK ** -0.5))
    r_v = [p_ref[b, :, 2 * RH * RDK + h * RDV:2 * RH * RDK + (h + 1) * RDV] for b, h in r_units]
    r_s = [ret_ref[b, h] for b, h in r_units]
    r_scores = [_mm_nt(q, k) * decay[h] for q, k, (_, h) in zip(r_q, r_k, r_units)]
    r_state_out = [_mm(q * inner[h], s) for q, s, (_, h) in zip(r_q, r_s, r_units)]
    r_new = [_mm_tn(k * kdec[h], v) for k, v, (_, h) in zip(r_k, r_v, r_units)]
    r_out = [_mm(sc, v) + so for sc, v, so in zip(r_scores, r_v, r_state_out)]
    for (b, h), s, new, out in zip(r_units, r_s, r_new, r_out):
        ret_ref[b, h] = math.exp(c * lgs[h]) * s + new
        rgate = p_ref[b, :, 2 * RH * RDK + RH * RDV + h * RDV:2 * RH * RDK + RH * RDV + (h + 1) * RDV]
        mix_ref[b, :, h * RDV:(h + 1) * RDV] = _head_gate_norm(out, rg_ref[...], rgate)

    g_off = 2 * RH * RDK + 2 * RH * RDV
    z_off = g_off + CONV_CH
    incl = ri >= ci
    strict = ri > ci
    tril = incl.astype(BF16)
    units = []
    for b in range(nb):
        ext_ref[b, 8:8 + c, :] = p_ref[b, :, g_off:g_off + CONV_CH]
        y = ext_ref[b, 5:5 + c, :] * cw_ref[0:1, :]
        for i in range(1, CONV_W):
            y = y + ext_ref[b, 5 + i:5 + i + c, :] * cw_ref[i:i + 1, :]
        conv_ref[b] = ext_ref[b, c + 5:c + 8, :]
        ext_ref[b, 0:8, :] = ext_ref[b, c:c + 8, :]
        y = _silu(y)
        tail = p_ref[b, :, z_off + GH * GDV:z_off + GH * GDV + LANES]
        g_all = -jnp.exp(alog_ref[...]) * _softplus(tail + dt_ref[...])
        beta_all = _sigmoid(tail)
        for h in range(GH):
            qh = y[:, h * GDK:(h + 1) * GDK]
            kh = y[:, GH * GDK + h * GDK:GH * GDK + (h + 1) * GDK]
            vh = y[:, 2 * GH * GDK + h * GDV:2 * GH * GDK + (h + 1) * GDV]
            qh = qh * lax.rsqrt(jnp.sum(qh * qh, axis=-1, keepdims=True) + NORM_EPS) * (GDK ** -0.5)
            kh = kh * lax.rsqrt(jnp.sum(kh * kh, axis=-1, keepdims=True) + NORM_EPS)
            gb = jnp.broadcast_to(g_all[:, h:h + 1], (c, c))
            bcol = beta_all[:, GH + h:GH + h + 1]
            units.append((b, h, qh, kh, vh, gb, bcol))

    gcs = [_mm_01_left(tril, u[5]) for u in units]
    decs = []
    for gc in gcs:
        dmat = gc - gc.T
        decs.append(jnp.where(incl, jnp.exp(jnp.where(incl, dmat, 0.0)), 0.0))
    kbs = [u[3] * u[6] for u in units]
    a_list = [jnp.where(strict, _mm_nt(kb, u[3]) * dec, 0.0) for kb, u, dec in zip(kbs, units, decs)]
    ts = [t.astype(BF16) for t in _unit_lower_inverses(a_list, ri, ci)]
    egcs = [jnp.exp(gc) for gc in gcs]
    us = [jnp.dot(t, (u[4] * u[6]).astype(BF16), preferred_element_type=F32) for t, u in zip(ts, units)]
    ws = [jnp.dot(t, (kb * egc).astype(BF16), preferred_element_type=F32) for t, kb, egc in zip(ts, kbs, egcs)]
    attns = [_mm_nt(u[2], u[3]) * dec for u, dec in zip(units, decs)]
    ss = [gdn_ref[u[0], u[1]] for u in units]
    v_news = [uu - _mm(w, s) for uu, w, s in zip(us, ws, ss)]
    outs = [_mm(u[2] * egc, s) + _mm(attn, vn) for u, egc, s, attn, vn in zip(units, egcs, ss, attns, v_news)]
    for u, gc, s, vn, out in zip(units, gcs, ss, v_news, outs):
        b, h = u[0], u[1]
        g_last = gc[c - 1:c, :]
        gdn_ref[b, h] = jnp.exp(g_last) * s + _mm_tn(u[3] * jnp.exp(g_last - gc), vn)
        gz = p_ref[b, :, z_off + h * GDV:z_off + (h + 1) * GDV]
        mix_ref[b, :, RH * RDV + h * GDV:RH * RDV + (h + 1) * GDV] = _head_gate_norm(out, gg_ref[...], gz)


def _mixer_a_prompt(proj, cos2, sin2, conv_w, alog_row, dt_row, rg_row, gg_row):
    b, l, _ = proj.shape
    fixed = lambda n: (0, 0)
    return pl.pallas_call(
        functools.partial(_mixer_a_prompt_kernel, nb=b),
        grid=(l // CHUNK,),
        in_specs=[pl.BlockSpec((b, CHUNK, A_COLS_PAD), lambda n: (0, n, 0)),
                  pl.BlockSpec((CHUNK, LANES), lambda n: (n, 0)),
                  pl.BlockSpec((CHUNK, LANES), lambda n: (n, 0)),
                  pl.BlockSpec((CONV_W, CONV_CH), fixed),
                  pl.BlockSpec((1, LANES), fixed), pl.BlockSpec((1, LANES), fixed),
                  pl.BlockSpec((1, LANES), fixed), pl.BlockSpec((1, LANES), fixed)],
        out_specs=[pl.BlockSpec((b, CHUNK, D), lambda n: (0, n, 0)),
                   pl.BlockSpec((b, RH, RDK, RDV), lambda n: (0, 0, 0, 0)),
                   pl.BlockSpec((b, GH, GDK, GDV), lambda n: (0, 0, 0, 0)),
                   pl.BlockSpec((b, CONV_W - 1, CONV_CH), lambda n: (0, 0, 0))],
        out_shape=[jax.ShapeDtypeStruct((b, l, D), F32),
                   jax.ShapeDtypeStruct((b, RH, RDK, RDV), F32),
                   jax.ShapeDtypeStruct((b, GH, GDK, GDV), F32),
                   jax.ShapeDtypeStruct((b, CONV_W - 1, CONV_CH), F32)],
        scratch_shapes=[pltpu.VMEM((b, CHUNK + 8, CONV_CH), F32)],
        compiler_params=_cp(("arbitrary",)),
        name="mixer_a_prompt",
    )(proj, cos2, sin2, conv_w, alog_row, dt_row, rg_row, gg_row)


def _mixer_a_sample_kernel(p_ref, cos_ref, sin_ref, cw_ref, alog_ref, dt_ref, rg_ref, gg_ref,
                           rs_ref, gs_ref, cs_ref, mix_ref, ret_ref, gdn_ref, conv_ref):
    eye = _iota2((LANES, LANES), 0) == _iota2((LANES, LANES), 1)

    def col(v):
        return jnp.sum(jnp.where(eye, jnp.broadcast_to(v, (LANES, LANES)), 0.0), axis=1, keepdims=True)

    def vec_mat(c, s):
        return jnp.sum(c * s, axis=0, keepdims=True)

    cosv = cos_ref[...]
    sinv = sin_ref[...]
    rq = [p_ref[0, :, h * RDK:(h + 1) * RDK] for h in range(RH)]
    rk = [p_ref[0, :, RH * RDK + h * RDK:RH * RDK + (h + 1) * RDK] for h in range(RH)]
    rv = [p_ref[0, :, 2 * RH * RDK + h * RDV:2 * RH * RDK + (h + 1) * RDV] for h in range(RH)]
    rq = [x * cosv + pltpu.roll(x, RDK // 2, 1) * sinv for x in rq]
    rk = [(x * cosv + pltpu.roll(x, RDK // 2, 1) * sinv) * (RDK ** -0.5) for x in rk]

    g_off = 2 * RH * RDK + 2 * RH * RDV
    gq_all = p_ref[0, :, g_off:g_off + CONV_CH]
    cb = cs_ref[0]
    y = cb[0:1, :] * cw_ref[0:1, :] + cb[1:2, :] * cw_ref[1:2, :] + cb[2:3, :] * cw_ref[2:3, :] \
        + gq_all * cw_ref[3:4, :]
    conv_ref[0, 0:2, :] = cb[1:3, :]
    conv_ref[0, 2:3, :] = gq_all
    y = _silu(y)
    z_off = g_off + CONV_CH
    tail = p_ref[0, :, z_off + GH * GDV:z_off + GH * GDV + LANES]
    g_all = -jnp.exp(alog_ref[...]) * _softplus(tail + dt_ref[...])
    beta_all = _sigmoid(tail)
    gq = [y[:, h * GDK:(h + 1) * GDK] for h in range(GH)]
    gk = [y[:, GH * GDK + h * GDK:GH * GDK + (h + 1) * GDK] for h in range(GH)]
    gv = [y[:, 2 * GH * GDK + h * GDV:2 * GH * GDK + (h + 1) * GDV] for h in range(GH)]
    gq = [x * lax.rsqrt(jnp.sum(x * x, axis=-1, keepdims=True) + NORM_EPS) * (GDK ** -0.5) for x in gq]
    gk = [x * lax.rsqrt(jnp.sum(x * x, axis=-1, keepdims=True) + NORM_EPS) for x in gk]
    eg = [jnp.exp(g_all[:, h:h + 1]) for h in range(GH)]
    beta = [beta_all[:, GH + h:GH + h + 1] for h in range(GH)]

    q_cols = [col(x) for x in rq + gq]
    k_cols = [col(x) for x in rk + gk]
    r_state = [rs_ref[0, h] for h in range(RH)]
    g_state = [gs_ref[0, h] for h in range(GH)]
    qs = [vec_mat(c, s) for c, s in zip(q_cols, r_state + g_state)]
    ws = [vec_mat(k_cols[RH + h] * (beta[h] * eg[h]), g_state[h]) for h in range(GH)]
    r_qk = [jnp.sum(a * b, axis=-1, keepdims=True) for a, b in zip(rq, rk)]
    g_qk = [jnp.sum(a * b, axis=-1, keepdims=True) for a, b in zip(gq, gk)]
    for h in range(RH):
        gamma = 1.0 - 2.0 ** (-5.0 - h)
        out = r_qk[h] * rv[h] + gamma * qs[h]
        ret_ref[0, h] = gamma * r_state[h] + k_cols[h] * rv[h]
        rgate = p_ref[0, :, 2 * RH * RDK + RH * RDV + h * RDV:2 * RH * RDK + RH * RDV + (h + 1) * RDV]
        mix_ref[0, :, h * RDV:(h + 1) * RDV] = _head_gate_norm(out, rg_ref[...], rgate)
    for h in range(GH):
        v_new = gv[h] * beta[h] - ws[h]
        out = eg[h] * qs[RH + h] + g_qk[h] * v_new
        gdn_ref[0, h] = eg[h] * g_state[h] + k_cols[RH + h] * v_new
        gz = p_ref[0, :, z_off + h * GDV:z_off + (h + 1) * GDV]
        mix_ref[0, :, RH * RDV + h * GDV:RH * RDV + (h + 1) * GDV] = _head_gate_norm(out, gg_ref[...], gz)


def _mixer_a_sample(proj, cos2, sin2, conv_w, alog_row, dt_row, rg_row, gg_row, ret_s, gdn_s, conv_s):
    b = proj.shape[0]
    fixed = lambda i: (0, 0)
    st = lambda i: (i, 0, 0, 0)
    return pl.pallas_call(
        _mixer_a_sample_kernel,
        grid=(b,),
        in_specs=[pl.BlockSpec((1, 1, A_COLS_PAD), lambda i: (i, 0, 0)),
                  pl.BlockSpec((1, LANES), fixed), pl.BlockSpec((1, LANES), fixed),
                  pl.BlockSpec((CONV_W, CONV_CH), fixed),
                  pl.BlockSpec((1, LANES), fixed), pl.BlockSpec((1, LANES), fixed),
                  pl.BlockSpec((1, LANES), fixed), pl.BlockSpec((1, LANES), fixed),
                  pl.BlockSpec((1, RH, RDK, RDV), st), pl.BlockSpec((1, GH, GDK, GDV), st),
                  pl.BlockSpec((1, CONV_W - 1, CONV_CH), lambda i: (i, 0, 0))],
        out_specs=[pl.BlockSpec((1, 1, D), lambda i: (i, 0, 0)),
                   pl.BlockSpec((1, RH, RDK, RDV), st), pl.BlockSpec((1, GH, GDK, GDV), st),
                   pl.BlockSpec((1, CONV_W - 1, CONV_CH), lambda i: (i, 0, 0))],
        out_shape=[jax.ShapeDtypeStruct((b, 1, D), F32),
                   jax.ShapeDtypeStruct((b, RH, RDK, RDV), F32),
                   jax.ShapeDtypeStruct((b, GH, GDK, GDV), F32),
                   jax.ShapeDtypeStruct((b, CONV_W - 1, CONV_CH), F32)],
        compiler_params=_cp(("parallel",)),
        name="mixer_a_sample",
    )(proj, cos2, sin2, conv_w, alog_row, dt_row, rg_row, gg_row, ret_s, gdn_s, conv_s)


def _fox_cumsum_kernel(x_ref, o_ref):
    n = x_ref.shape[1] // LANES
    upper = (_iota2((LANES, LANES), 0) <= _iota2((LANES, LANES), 1)).astype(BF16)
    carry = jnp.zeros((FH, 1), F32)
    for c in range(n):
        p = _mm_01_right(x_ref[:, c * LANES:(c + 1) * LANES], upper) + carry
        o_ref[0, :, c * LANES:(c + 1) * LANES] = p
        carry = p[:, LANES - 1:LANES]


def _fox_cumsum(lft, b, l):
    return pl.pallas_call(
        _fox_cumsum_kernel,
        grid=(b,),
        in_specs=[pl.BlockSpec((FH, l), lambda i: (0, i))],
        out_specs=pl.BlockSpec((1, FH, l), lambda i: (i, 0, 0)),
        out_shape=jax.ShapeDtypeStruct((b, FH, l), F32),
        compiler_params=_cp(("parallel",)),
        name="fox_cumsum",
    )(lft)


def _fox_flash_kernel(q_ref, k_ref, v_ref, ck_ref, o_ref, ka_ref, sa_ref, sb_ref, *, tq, tk):
    assert tq == 2 * tk and tk % LANES == 0
    qi = pl.program_id(2)
    nk = k_ref.shape[1] // tk

    @pl.when(qi == 0)
    def _augment_keys():
        eye = _iota2((LANES, LANES), 0) == _iota2((LANES, LANES), 1)
        lane = _iota2((LANES, LANES), 1)
        for j in range(nk):
            ka_ref[j * tk:(j + 1) * tk, 0:FHD] = k_ref[0, j * tk:(j + 1) * tk, :]
            for c in range(tk // LANES):
                row = jnp.broadcast_to(ck_ref[0, j:j + 1, c * LANES:(c + 1) * LANES], (LANES, LANES))
                col = jnp.sum(jnp.where(eye, row, 0.0), axis=1, keepdims=True)
                c1, c2, c3 = (t.astype(F32) for t in _split3(col))
                blk = jnp.where(lane == 0, c1, jnp.where(lane == 1, c2, jnp.where(lane == 2, c3, 0.0)))
                r0 = j * tk + c * LANES
                ka_ref[r0:r0 + LANES, FHD:2 * FHD] = blk.astype(BF16)

    sub = _iota2((FHD, tq), 0)
    minus_ones = jnp.where(sub < 3, -1.0, 0.0).astype(BF16)
    qa = jnp.concatenate([q_ref[0, 0], q_ref[1, 0]], axis=1)
    qa = jnp.concatenate([qa, minus_ones], axis=0)

    def scores(j, s_ref):
        ka = ka_ref[pl.ds(pl.multiple_of(j * tk, tk), tk), :]
        s_ref[...] = jnp.dot(ka, qa, preferred_element_type=F32)

    def update(j, s_ref, carry, masked):
        m_old, l_old, acc = carry
        s = s_ref[...]
        if masked:
            kpos = j * tk + _iota2((tk, tq), 0)
            qpos = qi * tq + _iota2((tk, tq), 1)
            s = jnp.where(kpos <= qpos, s, -jnp.inf)
        m_new = jnp.maximum(m_old, jnp.max(s, axis=0, keepdims=True))
        alpha = jnp.exp(m_old - m_new)
        p = jnp.exp(s - m_new)
        l_new = alpha * l_old + jnp.sum(p, axis=0, keepdims=True)
        acc = alpha * acc + jnp.dot(v_ref[j, 0], p.astype(BF16), preferred_element_type=F32)
        return m_new, l_new, acc

    def two_blocks(i, carry):
        scores(2 * i + 1, sb_ref)
        carry = update(2 * i, sa_ref, carry, False)
        scores(2 * i + 2, sa_ref)
        return update(2 * i + 1, sb_ref, carry, False)

    scores(0, sa_ref)
    init = (jnp.full((1, tq), -jnp.inf, F32), jnp.zeros((1, tq), F32), jnp.zeros((FHD, tq), F32))
    carry = lax.fori_loop(0, qi, two_blocks, init)
    scores(2 * qi + 1, sb_ref)
    carry = update(2 * qi, sa_ref, carry, True)
    carry = update(2 * qi + 1, sb_ref, carry, True)
    o_ref[0] = (carry[2] / carry[1]).T


def _fox_flash(qt, kb, vt, cum, tq, tk):
    b, l, _ = kb.shape
    nk = l // tk
    return pl.pallas_call(
        functools.partial(_fox_flash_kernel, tq=tq, tk=tk),
        grid=(b, FH, l // tq),
        in_specs=[pl.BlockSpec((tq // tk, 1, FHD, tk), lambda i, h, qi: (i * (l // tq) + qi, h, 0, 0)),
                  pl.BlockSpec((1, l, FHD), lambda i, h, qi: (i, 0, h)),
                  pl.BlockSpec((nk, 1, FHD, tk), lambda i, h, qi: (i, h, 0, 0)),
                  pl.BlockSpec((1, nk, tk), lambda i, h, qi: (i * FH + h, 0, 0))],
        out_specs=pl.BlockSpec((1, tq, FHD), lambda i, h, qi: (i, qi, h)),
        out_shape=jax.ShapeDtypeStruct((b, l, D), F32),
        scratch_shapes=[pltpu.VMEM((l, 2 * FHD), BF16), pltpu.VMEM((tk, tq), F32), pltpu.VMEM((tk, tq), F32)],
        compiler_params=_cp(("arbitrary", "arbitrary", "arbitrary")),
        name="fox_flash",
    )(qt, kb, vt, cum)


def _fox_sample_kernel(pt_ref, q_ref, kn_ref, vn_ref, lfn_ref, *rest):
    npp = PAGES_PER_STEP
    k_refs, v_refs, lf_refs = rest[0:npp], rest[npp:2 * npp], rest[2 * npp:3 * npp]
    o_ref, m_ref, l_ref, acc_ref, carry_ref = rest[3 * npp:]
    j = pl.program_id(1)
    scale = FHD ** -0.5
    rows = PAGE * FH

    @pl.when(j == 0)
    def _init():
        m_ref[...] = jnp.full(m_ref.shape, -jnp.inf, F32)
        l_ref[...] = jnp.zeros(l_ref.shape, F32)
        acc_ref[...] = jnp.zeros(acc_ref.shape, F32)
        carry_ref[...] = jnp.zeros(carry_ref.shape, F32)

    q = q_ref[0]
    qb = (q * scale).astype(BF16)
    own = (_iota2((FH, rows), 1) & (FH - 1)) == _iota2((FH, rows), 0)
    nch = rows // LANES
    ci = _iota2((LANES, 2 * LANES), 1)
    pre_tot = ((_iota2((LANES, 2 * LANES), 0) <= ci) | (ci >= LANES)).astype(BF16)
    stacked, s_pages = [], []
    for pg in range(npp):
        kp = k_refs[pg][0].reshape(rows, FHD).astype(BF16)
        s_pages.append(lax.dot_general(qb, kp, NT, preferred_element_type=F32))
        lf = jnp.where(own, jnp.broadcast_to(lf_refs[pg][0], (FH, rows)), 0.0)
        stacked += [lf[:, c * LANES:(c + 1) * LANES] for c in range(nch)]
    pt = _mm_01_right(jnp.concatenate(stacked, axis=0), pre_tot)
    carry = carry_ref[...]
    logits = []
    for pg in range(npp):
        cums = []
        for c in range(nch):
            r0 = (pg * nch + c) * FH
            cums.append(pt[r0:r0 + FH, 0:LANES] + carry)
            carry = carry + pt[r0:r0 + FH, LANES:2 * LANES]
        logits.append(jnp.where(own, s_pages[pg] - jnp.concatenate(cums, axis=1), -jnp.inf))
    carry_ref[...] = carry
    m_old = m_ref[...]
    m_new = m_old
    for lg in logits:
        m_new = jnp.maximum(m_new, jnp.max(lg, axis=1, keepdims=True))
    alpha = jnp.exp(m_old - m_new)
    l_new = alpha * l_ref[...]
    acc = alpha * acc_ref[...]
    for pg in range(npp):
        p = jnp.exp(logits[pg] - m_new)
        l_new = l_new + jnp.sum(p, axis=1, keepdims=True)
        vp = v_refs[pg][0].reshape(rows, FHD).astype(BF16)
        acc = acc + jnp.dot(p.astype(BF16), vp, preferred_element_type=F32)
    m_ref[...] = m_new
    l_ref[...] = l_new
    acc_ref[...] = acc

    @pl.when(j == pl.num_programs(1) - 1)
    def _fin():
        s_new = jnp.sum(q * kn_ref[0], axis=1, keepdims=True) * scale - (carry[:, 0:1] + lfn_ref[0])
        m2 = jnp.maximum(m_new, s_new)
        a2 = jnp.exp(m_new - m2)
        p2 = jnp.exp(s_new - m2)
        o_ref[0] = (a2 * acc + p2 * vn_ref[0]) / (a2 * l_new + p2)


def _fox_sample(page_table, q3, k3, v3, lf3, k_pool, v_pool, lf_pool):
    b = q3.shape[0]
    npg = page_table.shape[1]
    npp = PAGES_PER_STEP
    assert npg % npp == 0
    pt = page_table.reshape(-1)
    tok = lambda i, j, pt_ref: (i, 0, 0)

    def page(r, nd):
        return lambda i, j, pt_ref: (pt_ref[i * npg + j * npp + r],) + (0,) * nd

    grid_spec = pltpu.PrefetchScalarGridSpec(
        num_scalar_prefetch=1,
        grid=(b, npg // npp),
        in_specs=[pl.BlockSpec((1, FH, FHD), tok), pl.BlockSpec((1, FH, FHD), tok), pl.BlockSpec((1, FH, FHD), tok),
                  pl.BlockSpec((1, FH, 1), tok)]
        + [pl.BlockSpec((1, PAGE, FH, FHD), page(r, 3)) for r in range(npp)]
        + [pl.BlockSpec((1, PAGE, FH, FHD), page(r, 3)) for r in range(npp)]
        + [pl.BlockSpec((1, 1, PAGE * FH), page(r, 2)) for r in range(npp)],
        out_specs=pl.BlockSpec((1, FH, FHD), tok),
        scratch_shapes=[pltpu.VMEM((FH, 1), F32), pltpu.VMEM((FH, 1), F32), pltpu.VMEM((FH, FHD), F32),
                        pltpu.VMEM((FH, LANES), F32)],
    )
    lf_flat =lf_pool.reshape(lf_pool.shape[0], 1, PAGE * FH)
    return pl.pallas_call(
        _fox_sample_kernel,
        grid_spec=grid_spec,
        out_shape=jax.ShapeDtypeStruct((b, FH, FHD), F32),
        compiler_params=_cp(("parallel", "arbitrary")),
        name="fox_sample",
    )(pt, q3, k3, v3, lf3, *([k_pool] * npp), *([v_pool] * npp), *([lf_flat] * npp))


_CAND_BLOCKS = [(a, 16 if a == 0 else 8, 16 // (a + 1)) for a in range(8)]


_MARK0 = -(2.0 ** 127)
_MARK_STEP = 2.0 ** 120
_MARK_LIMIT = -(2.0 ** 126)


def _top16(s, iota):
    rank = jnp.full(s.shape, float(PTOPK), F32)
    vals = jnp.zeros((PTOPK, s.shape[1]), F32)
    i16 = _iota2((PTOPK, s.shape[1]), 0)
    for r in range(PTOPK):
        m = jnp.max(s, axis=0, keepdims=True)
        sel = iota == jnp.min(jnp.where(s == m, iota, 1e9), axis=0, keepdims=True)
        rank = jnp.where(sel, float(r), rank)
        s = jnp.where(sel, -jnp.inf, s)
        vals = jnp.where(i16 == r, m, vals)
    return rank, vals


def _top16_fast(s):
    lowest = jnp.min(s)
    vals = jnp.zeros((PTOPK, s.shape[1]), F32)
    i16 = _iota2((PTOPK, s.shape[1]), 0)
    for r in range(PTOPK):
        m = jnp.max(s, axis=0, keepdims=True)
        s = jnp.where(s == m, _MARK0 + r * _MARK_STEP, s)
        vals = jnp.where(i16 == r, m, vals)
    marked = s < _MARK_LIMIT
    rank = jnp.where(marked, (s - _MARK0) * (1.0 / _MARK_STEP), float(PTOPK))
    taken = jnp.sum(jnp.where(marked, 1.0, 0.0), axis=0, keepdims=True)
    return rank, vals, _not_16(taken) | (lowest <= _MARK_LIMIT)


def _pick16(cand, cidx, exact):
    picked = jnp.zeros(cand.shape, F32)
    for _ in range(PTOPK):
        m = jnp.max(cand, axis=0, keepdims=True)
        sel = cand == m
        if exact:
            sel = cidx == jnp.min(jnp.where(sel, cidx, 1e9), axis=0, keepdims=True)
        picked = jnp.where(sel, 1.0, picked)
        cand = jnp.where(sel, -jnp.inf, cand)
    return picked


def _not_16(count):
    return jnp.max(jnp.abs(count - float(PTOPK))) > 0.5


def _route_head(s_t):
    s1 = s_t[0:NKEYS]
    s2 = s_t[NKEYS:2 * NKEYS]
    w = s_t.shape[1]
    iota = _iota2((NKEYS, w), 0).astype(F32)
    rank1, v1, redo1 = _top16_fast(s1)
    rank2, v2, redo2 = _top16_fast(s2)
    rank1, v1, rank2, v2 = lax.cond(redo1 | redo2, lambda: _top16(s1, iota) + _top16(s2, iota),
                                    lambda: (rank1, v1, rank2, v2))
    e1 = jnp.exp(v1 - v1[0:1])
    e2 = jnp.exp(v2 - v2[0:1])
    i8 = _iota2((8, w), 0)
    i8f = i8.astype(F32)
    i16f = _iota2((16, w), 0).astype(F32)
    cand, cidx, cprob = [], [], []
    for a, rows, valid in _CAND_BLOCKS:
        c = v1[a:a + 1] + v2[0:rows]
        pr = e1[a:a + 1] * e2[0:rows]
        if rows == 16:
            ix = i16f
        else:
            ix = i8f + float(a * PTOPK)
            c = jnp.where(i8 < valid, c, -jnp.inf)
        cand.append(c)
        cidx.append(ix)
        cprob.append(pr)
    cand.append(v1[8:16] + v2[0:1])
    cidx.append((i8f + 8.0) * float(PTOPK))
    cprob.append(e1[8:16] * e2[0:1])
    cand = jnp.concatenate(cand, axis=0)
    cidx = jnp.concatenate(cidx, axis=0)
    cprob = jnp.concatenate(cprob, axis=0)
    picked = _pick16(cand, cidx, False)
    picked = lax.cond(_not_16(jnp.sum(picked, axis=0, keepdims=True)),
                      lambda: _pick16(cand, cidx, True), lambda: picked)
    z = jnp.sum(picked * cprob, axis=0, keepdims=True)
    n_low = jnp.zeros((8, w), F32)
    off = 0
    for a, rows, _ in _CAND_BLOCKS:
        cnt = jnp.sum(picked[off:off + rows], axis=0, keepdims=True)
        n_low = jnp.where(i8 == a, cnt, n_low)
        off += rows
    n16 = jnp.concatenate([n_low, picked[off:off + 8]], axis=0)
    n1d = jnp.zeros((NKEYS, w), F32)
    for a in range(PTOPK):
        n1d = jnp.where(rank1 == float(a), n16[a:a + 1], n1d)
    e1d = jnp.exp(s1 - v1[0:1]) * (0.5 / z)
    e2d = jnp.exp(s2 - v2[0:1])
    return rank2, e2d, n1d, e1d


def _peer_kernel(x_ref, wq_ref, keys_ref, u0_ref, un_ref, vt_ref, g_ref, b_ref, o_ref,
                 xb_ref, q_ref, r2_ref, e2_ref, n1_ref, e1_ref, ha_ref, hb_ref, at_ref, acc_ref, *, tm, te, rw):
    j = pl.program_id(1)
    nlg = tm // LANES
    nsub = te // NKEYS

    @pl.when(j == 0)
    def _route():
        xb = x_ref[...].astype(BF16)
        xb_ref[...] = xb
        q = jnp.dot(xb, wq_ref[...], preferred_element_type=F32)
        for h in range(PH):
            q_ref[h] = q[:, h * LANES:(h + 1) * LANES].astype(BF16)
        acc_ref[...] = jnp.zeros(acc_ref.shape, F32)
        ha_ref[:, 0:tm] = lax.dot_general(u0_ref[...], xb, NT, preferred_element_type=F32)

        def body(h, carry):
            for w0 in range(0, tm, rw):
                s_t = lax.dot_general(keys_ref[...], q_ref[h, w0:w0 + rw, :], NT,
                                      preferred_element_type=F32)
                r2, e2, n1, e1 = _route_head(s_t)
                for g in range(rw // LANES):
                    lg = slice(g * LANES, (g + 1) * LANES)
                    r2_ref[h, w0 // LANES + g] = r2[:, lg].astype(GATE_DT)
                    e2_ref[h, w0 // LANES + g] = e2[:, lg].astype(GATE_DT)
                    n1_ref[h, w0 // LANES + g] = n1[:, lg]
                    e1_ref[h, w0 // LANES + g] = e1[:, lg]
            return carry

        lax.fori_loop(0, PH, body, 0)

    def step(h_cur_ref, h_next_ref):
        h_next_ref[:, 0:tm] = lax.dot_general(un_ref[...], xb_ref[...], NT, preferred_element_type=F32)
        zero = jnp.zeros((), GATE_DT)
        pair = 2
        for pp in range(nsub // pair):
            for g in range(nlg):
                lanes = slice(g * LANES, (g + 1) * LANES)
                gates = [jnp.zeros((NKEYS, LANES), GATE_DT) for _ in range(pair)]
                for h in range(PH):
                    r2 = r2_ref[h, g]
                    e2 = e2_ref[h, g]
                    for ii in range(pair):
                        i1 = j * nsub + pp * pair + ii
                        n_row = n1_ref[h, g, pl.ds(i1, 1), :].astype(GATE_DT)
                        e_row = e1_ref[h, g, pl.ds(i1, 1), :].astype(GATE_DT)
                        gates[ii] = gates[ii] + jnp.where(r2 < n_row, e2, zero) * e_row
                for ii in range(pair):
                    r0 = (pp * pair + ii) * NKEYS
                    hb = h_cur_ref[r0:r0 + NKEYS, lanes]
                    act = hb * (1.0 + lax.erf(hb * (2.0 ** -0.5)))
                    at_ref[r0:r0 + NKEYS, lanes] = (act.astype(GATE_DT) * gates[ii]).astype(BF16)
        acc_ref[...] += jnp.dot(vt_ref[0], at_ref[:, 0:tm], preferred_element_type=F32)

    @pl.when(j % 2 == 0)
    def _even():
        step(ha_ref, hb_ref)

    @pl.when(j % 2 == 1)
    def _odd():
        step(hb_ref, ha_ref)

    @pl.when(j == pl.num_programs(1) - 1)
    def _fin():
        o_ref[...] = _layer_norm(ALPHA * x_ref[...] + acc_ref[...].T, g_ref[...], b_ref[...])


def _peer(x, wq, keys, u, vt, g, b, tm, te):
    t = x.shape[0]
    nj = NEXP // te
    fixed = lambda i, j: (0, 0)
    row_pad = LANES if (tm // LANES) % 4 == 0 else 0
    return pl.pallas_call(
        functools.partial(_peer_kernel, tm=tm, te=te, rw=tm),
        grid=(t // tm, nj),
        in_specs=[pl.BlockSpec((tm, D), lambda i, j: (i, 0)),
                  pl.BlockSpec((D, PH * LANES), fixed),
                  pl.BlockSpec((2 * NKEYS, LANES), fixed),
                  pl.BlockSpec((te, D), fixed),
                  pl.BlockSpec((te, D), lambda i, j: (jnp.minimum(j + 1, nj - 1), 0)),
                  pl.BlockSpec((1, D, te), lambda i, j: (j, 0, 0)),
                  pl.BlockSpec((1, D), fixed), pl.BlockSpec((1, D), fixed)],
        out_specs=pl.BlockSpec((tm, D), lambda i, j: (i, 0)),
        out_shape=jax.ShapeDtypeStruct((t, D), F32),
        scratch_shapes=[pltpu.VMEM((tm, D), BF16),
                        pltpu.VMEM((PH, tm, LANES), BF16),
                        pltpu.VMEM((PH, tm // LANES, NKEYS, LANES), GATE_DT),
                        pltpu.VMEM((PH, tm // LANES, NKEYS, LANES), GATE_DT),
                        pltpu.VMEM((PH, tm // LANES, NKEYS, LANES), F32),
                        pltpu.VMEM((PH, tm // LANES, NKEYS, LANES), F32),
                        pltpu.VMEM((te, tm + row_pad), F32),
                        pltpu.VMEM((te, tm + row_pad), F32),
                        pltpu.VMEM((te, tm + row_pad), BF16),
                        pltpu.VMEM((D, tm), F32)],
        compiler_params=_cp(("parallel", "arbitrary")),
        name="peer",
    )(x, wq, keys, u, u, vt, g, b)


def _rope_tables(pos):
    half = RDK // 2
    inv = ROPE_BASE ** (-jnp.arange(half, dtype=F32) / half)
    ang = pos.astype(F32)[:, None] * inv[None, :]
    cos, sin = jnp.cos(ang), jnp.sin(ang)
    return jnp.concatenate([cos, cos], -1), jnp.concatenate([-sin, sin], -1)


def _pad_lanes(v, width=LANES, offset=0):
    return jnp.zeros((1, width), F32).at[0, offset:offset + v.shape[0]].set(v.astype(F32))


def _peer_params(wq, sub_keys, u_tab, v_tab, te):
    half = sub_keys.shape[-1]
    keys = jnp.zeros((2 * NKEYS, LANES), F32)
    keys = keys.at[0:NKEYS, 0:half].set(sub_keys[0]).at[NKEYS:, half:2 * half].set(sub_keys[1])
    vt = v_tab.astype(BF16).reshape(NEXP // te, te, D).transpose(0, 2, 1)
    return wq.astype(BF16), keys.astype(BF16), u_tab.astype(BF16), vt


def _tile(t, pref):
    return pref if t % pref == 0 else LANES


def kernel(x_prompt, x_sample, state_ret, state_gdn, state_gdn_conv, cache_fox_k, cache_fox_v, cache_fox_logf,
           page_table, w_in_a, ret_norm_g, gdn_a_log, gdn_dt_bias, gdn_conv_w, gdn_norm_g, w_out_a, w_in_c,
           fox_b_f, w_out_c, peer_wq, peer_sub_keys, peer_u, peer_v, ln_g, ln_b):
    bp, lp, _ = x_prompt.shape
    bs = x_sample.shape[0]
    tp = bp * lp
    xp = x_prompt.reshape(tp, D)
    xs = x_sample.reshape(bs, D)
    tmp, tms = _tile(tp, 512), _tile(bs, 512)
    te = 1024
    ln = lambda layer, k: (ln_g[layer, k].reshape(1, D), ln_b[layer, k].reshape(1, D))

    w_in = jnp.pad(w_in_a[0], ((0, 0), (0, A_COLS_PAD - A_COLS))).astype(BF16)
    cos_p, sin_p = _rope_tables(jnp.arange(lp))
    cos_s, sin_s = _rope_tables(PAST + jnp.arange(1))
    alog, dtb = _pad_lanes(gdn_a_log[0]), _pad_lanes(gdn_dt_bias[0])
    rg, gg = ret_norm_g[0].reshape(1, RDV), gdn_norm_g[0].reshape(1, GDV)
    proj_p = _proj(xp, w_in, tmp).reshape(bp, lp, A_COLS_PAD)
    proj_s = _proj(xs, w_in, tms).reshape(bs, 1, A_COLS_PAD)
    mix_p, ret_p, gdn_p, conv_p = _mixer_a_prompt(proj_p, cos_p, sin_p, gdn_conv_w[0], alog, dtb, rg, gg)
    mix_s, ret_s, gdn_s, conv_s = _mixer_a_sample(proj_s, cos_s, sin_s, gdn_conv_w[0], alog, dtb, rg, gg,
                                                  state_ret[0], state_gdn[0], state_gdn_conv[0])
    w_out = w_out_a[0].astype(BF16)
    g0, b0 = ln(0, 0)
    xp = _outproj_ln(mix_p.reshape(tp, D), xp, w_out, g0, b0, tmp)
    xs = _outproj_ln(mix_s.reshape(bs, D), xs, w_out, g0, b0, tms)
    pw = _peer_params(peer_wq[0], peer_sub_keys[0], peer_u[0], peer_v[0], te)
    g1, b1 = ln(0, 1)
    xp = _peer(xp, *pw, g1, b1, tmp, te)
    xs = _peer(xs, *pw, g1, b1, tms, te)

    w_c = jnp.pad(w_in_c[0], ((0, 0), (0, C_COLS_PAD - w_in_c.shape[-1]))).astype(BF16)
    bf_row = _pad_lanes(fox_b_f[0])
    kp, vp, lfp, lftp, qtp, kbp, vtp = _fox_proj(xp, w_c, bf_row, tmp, True)
    ks, vs, lfs, _, qs = _fox_proj(xs, w_c, bf_row, tms, False)
    tk = tmp
    tq = 2 * tk
    cum = _fox_cumsum(lftp, bp, lp).reshape(bp * FH, lp // tk, tk)
    op = _fox_flash(qtp, kbp.reshape(bp, lp, D), vtp, cum, tq, tk)
    os_ = _fox_sample(page_table, qs.reshape(bs, FH, FHD), ks.reshape(bs, FH, FHD), vs.reshape(bs, FH, FHD),
                      lfs.reshape(bs, FH, 1), cache_fox_k[0], cache_fox_v[0], cache_fox_logf[0])
    w_oc = w_out_c[0].astype(BF16)
    g2, b2 = ln(1, 0)
    xp = _outproj_ln(op.reshape(tp, D), xp, w_oc, g2, b2, tmp)
    xs = _outproj_ln(os_.reshape(bs, D), xs, w_oc, g2, b2, tms)
    pw = _peer_params(peer_wq[1], peer_sub_keys[1], peer_u[1], peer_v[1], te)
    g3, b3 = ln(1, 1)
    xp = _peer(xp, *pw, g3, b3, tmp, te)
    xs = _peer(xs, *pw, g3, b3, tms, te)

    return (xp.reshape(bp, lp, D), xs.reshape(bs, 1, D),
            ret_p[None], ret_s[None], gdn_p[None], gdn_s[None], conv_p[None], conv_s[None],
            kp.reshape(1, bp, lp, FH, FHD), ks.reshape(1, bs, 1, FH, FHD),
            vp.reshape(1, bp, lp, FH, FHD), vs.reshape(1, bs, 1, FH, FHD),
            lfp.reshape(1, bp, lp, FH), lfs.reshape(1, bs, 1, FH))
```

```python
import functools
import math

import jax
import jax.numpy as jnp
from jax import lax
from jax.experimental import pallas as pl
from jax.experimental.pallas import tpu as pltpu

F32 = jnp.float32
BF16 = jnp.bfloat16

D = 1024
PAST = 2048
PAGE = 128
RH, RDK, RDV = 4, 128, 128
GH, GDK, GDV = 4, 128, 128
CONV_W = 4
CONV_CH = 2 * GH * GDK + GH * GDV
A_COLS = 4104
A_COLS_PAD = 4224
FH, FHD = 8, 128
C_COLS_PAD = 3200
PH, PTOPK, NKEYS = 8, 16, 128
NEXP = NKEYS * NKEYS
ALPHA = 4.0 ** 0.25
LN_EPS = 1e-5
NORM_EPS = 1e-6
ROPE_BASE = 10000.0
LOG2E = math.log2(math.e)
LANES = 128
CHUNK = 128
PAGES_PER_STEP = 8
GATE_DT = BF16
VMEM_LIMIT = 56 * 1024 * 1024

NT = (((1,), (1,)), ((), ()))
TN = (((0,), (0,)), ((), ()))


def _cp(sem):
    return pltpu.CompilerParams(dimension_semantics=sem, vmem_limit_bytes=VMEM_LIMIT)


def _mm(a, b):
    return jnp.dot(a.astype(BF16), b.astype(BF16), preferred_element_type=F32)


def _mm_nt(a, b):
    return lax.dot_general(a.astype(BF16), b.astype(BF16), NT, preferred_element_type=F32)


def _mm_tn(a, b):
    return lax.dot_general(a.astype(BF16), b.astype(BF16), TN, preferred_element_type=F32)


def _split3(x):
    x1 = x.astype(BF16)
    r1 = x - x1.astype(F32)
    x2 = r1.astype(BF16)
    return x1, x2, (r1 - x2.astype(F32)).astype(BF16)


def _mm_01_left(m01, x):
    return sum(jnp.dot(m01, t, preferred_element_type=F32) for t in _split3(x))


def _mm_01_right(x, m01):
    return sum(jnp.dot(t, m01, preferred_element_type=F32) for t in _split3(x))


def _split2(a):
    hi = a.astype(BF16)
    return hi, (a - hi.astype(F32)).astype(BF16)


def _mm_s(a, b):
    return (jnp.dot(a[0], b[0], preferred_element_type=F32) + jnp.dot(a[0], b[1], preferred_element_type=F32)
            + jnp.dot(a[1], b[0], preferred_element_type=F32))


def _sigmoid(x):
    return 1.0 / (1.0 + jnp.exp(-x))


def _silu(x):
    return x * _sigmoid(x)


def _softplus(x):
    return jnp.maximum(x, 0.0) + jnp.log1p(jnp.exp(-jnp.abs(x)))


def _log_sigmoid(x):
    return jnp.minimum(x, 0.0) - jnp.log1p(jnp.exp(-jnp.abs(x)))


def _layer_norm(z, g, b):
    mu = jnp.mean(z, axis=-1, keepdims=True)
    d = z - mu
    var = jnp.mean(d * d, axis=-1, keepdims=True)
    return d * lax.rsqrt(var + LN_EPS) * g + b


def _iota2(shape, axis):
    return lax.broadcasted_iota(jnp.int32, shape, axis)


def _proj_kernel(x_ref, w_ref, o_ref):
    o_ref[...] = jnp.dot(x_ref[...].astype(BF16), w_ref[...], preferred_element_type=F32)


def _proj(x, w, tm):
    t, k = x.shape
    n = w.shape[1]
    return pl.pallas_call(
        _proj_kernel,
        grid=(t // tm,),
        in_specs=[pl.BlockSpec((tm, k), lambda i: (i, 0)), pl.BlockSpec((k, n), lambda i: (0, 0))],
        out_specs=pl.BlockSpec((tm, n), lambda i: (i, 0)),
        out_shape=jax.ShapeDtypeStruct((t, n), F32),
        compiler_params=_cp(("parallel",)),
        name="proj_a",
    )(x, w)


def _fox_proj_kernel(x_ref, w_ref, bf_ref, k_ref, v_ref, lf_ref, lft_ref, *q_refs, attn_copies):
    p = jnp.dot(x_ref[...].astype(BF16), w_ref[...], preferred_element_type=F32)
    q = p[:, 0:D]
    k = p[:, D:2 * D]
    v = p[:, 2 * D:3 * D]
    k_ref[...] = k
    v_ref[...] = v
    lf = _log_sigmoid(p[:, 3 * D:3 * D + LANES] + bf_ref[...])
    lf_ref[...] = lf[:, 0:FH]
    lft_ref[...] = lf.T[0:FH, :]
    if attn_copies:
        qt_ref, kb_ref, vt_ref = q_refs
        kb_ref[...] = k.astype(BF16)
        qs = q * (FHD ** -0.5 * LOG2E)
        for h in range(FH):
            qt_ref[0, h] = qs[:, h * FHD:(h + 1) * FHD].T.astype(BF16)
            vt_ref[0, h] = v[:, h * FHD:(h + 1) * FHD].T.astype(BF16)
    else:
        q_refs[0][...] = q


def _fox_proj(x, w, bf_row, tm, attn_copies):
    t = x.shape[0]
    row = lambda i: (i, 0)
    out_specs = [pl.BlockSpec((tm, D), row), pl.BlockSpec((tm, D), row),
                 pl.BlockSpec((tm, FH), row), pl.BlockSpec((FH, tm), lambda i: (0, i))]
    out_shape = [jax.ShapeDtypeStruct((t, D), F32), jax.ShapeDtypeStruct((t, D), F32),
                 jax.ShapeDtypeStruct((t, FH), F32), jax.ShapeDtypeStruct((FH, t), F32)]
    if attn_copies:
        tile = pl.BlockSpec((1, FH, FHD, tm), lambda i: (i, 0, 0, 0))
        tiles = jax.ShapeDtypeStruct((t // tm, FH, FHD, tm), BF16)
        out_specs += [tile, pl.BlockSpec((tm, D), row), tile]
        out_shape += [tiles, jax.ShapeDtypeStruct((t, D), BF16), tiles]
    else:
        out_specs += [pl.BlockSpec((tm, D), row)]
        out_shape += [jax.ShapeDtypeStruct((t, D), F32)]
    return pl.pallas_call(
        functools.partial(_fox_proj_kernel, attn_copies=attn_copies),
        grid=(t // tm,),
        in_specs=[pl.BlockSpec((tm, D), row), pl.BlockSpec((D, C_COLS_PAD), lambda i: (0, 0)),
                  pl.BlockSpec((1, LANES), lambda i: (0, 0))],
        out_specs=out_specs,
        out_shape=out_shape,
        compiler_params=_cp(("parallel",)),
        name="proj_c",
    )(x, w, bf_row)


def _outproj_ln_kernel(a_ref, x_ref, w_ref, g_ref, b_ref, o_ref):
    h = jnp.dot(a_ref[...].astype(BF16), w_ref[...], preferred_element_type=F32)
    o_ref[...] = _layer_norm(ALPHA * x_ref[...] + h, g_ref[...], b_ref[...])


def _outproj_ln(a, x, w, g, b, tm):
    t = x.shape[0]
    row = lambda i: (i, 0)
    fixed = lambda i: (0, 0)
    return pl.pallas_call(
        _outproj_ln_kernel,
        grid=(t // tm,),
        in_specs=[pl.BlockSpec((tm, D), row), pl.BlockSpec((tm, D), row), pl.BlockSpec((D, D), fixed),
                  pl.BlockSpec((1, D), fixed), pl.BlockSpec((1, D), fixed)],
        out_specs=pl.BlockSpec((tm, D), row),
        out_shape=jax.ShapeDtypeStruct((t, D), F32),
        compiler_params=_cp(("parallel",)),
        name="outproj_ln",
    )(a, x, w, g, b)


def _unit_lower_inverses(a_list, ri, ci):
    eye = (ri == ci).astype(F32)
    blk = (ri >> 4) == (ci >> 4)
    d = [_split2(jnp.where(blk, a, 0.0)) for a in a_list]
    d2f = [_mm_s(x, x) for x in d]
    d2 = [_split2(x) for x in d2f]
    d4f = [_mm_s(x, x) for x in d2]
    d4 = [_split2(x) for x in d4f]
    d8 = [_split2(_mm_s(x, x)) for x in d4]
    xs = [eye - jnp.where(blk, a, 0.0) for a in a_list]
    for pw in (d2, d4, d8):
        xs = [x + _mm_s(_split2(x), p) for x, p in zip(xs, pw)]
    for s in (5, 6, 7):
        msk = ((ri >> s) == (ci >> s)) & ((ri >> (s - 1)) != (ci >> (s - 1)))
        es = [_split2(jnp.where(msk, a, 0.0)) for a in a_list]
        xsp = [_split2(x) for x in xs]
        xe = [_split2(_mm_s(x, e)) for x, e in zip(xsp, es)]
        xs = [x - _mm_s(y, xp) for x, y, xp in zip(xs, xe, xsp)]
    return xs


def _head_gate_norm(out, g_row, gate):
    return out * lax.rsqrt(jnp.mean(out * out, axis=-1, keepdims=True) + NORM_EPS) * g_row * _silu(gate)


def _mixer_a_prompt_kernel(p_ref, cos_ref, sin_ref, cw_ref, alog_ref, dt_ref, rg_ref, gg_ref,
                           mix_ref, ret_ref, gdn_ref, conv_ref, ext_ref, *, nb):
    n = pl.program_id(0)
    c = CHUNK

    @pl.when(n == 0)
    def _init():
        ret_ref[...] = jnp.zeros_like(ret_ref)
        gdn_ref[...] = jnp.zeros_like(gdn_ref)
        ext_ref[:, 0:8, :] = jnp.zeros((nb, 8, CONV_CH), F32)

    ri = _iota2((c, c), 0)
    ci = _iota2((c, c), 1)
    rf = ri.astype(F32)
    diff = rf - ci.astype(F32)
    cosv = cos_ref[...]
    sinv = sin_ref[...]

    lgs = [math.log1p(-(2.0 ** (-5.0 - h))) for h in range(RH)]
    decay = [jnp.where(diff >= 0, jnp.exp(jnp.maximum(diff, 0.0) * lg), 0.0) for lg in lgs]
    inner = [jnp.exp((rf + 1.0) * lg) for lg in lgs]
    kdec = [jnp.exp((c - 1.0 - rf) * lg) for lg in lgs]
    r_units = [(b, h) for b in range(nb) for h in range(RH)]
    r_q, r_k = [], []
    for b, h in r_units:
        rq = p_ref[b, :, h * RDK:(h + 1) * RDK]
        rk = p_ref[b, :, RH * RDK + h * RDK:RH * RDK + (h + 1) * RDK]
        r_q.append(rq * cosv + pltpu.roll(rq, RDK // 2, 1) * sinv)
        r_k.append((rk * cosv + pltpu.roll(rk, RDK // 2, 1) * sinv) * (RDK ** -0.5))
    r_v = [p_ref[b, :, 2 * RH * RDK + h * RDV:2 * RH * RDK + (h + 1) * RDV] for b, h in r_units]
    r_s = [ret_ref[b, h] for b, h in r_units]
    r_scores = [_mm_nt(q, k) * decay[h] for q, k, (_, h) in zip(r_q, r_k, r_units)]
    r_state_out = [_mm(q * inner[h], s) for q, s, (_, h) in zip(r_q, r_s, r_units)]
    r_new = [_mm_tn(k * kdec[h], v) for k, v, (_, h) in zip(r_k, r_v, r_units)]
    r_out = [_mm(sc, v) + so for sc, v, so in zip(r_scores, r_v, r_state_out)]
    for (b, h), s, new, out in zip(r_units, r_s, r_new, r_out):
        ret_ref[b, h] = math.exp(c * lgs[h]) * s + new
        rgate = p_ref[b, :, 2 * RH * RDK + RH * RDV + h * RDV:2 * RH * RDK + RH * RDV + (h + 1) * RDV]
        mix_ref[b, :, h * RDV:(h + 1) * RDV] = _head_gate_norm(out, rg_ref[...], rgate)

    g_off = 2 * RH * RDK + 2 * RH * RDV
    z_off = g_off + CONV_CH
    incl = ri >= ci
    strict = ri > ci
    tril = incl.astype(BF16)
    units = []
    for b in range(nb):
        ext_ref[b, 8:8 + c, :] = p_ref[b, :, g_off:g_off + CONV_CH]
        y = ext_ref[b, 5:5 + c, :] * cw_ref[0:1, :]
        for i in range(1, CONV_W):
            y = y + ext_ref[b, 5 + i:5 + i + c, :] * cw_ref[i:i + 1, :]
        conv_ref[b] = ext_ref[b, c + 5:c + 8, :]
        ext_ref[b, 0:8, :] = ext_ref[b, c:c + 8, :]
        y = _silu(y)
        tail = p_ref[b, :, z_off + GH * GDV:z_off + GH * GDV + LANES]
        g_all = -jnp.exp(alog_ref[...]) * _softplus(tail + dt_ref[...])
        beta_all = _sigmoid(tail)
        for h in range(GH):
            qh = y[:, h * GDK:(h + 1) * GDK]
            kh = y[:, GH * GDK + h * GDK:GH * GDK + (h + 1) * GDK]
            vh = y[:, 2 * GH * GDK + h * GDV:2 * GH * GDK + (h + 1) * GDV]
            qh = qh * lax.rsqrt(jnp.sum(qh * qh, axis=-1, keepdims=True) + NORM_EPS) * (GDK ** -0.5)
            kh = kh * lax.rsqrt(jnp.sum(kh * kh, axis=-1, keepdims=True) + NORM_EPS)
            gb = jnp.broadcast_to(g_all[:, h:h + 1], (c, c))
            bcol = beta_all[:, GH + h:GH + h + 1]
            units.append((b, h, qh, kh, vh, gb, bcol))

    gcs = [_mm_01_left(tril, u[5]) for u in units]
    decs = []
    for gc in gcs:
        dmat = gc - gc.T
        decs.append(jnp.where(incl, jnp.exp(jnp.where(incl, dmat, 0.0)), 0.0))
    kbs = [u[3] * u[6] for u in units]
    a_list = [jnp.where(strict, _mm_nt(kb, u[3]) * dec, 0.0) for kb, u, dec in zip(kbs, units, decs)]
    ts = [t.astype(BF16) for t in _unit_lower_inverses(a_list, ri, ci)]
    egcs = [jnp.exp(gc) for gc in gcs]
    us = [jnp.dot(t, (u[4] * u[6]).astype(BF16), preferred_element_type=F32) for t, u in zip(ts, units)]
    ws = [jnp.dot(t, (kb * egc).astype(BF16), preferred_element_type=F32) for t, kb, egc in zip(ts, kbs, egcs)]
    attns = [_mm_nt(u[2], u[3]) * dec for u, dec in zip(units, decs)]
    ss = [gdn_ref[u[0], u[1]] for u in units]
    v_news = [uu - _mm(w, s) for uu, w, s in zip(us, ws, ss)]
    outs = [_mm(u[2] * egc, s) + _mm(attn, vn) for u, egc, s, attn, vn in zip(units, egcs, ss, attns, v_news)]
    for u, gc, s, vn, out in zip(units, gcs, ss, v_news, outs):
        b, h = u[0], u[1]
        g_last = gc[c - 1:c, :]
        gdn_ref[b, h] = jnp.exp(g_last) * s + _mm_tn(u[3] * jnp.exp(g_last - gc), vn)
        gz = p_ref[b, :, z_off + h * GDV:z_off + (h + 1) * GDV]
        mix_ref[b, :, RH * RDV + h * GDV:RH * RDV + (h + 1) * GDV] = _head_gate_norm(out, gg_ref[...], gz)


def _mixer_a_prompt(proj, cos2, sin2, conv_w, alog_row, dt_row, rg_row, gg_row):
    b, l, _ = proj.shape
    fixed = lambda n: (0, 0)
    return pl.pallas_call(
        functools.partial(_mixer_a_prompt_kernel, nb=b),
        grid=(l // CHUNK,),
        in_specs=[pl.BlockSpec((b, CHUNK, A_COLS_PAD), lambda n: (0, n, 0)),
                  pl.BlockSpec((CHUNK, LANES), lambda n: (n, 0)),
                  pl.BlockSpec((CHUNK, LANES), lambda n: (n, 0)),
                  pl.BlockSpec((CONV_W, CONV_CH), fixed),
                  pl.BlockSpec((1, LANES), fixed), pl.BlockSpec((1, LANES), fixed),
                  pl.BlockSpec((1, LANES), fixed), pl.BlockSpec((1, LANES), fixed)],
        out_specs=[pl.BlockSpec((b, CHUNK, D), lambda n: (0, n, 0)),
                   pl.BlockSpec((b, RH, RDK, RDV), lambda n: (0, 0, 0, 0)),
                   pl.BlockSpec((b, GH, GDK, GDV), lambda n: (0, 0, 0, 0)),
                   pl.BlockSpec((b, CONV_W - 1, CONV_CH), lambda n: (0, 0, 0))],
        out_shape=[jax.ShapeDtypeStruct((b, l, D), F32),
                   jax.ShapeDtypeStruct((b, RH, RDK, RDV), F32),
                   jax.ShapeDtypeStruct((b, GH, GDK, GDV), F32),
                   jax.ShapeDtypeStruct((b, CONV_W - 1, CONV_CH), F32)],
        scratch_shapes=[pltpu.VMEM((b, CHUNK + 8, CONV_CH), F32)],
        compiler_params=_cp(("arbitrary",)),
        name="mixer_a_prompt",
    )(proj, cos2, sin2, conv_w, alog_row, dt_row, rg_row, gg_row)


def _mixer_a_sample_kernel(p_ref, cos_ref, sin_ref, cw_ref, alog_ref, dt_ref, rg_ref, gg_ref,
                           rs_ref, gs_ref, cs_ref, mix_ref, ret_ref, gdn_ref, conv_ref):
    eye = _iota2((LANES, LANES), 0) == _iota2((LANES, LANES), 1)

    def col(v):
        return jnp.sum(jnp.where(eye, jnp.broadcast_to(v, (LANES, LANES)), 0.0), axis=1, keepdims=True)

    def vec_mat(c, s):
        return jnp.sum(c * s, axis=0, keepdims=True)

    cosv = cos_ref[...]
    sinv = sin_ref[...]
    rq = [p_ref[0, :, h * RDK:(h + 1) * RDK] for h in range(RH)]
    rk = [p_ref[0, :, RH * RDK + h * RDK:RH * RDK + (h + 1) * RDK] for h in range(RH)]
    rv = [p_ref[0, :, 2 * RH * RDK + h * RDV:2 * RH * RDK + (h + 1) * RDV] for h in range(RH)]
    rq = [x * cosv + pltpu.roll(x, RDK // 2, 1) * sinv for x in rq]
    rk = [(x * cosv + pltpu.roll(x, RDK // 2, 1) * sinv) * (RDK ** -0.5) for x in rk]

    g_off = 2 * RH * RDK + 2 * RH * RDV
    gq_all = p_ref[0, :, g_off:g_off + CONV_CH]
    cb = cs_ref[0]
    y = cb[0:1, :] * cw_ref[0:1, :] + cb[1:2, :] * cw_ref[1:2, :] + cb[2:3, :] * cw_ref[2:3, :] \
        + gq_all * cw_ref[3:4, :]
    conv_ref[0, 0:2, :] = cb[1:3, :]
    conv_ref[0, 2:3, :] = gq_all
    y = _silu(y)
    z_off = g_off + CONV_CH
    tail = p_ref[0, :, z_off + GH * GDV:z_off + GH * GDV + LANES]
    g_all = -jnp.exp(alog_ref[...]) * _softplus(tail + dt_ref[...])
    beta_all = _sigmoid(tail)
    gq = [y[:, h * GDK:(h + 1) * GDK] for h in range(GH)]
    gk = [y[:, GH * GDK + h * GDK:GH * GDK + (h + 1) * GDK] for h in range(GH)]
    gv = [y[:, 2 * GH * GDK + h * GDV:2 * GH * GDK + (h + 1) * GDV] for h in range(GH)]
    gq = [x * lax.rsqrt(jnp.sum(x * x, axis=-1, keepdims=True) + NORM_EPS) * (GDK ** -0.5) for x in gq]
    gk = [x * lax.rsqrt(jnp.sum(x * x, axis=-1, keepdims=True) + NORM_EPS) for x in gk]
    eg = [jnp.exp(g_all[:, h:h + 1]) for h in range(GH)]
    beta = [beta_all[:, GH + h:GH + h + 1] for h in range(GH)]

    q_cols = [col(x) for x in rq + gq]
    k_cols = [col(x) for x in rk + gk]
    r_state = [rs_ref[0, h] for h in range(RH)]
    g_state = [gs_ref[0, h] for h in range(GH)]
    qs = [vec_mat(c, s) for c, s in zip(q_cols, r_state + g_state)]
    ws = [vec_mat(k_cols[RH + h] * (beta[h] * eg[h]), g_state[h]) for h in range(GH)]
    r_qk = [jnp.sum(a * b, axis=-1, keepdims=True) for a, b in zip(rq, rk)]
    g_qk = [jnp.sum(a * b, axis=-1, keepdims=True) for a, b in zip(gq, gk)]
    for h in range(RH):
        gamma = 1.0 - 2.0 ** (-5.0 - h)
        out = r_qk[h] * rv[h] + gamma * qs[h]
        ret_ref[0, h] = gamma * r_state[h] + k_cols[h] * rv[h]
        rgate = p_ref[0, :, 2 * RH * RDK + RH * RDV + h * RDV:2 * RH * RDK + RH * RDV + (h + 1) * RDV]
        mix_ref[0, :, h * RDV:(h + 1) * RDV] = _head_gate_norm(out, rg_ref[...], rgate)
    for h in range(GH):
        v_new = gv[h] * beta[h] - ws[h]
        out = eg[h] * qs[RH + h] + g_qk[h] * v_new
        gdn_ref[0, h] = eg[h] * g_state[h] + k_cols[RH + h] * v_new
        gz = p_ref[0, :, z_off + h * GDV:z_off + (h + 1) * GDV]
        mix_ref[0, :, RH * RDV + h * GDV:RH * RDV + (h + 1) * GDV] = _head_gate_norm(out, gg_ref[...], gz)


def _mixer_a_sample(proj, cos2, sin2, conv_w, alog_row, dt_row, rg_row, gg_row, ret_s, gdn_s, conv_s):
    b = proj.shape[0]
    fixed = lambda i: (0, 0)
    st = lambda i: (i, 0, 0, 0)
    return pl.pallas_call(
        _mixer_a_sample_kernel,
        grid=(b,),
        in_specs=[pl.BlockSpec((1, 1, A_COLS_PAD), lambda i: (i, 0, 0)),
                  pl.BlockSpec((1, LANES), fixed), pl.BlockSpec((1, LANES), fixed),
                  pl.BlockSpec((CONV_W, CONV_CH), fixed),
                  pl.BlockSpec((1, LANES), fixed), pl.BlockSpec((1, LANES), fixed),
                  pl.BlockSpec((1, LANES), fixed), pl.BlockSpec((1, LANES), fixed),
                  pl.BlockSpec((1, RH, RDK, RDV), st), pl.BlockSpec((1, GH, GDK, GDV), st),
                  pl.BlockSpec((1, CONV_W - 1, CONV_CH), lambda i: (i, 0, 0))],
        out_specs=[pl.BlockSpec((1, 1, D), lambda i: (i, 0, 0)),
                   pl.BlockSpec((1, RH, RDK, RDV), st), pl.BlockSpec((1, GH, GDK, GDV), st),
                   pl.BlockSpec((1, CONV_W - 1, CONV_CH), lambda i: (i, 0, 0))],
        out_shape=[jax.ShapeDtypeStruct((b, 1, D), F32),
                   jax.ShapeDtypeStruct((b, RH, RDK, RDV), F32),
                   jax.ShapeDtypeStruct((b, GH, GDK, GDV), F32),
                   jax.ShapeDtypeStruct((b, CONV_W - 1, CONV_CH), F32)],
        compiler_params=_cp(("parallel",)),
        name="mixer_a_sample",
    )(proj, cos2, sin2, conv_w, alog_row, dt_row, rg_row, gg_row, ret_s, gdn_s, conv_s)


def _fox_cumsum_kernel(x_ref, o_ref):
    n = x_ref.shape[1] // LANES
    upper = (_iota2((LANES, LANES), 0) <= _iota2((LANES, LANES), 1)).astype(BF16)
    carry = jnp.zeros((FH, 1), F32)
    for c in range(n):
        p = _mm_01_right(x_ref[:, c * LANES:(c + 1) * LANES], upper) + carry
        o_ref[0, :, c * LANES:(c + 1) * LANES] = p
        carry = p[:, LANES - 1:LANES]


def _fox_cumsum(lft, b, l):
    return pl.pallas_call(
        _fox_cumsum_kernel,
        grid=(b,),
        in_specs=[pl.BlockSpec((FH, l), lambda i: (0, i))],
        out_specs=pl.BlockSpec((1, FH, l), lambda i: (i, 0, 0)),
        out_shape=jax.ShapeDtypeStruct((b, FH, l), F32),
        compiler_params=_cp(("parallel",)),
        name="fox_cumsum",
    )(lft)


def _fox_flash_kernel(q_ref, k_ref, v_ref, ck_ref, o_ref, ka_ref, sa_ref, sb_ref, *, tq, tk):
    assert tq == 2 * tk and tk % LANES == 0
    qi = pl.program_id(2)
    nk = k_ref.shape[1] // tk

    @pl.when(qi == 0)
    def _augment_keys():
        eye = _iota2((LANES, LANES), 0) == _iota2((LANES, LANES), 1)
        lane = _iota2((LANES, LANES), 1)
        for j in range(nk):
            ka_ref[j * tk:(j + 1) * tk, 0:FHD] = k_ref[0, j * tk:(j + 1) * tk, :]
            for c in range(tk // LANES):
                row = jnp.broadcast_to(ck_ref[0, j:j + 1, c * LANES:(c + 1) * LANES], (LANES, LANES))
                col = jnp.sum(jnp.where(eye, row, 0.0), axis=1, keepdims=True) * LOG2E
                c1, c2, c3 = (t.astype(F32) for t in _split3(col))
                blk = jnp.where(lane == 0, c1, jnp.where(lane == 1, c2, jnp.where(lane == 2, c3, 0.0)))
                r0 = j * tk + c * LANES
                ka_ref[r0:r0 + LANES, FHD:2 * FHD] = blk.astype(BF16)

    sub = _iota2((FHD, tq), 0)
    minus_ones = jnp.where(sub < 3, -1.0, 0.0).astype(BF16)
    qa = jnp.concatenate([q_ref[0, 0], q_ref[1, 0]], axis=1)
    qa = jnp.concatenate([qa, minus_ones], axis=0)

    def scores(j, s_ref):
        ka = ka_ref[pl.ds(pl.multiple_of(j * tk, tk), tk), :]
        s_ref[...] = jnp.dot(ka, qa, preferred_element_type=F32)

    ones_rows = jnp.ones((16, tk), BF16)

    def update(j, s_ref, carry, masked):
        m_old, acc = carry
        s = s_ref[...]
        if masked:
            kpos = j * tk + _iota2((tk, tq), 0)
            qpos = qi * tq + _iota2((tk, tq), 1)
            s = jnp.where(kpos <= qpos, s, -jnp.inf)
        m_new = jnp.maximum(m_old, jnp.max(s, axis=0, keepdims=True))
        alpha = jnp.exp2(m_old - m_new)
        p = jnp.exp2(s - m_new).astype(BF16)
        v_aug = jnp.concatenate([v_ref[j, 0], ones_rows], axis=0)
        return m_new, alpha * acc + jnp.dot(v_aug, p, preferred_element_type=F32)

    def two_blocks(i, carry):
        scores(2 * i + 1, sb_ref)
        carry = update(2 * i, sa_ref, carry, False)
        scores(2 * i + 2, sa_ref)
        return update(2 * i + 1, sb_ref, carry, False)

    scores(0, sa_ref)
    init = (jnp.full((1, tq), -jnp.inf, F32), jnp.zeros((FHD + 16, tq), F32))
    carry = lax.fori_loop(0, qi, two_blocks, init)
    scores(2 * qi + 1, sb_ref)
    carry = update(2 * qi, sa_ref, carry, True)
    carry = update(2 * qi + 1, sb_ref, carry, True)
    acc = carry[1]
    o_ref[0] = (acc[0:FHD] / acc[FHD:FHD + 1]).T


def _fox_flash(qt, kb, vt, cum, tq, tk):
    b, l, _ = kb.shape
    nk = l // tk
    return pl.pallas_call(
        functools.partial(_fox_flash_kernel, tq=tq, tk=tk),
        grid=(b, FH, l // tq),
        in_specs=[pl.BlockSpec((tq // tk, 1, FHD, tk), lambda i, h, qi: (i * (l // tq) + qi, h, 0, 0)),
                  pl.BlockSpec((1, l, FHD), lambda i, h, qi: (i, 0, h)),
                  pl.BlockSpec((nk, 1, FHD, tk), lambda i, h, qi: (i, h, 0, 0)),
                  pl.BlockSpec((1, nk, tk), lambda i, h, qi: (i * FH + h, 0, 0))],
        out_specs=pl.BlockSpec((1, tq, FHD), lambda i, h, qi: (i, qi, h)),
        out_shape=jax.ShapeDtypeStruct((b, l, D), F32),
        scratch_shapes=[pltpu.VMEM((l, 2 * FHD), BF16), pltpu.VMEM((tk, tq), F32), pltpu.VMEM((tk, tq), F32)],
        compiler_params=_cp(("arbitrary", "arbitrary", "arbitrary")),
        name="fox_flash",
    )(qt, kb, vt, cum)


def _fox_sample_kernel(pt_ref, q_ref, kn_ref, vn_ref, lfn_ref, *rest):
    npp = PAGES_PER_STEP
    k_refs, v_refs, lf_refs = rest[0:npp], rest[npp:2 * npp], rest[2 * npp:3 * npp]
    o_ref, m_ref, l_ref, acc_ref, carry_ref = rest[3 * npp:]
    j = pl.program_id(1)
    scale = FHD ** -0.5
    rows = PAGE * FH

    @pl.when(j == 0)
    def _init():
        m_ref[...] = jnp.full(m_ref.shape, -jnp.inf, F32)
        l_ref[...] = jnp.zeros(l_ref.shape, F32)
        acc_ref[...] = jnp.zeros(acc_ref.shape, F32)
        carry_ref[...] = jnp.zeros(carry_ref.shape, F32)

    q = q_ref[0]
    qb = (q * scale).astype(BF16)
    own = (_iota2((FH, rows), 1) & (FH - 1)) == _iota2((FH, rows), 0)
    nch = rows // LANES
    ci = _iota2((LANES, 2 * LANES), 1)
    pre_tot = ((_iota2((LANES, 2 * LANES), 0) <= ci) | (ci >= LANES)).astype(BF16)
    stacked, s_pages = [], []
    for pg in range(npp):
        kp = k_refs[pg][0].reshape(rows, FHD).astype(BF16)
        s_pages.append(lax.dot_general(qb, kp, NT, preferred_element_type=F32))
        lf = jnp.where(own, jnp.broadcast_to(lf_refs[pg][0], (FH, rows)), 0.0)
        stacked += [lf[:, c * LANES:(c + 1) * LANES] for c in range(nch)]
    pt = _mm_01_right(jnp.concatenate(stacked, axis=0), pre_tot)
    carry = carry_ref[...]
    logits = []
    for pg in range(npp):
        cums = []
        for c in range(nch):
            r0 = (pg * nch + c) * FH
            cums.append(pt[r0:r0 + FH, 0:LANES] + carry)
            carry = carry + pt[r0:r0 + FH, LANES:2 * LANES]
        logits.append(jnp.where(own, s_pages[pg] - jnp.concatenate(cums, axis=1), -jnp.inf))
    carry_ref[...] = carry
    m_old = m_ref[...]
    m_new = m_old
    for lg in logits:
        m_new = jnp.maximum(m_new, jnp.max(lg, axis=1, keepdims=True))
    alpha = jnp.exp(m_old - m_new)
    l_new = alpha * l_ref[...]
    acc = alpha * acc_ref[...]
    for pg in range(npp):
        p = jnp.exp(logits[pg] - m_new)
        l_new = l_new + jnp.sum(p, axis=1, keepdims=True)
        vp = v_refs[pg][0].reshape(rows, FHD).astype(BF16)
        acc = acc + jnp.dot(p.astype(BF16), vp, preferred_element_type=F32)
    m_ref[...] = m_new
    l_ref[...] = l_new
    acc_ref[...] = acc

    @pl.when(j == pl.num_programs(1) - 1)
    def _fin():
        s_new = jnp.sum(q * kn_ref[0], axis=1, keepdims=True) * scale - (carry[:, 0:1] + lfn_ref[0])
        m2 = jnp.maximum(m_new, s_new)
        a2 = jnp.exp(m_new - m2)
        p2 = jnp.exp(s_new - m2)
        o_ref[0] = (a2 * acc + p2 * vn_ref[0]) / (a2 * l_new + p2)


def _fox_sample(page_table, q3, k3, v3, lf3, k_pool, v_pool, lf_pool):
    b = q3.shape[0]
    npg = page_table.shape[1]
    npp = PAGES_PER_STEP
    assert npg % npp == 0
    pt = page_table.reshape(-1)
    tok = lambda i, j, pt_ref: (i, 0, 0)

    def page(r, nd):
        return lambda i, j, pt_ref: (pt_ref[i * npg + j * npp + r],) + (0,) * nd

    grid_spec = pltpu.PrefetchScalarGridSpec(
        num_scalar_prefetch=1,
        grid=(b, npg // npp),
        in_specs=[pl.BlockSpec((1, FH, FHD), tok), pl.BlockSpec((1, FH, FHD), tok), pl.BlockSpec((1, FH, FHD), tok),
                  pl.BlockSpec((1, FH, 1), tok)]
        + [pl.BlockSpec((1, PAGE, FH, FHD), page(r, 3)) for r in range(npp)]
        + [pl.BlockSpec((1, PAGE, FH, FHD), page(r, 3)) for r in range(npp)]
        + [pl.BlockSpec((1, 1, PAGE * FH), page(r, 2)) for r in range(npp)],
        out_specs=pl.BlockSpec((1, FH, FHD), tok),
        scratch_shapes=[pltpu.VMEM((FH, 1), F32), pltpu.VMEM((FH, 1), F32), pltpu.VMEM((FH, FHD), F32),
                        pltpu.VMEM((FH, LANES), F32)],
    )
    lf_flat =lf_pool.reshape(lf_pool.shape[0], 1, PAGE * FH)
    return pl.pallas_call(
        _fox_sample_kernel,
        grid_spec=grid_spec,
        out_shape=jax.ShapeDtypeStruct((b, FH, FHD), F32),
        compiler_params=_cp(("parallel", "arbitrary")),
        name="fox_sample",
    )(pt, q3, k3, v3, lf3, *([k_pool] * npp), *([v_pool] * npp), *([lf_flat] * npp))


_CAND_BLOCKS = [(a, 16 if a == 0 else 8, 16 // (a + 1)) for a in range(8)]


_MARK0 = -(2.0 ** 127)
_MARK_STEP = 2.0 ** 120
_MARK_LIMIT = -(2.0 ** 126)


def _top16(s, iota):
    rank = jnp.full(s.shape, float(PTOPK), F32)
    vals = jnp.zeros((PTOPK, s.shape[1]), F32)
    i16 = _iota2((PTOPK, s.shape[1]), 0)
    for r in range(PTOPK):
        m = jnp.max(s, axis=0, keepdims=True)
        sel = iota == jnp.min(jnp.where(s == m, iota, 1e9), axis=0, keepdims=True)
        rank = jnp.where(sel, float(r), rank)
        s = jnp.where(sel, -jnp.inf, s)
        vals = jnp.where(i16 == r, m, vals)
    return rank, vals


def _top16_fast(s):
    lowest = jnp.min(s)
    vals = jnp.zeros((PTOPK, s.shape[1]), F32)
    i16 = _iota2((PTOPK, s.shape[1]), 0)
    for r in range(PTOPK):
        m = jnp.max(s, axis=0, keepdims=True)
        s = jnp.where(s == m, _MARK0 + r * _MARK_STEP, s)
        vals = jnp.where(i16 == r, m, vals)
    marked = s < _MARK_LIMIT
    rank = jnp.where(marked, (s - _MARK0) * (1.0 / _MARK_STEP), float(PTOPK))
    taken = jnp.sum(jnp.where(marked, 1.0, 0.0), axis=0, keepdims=True)
    return rank, vals, _not_16(taken) | (lowest <= _MARK_LIMIT)


def _pick16(cand, cidx, exact):
    picked = jnp.zeros(cand.shape, F32)
    for _ in range(PTOPK):
        m = jnp.max(cand, axis=0, keepdims=True)
        sel = cand == m
        if exact:
            sel = cidx == jnp.min(jnp.where(sel, cidx, 1e9), axis=0, keepdims=True)
        picked = jnp.where(sel, 1.0, picked)
        cand = jnp.where(sel, -jnp.inf, cand)
    return picked


def _not_16(count):
    return jnp.max(jnp.abs(count - float(PTOPK))) > 0.5


def _route_head(s_t):
    s1 = s_t[0:NKEYS]
    s2 = s_t[NKEYS:2 * NKEYS]
    w = s_t.shape[1]
    iota = _iota2((NKEYS, w), 0).astype(F32)
    rank1, v1, redo1 = _top16_fast(s1)
    rank2, v2, redo2 = _top16_fast(s2)
    rank1, v1, rank2, v2 = lax.cond(redo1 | redo2, lambda: _top16(s1, iota) + _top16(s2, iota),
                                    lambda: (rank1, v1, rank2, v2))
    e1 = jnp.exp(v1 - v1[0:1])
    e2 = jnp.exp(v2 - v2[0:1])
    i8 = _iota2((8, w), 0)
    i8f = i8.astype(F32)
    i16f = _iota2((16, w), 0).astype(F32)
    cand, cidx, cprob = [], [], []
    for a, rows, valid in _CAND_BLOCKS:
        c = v1[a:a + 1] + v2[0:rows]
        pr = e1[a:a + 1] * e2[0:rows]
        if rows == 16:
            ix = i16f
        else:
            ix = i8f + float(a * PTOPK)
            c = jnp.where(i8 < valid, c, -jnp.inf)
        cand.append(c)
        cidx.append(ix)
        cprob.append(pr)
    cand.append(v1[8:16] + v2[0:1])
    cidx.append((i8f + 8.0) * float(PTOPK))
    cprob.append(e1[8:16] * e2[0:1])
    cand = jnp.concatenate(cand, axis=0)
    cidx = jnp.concatenate(cidx, axis=0)
    cprob = jnp.concatenate(cprob, axis=0)
    picked = _pick16(cand, cidx, False)
    picked = lax.cond(_not_16(jnp.sum(picked, axis=0, keepdims=True)),
                      lambda: _pick16(cand, cidx, True), lambda: picked)
    z = jnp.sum(picked * cprob, axis=0, keepdims=True)
    n_low = jnp.zeros((8, w), F32)
    off = 0
    for a, rows, _ in _CAND_BLOCKS:
        cnt = jnp.sum(picked[off:off + rows], axis=0, keepdims=True)
        n_low = jnp.where(i8 == a, cnt, n_low)
        off += rows
    n16 = jnp.concatenate([n_low, picked[off:off + 8]], axis=0)
    n1d = jnp.zeros((NKEYS, w), F32)
    for a in range(PTOPK):
        n1d = jnp.where(rank1 == float(a), n16[a:a + 1], n1d)
    e1d = jnp.exp(s1 - v1[0:1]) * (0.5 / z)
    e2d = jnp.exp(s2 - v2[0:1])
    return rank2, e2d, n1d, e1d


def _peer_kernel(x_ref, wq_ref, keys_ref, u0_ref, un_ref, vt_ref, g_ref, b_ref, o_ref,
                 xb_ref, q_ref, r2_ref, e2_ref, n1_ref, e1_ref, ha_ref, hb_ref, at_ref, acc_ref, *, tm, te, rw):
    j = pl.program_id(1)
    nlg = tm // LANES
    nsub = te // NKEYS

    @pl.when(j == 0)
    def _route():
        xb = x_ref[...].astype(BF16)
        xb_ref[...] = xb
        q = jnp.dot(xb, wq_ref[...], preferred_element_type=F32)
        for h in range(PH):
            q_ref[h] = q[:, h * LANES:(h + 1) * LANES].astype(BF16)
        acc_ref[...] = jnp.zeros(acc_ref.shape, F32)
        ha_ref[:, 0:tm] = lax.dot_general(u0_ref[...], xb, NT, preferred_element_type=F32)

        def body(h, carry):
            for w0 in range(0, tm, rw):
                s_t = lax.dot_general(keys_ref[...], q_ref[h, w0:w0 + rw, :], NT,
                                      preferred_element_type=F32)
                r2, e2, n1, e1 = _route_head(s_t)
                for g in range(rw // LANES):
                    lg = slice(g * LANES, (g + 1) * LANES)
                    r2_ref[h, w0 // LANES + g] = r2[:, lg].astype(GATE_DT)
                    e2_ref[h, w0 // LANES + g] = e2[:, lg].astype(GATE_DT)
                    n1_ref[h, w0 // LANES + g] = n1[:, lg]
                    e1_ref[h, w0 // LANES + g] = e1[:, lg]
            return carry

        lax.fori_loop(0, PH, body, 0)

    def step(h_cur_ref, h_next_ref):
        h_next_ref[:, 0:tm] = lax.dot_general(un_ref[...], xb_ref[...], NT, preferred_element_type=F32)
        zero = jnp.zeros((), GATE_DT)
        pair = 2
        for pp in range(nsub // pair):
            for g in range(nlg):
                lanes = slice(g * LANES, (g + 1) * LANES)
                gates = [jnp.zeros((NKEYS, LANES), GATE_DT) for _ in range(pair)]
                for h in range(PH):
                    r2 = r2_ref[h, g]
                    e2 = e2_ref[h, g]
                    for ii in range(pair):
                        i1 = j * nsub + pp * pair + ii
                        n_row = n1_ref[h, g, pl.ds(i1, 1), :].astype(GATE_DT)
                        e_row = e1_ref[h, g, pl.ds(i1, 1), :].astype(GATE_DT)
                        gates[ii] = gates[ii] + jnp.where(r2 < n_row, e2, zero) * e_row
                for ii in range(pair):
                    r0 = (pp * pair + ii) * NKEYS
                    hb = h_cur_ref[r0:r0 + NKEYS, lanes]
                    act = hb * (1.0 + lax.erf(hb * (2.0 ** -0.5)))
                    at_ref[r0:r0 + NKEYS, lanes] = (act.astype(GATE_DT) * gates[ii]).astype(BF16)
        acc_ref[...] += jnp.dot(vt_ref[0], at_ref[:, 0:tm], preferred_element_type=F32)

    @pl.when(j % 2 == 0)
    def _even():
        step(ha_ref, hb_ref)

    @pl.when(j % 2 == 1)
    def _odd():
        step(hb_ref, ha_ref)

    @pl.when(j == pl.num_programs(1) - 1)
    def _fin():
        o_ref[...] = _layer_norm(ALPHA * x_ref[...] + acc_ref[...].T, g_ref[...], b_ref[...])


def _peer(x, wq, keys, u, vt, g, b, tm, te):
    t = x.shape[0]
    nj = NEXP // te
    fixed = lambda i, j: (0, 0)
    row_pad = LANES if (tm // LANES) % 4 == 0 else 0
    return pl.pallas_call(
        functools.partial(_peer_kernel, tm=tm, te=te, rw=tm),
        grid=(t // tm, nj),
        in_specs=[pl.BlockSpec((tm, D), lambda i, j: (i, 0)),
                  pl.BlockSpec((D, PH * LANES), fixed),
                  pl.BlockSpec((2 * NKEYS, LANES), fixed),
                  pl.BlockSpec((te, D), fixed),
                  pl.BlockSpec((te, D), lambda i, j: (jnp.minimum(j + 1, nj - 1), 0)),
                  pl.BlockSpec((1, D, te), lambda i, j: (j, 0, 0)),
                  pl.BlockSpec((1, D), fixed), pl.BlockSpec((1, D), fixed)],
        out_specs=pl.BlockSpec((tm, D), lambda i, j: (i, 0)),
        out_shape=jax.ShapeDtypeStruct((t, D), F32),
        scratch_shapes=[pltpu.VMEM((tm, D), BF16),
                        pltpu.VMEM((PH, tm, LANES), BF16),
                        pltpu.VMEM((PH, tm // LANES, NKEYS, LANES), GATE_DT),
                        pltpu.VMEM((PH, tm // LANES, NKEYS, LANES), GATE_DT),
                        pltpu.VMEM((PH, tm // LANES, NKEYS, LANES), F32),
                        pltpu.VMEM((PH, tm // LANES, NKEYS, LANES), F32),
                        pltpu.VMEM((te, tm + row_pad), F32),
                        pltpu.VMEM((te, tm + row_pad), F32),
                        pltpu.VMEM((te, tm + row_pad), BF16),
                        pltpu.VMEM((D, tm), F32)],
        compiler_params=_cp(("parallel", "arbitrary")),
        name="peer",
    )(x, wq, keys, u, u, vt, g, b)


def _rope_tables(pos):
    half = RDK // 2
    inv = ROPE_BASE ** (-jnp.arange(half, dtype=F32) / half)
    ang = pos.astype(F32)[:, None] * inv[None, :]
    cos, sin = jnp.cos(ang), jnp.sin(ang)
    return jnp.concatenate([cos, cos], -1), jnp.concatenate([-sin, sin], -1)


def _pad_lanes(v, width=LANES, offset=0):
    return jnp.zeros((1, width), F32).at[0, offset:offset + v.shape[0]].set(v.astype(F32))


def _peer_params(wq, sub_keys, u_tab, v_tab, te):
    half = sub_keys.shape[-1]
    keys = jnp.zeros((2 * NKEYS, LANES), F32)
    keys = keys.at[0:NKEYS, 0:half].set(sub_keys[0]).at[NKEYS:, half:2 * half].set(sub_keys[1])
    vt = v_tab.astype(BF16).reshape(NEXP // te, te, D).transpose(0, 2, 1)
    return wq.astype(BF16), keys.astype(BF16), u_tab.astype(BF16), vt


def _tile(t, pref):
    return pref if t % pref == 0 else LANES


def kernel(x_prompt, x_sample, state_ret, state_gdn, state_gdn_conv, cache_fox_k, cache_fox_v, cache_fox_logf,
           page_table, w_in_a, ret_norm_g, gdn_a_log, gdn_dt_bias, gdn_conv_w, gdn_norm_g, w_out_a, w_in_c,
           fox_b_f, w_out_c, peer_wq, peer_sub_keys, peer_u, peer_v, ln_g, ln_b):
    bp, lp, _ = x_prompt.shape
    bs = x_sample.shape[0]
    tp = bp * lp
    xp = x_prompt.reshape(tp, D)
    xs = x_sample.reshape(bs, D)
    tmp, tms = _tile(tp, 512), _tile(bs, 512)
    te = 1024
    ln = lambda layer, k: (ln_g[layer, k].reshape(1, D), ln_b[layer, k].reshape(1, D))

    w_in = jnp.pad(w_in_a[0], ((0, 0), (0, A_COLS_PAD - A_COLS))).astype(BF16)
    cos_p, sin_p = _rope_tables(jnp.arange(lp))
    cos_s, sin_s = _rope_tables(PAST + jnp.arange(1))
    alog, dtb = _pad_lanes(gdn_a_log[0]), _pad_lanes(gdn_dt_bias[0])
    rg, gg = ret_norm_g[0].reshape(1, RDV), gdn_norm_g[0].reshape(1, GDV)
    proj_p = _proj(xp, w_in, tmp).reshape(bp, lp, A_COLS_PAD)
    proj_s = _proj(xs, w_in, tms).reshape(bs, 1, A_COLS_PAD)
    mix_p, ret_p, gdn_p, conv_p = _mixer_a_prompt(proj_p, cos_p, sin_p, gdn_conv_w[0], alog, dtb, rg, gg)
    mix_s, ret_s, gdn_s, conv_s = _mixer_a_sample(proj_s, cos_s, sin_s, gdn_conv_w[0], alog, dtb, rg, gg,
                                                  state_ret[0], state_gdn[0], state_gdn_conv[0])
    w_out = w_out_a[0].astype(BF16)
    g0, b0 = ln(0, 0)
    xp = _outproj_ln(mix_p.reshape(tp, D), xp, w_out, g0, b0, tmp)
    xs = _outproj_ln(mix_s.reshape(bs, D), xs, w_out, g0, b0, tms)
    pw = _peer_params(peer_wq[0], peer_sub_keys[0], peer_u[0], peer_v[0], te)
    g1, b1 = ln(0, 1)
    xp = _peer(xp, *pw, g1, b1, tmp, te)
    xs = _peer(xs, *pw, g1, b1, tms, te)

    w_c = jnp.pad(w_in_c[0], ((0, 0), (0, C_COLS_PAD - w_in_c.shape[-1]))).astype(BF16)
    bf_row = _pad_lanes(fox_b_f[0])
    kp, vp, lfp, lftp, qtp, kbp, vtp = _fox_proj(xp, w_c, bf_row, tmp, True)
    ks, vs, lfs, _, qs = _fox_proj(xs, w_c, bf_row, tms, False)
    tk = tmp
    tq = 2 * tk
    cum = _fox_cumsum(lftp, bp, lp).reshape(bp * FH, lp // tk, tk)
    op = _fox_flash(qtp, kbp.reshape(bp, lp, D), vtp, cum, tq, tk)
    os_ = _fox_sample(page_table, qs.reshape(bs, FH, FHD), ks.reshape(bs, FH, FHD), vs.reshape(bs, FH, FHD),
                      lfs.reshape(bs, FH, 1), cache_fox_k[0], cache_fox_v[0], cache_fox_logf[0])
    w_oc = w_out_c[0].astype(BF16)
    g2, b2 = ln(1, 0)
    xp = _outproj_ln(op.reshape(tp, D), xp, w_oc, g2, b2, tmp)
    xs = _outproj_ln(os_.reshape(bs, D), xs, w_oc, g2, b2, tms)
    pw = _peer_params(peer_wq[1], peer_sub_keys[1], peer_u[1], peer_v[1], te)
    g3, b3 = ln(1, 1)
    xp = _peer(xp, *pw, g3, b3, tmp, te)
    xs = _peer(xs, *pw, g3, b3, tms, te)

    return (xp.reshape(bp, lp, D), xs.reshape(bs, 1, D),
            ret_p[None], ret_s[None], gdn_p[None], gdn_s[None], conv_p[None], conv_s[None],
            kp.reshape(1, bp, lp, FH, FHD), ks.reshape(1, bs, 1, FH, FHD),
            vp.reshape(1, bp, lp, FH, FHD), vs.reshape(1, bs, 1, FH, FHD),
            lfp.reshape(1, bp, lp, FH), lfs.reshape(1, bs, 1, FH))
```

```python
import functools
import math

import jax
import jax.numpy as jnp
from jax import lax
from jax.experimental import pallas as pl
from jax.experimental.pallas import tpu as pltpu

F32 = jnp.float32
BF16 = jnp.bfloat16

D = 1024
PAST = 2048
PAGE = 128
RH, RDK, RDV = 4, 128, 128
GH, GDK, GDV = 4, 128, 128
CONV_W = 4
CONV_CH = 2 * GH * GDK + GH * GDV
A_COLS = 4104
A_COLS_PAD = 4224
FH, FHD = 8, 128
C_COLS_PAD = 3200
PH, PTOPK, NKEYS = 8, 16, 128
NEXP = NKEYS * NKEYS
ALPHA = 4.0 ** 0.25
LN_EPS = 1e-5
NORM_EPS = 1e-6
ROPE_BASE = 10000.0
LOG2E = math.log2(math.e)
GELU_GATE_SCALE = 2.0 ** -0.5
LANES = 128
CHUNK = 128
PAGES_PER_STEP = 8
GATE_DT = BF16
VMEM_LIMIT = 56 * 1024 * 1024

NT = (((1,), (1,)), ((), ()))
TN = (((0,), (0,)), ((), ()))


def _cp(sem):
    return pltpu.CompilerParams(dimension_semantics=sem, vmem_limit_bytes=VMEM_LIMIT)


def _mm(a, b):
    return jnp.dot(a.astype(BF16), b.astype(BF16), preferred_element_type=F32)


def _mm_nt(a, b):
    return lax.dot_general(a.astype(BF16), b.astype(BF16), NT, preferred_element_type=F32)


def _mm_tn(a, b):
    return lax.dot_general(a.astype(BF16), b.astype(BF16), TN, preferred_element_type=F32)


def _split3(x):
    x1 = x.astype(BF16)
    r1 = x - x1.astype(F32)
    x2 = r1.astype(BF16)
    return x1, x2, (r1 - x2.astype(F32)).astype(BF16)


def _mm_01_left(m01, x):
    return sum(jnp.dot(m01, t, preferred_element_type=F32) for t in _split3(x))


def _mm_01_right(x, m01):
    return sum(jnp.dot(t, m01, preferred_element_type=F32) for t in _split3(x))


def _split2(a):
    hi = a.astype(BF16)
    return hi, (a - hi.astype(F32)).astype(BF16)


def _mm_s(a, b):
    return (jnp.dot(a[0], b[0], preferred_element_type=F32) + jnp.dot(a[0], b[1], preferred_element_type=F32)
            + jnp.dot(a[1], b[0], preferred_element_type=F32))


def _sigmoid(x):
    return 1.0 / (1.0 + jnp.exp(-x))


def _silu(x):
    return x * _sigmoid(x)


def _softplus(x):
    return jnp.maximum(x, 0.0) + jnp.log1p(jnp.exp(-jnp.abs(x)))


def _log_sigmoid(x):
    return jnp.minimum(x, 0.0) - jnp.log1p(jnp.exp(-jnp.abs(x)))


def _layer_norm(z, g, b):
    mu = jnp.mean(z, axis=-1, keepdims=True)
    d = z - mu
    var = jnp.mean(d * d, axis=-1, keepdims=True)
    return d * lax.rsqrt(var + LN_EPS) * g + b


def _iota2(shape, axis):
    return lax.broadcasted_iota(jnp.int32, shape, axis)


def _proj_kernel(x_ref, w_ref, o_ref):
    o_ref[...] = jnp.dot(x_ref[...].astype(BF16), w_ref[...], preferred_element_type=F32)


def _proj(x, w, tm):
    t, k = x.shape
    n = w.shape[1]
    return pl.pallas_call(
        _proj_kernel,
        grid=(t // tm,),
        in_specs=[pl.BlockSpec((tm, k), lambda i: (i, 0)), pl.BlockSpec((k, n), lambda i: (0, 0))],
        out_specs=pl.BlockSpec((tm, n), lambda i: (i, 0)),
        out_shape=jax.ShapeDtypeStruct((t, n), F32),
        compiler_params=_cp(("parallel",)),
        name="proj_a",
    )(x, w)


def _fox_proj_kernel(x_ref, w_ref, bf_ref, k_ref, v_ref, lf_ref, lft_ref, *q_refs, attn_copies):
    p = jnp.dot(x_ref[...].astype(BF16), w_ref[...], preferred_element_type=F32)
    q = p[:, 0:D]
    k = p[:, D:2 * D]
    v = p[:, 2 * D:3 * D]
    k_ref[...] = k
    v_ref[...] = v
    lf = _log_sigmoid(p[:, 3 * D:3 * D + LANES] + bf_ref[...])
    lf_ref[...] = lf[:, 0:FH]
    lft_ref[...] = lf.T[0:FH, :]
    if attn_copies:
        qt_ref, kb_ref, vt_ref = q_refs
        kb_ref[...] = k.astype(BF16)
        qs = q * (FHD ** -0.5 * LOG2E)
        for h in range(FH):
            qt_ref[0, h] = qs[:, h * FHD:(h + 1) * FHD].T.astype(BF16)
            vt_ref[0, h] = v[:, h * FHD:(h + 1) * FHD].T.astype(BF16)
    else:
        q_refs[0][...] = q


def _fox_proj(x, w, bf_row, tm, attn_copies):
    t = x.shape[0]
    row = lambda i: (i, 0)
    out_specs = [pl.BlockSpec((tm, D), row), pl.BlockSpec((tm, D), row),
                 pl.BlockSpec((tm, FH), row), pl.BlockSpec((FH, tm), lambda i: (0, i))]
    out_shape = [jax.ShapeDtypeStruct((t, D), F32), jax.ShapeDtypeStruct((t, D), F32),
                 jax.ShapeDtypeStruct((t, FH), F32), jax.ShapeDtypeStruct((FH, t), F32)]
    if attn_copies:
        tile = pl.BlockSpec((1, FH, FHD, tm), lambda i: (i, 0, 0, 0))
        tiles = jax.ShapeDtypeStruct((t // tm, FH, FHD, tm), BF16)
        out_specs += [tile, pl.BlockSpec((tm, D), row), tile]
        out_shape += [tiles, jax.ShapeDtypeStruct((t, D), BF16), tiles]
    else:
        out_specs += [pl.BlockSpec((tm, D), row)]
        out_shape += [jax.ShapeDtypeStruct((t, D), F32)]
    return pl.pallas_call(
        functools.partial(_fox_proj_kernel, attn_copies=attn_copies),
        grid=(t // tm,),
        in_specs=[pl.BlockSpec((tm, D), row), pl.BlockSpec((D, C_COLS_PAD), lambda i: (0, 0)),
                  pl.BlockSpec((1, LANES), lambda i: (0, 0))],
        out_specs=out_specs,
        out_shape=out_shape,
        compiler_params=_cp(("parallel",)),
        name="proj_c",
    )(x, w, bf_row)


def _outproj_ln_kernel(a_ref, x_ref, w_ref, g_ref, b_ref, o_ref):
    h = jnp.dot(a_ref[...].astype(BF16), w_ref[...], preferred_element_type=F32)
    o_ref[...] = _layer_norm(ALPHA * x_ref[...] + h, g_ref[...], b_ref[...])


def _outproj_ln(a, x, w, g, b, tm):
    t = x.shape[0]
    row = lambda i: (i, 0)
    fixed = lambda i: (0, 0)
    return pl.pallas_call(
        _outproj_ln_kernel,
        grid=(t // tm,),
        in_specs=[pl.BlockSpec((tm, D), row), pl.BlockSpec((tm, D), row), pl.BlockSpec((D, D), fixed),
                  pl.BlockSpec((1, D), fixed), pl.BlockSpec((1, D), fixed)],
        out_specs=pl.BlockSpec((tm, D), row),
        out_shape=jax.ShapeDtypeStruct((t, D), F32),
        compiler_params=_cp(("parallel",)),
        name="outproj_ln",
    )(a, x, w, g, b)


def _unit_lower_inverses(a_list, ri, ci):
    eye = (ri == ci).astype(F32)
    blk = (ri >> 4) == (ci >> 4)
    d = [_split2(jnp.where(blk, a, 0.0)) for a in a_list]
    d2f = [_mm_s(x, x) for x in d]
    d2 = [_split2(x) for x in d2f]
    d4f = [_mm_s(x, x) for x in d2]
    d4 = [_split2(x) for x in d4f]
    d8 = [_split2(_mm_s(x, x)) for x in d4]
    xs = [eye - jnp.where(blk, a, 0.0) for a in a_list]
    for pw in (d2, d4, d8):
        xs = [x + _mm_s(_split2(x), p) for x, p in zip(xs, pw)]
    for s in (5, 6, 7):
        msk = ((ri >> s) == (ci >> s)) & ((ri >> (s - 1)) != (ci >> (s - 1)))
        es = [_split2(jnp.where(msk, a, 0.0)) for a in a_list]
        xsp = [_split2(x) for x in xs]
        xe = [_split2(_mm_s(x, e)) for x, e in zip(xsp, es)]
        xs = [x - _mm_s(y, xp) for x, y, xp in zip(xs, xe, xsp)]
    return xs


def _head_gate_norm(out, g_row, gate):
    return out * lax.rsqrt(jnp.mean(out * out, axis=-1, keepdims=True) + NORM_EPS) * g_row * _silu(gate)


def _mixer_a_prompt_kernel(p_ref, cos_ref, sin_ref, cw_ref, alog_ref, dt_ref, rg_ref, gg_ref,
                           mix_ref, ret_ref, gdn_ref, conv_ref, ext_ref, *, nb):
    n = pl.program_id(0)
    c = CHUNK

    @pl.when(n == 0)
    def _init():
        ret_ref[...] = jnp.zeros_like(ret_ref)
        gdn_ref[...] = jnp.zeros_like(gdn_ref)
        ext_ref[:, 0:8, :] = jnp.zeros((nb, 8, CONV_CH), F32)

    ri = _iota2((c, c), 0)
    ci = _iota2((c, c), 1)
    rf = ri.astype(F32)
    diff = rf - ci.astype(F32)
    cosv = cos_ref[...]
    sinv = sin_ref[...]

    lgs = [math.log1p(-(2.0 ** (-5.0 - h))) for h in range(RH)]
    decay = [jnp.where(diff >= 0, jnp.exp(jnp.maximum(diff, 0.0) * lg), 0.0) for lg in lgs]
    inner = [jnp.exp((rf + 1.0) * lg) for lg in lgs]
    kdec = [jnp.exp((c - 1.0 - rf) * lg) for lg in lgs]
    r_units = [(b, h) for b in range(nb) for h in range(RH)]
    r_q, r_k = [], []
    for b, h in r_units:
        rq = p_ref[b, :, h * RDK:(h + 1) * RDK]
        rk = p_ref[b, :, RH * RDK + h * RDK:RH * RDK + (h + 1) * RDK]
        r_q.append(rq * cosv + pltpu.roll(rq, RDK // 2, 1) * sinv)
        r_k.append((rk * cosv + pltpu.roll(rk, RDK // 2, 1) * sinv) * (RDK ** -0.5))
    r_v = [p_ref[b, :, 2 * RH * RDK + h * RDV:2 * RH * RDK + (h + 1) * RDV] for b, h in r_units]
    r_s = [ret_ref[b, h] for b, h in r_units]
    r_scores = [_mm_nt(q, k) * decay[h] for q, k, (_, h) in zip(r_q, r_k, r_units)]
    r_state_out = [_mm(q * inner[h], s) for q, s, (_, h) in zip(r_q, r_s, r_units)]
    r_new = [_mm_tn(k * kdec[h], v) for k, v, (_, h) in zip(r_k, r_v, r_units)]
    r_out = [_mm(sc, v) + so for sc, v, so in zip(r_scores, r_v, r_state_out)]
    for (b, h), s, new, out in zip(r_units, r_s, r_new, r_out):
        ret_ref[b, h] = math.exp(c * lgs[h]) * s + new
        rgate = p_ref[b, :, 2 * RH * RDK + RH * RDV + h * RDV:2 * RH * RDK + RH * RDV + (h + 1) * RDV]
        mix_ref[b, :, h * RDV:(h + 1) * RDV] = _head_gate_norm(out, rg_ref[...], rgate)

    g_off = 2 * RH * RDK + 2 * RH * RDV
    z_off = g_off + CONV_CH
    incl = ri >= ci
    strict = ri > ci
    tril = incl.astype(BF16)
    units = []
    for b in range(nb):
        ext_ref[b, 8:8 + c, :] = p_ref[b, :, g_off:g_off + CONV_CH]
        y = ext_ref[b, 5:5 + c, :] * cw_ref[0:1, :]
        for i in range(1, CONV_W):
            y = y + ext_ref[b, 5 + i:5 + i + c, :] * cw_ref[i:i + 1, :]
        conv_ref[b] = ext_ref[b, c + 5:c + 8, :]
        ext_ref[b, 0:8, :] = ext_ref[b, c:c + 8, :]
        y = _silu(y)
        tail = p_ref[b, :, z_off + GH * GDV:z_off + GH * GDV + LANES]
        g_all = -jnp.exp(alog_ref[...]) * _softplus(tail + dt_ref[...])
        beta_all = _sigmoid(tail)
        for h in range(GH):
            qh = y[:, h * GDK:(h + 1) * GDK]
            kh = y[:, GH * GDK + h * GDK:GH * GDK + (h + 1) * GDK]
            vh = y[:, 2 * GH * GDK + h * GDV:2 * GH * GDK + (h + 1) * GDV]
            qh = qh * lax.rsqrt(jnp.sum(qh * qh, axis=-1, keepdims=True) + NORM_EPS) * (GDK ** -0.5)
            kh = kh * lax.rsqrt(jnp.sum(kh * kh, axis=-1, keepdims=True) + NORM_EPS)
            gb = jnp.broadcast_to(g_all[:, h:h + 1], (c, c))
            bcol = beta_all[:, GH + h:GH + h + 1]
            units.append((b, h, qh, kh, vh, gb, bcol))

    gcs = [_mm_01_left(tril, u[5]) for u in units]
    decs = []
    for gc in gcs:
        dmat = gc - gc.T
        decs.append(jnp.where(incl, jnp.exp(jnp.where(incl, dmat, 0.0)), 0.0))
    kbs = [u[3] * u[6] for u in units]
    a_list = [jnp.where(strict, _mm_nt(kb, u[3]) * dec, 0.0) for kb, u, dec in zip(kbs, units, decs)]
    ts = [t.astype(BF16) for t in _unit_lower_inverses(a_list, ri, ci)]
    egcs = [jnp.exp(gc) for gc in gcs]
    us = [jnp.dot(t, (u[4] * u[6]).astype(BF16), preferred_element_type=F32) for t, u in zip(ts, units)]
    ws = [jnp.dot(t, (kb * egc).astype(BF16), preferred_element_type=F32) for t, kb, egc in zip(ts, kbs, egcs)]
    attns = [_mm_nt(u[2], u[3]) * dec for u, dec in zip(units, decs)]
    ss = [gdn_ref[u[0], u[1]] for u in units]
    v_news = [uu - _mm(w, s) for uu, w, s in zip(us, ws, ss)]
    outs = [_mm(u[2] * egc, s) + _mm(attn, vn) for u, egc, s, attn, vn in zip(units, egcs, ss, attns, v_news)]
    for u, gc, s, vn, out in zip(units, gcs, ss, v_news, outs):
        b, h = u[0], u[1]
        g_last = gc[c - 1:c, :]
        gdn_ref[b, h] = jnp.exp(g_last) * s + _mm_tn(u[3] * jnp.exp(g_last - gc), vn)
        gz = p_ref[b, :, z_off + h * GDV:z_off + (h + 1) * GDV]
        mix_ref[b, :, RH * RDV + h * GDV:RH * RDV + (h + 1) * GDV] = _head_gate_norm(out, gg_ref[...], gz)


def _mixer_a_prompt(proj, cos2, sin2, conv_w, alog_row, dt_row, rg_row, gg_row):
    b, l, _ = proj.shape
    fixed = lambda n: (0, 0)
    return pl.pallas_call(
        functools.partial(_mixer_a_prompt_kernel, nb=b),
        grid=(l // CHUNK,),
        in_specs=[pl.BlockSpec((b, CHUNK, A_COLS_PAD), lambda n: (0, n, 0)),
                  pl.BlockSpec((CHUNK, LANES), lambda n: (n, 0)),
                  pl.BlockSpec((CHUNK, LANES), lambda n: (n, 0)),
                  pl.BlockSpec((CONV_W, CONV_CH), fixed),
                  pl.BlockSpec((1, LANES), fixed), pl.BlockSpec((1, LANES), fixed),
                  pl.BlockSpec((1, LANES), fixed), pl.BlockSpec((1, LANES), fixed)],
        out_specs=[pl.BlockSpec((b, CHUNK, D), lambda n: (0, n, 0)),
                   pl.BlockSpec((b, RH, RDK, RDV), lambda n: (0, 0, 0, 0)),
                   pl.BlockSpec((b, GH, GDK, GDV), lambda n: (0, 0, 0, 0)),
                   pl.BlockSpec((b, CONV_W - 1, CONV_CH), lambda n: (0, 0, 0))],
        out_shape=[jax.ShapeDtypeStruct((b, l, D), F32),
                   jax.ShapeDtypeStruct((b, RH, RDK, RDV), F32),
                   jax.ShapeDtypeStruct((b, GH, GDK, GDV), F32),
                   jax.ShapeDtypeStruct((b, CONV_W - 1, CONV_CH), F32)],
        scratch_shapes=[pltpu.VMEM((b, CHUNK + 8, CONV_CH), F32)],
        compiler_params=_cp(("arbitrary",)),
        name="mixer_a_prompt",
    )(proj, cos2, sin2, conv_w, alog_row, dt_row, rg_row, gg_row)


def _mixer_a_sample_kernel(p_ref, cos_ref, sin_ref, cw_ref, alog_ref, dt_ref, rg_ref, gg_ref,
                           rs_ref, gs_ref, cs_ref, mix_ref, ret_ref, gdn_ref, conv_ref):
    eye = _iota2((LANES, LANES), 0) == _iota2((LANES, LANES), 1)

    def col(v):
        return jnp.sum(jnp.where(eye, jnp.broadcast_to(v, (LANES, LANES)), 0.0), axis=1, keepdims=True)

    def vec_mat(c, s):
        return jnp.sum(c * s, axis=0, keepdims=True)

    cosv = cos_ref[...]
    sinv = sin_ref[...]
    rq = [p_ref[0, :, h * RDK:(h + 1) * RDK] for h in range(RH)]
    rk = [p_ref[0, :, RH * RDK + h * RDK:RH * RDK + (h + 1) * RDK] for h in range(RH)]
    rv = [p_ref[0, :, 2 * RH * RDK + h * RDV:2 * RH * RDK + (h + 1) * RDV] for h in range(RH)]
    rq = [x * cosv + pltpu.roll(x, RDK // 2, 1) * sinv for x in rq]
    rk = [(x * cosv + pltpu.roll(x, RDK // 2, 1) * sinv) * (RDK ** -0.5) for x in rk]

    g_off = 2 * RH * RDK + 2 * RH * RDV
    gq_all = p_ref[0, :, g_off:g_off + CONV_CH]
    cb = cs_ref[0]
    y = cb[0:1, :] * cw_ref[0:1, :] + cb[1:2, :] * cw_ref[1:2, :] + cb[2:3, :] * cw_ref[2:3, :] \
        + gq_all * cw_ref[3:4, :]
    conv_ref[0, 0:2, :] = cb[1:3, :]
    conv_ref[0, 2:3, :] = gq_all
    y = _silu(y)
    z_off = g_off + CONV_CH
    tail = p_ref[0, :, z_off + GH * GDV:z_off + GH * GDV + LANES]
    g_all = -jnp.exp(alog_ref[...]) * _softplus(tail + dt_ref[...])
    beta_all = _sigmoid(tail)
    gq = [y[:, h * GDK:(h + 1) * GDK] for h in range(GH)]
    gk = [y[:, GH * GDK + h * GDK:GH * GDK + (h + 1) * GDK] for h in range(GH)]
    gv = [y[:, 2 * GH * GDK + h * GDV:2 * GH * GDK + (h + 1) * GDV] for h in range(GH)]
    gq = [x * lax.rsqrt(jnp.sum(x * x, axis=-1, keepdims=True) + NORM_EPS) * (GDK ** -0.5) for x in gq]
    gk = [x * lax.rsqrt(jnp.sum(x * x, axis=-1, keepdims=True) + NORM_EPS) for x in gk]
    eg = [jnp.exp(g_all[:, h:h + 1]) for h in range(GH)]
    beta = [beta_all[:, GH + h:GH + h + 1] for h in range(GH)]

    q_cols = [col(x) for x in rq + gq]
    k_cols = [col(x) for x in rk + gk]
    r_state = [rs_ref[0, h] for h in range(RH)]
    g_state = [gs_ref[0, h] for h in range(GH)]
    qs = [vec_mat(c, s) for c, s in zip(q_cols, r_state + g_state)]
    ws = [vec_mat(k_cols[RH + h] * (beta[h] * eg[h]), g_state[h]) for h in range(GH)]
    r_qk = [jnp.sum(a * b, axis=-1, keepdims=True) for a, b in zip(rq, rk)]
    g_qk = [jnp.sum(a * b, axis=-1, keepdims=True) for a, b in zip(gq, gk)]
    for h in range(RH):
        gamma = 1.0 - 2.0 ** (-5.0 - h)
        out = r_qk[h] * rv[h] + gamma * qs[h]
        ret_ref[0, h] = gamma * r_state[h] + k_cols[h] * rv[h]
        rgate = p_ref[0, :, 2 * RH * RDK + RH * RDV + h * RDV:2 * RH * RDK + RH * RDV + (h + 1) * RDV]
        mix_ref[0, :, h * RDV:(h + 1) * RDV] = _head_gate_norm(out, rg_ref[...], rgate)
    for h in range(GH):
        v_new = gv[h] * beta[h] - ws[h]
        out = eg[h] * qs[RH + h] + g_qk[h] * v_new
        gdn_ref[0, h] = eg[h] * g_state[h] + k_cols[RH + h] * v_new
        gz = p_ref[0, :, z_off + h * GDV:z_off + (h + 1) * GDV]
        mix_ref[0, :, RH * RDV + h * GDV:RH * RDV + (h + 1) * GDV] = _head_gate_norm(out, gg_ref[...], gz)


def _mixer_a_sample(proj, cos2, sin2, conv_w, alog_row, dt_row, rg_row, gg_row, ret_s, gdn_s, conv_s):
    b = proj.shape[0]
    fixed = lambda i: (0, 0)
    st = lambda i: (i, 0, 0, 0)
    return pl.pallas_call(
        _mixer_a_sample_kernel,
        grid=(b,),
        in_specs=[pl.BlockSpec((1, 1, A_COLS_PAD), lambda i: (i, 0, 0)),
                  pl.BlockSpec((1, LANES), fixed), pl.BlockSpec((1, LANES), fixed),
                  pl.BlockSpec((CONV_W, CONV_CH), fixed),
                  pl.BlockSpec((1, LANES), fixed), pl.BlockSpec((1, LANES), fixed),
                  pl.BlockSpec((1, LANES), fixed), pl.BlockSpec((1, LANES), fixed),
                  pl.BlockSpec((1, RH, RDK, RDV), st), pl.BlockSpec((1, GH, GDK, GDV), st),
                  pl.BlockSpec((1, CONV_W - 1, CONV_CH), lambda i: (i, 0, 0))],
        out_specs=[pl.BlockSpec((1, 1, D), lambda i: (i, 0, 0)),
                   pl.BlockSpec((1, RH, RDK, RDV), st), pl.BlockSpec((1, GH, GDK, GDV), st),
                   pl.BlockSpec((1, CONV_W - 1, CONV_CH), lambda i: (i, 0, 0))],
        out_shape=[jax.ShapeDtypeStruct((b, 1, D), F32),
                   jax.ShapeDtypeStruct((b, RH, RDK, RDV), F32),
                   jax.ShapeDtypeStruct((b, GH, GDK, GDV), F32),
                   jax.ShapeDtypeStruct((b, CONV_W - 1, CONV_CH), F32)],
        compiler_params=_cp(("parallel",)),
        name="mixer_a_sample",
    )(proj, cos2, sin2, conv_w, alog_row, dt_row, rg_row, gg_row, ret_s, gdn_s, conv_s)


def _fox_cumsum_kernel(x_ref, o_ref):
    n = x_ref.shape[1] // LANES
    upper = (_iota2((LANES, LANES), 0) <= _iota2((LANES, LANES), 1)).astype(BF16)
    carry = jnp.zeros((FH, 1), F32)
    for c in range(n):
        p = _mm_01_right(x_ref[:, c * LANES:(c + 1) * LANES], upper) + carry
        o_ref[0, :, c * LANES:(c + 1) * LANES] = p
        carry = p[:, LANES - 1:LANES]


def _fox_cumsum(lft, b, l):
    return pl.pallas_call(
        _fox_cumsum_kernel,
        grid=(b,),
        in_specs=[pl.BlockSpec((FH, l), lambda i: (0, i))],
        out_specs=pl.BlockSpec((1, FH, l), lambda i: (i, 0, 0)),
        out_shape=jax.ShapeDtypeStruct((b, FH, l), F32),
        compiler_params=_cp(("parallel",)),
        name="fox_cumsum",
    )(lft)


def _fox_flash_kernel(q_ref, k_ref, v_ref, ck_ref, o_ref, ka_ref, sa_ref, sb_ref, *, tq, tk):
    assert tq == 2 * tk and tk % LANES == 0
    qi = pl.program_id(2)
    nk = k_ref.shape[1] // tk

    @pl.when(qi == 0)
    def _augment_keys():
        eye = _iota2((LANES, LANES), 0) == _iota2((LANES, LANES), 1)
        lane = _iota2((LANES, LANES), 1)
        for j in range(nk):
            ka_ref[j * tk:(j + 1) * tk, 0:FHD] = k_ref[0, j * tk:(j + 1) * tk, :]
            for c in range(tk // LANES):
                row = jnp.broadcast_to(ck_ref[0, j:j + 1, c * LANES:(c + 1) * LANES], (LANES, LANES))
                col = jnp.sum(jnp.where(eye, row, 0.0), axis=1, keepdims=True) * LOG2E
                c1, c2, c3 = (t.astype(F32) for t in _split3(col))
                blk = jnp.where(lane == 0, c1, jnp.where(lane == 1, c2, jnp.where(lane == 2, c3, 0.0)))
                r0 = j * tk + c * LANES
                ka_ref[r0:r0 + LANES, FHD:2 * FHD] = blk.astype(BF16)

    sub = _iota2((FHD, tq), 0)
    minus_ones = jnp.where(sub < 3, -1.0, 0.0).astype(BF16)
    qa = jnp.concatenate([q_ref[0, 0], q_ref[1, 0]], axis=1)
    qa = jnp.concatenate([qa, minus_ones], axis=0)

    def scores(j, s_ref):
        ka = ka_ref[pl.ds(pl.multiple_of(j * tk, tk), tk), :]
        s_ref[...] = jnp.dot(ka, qa, preferred_element_type=F32)

    ones_rows = jnp.ones((16, tk), BF16)

    def update(j, s_ref, carry, masked):
        m_old, acc = carry
        s = s_ref[...]
        if masked:
            kpos = j * tk + _iota2((tk, tq), 0)
            qpos = qi * tq + _iota2((tk, tq), 1)
            s = jnp.where(kpos <= qpos, s, -jnp.inf)
        m_new = jnp.maximum(m_old, jnp.max(s, axis=0, keepdims=True))
        alpha = jnp.exp2(m_old - m_new)
        p = jnp.exp2(s - m_new).astype(BF16)
        v_aug = jnp.concatenate([v_ref[j, 0], ones_rows], axis=0)
        return m_new, alpha * acc + jnp.dot(v_aug, p, preferred_element_type=F32)

    def two_blocks(i, carry):
        scores(2 * i + 1, sb_ref)
        carry = update(2 * i, sa_ref, carry, False)
        scores(2 * i + 2, sa_ref)
        return update(2 * i + 1, sb_ref, carry, False)

    scores(0, sa_ref)
    init = (jnp.full((1, tq), -jnp.inf, F32), jnp.zeros((FHD + 16, tq), F32))
    carry = lax.fori_loop(0, qi, two_blocks, init)
    scores(2 * qi + 1, sb_ref)
    carry = update(2 * qi, sa_ref, carry, True)
    carry = update(2 * qi + 1, sb_ref, carry, True)
    acc = carry[1]
    o_ref[0] = (acc[0:FHD] / acc[FHD:FHD + 1]).T


def _fox_flash(qt, kb, vt, cum, tq, tk):
    b, l, _ = kb.shape
    nk = l // tk
    return pl.pallas_call(
        functools.partial(_fox_flash_kernel, tq=tq, tk=tk),
        grid=(b, FH, l // tq),
        in_specs=[pl.BlockSpec((tq // tk, 1, FHD, tk), lambda i, h, qi: (i * (l // tq) + qi, h, 0, 0)),
                  pl.BlockSpec((1, l, FHD), lambda i, h, qi: (i, 0, h)),
                  pl.BlockSpec((nk, 1, FHD, tk), lambda i, h, qi: (i, h, 0, 0)),
                  pl.BlockSpec((1, nk, tk), lambda i, h, qi: (i * FH + h, 0, 0))],
        out_specs=pl.BlockSpec((1, tq, FHD), lambda i, h, qi: (i, qi, h)),
        out_shape=jax.ShapeDtypeStruct((b, l, D), F32),
        scratch_shapes=[pltpu.VMEM((l, 2 * FHD), BF16), pltpu.VMEM((tk, tq), F32), pltpu.VMEM((tk, tq), F32)],
        compiler_params=_cp(("arbitrary", "arbitrary", "arbitrary")),
        name="fox_flash",
    )(qt, kb, vt, cum)


def _fox_sample_kernel(pt_ref, q_ref, kn_ref, vn_ref, lfn_ref, *rest):
    npp = PAGES_PER_STEP
    k_refs, v_refs, lf_refs = rest[0:npp], rest[npp:2 * npp], rest[2 * npp:3 * npp]
    o_ref, m_ref, l_ref, acc_ref, carry_ref = rest[3 * npp:]
    j = pl.program_id(1)
    scale = FHD ** -0.5
    rows = PAGE * FH

    @pl.when(j == 0)
    def _init():
        m_ref[...] = jnp.full(m_ref.shape, -jnp.inf, F32)
        l_ref[...] = jnp.zeros(l_ref.shape, F32)
        acc_ref[...] = jnp.zeros(acc_ref.shape, F32)
        carry_ref[...] = jnp.zeros(carry_ref.shape, F32)

    q = q_ref[0]
    qb = (q * scale).astype(BF16)
    own = (_iota2((FH, rows), 1) & (FH - 1)) == _iota2((FH, rows), 0)
    nch = rows // LANES
    ci = _iota2((LANES, 2 * LANES), 1)
    pre_tot = ((_iota2((LANES, 2 * LANES), 0) <= ci) | (ci >= LANES)).astype(BF16)
    stacked, s_pages = [], []
    for pg in range(npp):
        kp = k_refs[pg][0].reshape(rows, FHD).astype(BF16)
        s_pages.append(lax.dot_general(qb, kp, NT, preferred_element_type=F32))
        lf = jnp.where(own, jnp.broadcast_to(lf_refs[pg][0], (FH, rows)), 0.0)
        stacked += [lf[:, c * LANES:(c + 1) * LANES] for c in range(nch)]
    pt = _mm_01_right(jnp.concatenate(stacked, axis=0), pre_tot)
    carry = carry_ref[...]
    logits = []
    for pg in range(npp):
        cums = []
        for c in range(nch):
            r0 = (pg * nch + c) * FH
            cums.append(pt[r0:r0 + FH, 0:LANES] + carry)
            carry = carry + pt[r0:r0 + FH, LANES:2 * LANES]
        logits.append(jnp.where(own, s_pages[pg] - jnp.concatenate(cums, axis=1), -jnp.inf))
    carry_ref[...] = carry
    m_old = m_ref[...]
    m_new = m_old
    for lg in logits:
        m_new = jnp.maximum(m_new, jnp.max(lg, axis=1, keepdims=True))
    alpha = jnp.exp(m_old - m_new)
    l_new = alpha * l_ref[...]
    acc = alpha * acc_ref[...]
    for pg in range(npp):
        p = jnp.exp(logits[pg] - m_new)
        l_new = l_new + jnp.sum(p, axis=1, keepdims=True)
        vp = v_refs[pg][0].reshape(rows, FHD).astype(BF16)
        acc = acc + jnp.dot(p.astype(BF16), vp, preferred_element_type=F32)
    m_ref[...] = m_new
    l_ref[...] = l_new
    acc_ref[...] = acc

    @pl.when(j == pl.num_programs(1) - 1)
    def _fin():
        s_new = jnp.sum(q * kn_ref[0], axis=1, keepdims=True) * scale - (carry[:, 0:1] + lfn_ref[0])
        m2 = jnp.maximum(m_new, s_new)
        a2 = jnp.exp(m_new - m2)
        p2 = jnp.exp(s_new - m2)
        o_ref[0] = (a2 * acc + p2 * vn_ref[0]) / (a2 * l_new + p2)


def _fox_sample(page_table, q3, k3, v3, lf3, k_pool, v_pool, lf_pool):
    b = q3.shape[0]
    npg = page_table.shape[1]
    npp = PAGES_PER_STEP
    assert npg % npp == 0
    pt = page_table.reshape(-1)
    tok = lambda i, j, pt_ref: (i, 0, 0)

    def page(r, nd):
        return lambda i, j, pt_ref: (pt_ref[i * npg + j * npp + r],) + (0,) * nd

    grid_spec = pltpu.PrefetchScalarGridSpec(
        num_scalar_prefetch=1,
        grid=(b, npg // npp),
        in_specs=[pl.BlockSpec((1, FH, FHD), tok), pl.BlockSpec((1, FH, FHD), tok), pl.BlockSpec((1, FH, FHD), tok),
                  pl.BlockSpec((1, FH, 1), tok)]
        + [pl.BlockSpec((1, PAGE, FH, FHD), page(r, 3)) for r in range(npp)]
        + [pl.BlockSpec((1, PAGE, FH, FHD), page(r, 3)) for r in range(npp)]
        + [pl.BlockSpec((1, 1, PAGE * FH), page(r, 2)) for r in range(npp)],
        out_specs=pl.BlockSpec((1, FH, FHD), tok),
        scratch_shapes=[pltpu.VMEM((FH, 1), F32), pltpu.VMEM((FH, 1), F32), pltpu.VMEM((FH, FHD), F32),
                        pltpu.VMEM((FH, LANES), F32)],
    )
    lf_flat =lf_pool.reshape(lf_pool.shape[0], 1, PAGE * FH)
    return pl.pallas_call(
        _fox_sample_kernel,
        grid_spec=grid_spec,
        out_shape=jax.ShapeDtypeStruct((b, FH, FHD), F32),
        compiler_params=_cp(("parallel", "arbitrary")),
        name="fox_sample",
    )(pt, q3, k3, v3, lf3, *([k_pool] * npp), *([v_pool] * npp), *([lf_flat] * npp))


_CAND_BLOCKS = [(a, 16 if a == 0 else 8, 16 // (a + 1)) for a in range(8)]


_MARK0 = -(2.0 ** 127)
_MARK_STEP = 2.0 ** 120
_MARK_LIMIT = -(2.0 ** 126)


def _top16(s, iota):
    rank = jnp.full(s.shape, float(PTOPK), F32)
    vals = jnp.zeros((PTOPK, s.shape[1]), F32)
    i16 = _iota2((PTOPK, s.shape[1]), 0)
    for r in range(PTOPK):
        m = jnp.max(s, axis=0, keepdims=True)
        sel = iota == jnp.min(jnp.where(s == m, iota, 1e9), axis=0, keepdims=True)
        rank = jnp.where(sel, float(r), rank)
        s = jnp.where(sel, -jnp.inf, s)
        vals = jnp.where(i16 == r, m, vals)
    return rank, vals


def _top16_fast(s):
    lowest = jnp.min(s)
    vals = jnp.zeros((PTOPK, s.shape[1]), F32)
    i16 = _iota2((PTOPK, s.shape[1]), 0)
    for r in range(PTOPK):
        m = jnp.max(s, axis=0, keepdims=True)
        s = jnp.where(s == m, _MARK0 + r * _MARK_STEP, s)
        vals = jnp.where(i16 == r, m, vals)
    marked = s < _MARK_LIMIT
    rank = jnp.where(marked, (s - _MARK0) * (1.0 / _MARK_STEP), float(PTOPK))
    taken = jnp.sum(jnp.where(marked, 1.0, 0.0), axis=0, keepdims=True)
    return rank, vals, _not_16(taken) | (lowest <= _MARK_LIMIT)


def _pick16(cand, cidx, exact):
    picked = jnp.zeros(cand.shape, F32)
    for _ in range(PTOPK):
        m = jnp.max(cand, axis=0, keepdims=True)
        sel = cand == m
        if exact:
            sel = cidx == jnp.min(jnp.where(sel, cidx, 1e9), axis=0, keepdims=True)
        picked = jnp.where(sel, 1.0, picked)
        cand = jnp.where(sel, -jnp.inf, cand)
    return picked


def _not_16(count):
    return jnp.max(jnp.abs(count - float(PTOPK))) > 0.5


def _route_head(s_t):
    s1 = s_t[0:NKEYS]
    s2 = s_t[NKEYS:2 * NKEYS]
    w = s_t.shape[1]
    iota = _iota2((NKEYS, w), 0).astype(F32)
    rank1, v1, redo1 = _top16_fast(s1)
    rank2, v2, redo2 = _top16_fast(s2)
    rank1, v1, rank2, v2 = lax.cond(redo1 | redo2, lambda: _top16(s1, iota) + _top16(s2, iota),
                                    lambda: (rank1, v1, rank2, v2))
    e1 = jnp.exp(v1 - v1[0:1])
    e2 = jnp.exp(v2 - v2[0:1])
    i8 = _iota2((8, w), 0)
    i8f = i8.astype(F32)
    i16f = _iota2((16, w), 0).astype(F32)
    cand, cidx, cprob = [], [], []
    for a, rows, valid in _CAND_BLOCKS:
        c = v1[a:a + 1] + v2[0:rows]
        pr = e1[a:a + 1] * e2[0:rows]
        if rows == 16:
            ix = i16f
        else:
            ix = i8f + float(a * PTOPK)
            c = jnp.where(i8 < valid, c, -jnp.inf)
        cand.append(c)
        cidx.append(ix)
        cprob.append(pr)
    cand.append(v1[8:16] + v2[0:1])
    cidx.append((i8f + 8.0) * float(PTOPK))
    cprob.append(e1[8:16] * e2[0:1])
    cand = jnp.concatenate(cand, axis=0)
    cidx = jnp.concatenate(cidx, axis=0)
    cprob = jnp.concatenate(cprob, axis=0)
    picked = _pick16(cand, cidx, False)
    picked = lax.cond(_not_16(jnp.sum(picked, axis=0, keepdims=True)),
                      lambda: _pick16(cand, cidx, True), lambda: picked)
    z = jnp.sum(picked * cprob, axis=0, keepdims=True)
    n_low = jnp.zeros((8, w), F32)
    off = 0
    for a, rows, _ in _CAND_BLOCKS:
        cnt = jnp.sum(picked[off:off + rows], axis=0, keepdims=True)
        n_low = jnp.where(i8 == a, cnt, n_low)
        off += rows
    n16 = jnp.concatenate([n_low, picked[off:off + 8]], axis=0)
    n1d = jnp.zeros((NKEYS, w), F32)
    for a in range(PTOPK):
        n1d = jnp.where(rank1 == float(a), n16[a:a + 1], n1d)
    e1d = jnp.exp(s1 - v1[0:1]) * (GELU_GATE_SCALE / z)
    e2d = jnp.exp(s2 - v2[0:1])
    return rank2, e2d, n1d, e1d


def _peer_kernel(x_ref, wq_ref, keys_ref, u0_ref, un_ref, vt_ref, g_ref, b_ref, o_ref,
                 xb_ref, q_ref, r2_ref, e2_ref, n1_ref, e1_ref, ha_ref, hb_ref, at_ref, acc_ref, *, tm, te, rw):
    j = pl.program_id(1)
    nlg = tm // LANES
    nsub = te // NKEYS

    @pl.when(j == 0)
    def _route():
        xb = x_ref[...].astype(BF16)
        xb_ref[...] = xb
        q = jnp.dot(xb, wq_ref[...], preferred_element_type=F32)
        for h in range(PH):
            q_ref[h] = q[:, h * LANES:(h + 1) * LANES].astype(BF16)
        acc_ref[...] = jnp.zeros(acc_ref.shape, F32)
        ha_ref[:, 0:tm] = lax.dot_general(u0_ref[...], xb, NT, preferred_element_type=F32)

        def body(h, carry):
            for w0 in range(0, tm, rw):
                s_t = lax.dot_general(keys_ref[...], q_ref[h, w0:w0 + rw, :], NT,
                                      preferred_element_type=F32)
                r2, e2, n1, e1 = _route_head(s_t)
                for g in range(rw // LANES):
                    lg = slice(g * LANES, (g + 1) * LANES)
                    r2_ref[h, w0 // LANES + g] = r2[:, lg].astype(GATE_DT)
                    e2_ref[h, w0 // LANES + g] = e2[:, lg].astype(GATE_DT)
                    n1_ref[h, w0 // LANES + g] = n1[:, lg]
                    e1_ref[h, w0 // LANES + g] = e1[:, lg]
            return carry

        lax.fori_loop(0, PH, body, 0)

    def step(h_cur_ref, h_next_ref):
        h_next_ref[:, 0:tm] = lax.dot_general(un_ref[...], xb_ref[...], NT, preferred_element_type=F32)
        zero = jnp.zeros((), GATE_DT)
        pair = 2
        for pp in range(nsub // pair):
            for g in range(nlg):
                lanes = slice(g * LANES, (g + 1) * LANES)
                gates = [jnp.zeros((NKEYS, LANES), GATE_DT) for _ in range(pair)]
                for h in range(PH):
                    r2 = r2_ref[h, g]
                    e2 = e2_ref[h, g]
                    for ii in range(pair):
                        i1 = j * nsub + pp * pair + ii
                        n_row = n1_ref[h, g, pl.ds(i1, 1), :].astype(GATE_DT)
                        e_row = e1_ref[h, g, pl.ds(i1, 1), :].astype(GATE_DT)
                        gates[ii] = gates[ii] + jnp.where(r2 < n_row, e2, zero) * e_row
                for ii in range(pair):
                    r0 = (pp * pair + ii) * NKEYS
                    hb = h_cur_ref[r0:r0 + NKEYS, lanes]
                    act = hb * (1.0 + lax.erf(hb))
                    at_ref[r0:r0 + NKEYS, lanes] = (act.astype(GATE_DT) * gates[ii]).astype(BF16)
        acc_ref[...] += jnp.dot(vt_ref[0], at_ref[:, 0:tm], preferred_element_type=F32)

    @pl.when(j % 2 == 0)
    def _even():
        step(ha_ref, hb_ref)

    @pl.when(j % 2 == 1)
    def _odd():
        step(hb_ref, ha_ref)

    @pl.when(j == pl.num_programs(1) - 1)
    def _fin():
        o_ref[...] = _layer_norm(ALPHA * x_ref[...] + acc_ref[...].T, g_ref[...], b_ref[...])


def _peer(x, wq, keys, u, vt, g, b, tm, te):
    t = x.shape[0]
    nj = NEXP // te
    fixed = lambda i, j: (0, 0)
    row_pad = LANES if (tm // LANES) % 4 == 0 else 0
    return pl.pallas_call(
        functools.partial(_peer_kernel, tm=tm, te=te, rw=tm),
        grid=(t // tm, nj),
        in_specs=[pl.BlockSpec((tm, D), lambda i, j: (i, 0)),
                  pl.BlockSpec((D, PH * LANES), fixed),
                  pl.BlockSpec((2 * NKEYS, LANES), fixed),
                  pl.BlockSpec((te, D), fixed),
                  pl.BlockSpec((te, D), lambda i, j: (jnp.minimum(j + 1, nj - 1), 0)),
                  pl.BlockSpec((1, D, te), lambda i, j: (j, 0, 0)),
                  pl.BlockSpec((1, D), fixed), pl.BlockSpec((1, D), fixed)],
        out_specs=pl.BlockSpec((tm, D), lambda i, j: (i, 0)),
        out_shape=jax.ShapeDtypeStruct((t, D), F32),
        scratch_shapes=[pltpu.VMEM((tm, D), BF16),
                        pltpu.VMEM((PH, tm, LANES), BF16),
                        pltpu.VMEM((PH, tm // LANES, NKEYS, LANES), GATE_DT),
                        pltpu.VMEM((PH, tm // LANES, NKEYS, LANES), GATE_DT),
                        pltpu.VMEM((PH, tm // LANES, NKEYS, LANES), F32),
                        pltpu.VMEM((PH, tm // LANES, NKEYS, LANES), F32),
                        pltpu.VMEM((te, tm + row_pad), F32),
                        pltpu.VMEM((te, tm + row_pad), F32),
                        pltpu.VMEM((te, tm + row_pad), BF16),
                        pltpu.VMEM((D, tm), F32)],
        compiler_params=_cp(("parallel", "arbitrary")),
        name="peer",
    )(x, wq, keys, u, u, vt, g, b)


def _rope_tables(pos):
    half = RDK // 2
    inv = ROPE_BASE ** (-jnp.arange(half, dtype=F32) / half)
    ang = pos.astype(F32)[:, None] * inv[None, :]
    cos, sin = jnp.cos(ang), jnp.sin(ang)
    return jnp.concatenate([cos, cos], -1), jnp.concatenate([-sin, sin], -1)


def _pad_lanes(v, width=LANES, offset=0):
    return jnp.zeros((1, width), F32).at[0, offset:offset + v.shape[0]].set(v.astype(F32))


def _peer_params(wq, sub_keys, u_tab, v_tab, te):
    half = sub_keys.shape[-1]
    keys = jnp.zeros((2 * NKEYS, LANES), F32)
    keys = keys.at[0:NKEYS, 0:half].set(sub_keys[0]).at[NKEYS:, half:2 * half].set(sub_keys[1])
    vt = v_tab.astype(BF16).reshape(NEXP // te, te, D).transpose(0, 2, 1)
    return wq.astype(BF16), keys.astype(BF16), (u_tab * (2.0 ** -0.5)).astype(BF16), vt


def _tile(t, pref):
    return pref if t % pref == 0 else LANES


def kernel(x_prompt, x_sample, state_ret, state_gdn, state_gdn_conv, cache_fox_k, cache_fox_v, cache_fox_logf,
           page_table, w_in_a, ret_norm_g, gdn_a_log, gdn_dt_bias, gdn_conv_w, gdn_norm_g, w_out_a, w_in_c,
           fox_b_f, w_out_c, peer_wq, peer_sub_keys, peer_u, peer_v, ln_g, ln_b):
    bp, lp, _ = x_prompt.shape
    bs = x_sample.shape[0]
    tp = bp * lp
    xp = x_prompt.reshape(tp, D)
    xs = x_sample.reshape(bs, D)
    tmp, tms = _tile(tp, 512), _tile(bs, 512)
    te = 1024
    ln = lambda layer, k: (ln_g[layer, k].reshape(1, D), ln_b[layer, k].reshape(1, D))

    w_in = jnp.pad(w_in_a[0], ((0, 0), (0, A_COLS_PAD - A_COLS))).astype(BF16)
    cos_p, sin_p = _rope_tables(jnp.arange(lp))
    cos_s, sin_s = _rope_tables(PAST + jnp.arange(1))
    alog, dtb = _pad_lanes(gdn_a_log[0]), _pad_lanes(gdn_dt_bias[0])
    rg, gg = ret_norm_g[0].reshape(1, RDV), gdn_norm_g[0].reshape(1, GDV)
    proj_p = _proj(xp, w_in, tmp).reshape(bp, lp, A_COLS_PAD)
    proj_s = _proj(xs, w_in, tms).reshape(bs, 1, A_COLS_PAD)
    mix_p, ret_p, gdn_p, conv_p = _mixer_a_prompt(proj_p, cos_p, sin_p, gdn_conv_w[0], alog, dtb, rg, gg)
    mix_s, ret_s, gdn_s, conv_s = _mixer_a_sample(proj_s, cos_s, sin_s, gdn_conv_w[0], alog, dtb, rg, gg,
                                                  state_ret[0], state_gdn[0], state_gdn_conv[0])
    w_out = w_out_a[0].astype(BF16)
    g0, b0 = ln(0, 0)
    xp = _outproj_ln(mix_p.reshape(tp, D), xp, w_out, g0, b0, tmp)
    xs = _outproj_ln(mix_s.reshape(bs, D), xs, w_out, g0, b0, tms)
    pw = _peer_params(peer_wq[0], peer_sub_keys[0], peer_u[0], peer_v[0], te)
    g1, b1 = ln(0, 1)
    xp = _peer(xp, *pw, g1, b1, tmp, te)
    xs = _peer(xs, *pw, g1, b1, tms, te)

    w_c = jnp.pad(w_in_c[0], ((0, 0), (0, C_COLS_PAD - w_in_c.shape[-1]))).astype(BF16)
    bf_row = _pad_lanes(fox_b_f[0])
    kp, vp, lfp, lftp, qtp, kbp, vtp = _fox_proj(xp, w_c, bf_row, tmp, True)
    ks, vs, lfs, _, qs = _fox_proj(xs, w_c, bf_row, tms, False)
    tk = tmp
    tq = 2 * tk
    cum = _fox_cumsum(lftp, bp, lp).reshape(bp * FH, lp // tk, tk)
    op = _fox_flash(qtp, kbp.reshape(bp, lp, D), vtp, cum, tq, tk)
    os_ = _fox_sample(page_table, qs.reshape(bs, FH, FHD), ks.reshape(bs, FH, FHD), vs.reshape(bs, FH, FHD),
                      lfs.reshape(bs, FH, 1), cache_fox_k[0], cache_fox_v[0], cache_fox_logf[0])
    w_oc = w_out_c[0].astype(BF16)
    g2, b2 = ln(1, 0)
    xp = _outproj_ln(op.reshape(tp, D), xp, w_oc, g2, b2, tmp)
    xs = _outproj_ln(os_.reshape(bs, D), xs, w_oc, g2, b2, tms)
    pw = _peer_params(peer_wq[1], peer_sub_keys[1], peer_u[1], peer_v[1], te)
    g3, b3 = ln(1, 1)
    xp = _peer(xp, *pw, g3, b3, tmp, te)
    xs = _peer(xs, *pw, g3, b3, tms, te)

    return (xp.reshape(bp, lp, D), xs.reshape(bs, 1, D),
            ret_p[None], ret_s[None], gdn_p[None], gdn_s[None], conv_p[None], conv_s[None],
            kp.reshape(1, bp, lp, FH, FHD), ks.reshape(1, bs, 1, FH, FHD),
            vp.reshape(1, bp, lp, FH, FHD), vs.reshape(1, bs, 1, FH, FHD),
            lfp.reshape(1, bp, lp, FH), lfs.reshape(1, bs, 1, FH))
```

```python
import functools
import math

import jax
import jax.numpy as jnp
from jax import lax
from jax.experimental import pallas as pl
from jax.experimental.pallas import tpu as pltpu

F32 = jnp.float32
BF16 = jnp.bfloat16

D = 1024
PAST = 2048
PAGE = 128
RH, RDK, RDV = 4, 128, 128
GH, GDK, GDV = 4, 128, 128
CONV_W = 4
CONV_CH = 2 * GH * GDK + GH * GDV
A_COLS = 4104
A_COLS_PAD = 4224
FH, FHD = 8, 128
C_COLS_PAD = 3200
PH, PTOPK, NKEYS = 8, 16, 128
NEXP = NKEYS * NKEYS
ALPHA = 4.0 ** 0.25
LN_EPS = 1e-5
NORM_EPS = 1e-6
ROPE_BASE = 10000.0
LOG2E = math.log2(math.e)
GELU_GATE_SCALE = 2.0 ** -0.5
LANES = 128
CHUNK = 128
PAGES_PER_STEP = 16
GATE_DT = BF16
VMEM_LIMIT = 56 * 1024 * 1024

NT = (((1,), (1,)), ((), ()))
TN = (((0,), (0,)), ((), ()))


def _cp(sem):
    return pltpu.CompilerParams(dimension_semantics=sem, vmem_limit_bytes=VMEM_LIMIT)


def _mm(a, b):
    return jnp.dot(a.astype(BF16), b.astype(BF16), preferred_element_type=F32)


def _mm_nt(a, b):
    return lax.dot_general(a.astype(BF16), b.astype(BF16), NT, preferred_element_type=F32)


def _mm_tn(a, b):
    return lax.dot_general(a.astype(BF16), b.astype(BF16), TN, preferred_element_type=F32)


def _split3(x):
    x1 = x.astype(BF16)
    r1 = x - x1.astype(F32)
    x2 = r1.astype(BF16)
    return x1, x2, (r1 - x2.astype(F32)).astype(BF16)


def _mm_01_left(m01, x):
    return sum(jnp.dot(m01, t, preferred_element_type=F32) for t in _split3(x))


def _mm_01_right(x, m01):
    return sum(jnp.dot(t, m01, preferred_element_type=F32) for t in _split3(x))


def _split2(a):
    hi = a.astype(BF16)
    return hi, (a - hi.astype(F32)).astype(BF16)


def _mm_s(a, b):
    return (jnp.dot(a[0], b[0], preferred_element_type=F32) + jnp.dot(a[0], b[1], preferred_element_type=F32)
            + jnp.dot(a[1], b[0], preferred_element_type=F32))


def _sigmoid(x):
    return 1.0 / (1.0 + jnp.exp(-x))


def _silu(x):
    return x * _sigmoid(x)


def _softplus(x):
    return jnp.maximum(x, 0.0) + jnp.log1p(jnp.exp(-jnp.abs(x)))


def _log_sigmoid(x):
    return jnp.minimum(x, 0.0) - jnp.log1p(jnp.exp(-jnp.abs(x)))


def _layer_norm(z, g, b):
    mu = jnp.mean(z, axis=-1, keepdims=True)
    d = z - mu
    var = jnp.mean(d * d, axis=-1, keepdims=True)
    return d * lax.rsqrt(var + LN_EPS) * g + b


def _iota2(shape, axis):
    return lax.broadcasted_iota(jnp.int32, shape, axis)


def _proj_kernel(x_ref, w_ref, o_ref):
    o_ref[...] = jnp.dot(x_ref[...].astype(BF16), w_ref[...], preferred_element_type=F32)


def _proj(x, w, tm):
    t, k = x.shape
    n = w.shape[1]
    return pl.pallas_call(
        _proj_kernel,
        grid=(t // tm,),
        in_specs=[pl.BlockSpec((tm, k), lambda i: (i, 0)), pl.BlockSpec((k, n), lambda i: (0, 0))],
        out_specs=pl.BlockSpec((tm, n), lambda i: (i, 0)),
        out_shape=jax.ShapeDtypeStruct((t, n), F32),
        compiler_params=_cp(("parallel",)),
        name="proj_a",
    )(x, w)


def _fox_proj_kernel(x_ref, w_ref, bf_ref, k_ref, v_ref, lf_ref, lft_ref, *q_refs, attn_copies):
    p = jnp.dot(x_ref[...].astype(BF16), w_ref[...], preferred_element_type=F32)
    q = p[:, 0:D]
    k = p[:, D:2 * D]
    v = p[:, 2 * D:3 * D]
    k_ref[...] = k
    v_ref[...] = v
    lf = _log_sigmoid(p[:, 3 * D:3 * D + LANES] + bf_ref[...])
    lf_ref[...] = lf[:, 0:FH]
    lft_ref[...] = lf.T[0:FH, :]
    if attn_copies:
        qt_ref, kb_ref, vt_ref = q_refs
        kb_ref[...] = k.astype(BF16)
        qs = q * (FHD ** -0.5 * LOG2E)
        for h in range(FH):
            qt_ref[0, h] = qs[:, h * FHD:(h + 1) * FHD].T.astype(BF16)
            vt_ref[0, h] = v[:, h * FHD:(h + 1) * FHD].T.astype(BF16)
    else:
        q_refs[0][...] = q


def _fox_proj(x, w, bf_row, tm, attn_copies):
    t = x.shape[0]
    row = lambda i: (i, 0)
    out_specs = [pl.BlockSpec((tm, D), row), pl.BlockSpec((tm, D), row),
                 pl.BlockSpec((tm, FH), row), pl.BlockSpec((FH, tm), lambda i: (0, i))]
    out_shape = [jax.ShapeDtypeStruct((t, D), F32), jax.ShapeDtypeStruct((t, D), F32),
                 jax.ShapeDtypeStruct((t, FH), F32), jax.ShapeDtypeStruct((FH, t), F32)]
    if attn_copies:
        tile = pl.BlockSpec((1, FH, FHD, tm), lambda i: (i, 0, 0, 0))
        tiles = jax.ShapeDtypeStruct((t // tm, FH, FHD, tm), BF16)
        out_specs += [tile, pl.BlockSpec((tm, D), row), tile]
        out_shape += [tiles, jax.ShapeDtypeStruct((t, D), BF16), tiles]
    else:
        out_specs += [pl.BlockSpec((tm, D), row)]
        out_shape += [jax.ShapeDtypeStruct((t, D), F32)]
    return pl.pallas_call(
        functools.partial(_fox_proj_kernel, attn_copies=attn_copies),
        grid=(t // tm,),
        in_specs=[pl.BlockSpec((tm, D), row), pl.BlockSpec((D, C_COLS_PAD), lambda i: (0, 0)),
                  pl.BlockSpec((1, LANES), lambda i: (0, 0))],
        out_specs=out_specs,
        out_shape=out_shape,
        compiler_params=_cp(("parallel",)),
        name="proj_c",
    )(x, w, bf_row)


def _outproj_ln_kernel(a_ref, x_ref, w_ref, g_ref, b_ref, o_ref):
    h = jnp.dot(a_ref[...].astype(BF16), w_ref[...], preferred_element_type=F32)
    o_ref[...] = _layer_norm(ALPHA * x_ref[...] + h, g_ref[...], b_ref[...])


def _outproj_ln(a, x, w, g, b, tm):
    t = x.shape[0]
    row = lambda i: (i, 0)
    fixed = lambda i: (0, 0)
    return pl.pallas_call(
        _outproj_ln_kernel,
        grid=(t // tm,),
        in_specs=[pl.BlockSpec((tm, D), row), pl.BlockSpec((tm, D), row), pl.BlockSpec((D, D), fixed),
                  pl.BlockSpec((1, D), fixed), pl.BlockSpec((1, D), fixed)],
        out_specs=pl.BlockSpec((tm, D), row),
        out_shape=jax.ShapeDtypeStruct((t, D), F32),
        compiler_params=_cp(("parallel",)),
        name="outproj_ln",
    )(a, x, w, g, b)


def _unit_lower_inverses(a_list, ri, ci):
    eye = (ri == ci).astype(F32)
    blk = (ri >> 4) == (ci >> 4)
    d = [_split2(jnp.where(blk, a, 0.0)) for a in a_list]
    d2f = [_mm_s(x, x) for x in d]
    d2 = [_split2(x) for x in d2f]
    d4f = [_mm_s(x, x) for x in d2]
    d4 = [_split2(x) for x in d4f]
    d8 = [_split2(_mm_s(x, x)) for x in d4]
    xs = [eye - jnp.where(blk, a, 0.0) for a in a_list]
    for pw in (d2, d4, d8):
        xs = [x + _mm_s(_split2(x), p) for x, p in zip(xs, pw)]
    for s in (5, 6, 7):
        msk = ((ri >> s) == (ci >> s)) & ((ri >> (s - 1)) != (ci >> (s - 1)))
        es = [_split2(jnp.where(msk, a, 0.0)) for a in a_list]
        xsp = [_split2(x) for x in xs]
        xe = [_split2(_mm_s(x, e)) for x, e in zip(xsp, es)]
        xs = [x - _mm_s(y, xp) for x, y, xp in zip(xs, xe, xsp)]
    return xs


def _head_gate_norm(out, g_row, gate):
    return out * lax.rsqrt(jnp.mean(out * out, axis=-1, keepdims=True) + NORM_EPS) * g_row * _silu(gate)


def _mixer_a_prompt_kernel(p_ref, cos_ref, sin_ref, cw_ref, alog_ref, dt_ref, rg_ref, gg_ref,
                           mix_ref, ret_ref, gdn_ref, conv_ref, ext_ref, *, nb):
    n = pl.program_id(0)
    c = CHUNK

    @pl.when(n == 0)
    def _init():
        ret_ref[...] = jnp.zeros_like(ret_ref)
        gdn_ref[...] = jnp.zeros_like(gdn_ref)
        ext_ref[:, 0:8, :] = jnp.zeros((nb, 8, CONV_CH), F32)

    ri = _iota2((c, c), 0)
    ci = _iota2((c, c), 1)
    rf = ri.astype(F32)
    diff = rf - ci.astype(F32)
    cosv = cos_ref[...]
    sinv = sin_ref[...]

    lgs = [math.log1p(-(2.0 ** (-5.0 - h))) for h in range(RH)]
    decay = [jnp.where(diff >= 0, jnp.exp(jnp.maximum(diff, 0.0) * lg), 0.0) for lg in lgs]
    inner = [jnp.exp((rf + 1.0) * lg) for lg in lgs]
    kdec = [jnp.exp((c - 1.0 - rf) * lg) for lg in lgs]
    r_units = [(b, h) for b in range(nb) for h in range(RH)]
    r_q, r_k = [], []
    for b, h in r_units:
        rq = p_ref[b, :, h * RDK:(h + 1) * RDK]
        rk = p_ref[b, :, RH * RDK + h * RDK:RH * RDK + (h + 1) * RDK]
        r_q.append(rq * cosv + pltpu.roll(rq, RDK // 2, 1) * sinv)
        r_k.append((rk * cosv + pltpu.roll(rk, RDK // 2, 1) * sinv) * (RDK ** -0.5))
    r_v = [p_ref[b, :, 2 * RH * RDK + h * RDV:2 * RH * RDK + (h + 1) * RDV] for b, h in r_units]
    r_s = [ret_ref[b, h] for b, h in r_units]
    r_scores = [_mm_nt(q, k) * decay[h] for q, k, (_, h) in zip(r_q, r_k, r_units)]
    r_state_out = [_mm(q * inner[h], s) for q, s, (_, h) in zip(r_q, r_s, r_units)]
    r_new = [_mm_tn(k * kdec[h], v) for k, v, (_, h) in zip(r_k, r_v, r_units)]
    r_out = [_mm(sc, v) + so for sc, v, so in zip(r_scores, r_v, r_state_out)]
    for (b, h), s, new, out in zip(r_units, r_s, r_new, r_out):
        ret_ref[b, h] = math.exp(c * lgs[h]) * s + new
        rgate = p_ref[b, :, 2 * RH * RDK + RH * RDV + h * RDV:2 * RH * RDK + RH * RDV + (h + 1) * RDV]
        mix_ref[b, :, h * RDV:(h + 1) * RDV] = _head_gate_norm(out, rg_ref[...], rgate)

    g_off = 2 * RH * RDK + 2 * RH * RDV
    z_off = g_off + CONV_CH
    incl = ri >= ci
    strict = ri > ci
    tril = incl.astype(BF16)
    units = []
    for b in range(nb):
        ext_ref[b, 8:8 + c, :] = p_ref[b, :, g_off:g_off + CONV_CH]
        y = ext_ref[b, 5:5 + c, :] * cw_ref[0:1, :]
        for i in range(1, CONV_W):
            y = y + ext_ref[b, 5 + i:5 + i + c, :] * cw_ref[i:i + 1, :]
        conv_ref[b] = ext_ref[b, c + 5:c + 8, :]
        ext_ref[b, 0:8, :] = ext_ref[b, c:c + 8, :]
        y = _silu(y)
        tail = p_ref[b, :, z_off + GH * GDV:z_off + GH * GDV + LANES]
        g_all = -jnp.exp(alog_ref[...]) * _softplus(tail + dt_ref[...])
        beta_all = _sigmoid(tail)
        for h in range(GH):
            qh = y[:, h * GDK:(h + 1) * GDK]
            kh = y[:, GH * GDK + h * GDK:GH * GDK + (h + 1) * GDK]
            vh = y[:, 2 * GH * GDK + h * GDV:2 * GH * GDK + (h + 1) * GDV]
            qh = qh * lax.rsqrt(jnp.sum(qh * qh, axis=-1, keepdims=True) + NORM_EPS) * (GDK ** -0.5)
            kh = kh * lax.rsqrt(jnp.sum(kh * kh, axis=-1, keepdims=True) + NORM_EPS)
            gb = jnp.broadcast_to(g_all[:, h:h + 1], (c, c))
            bcol = beta_all[:, GH + h:GH + h + 1]
            units.append((b, h, qh, kh, vh, gb, bcol))

    gcs = [_mm_01_left(tril, u[5]) for u in units]
    decs = []
    for gc in gcs:
        dmat = gc - gc.T
        decs.append(jnp.where(incl, jnp.exp(jnp.where(incl, dmat, 0.0)), 0.0))
    kbs = [u[3] * u[6] for u in units]
    a_list = [jnp.where(strict, _mm_nt(kb, u[3]) * dec, 0.0) for kb, u, dec in zip(kbs, units, decs)]
    ts = [t.astype(BF16) for t in _unit_lower_inverses(a_list, ri, ci)]
    egcs = [jnp.exp(gc) for gc in gcs]
    us = [jnp.dot(t, (u[4] * u[6]).astype(BF16), preferred_element_type=F32) for t, u in zip(ts, units)]
    ws = [jnp.dot(t, (kb * egc).astype(BF16), preferred_element_type=F32) for t, kb, egc in zip(ts, kbs, egcs)]
    attns = [_mm_nt(u[2], u[3]) * dec for u, dec in zip(units, decs)]
    ss = [gdn_ref[u[0], u[1]] for u in units]
    v_news = [uu - _mm(w, s) for uu, w, s in zip(us, ws, ss)]
    outs = [_mm(u[2] * egc, s) + _mm(attn, vn) for u, egc, s, attn, vn in zip(units, egcs, ss, attns, v_news)]
    for u, gc, s, vn, out in zip(units, gcs, ss, v_news, outs):
        b, h = u[0], u[1]
        g_last = gc[c - 1:c, :]
        gdn_ref[b, h] = jnp.exp(g_last) * s + _mm_tn(u[3] * jnp.exp(g_last - gc), vn)
        gz = p_ref[b, :, z_off + h * GDV:z_off + (h + 1) * GDV]
        mix_ref[b, :, RH * RDV + h * GDV:RH * RDV + (h + 1) * GDV] = _head_gate_norm(out, gg_ref[...], gz)


def _mixer_a_prompt(proj, cos2, sin2, conv_w, alog_row, dt_row, rg_row, gg_row):
    b, l, _ = proj.shape
    fixed = lambda n: (0, 0)
    return pl.pallas_call(
        functools.partial(_mixer_a_prompt_kernel, nb=b),
        grid=(l // CHUNK,),
        in_specs=[pl.BlockSpec((b, CHUNK, A_COLS_PAD), lambda n: (0, n, 0)),
                  pl.BlockSpec((CHUNK, LANES), lambda n: (n, 0)),
                  pl.BlockSpec((CHUNK, LANES), lambda n: (n, 0)),
                  pl.BlockSpec((CONV_W, CONV_CH), fixed),
                  pl.BlockSpec((1, LANES), fixed), pl.BlockSpec((1, LANES), fixed),
                  pl.BlockSpec((1, LANES), fixed), pl.BlockSpec((1, LANES), fixed)],
        out_specs=[pl.BlockSpec((b, CHUNK, D), lambda n: (0, n, 0)),
                   pl.BlockSpec((b, RH, RDK, RDV), lambda n: (0, 0, 0, 0)),
                   pl.BlockSpec((b, GH, GDK, GDV), lambda n: (0, 0, 0, 0)),
                   pl.BlockSpec((b, CONV_W - 1, CONV_CH), lambda n: (0, 0, 0))],
        out_shape=[jax.ShapeDtypeStruct((b, l, D), F32),
                   jax.ShapeDtypeStruct((b, RH, RDK, RDV), F32),
                   jax.ShapeDtypeStruct((b, GH, GDK, GDV), F32),
                   jax.ShapeDtypeStruct((b, CONV_W - 1, CONV_CH), F32)],
        scratch_shapes=[pltpu.VMEM((b, CHUNK + 8, CONV_CH), F32)],
        compiler_params=_cp(("arbitrary",)),
        name="mixer_a_prompt",
    )(proj, cos2, sin2, conv_w, alog_row, dt_row, rg_row, gg_row)


def _mixer_a_sample_kernel(p_ref, cos_ref, sin_ref, cw_ref, alog_ref, dt_ref, rg_ref, gg_ref,
                           rs_ref, gs_ref, cs_ref, mix_ref, ret_ref, gdn_ref, conv_ref):
    eye = _iota2((LANES, LANES), 0) == _iota2((LANES, LANES), 1)

    def col(v):
        return jnp.sum(jnp.where(eye, jnp.broadcast_to(v, (LANES, LANES)), 0.0), axis=1, keepdims=True)

    def vec_mat(c, s):
        return jnp.sum(c * s, axis=0, keepdims=True)

    cosv = cos_ref[...]
    sinv = sin_ref[...]
    rq = [p_ref[0, :, h * RDK:(h + 1) * RDK] for h in range(RH)]
    rk = [p_ref[0, :, RH * RDK + h * RDK:RH * RDK + (h + 1) * RDK] for h in range(RH)]
    rv = [p_ref[0, :, 2 * RH * RDK + h * RDV:2 * RH * RDK + (h + 1) * RDV] for h in range(RH)]
    rq = [x * cosv + pltpu.roll(x, RDK // 2, 1) * sinv for x in rq]
    rk = [(x * cosv + pltpu.roll(x, RDK // 2, 1) * sinv) * (RDK ** -0.5) for x in rk]

    g_off = 2 * RH * RDK + 2 * RH * RDV
    gq_all = p_ref[0, :, g_off:g_off + CONV_CH]
    cb = cs_ref[0]
    y = cb[0:1, :] * cw_ref[0:1, :] + cb[1:2, :] * cw_ref[1:2, :] + cb[2:3, :] * cw_ref[2:3, :] \
        + gq_all * cw_ref[3:4, :]
    conv_ref[0, 0:2, :] = cb[1:3, :]
    conv_ref[0, 2:3, :] = gq_all
    y = _silu(y)
    z_off = g_off + CONV_CH
    tail = p_ref[0, :, z_off + GH * GDV:z_off + GH * GDV + LANES]
    g_all = -jnp.exp(alog_ref[...]) * _softplus(tail + dt_ref[...])
    beta_all = _sigmoid(tail)
    gq = [y[:, h * GDK:(h + 1) * GDK] for h in range(GH)]
    gk = [y[:, GH * GDK + h * GDK:GH * GDK + (h + 1) * GDK] for h in range(GH)]
    gv = [y[:, 2 * GH * GDK + h * GDV:2 * GH * GDK + (h + 1) * GDV] for h in range(GH)]
    gq = [x * lax.rsqrt(jnp.sum(x * x, axis=-1, keepdims=True) + NORM_EPS) * (GDK ** -0.5) for x in gq]
    gk = [x * lax.rsqrt(jnp.sum(x * x, axis=-1, keepdims=True) + NORM_EPS) for x in gk]
    eg = [jnp.exp(g_all[:, h:h + 1]) for h in range(GH)]
    beta = [beta_all[:, GH + h:GH + h + 1] for h in range(GH)]

    q_cols = [col(x) for x in rq + gq]
    k_cols = [col(x) for x in rk + gk]
    r_state = [rs_ref[0, h] for h in range(RH)]
    g_state = [gs_ref[0, h] for h in range(GH)]
    qs = [vec_mat(c, s) for c, s in zip(q_cols, r_state + g_state)]
    ws = [vec_mat(k_cols[RH + h] * (beta[h] * eg[h]), g_state[h]) for h in range(GH)]
    r_qk = [jnp.sum(a * b, axis=-1, keepdims=True) for a, b in zip(rq, rk)]
    g_qk = [jnp.sum(a * b, axis=-1, keepdims=True) for a, b in zip(gq, gk)]
    for h in range(RH):
        gamma = 1.0 - 2.0 ** (-5.0 - h)
        out = r_qk[h] * rv[h] + gamma * qs[h]
        ret_ref[0, h] = gamma * r_state[h] + k_cols[h] * rv[h]
        rgate = p_ref[0, :, 2 * RH * RDK + RH * RDV + h * RDV:2 * RH * RDK + RH * RDV + (h + 1) * RDV]
        mix_ref[0, :, h * RDV:(h + 1) * RDV] = _head_gate_norm(out, rg_ref[...], rgate)
    for h in range(GH):
        v_new = gv[h] * beta[h] - ws[h]
        out = eg[h] * qs[RH + h] + g_qk[h] * v_new
        gdn_ref[0, h] = eg[h] * g_state[h] + k_cols[RH + h] * v_new
        gz = p_ref[0, :, z_off + h * GDV:z_off + (h + 1) * GDV]
        mix_ref[0, :, RH * RDV + h * GDV:RH * RDV + (h + 1) * GDV] = _head_gate_norm(out, gg_ref[...], gz)


def _mixer_a_sample(proj, cos2, sin2, conv_w, alog_row, dt_row, rg_row, gg_row, ret_s, gdn_s, conv_s):
    b = proj.shape[0]
    fixed = lambda i: (0, 0)
    st = lambda i: (i, 0, 0, 0)
    return pl.pallas_call(
        _mixer_a_sample_kernel,
        grid=(b,),
        in_specs=[pl.BlockSpec((1, 1, A_COLS_PAD), lambda i: (i, 0, 0)),
                  pl.BlockSpec((1, LANES), fixed), pl.BlockSpec((1, LANES), fixed),
                  pl.BlockSpec((CONV_W, CONV_CH), fixed),
                  pl.BlockSpec((1, LANES), fixed), pl.BlockSpec((1, LANES), fixed),
                  pl.BlockSpec((1, LANES), fixed), pl.BlockSpec((1, LANES), fixed),
                  pl.BlockSpec((1, RH, RDK, RDV), st), pl.BlockSpec((1, GH, GDK, GDV), st),
                  pl.BlockSpec((1, CONV_W - 1, CONV_CH), lambda i: (i, 0, 0))],
        out_specs=[pl.BlockSpec((1, 1, D), lambda i: (i, 0, 0)),
                   pl.BlockSpec((1, RH, RDK, RDV), st), pl.BlockSpec((1, GH, GDK, GDV), st),
                   pl.BlockSpec((1, CONV_W - 1, CONV_CH), lambda i: (i, 0, 0))],
        out_shape=[jax.ShapeDtypeStruct((b, 1, D), F32),
                   jax.ShapeDtypeStruct((b, RH, RDK, RDV), F32),
                   jax.ShapeDtypeStruct((b, GH, GDK, GDV), F32),
                   jax.ShapeDtypeStruct((b, CONV_W - 1, CONV_CH), F32)],
        compiler_params=_cp(("parallel",)),
        name="mixer_a_sample",
    )(proj, cos2, sin2, conv_w, alog_row, dt_row, rg_row, gg_row, ret_s, gdn_s, conv_s)


def _fox_cumsum_kernel(x_ref, o_ref):
    n = x_ref.shape[1] // LANES
    upper = (_iota2((LANES, LANES), 0) <= _iota2((LANES, LANES), 1)).astype(BF16)
    carry = jnp.zeros((FH, 1), F32)
    for c in range(n):
        p = _mm_01_right(x_ref[:, c * LANES:(c + 1) * LANES], upper) + carry
        o_ref[0, :, c * LANES:(c + 1) * LANES] = p
        carry = p[:, LANES - 1:LANES]


def _fox_cumsum(lft, b, l):
    return pl.pallas_call(
        _fox_cumsum_kernel,
        grid=(b,),
        in_specs=[pl.BlockSpec((FH, l), lambda i: (0, i))],
        out_specs=pl.BlockSpec((1, FH, l), lambda i: (i, 0, 0)),
        out_shape=jax.ShapeDtypeStruct((b, FH, l), F32),
        compiler_params=_cp(("parallel",)),
        name="fox_cumsum",
    )(lft)


def _fox_flash_kernel(q_ref, k_ref, v_ref, ck_ref, o_ref, ka_ref, sa_ref, sb_ref, *, tq, tk):
    assert tq == 2 * tk and tk % LANES == 0
    qi = pl.program_id(2)
    nk = k_ref.shape[1] // tk

    @pl.when(qi == 0)
    def _augment_keys():
        eye = _iota2((LANES, LANES), 0) == _iota2((LANES, LANES), 1)
        lane = _iota2((LANES, LANES), 1)
        for j in range(nk):
            ka_ref[j * tk:(j + 1) * tk, 0:FHD] = k_ref[0, j * tk:(j + 1) * tk, :]
            for c in range(tk // LANES):
                row = jnp.broadcast_to(ck_ref[0, j:j + 1, c * LANES:(c + 1) * LANES], (LANES, LANES))
                col = jnp.sum(jnp.where(eye, row, 0.0), axis=1, keepdims=True) * LOG2E
                c1, c2, c3 = (t.astype(F32) for t in _split3(col))
                blk = jnp.where(lane == 0, c1, jnp.where(lane == 1, c2, jnp.where(lane == 2, c3, 0.0)))
                r0 = j * tk + c * LANES
                ka_ref[r0:r0 + LANES, FHD:2 * FHD] = blk.astype(BF16)

    sub = _iota2((FHD, tq), 0)
    minus_ones = jnp.where(sub < 3, -1.0, 0.0).astype(BF16)
    qa = jnp.concatenate([q_ref[0, 0], q_ref[1, 0]], axis=1)
    qa = jnp.concatenate([qa, minus_ones], axis=0)

    def scores(j, s_ref):
        ka = ka_ref[pl.ds(pl.multiple_of(j * tk, tk), tk), :]
        s_ref[...] = jnp.dot(ka, qa, preferred_element_type=F32)

    ones_rows = jnp.ones((16, tk), BF16)

    def update(j, s_ref, carry, masked):
        m_old, acc = carry
        s = s_ref[...]
        if masked:
            kpos = j * tk + _iota2((tk, tq), 0)
            qpos = qi * tq + _iota2((tk, tq), 1)
            s = jnp.where(kpos <= qpos, s, -jnp.inf)
        m_new = jnp.maximum(m_old, jnp.max(s, axis=0, keepdims=True))
        alpha = jnp.exp2(m_old - m_new)
        p = jnp.exp2(s - m_new).astype(BF16)
        v_aug = jnp.concatenate([v_ref[j, 0], ones_rows], axis=0)
        return m_new, alpha * acc + jnp.dot(v_aug, p, preferred_element_type=F32)

    def two_blocks(i, carry):
        scores(2 * i + 1, sb_ref)
        carry = update(2 * i, sa_ref, carry, False)
        scores(2 * i + 2, sa_ref)
        return update(2 * i + 1, sb_ref, carry, False)

    scores(0, sa_ref)
    init = (jnp.full((1, tq), -jnp.inf, F32), jnp.zeros((FHD + 16, tq), F32))
    carry = lax.fori_loop(0, qi, two_blocks, init)
    scores(2 * qi + 1, sb_ref)
    carry = update(2 * qi, sa_ref, carry, True)
    carry = update(2 * qi + 1, sb_ref, carry, True)
    acc = carry[1]
    o_ref[0] = (acc[0:FHD] / acc[FHD:FHD + 1]).T


def _fox_flash(qt, kb, vt, cum, tq, tk):
    b, l, _ = kb.shape
    nk = l // tk
    return pl.pallas_call(
        functools.partial(_fox_flash_kernel, tq=tq, tk=tk),
        grid=(b, FH, l // tq),
        in_specs=[pl.BlockSpec((tq // tk, 1, FHD, tk), lambda i, h, qi: (i * (l // tq) + qi, h, 0, 0)),
                  pl.BlockSpec((1, l, FHD), lambda i, h, qi: (i, 0, h)),
                  pl.BlockSpec((nk, 1, FHD, tk), lambda i, h, qi: (i, h, 0, 0)),
                  pl.BlockSpec((1, nk, tk), lambda i, h, qi: (i * FH + h, 0, 0))],
        out_specs=pl.BlockSpec((1, tq, FHD), lambda i, h, qi: (i, qi, h)),
        out_shape=jax.ShapeDtypeStruct((b, l, D), F32),
        scratch_shapes=[pltpu.VMEM((l, 2 * FHD), BF16), pltpu.VMEM((tk, tq), F32), pltpu.VMEM((tk, tq), F32)],
        compiler_params=_cp(("arbitrary", "arbitrary", "arbitrary")),
        name="fox_flash",
    )(qt, kb, vt, cum)


def _fox_sample_kernel(pt_ref, q_ref, kn_ref, vn_ref, lfn_ref, *rest):
    npp = PAGES_PER_STEP
    k_refs, v_refs, lf_refs = rest[0:npp], rest[npp:2 * npp], rest[2 * npp:3 * npp]
    o_ref, m_ref, l_ref, acc_ref, carry_ref = rest[3 * npp:]
    j = pl.program_id(1)
    scale = FHD ** -0.5
    rows = PAGE * FH

    @pl.when(j == 0)
    def _init():
        m_ref[...] = jnp.full(m_ref.shape, -jnp.inf, F32)
        l_ref[...] = jnp.zeros(l_ref.shape, F32)
        acc_ref[...] = jnp.zeros(acc_ref.shape, F32)
        carry_ref[...] = jnp.zeros(carry_ref.shape, F32)

    q = q_ref[0]
    qb = (q * scale).astype(BF16)
    own = (_iota2((FH, rows), 1) & (FH - 1)) == _iota2((FH, rows), 0)
    nch = rows // LANES
    ci = _iota2((LANES, 2 * LANES), 1)
    pre_tot = ((_iota2((LANES, 2 * LANES), 0) <= ci) | (ci >= LANES)).astype(BF16)
    stacked, s_pages = [], []
    for pg in range(npp):
        kp = k_refs[pg][0].reshape(rows, FHD).astype(BF16)
        s_pages.append(lax.dot_general(qb, kp, NT, preferred_element_type=F32))
        lf = jnp.where(own, jnp.broadcast_to(lf_refs[pg][0], (FH, rows)), 0.0)
        stacked += [lf[:, c * LANES:(c + 1) * LANES] for c in range(nch)]
    pt = _mm_01_right(jnp.concatenate(stacked, axis=0), pre_tot)
    carry = carry_ref[...]
    logits = []
    for pg in range(npp):
        cums = []
        for c in range(nch):
            r0 = (pg * nch + c) * FH
            cums.append(pt[r0:r0 + FH, 0:LANES] + carry)
            carry = carry + pt[r0:r0 + FH, LANES:2 * LANES]
        logits.append(jnp.where(own, s_pages[pg] - jnp.concatenate(cums, axis=1), -jnp.inf))
    carry_ref[...] = carry
    m_old = m_ref[...]
    m_new = m_old
    for lg in logits:
        m_new = jnp.maximum(m_new, jnp.max(lg, axis=1, keepdims=True))
    alpha = jnp.exp(m_old - m_new)
    l_new = alpha * l_ref[...]
    acc = alpha * acc_ref[...]
    for pg in range(npp):
        p = jnp.exp(logits[pg] - m_new)
        l_new = l_new + jnp.sum(p, axis=1, keepdims=True)
        vp = v_refs[pg][0].reshape(rows, FHD).astype(BF16)
        acc = acc + jnp.dot(p.astype(BF16), vp, preferred_element_type=F32)
    m_ref[...] = m_new
    l_ref[...] = l_new
    acc_ref[...] = acc

    @pl.when(j == pl.num_programs(1) - 1)
    def _fin():
        s_new = jnp.sum(q * kn_ref[0], axis=1, keepdims=True) * scale - (carry[:, 0:1] + lfn_ref[0])
        m2 = jnp.maximum(m_new, s_new)
        a2 = jnp.exp(m_new - m2)
        p2 = jnp.exp(s_new - m2)
        o_ref[0] = (a2 * acc + p2 * vn_ref[0]) / (a2 * l_new + p2)


def _fox_sample(page_table, q3, k3, v3, lf3, k_pool, v_pool, lf_pool):
    b = q3.shape[0]
    npg = page_table.shape[1]
    npp = PAGES_PER_STEP
    assert npg % npp == 0
    pt = page_table.reshape(-1)
    tok = lambda i, j, pt_ref: (i, 0, 0)

    def page(r, nd):
        return lambda i, j, pt_ref: (pt_ref[i * npg + j * npp + r],) + (0,) * nd

    grid_spec = pltpu.PrefetchScalarGridSpec(
        num_scalar_prefetch=1,
        grid=(b, npg // npp),
        in_specs=[pl.BlockSpec((1, FH, FHD), tok), pl.BlockSpec((1, FH, FHD), tok), pl.BlockSpec((1, FH, FHD), tok),
                  pl.BlockSpec((1, FH, 1), tok)]
        + [pl.BlockSpec((1, PAGE, FH, FHD), page(r, 3)) for r in range(npp)]
        + [pl.BlockSpec((1, PAGE, FH, FHD), page(r, 3)) for r in range(npp)]
        + [pl.BlockSpec((1, 1, PAGE * FH), page(r, 2)) for r in range(npp)],
        out_specs=pl.BlockSpec((1, FH, FHD), tok),
        scratch_shapes=[pltpu.VMEM((FH, 1), F32), pltpu.VMEM((FH, 1), F32), pltpu.VMEM((FH, FHD), F32),
                        pltpu.VMEM((FH, LANES), F32)],
    )
    lf_flat =lf_pool.reshape(lf_pool.shape[0], 1, PAGE * FH)
    return pl.pallas_call(
        _fox_sample_kernel,
        grid_spec=grid_spec,
        out_shape=jax.ShapeDtypeStruct((b, FH, FHD), F32),
        compiler_params=_cp(("parallel", "arbitrary")),
        name="fox_sample",
    )(pt, q3, k3, v3, lf3, *([k_pool] * npp), *([v_pool] * npp), *([lf_flat] * npp))


_CAND_BLOCKS = [(a, 16 if a == 0 else 8, 16 // (a + 1)) for a in range(8)]


_MARK0 = -(2.0 ** 127)
_MARK_STEP = 2.0 ** 120
_MARK_LIMIT = -(2.0 ** 126)


def _top16(s, iota):
    rank = jnp.full(s.shape, float(PTOPK), F32)
    vals = jnp.zeros((PTOPK, s.shape[1]), F32)
    i16 = _iota2((PTOPK, s.shape[1]), 0)
    for r in range(PTOPK):
        m = jnp.max(s, axis=0, keepdims=True)
        sel = iota == jnp.min(jnp.where(s == m, iota, 1e9), axis=0, keepdims=True)
        rank = jnp.where(sel, float(r), rank)
        s = jnp.where(sel, -jnp.inf, s)
        vals = jnp.where(i16 == r, m, vals)
    return rank, vals


def _top16_fast(s):
    lowest = jnp.min(s)
    vals = jnp.zeros((PTOPK, s.shape[1]), F32)
    i16 = _iota2((PTOPK, s.shape[1]), 0)
    for r in range(PTOPK):
        m = jnp.max(s, axis=0, keepdims=True)
        s = jnp.where(s == m, _MARK0 + r * _MARK_STEP, s)
        vals = jnp.where(i16 == r, m, vals)
    marked = s < _MARK_LIMIT
    rank = jnp.where(marked, (s - _MARK0) * (1.0 / _MARK_STEP), float(PTOPK))
    taken = jnp.sum(jnp.where(marked, 1.0, 0.0), axis=0, keepdims=True)
    return rank, vals, _not_16(taken) | (lowest <= _MARK_LIMIT)


def _pick16(cand, cidx, exact):
    picked = jnp.zeros(cand.shape, F32)
    for _ in range(PTOPK):
        m = jnp.max(cand, axis=0, keepdims=True)
        sel = cand == m
        if exact:
            sel = cidx == jnp.min(jnp.where(sel, cidx, 1e9), axis=0, keepdims=True)
        picked = jnp.where(sel, 1.0, picked)
        cand = jnp.where(sel, -jnp.inf, cand)
    return picked


def _not_16(count):
    return jnp.max(jnp.abs(count - float(PTOPK))) > 0.5


def _route_head(s_t):
    s1 = s_t[0:NKEYS]
    s2 = s_t[NKEYS:2 * NKEYS]
    w = s_t.shape[1]
    iota = _iota2((NKEYS, w), 0).astype(F32)
    rank1, v1, redo1 = _top16_fast(s1)
    rank2, v2, redo2 = _top16_fast(s2)
    rank1, v1, rank2, v2 = lax.cond(redo1 | redo2, lambda: _top16(s1, iota) + _top16(s2, iota),
                                    lambda: (rank1, v1, rank2, v2))
    e1 = jnp.exp(v1 - v1[0:1])
    e2 = jnp.exp(v2 - v2[0:1])
    i8 = _iota2((8, w), 0)
    i8f = i8.astype(F32)
    i16f = _iota2((16, w), 0).astype(F32)
    cand, cidx, cprob = [], [], []
    for a, rows, valid in _CAND_BLOCKS:
        c = v1[a:a + 1] + v2[0:rows]
        pr = e1[a:a + 1] * e2[0:rows]
        if rows == 16:
            ix = i16f
        else:
            ix = i8f + float(a * PTOPK)
            c = jnp.where(i8 < valid, c, -jnp.inf)
        cand.append(c)
        cidx.append(ix)
        cprob.append(pr)
    cand.append(v1[8:16] + v2[0:1])
    cidx.append((i8f + 8.0) * float(PTOPK))
    cprob.append(e1[8:16] * e2[0:1])
    cand = jnp.concatenate(cand, axis=0)
    cidx = jnp.concatenate(cidx, axis=0)
    cprob = jnp.concatenate(cprob, axis=0)
    picked = _pick16(cand, cidx, False)
    picked = lax.cond(_not_16(jnp.sum(picked, axis=0, keepdims=True)),
                      lambda: _pick16(cand, cidx, True), lambda: picked)
    z = jnp.sum(picked * cprob, axis=0, keepdims=True)
    n_low = jnp.zeros((8, w), F32)
    off = 0
    for a, rows, _ in _CAND_BLOCKS:
        cnt = jnp.sum(picked[off:off + rows], axis=0, keepdims=True)
        n_low = jnp.where(i8 == a, cnt, n_low)
        off += rows
    n16 = jnp.concatenate([n_low, picked[off:off + 8]], axis=0)
    n1d = jnp.zeros((NKEYS, w), F32)
    for a in range(PTOPK):
        n1d = jnp.where(rank1 == float(a), n16[a:a + 1], n1d)
    e1d = jnp.exp(s1 - v1[0:1]) * (GELU_GATE_SCALE / z)
    e2d = jnp.exp(s2 - v2[0:1])
    return rank2, e2d, n1d, e1d


def _peer_kernel(x_ref, wq_ref, keys_ref, u0_ref, un_ref, vt_ref, g_ref, b_ref, o_ref,
                 xb_ref, q_ref, r2_ref, e2_ref, n1_ref, e1_ref, ha_ref, hb_ref, at_ref, acc_ref, *, tm, te, rw):
    j = pl.program_id(1)
    nlg = tm // LANES
    nsub = te // NKEYS

    @pl.when(j == 0)
    def _route():
        xb = x_ref[...].astype(BF16)
        xb_ref[...] = xb
        q = jnp.dot(xb, wq_ref[...], preferred_element_type=F32)
        for h in range(PH):
            q_ref[h] = q[:, h * LANES:(h + 1) * LANES].astype(BF16)
        acc_ref[...] = jnp.zeros(acc_ref.shape, F32)
        ha_ref[:, 0:tm] = lax.dot_general(u0_ref[...], xb, NT, preferred_element_type=F32)

        def body(h, carry):
            for w0 in range(0, tm, rw):
                s_t = lax.dot_general(keys_ref[...], q_ref[h, w0:w0 + rw, :], NT,
                                      preferred_element_type=F32)
                r2, e2, n1, e1 = _route_head(s_t)
                for g in range(rw // LANES):
                    lg = slice(g * LANES, (g + 1) * LANES)
                    r2_ref[h, w0 // LANES + g] = r2[:, lg].astype(GATE_DT)
                    e2_ref[h, w0 // LANES + g] = e2[:, lg].astype(GATE_DT)
                    n1_ref[h, w0 // LANES + g] = n1[:, lg]
                    e1_ref[h, w0 // LANES + g] = e1[:, lg]
            return carry

        lax.fori_loop(0, PH, body, 0)

    def step(h_cur_ref, h_next_ref):
        h_next_ref[:, 0:tm] = lax.dot_general(un_ref[...], xb_ref[...], NT, preferred_element_type=F32)
        zero = jnp.zeros((), GATE_DT)
        pair = 2
        for pp in range(nsub // pair):
            for g in range(nlg):
                lanes = slice(g * LANES, (g + 1) * LANES)
                gates = [jnp.zeros((NKEYS, LANES), GATE_DT) for _ in range(pair)]
                for h in range(PH):
                    r2 = r2_ref[h, g]
                    e2 = e2_ref[h, g]
                    for ii in range(pair):
                        i1 = j * nsub + pp * pair + ii
                        n_row = n1_ref[h, g, pl.ds(i1, 1), :].astype(GATE_DT)
                        e_row = e1_ref[h, g, pl.ds(i1, 1), :].astype(GATE_DT)
                        gates[ii] = gates[ii] + jnp.where(r2 < n_row, e2, zero) * e_row
                for ii in range(pair):
                    r0 = (pp * pair + ii) * NKEYS
                    hb = h_cur_ref[r0:r0 + NKEYS, lanes]
                    act = hb * (1.0 + lax.erf(hb))
                    at_ref[r0:r0 + NKEYS, lanes] = (act.astype(GATE_DT) * gates[ii]).astype(BF16)
        acc_ref[...] += jnp.dot(vt_ref[0], at_ref[:, 0:tm], preferred_element_type=F32)

    @pl.when(j % 2 == 0)
    def _even():
        step(ha_ref, hb_ref)

    @pl.when(j % 2 == 1)
    def _odd():
        step(hb_ref, ha_ref)

    @pl.when(j == pl.num_programs(1) - 1)
    def _fin():
        o_ref[...] = _layer_norm(ALPHA * x_ref[...] + acc_ref[...].T, g_ref[...], b_ref[...])


def _peer(x, wq, keys, u, vt, g, b, tm, te):
    t = x.shape[0]
    nj = NEXP // te
    fixed = lambda i, j: (0, 0)
    row_pad = LANES if (tm // LANES) % 4 == 0 else 0
    return pl.pallas_call(
        functools.partial(_peer_kernel, tm=tm, te=te, rw=tm),
        grid=(t // tm, nj),
        in_specs=[pl.BlockSpec((tm, D), lambda i, j: (i, 0)),
                  pl.BlockSpec((D, PH * LANES), fixed),
                  pl.BlockSpec((2 * NKEYS, LANES), fixed),
                  pl.BlockSpec((te, D), fixed),
                  pl.BlockSpec((te, D), lambda i, j: (jnp.minimum(j + 1, nj - 1), 0)),
                  pl.BlockSpec((1, D, te), lambda i, j: (j, 0, 0)),
                  pl.BlockSpec((1, D), fixed), pl.BlockSpec((1, D), fixed)],
        out_specs=pl.BlockSpec((tm, D), lambda i, j: (i, 0)),
        out_shape=jax.ShapeDtypeStruct((t, D), F32),
        scratch_shapes=[pltpu.VMEM((tm, D), BF16),
                        pltpu.VMEM((PH, tm, LANES), BF16),
                        pltpu.VMEM((PH, tm // LANES, NKEYS, LANES), GATE_DT),
                        pltpu.VMEM((PH, tm // LANES, NKEYS, LANES), GATE_DT),
                        pltpu.VMEM((PH, tm // LANES, NKEYS, LANES), F32),
                        pltpu.VMEM((PH, tm // LANES, NKEYS, LANES), F32),
                        pltpu.VMEM((te, tm + row_pad), F32),
                        pltpu.VMEM((te, tm + row_pad), F32),
                        pltpu.VMEM((te, tm + row_pad), BF16),
                        pltpu.VMEM((D, tm), F32)],
        compiler_params=_cp(("parallel", "arbitrary")),
        name="peer",
    )(x, wq, keys, u, u, vt, g, b)


def _rope_tables(pos):
    half = RDK // 2
    inv = ROPE_BASE ** (-jnp.arange(half, dtype=F32) / half)
    ang = pos.astype(F32)[:, None] * inv[None, :]
    cos, sin = jnp.cos(ang), jnp.sin(ang)
    return jnp.concatenate([cos, cos], -1), jnp.concatenate([-sin, sin], -1)


def _pad_lanes(v, width=LANES, offset=0):
    return jnp.zeros((1, width), F32).at[0, offset:offset + v.shape[0]].set(v.astype(F32))


def _peer_params(wq, sub_keys, u_tab, v_tab, te):
    half = sub_keys.shape[-1]
    keys = jnp.zeros((2 * NKEYS, LANES), F32)
    keys = keys.at[0:NKEYS, 0:half].set(sub_keys[0]).at[NKEYS:, half:2 * half].set(sub_keys[1])
    vt = v_tab.astype(BF16).reshape(NEXP // te, te, D).transpose(0, 2, 1)
    return wq.astype(BF16), keys.astype(BF16), (u_tab * (2.0 ** -0.5)).astype(BF16), vt


def _tile(t, pref):
    return pref if t % pref == 0 else LANES


def kernel(x_prompt, x_sample, state_ret, state_gdn, state_gdn_conv, cache_fox_k, cache_fox_v, cache_fox_logf,
           page_table, w_in_a, ret_norm_g, gdn_a_log, gdn_dt_bias, gdn_conv_w, gdn_norm_g, w_out_a, w_in_c,
           fox_b_f, w_out_c, peer_wq, peer_sub_keys, peer_u, peer_v, ln_g, ln_b):
    bp, lp, _ = x_prompt.shape
    bs = x_sample.shape[0]
    tp = bp * lp
    xp = x_prompt.reshape(tp, D)
    xs = x_sample.reshape(bs, D)
    tmp, tms = _tile(tp, 512), _tile(bs, 512)
    te = 1024
    ln = lambda layer, k: (ln_g[layer, k].reshape(1, D), ln_b[layer, k].reshape(1, D))

    w_in = jnp.pad(w_in_a[0], ((0, 0), (0, A_COLS_PAD - A_COLS))).astype(BF16)
    cos_p, sin_p = _rope_tables(jnp.arange(lp))
    cos_s, sin_s = _rope_tables(PAST + jnp.arange(1))
    alog, dtb = _pad_lanes(gdn_a_log[0]), _pad_lanes(gdn_dt_bias[0])
    rg, gg = ret_norm_g[0].reshape(1, RDV), gdn_norm_g[0].reshape(1, GDV)
    proj_p = _proj(xp, w_in, tmp).reshape(bp, lp, A_COLS_PAD)
    proj_s = _proj(xs, w_in, tms).reshape(bs, 1, A_COLS_PAD)
    mix_p, ret_p, gdn_p, conv_p = _mixer_a_prompt(proj_p, cos_p, sin_p, gdn_conv_w[0], alog, dtb, rg, gg)
    mix_s, ret_s, gdn_s, conv_s = _mixer_a_sample(proj_s, cos_s, sin_s, gdn_conv_w[0], alog, dtb, rg, gg,
                                                  state_ret[0], state_gdn[0], state_gdn_conv[0])
    w_out = w_out_a[0].astype(BF16)
    g0, b0 = ln(0, 0)
    xp = _outproj_ln(mix_p.reshape(tp, D), xp, w_out, g0, b0, tmp)
    xs = _outproj_ln(mix_s.reshape(bs, D), xs, w_out, g0, b0, tms)
    pw = _peer_params(peer_wq[0], peer_sub_keys[0], peer_u[0], peer_v[0], te)
    g1, b1 = ln(0, 1)
    xp = _peer(xp, *pw, g1, b1, tmp, te)
    xs = _peer(xs, *pw, g1, b1, tms, te)

    w_c = jnp.pad(w_in_c[0], ((0, 0), (0, C_COLS_PAD - w_in_c.shape[-1]))).astype(BF16)
    bf_row = _pad_lanes(fox_b_f[0])
    kp, vp, lfp, lftp, qtp, kbp, vtp = _fox_proj(xp, w_c, bf_row, tmp, True)
    ks, vs, lfs, _, qs = _fox_proj(xs, w_c, bf_row, tms, False)
    tk = tmp
    tq = 2 * tk
    cum = _fox_cumsum(lftp, bp, lp).reshape(bp * FH, lp // tk, tk)
    op = _fox_flash(qtp, kbp.reshape(bp, lp, D), vtp, cum, tq, tk)
    os_ = _fox_sample(page_table, qs.reshape(bs, FH, FHD), ks.reshape(bs, FH, FHD), vs.reshape(bs, FH, FHD),
                      lfs.reshape(bs, FH, 1), cache_fox_k[0], cache_fox_v[0], cache_fox_logf[0])
    w_oc = w_out_c[0].astype(BF16)
    g2, b2 = ln(1, 0)
    xp = _outproj_ln(op.reshape(tp, D), xp, w_oc, g2, b2, tmp)
    xs = _outproj_ln(os_.reshape(bs, D), xs, w_oc, g2, b2, tms)
    pw = _peer_params(peer_wq[1], peer_sub_keys[1], peer_u[1], peer_v[1], te)
    g3, b3 = ln(1, 1)
    xp = _peer(xp, *pw, g3, b3, tmp, te)
    xs = _peer(xs, *pw, g3, b3, tms, te)

    return (xp.reshape(bp, lp, D), xs.reshape(bs, 1, D),
            ret_p[None], ret_s[None], gdn_p[None], gdn_s[None], conv_p[None], conv_s[None],
            kp.reshape(1, bp, lp, FH, FHD), ks.reshape(1, bs, 1, FH, FHD),
            vp.reshape(1, bp, lp, FH, FHD), vs.reshape(1, bs, 1, FH, FHD),
            lfp.reshape(1, bp, lp, FH), lfs.reshape(1, bs, 1, FH))
```

```python
import functools
import math

import jax
import jax.numpy as jnp
from jax import lax
from jax.experimental import pallas as pl
from jax.experimental.pallas import tpu as pltpu

F32 = jnp.float32
BF16 = jnp.bfloat16

D = 1024
PAST = 2048
PAGE = 128
RH, RDK, RDV = 4, 128, 128
GH, GDK, GDV = 4, 128, 128
CONV_W = 4
CONV_CH = 2 * GH * GDK + GH * GDV
A_COLS = 4104
A_COLS_PAD = 4224
FH, FHD = 8, 128
C_COLS_PAD = 3200
PH, PTOPK, NKEYS = 8, 16, 128
NEXP = NKEYS * NKEYS
ALPHA = 4.0 ** 0.25
LN_EPS = 1e-5
NORM_EPS = 1e-6
ROPE_BASE = 10000.0
LOG2E = math.log2(math.e)
GELU_GATE_SCALE = 2.0 ** -0.5
LANES = 128
CHUNK = 128
PAGES_PER_STEP = 16
GATE_DT = BF16
VMEM_LIMIT = 56 * 1024 * 1024

NT = (((1,), (1,)), ((), ()))
TN = (((0,), (0,)), ((), ()))


def _cp(sem):
    return pltpu.CompilerParams(dimension_semantics=sem, vmem_limit_bytes=VMEM_LIMIT)


def _mm(a, b):
    return jnp.dot(a.astype(BF16), b.astype(BF16), preferred_element_type=F32)


def _mm_nt(a, b):
    return lax.dot_general(a.astype(BF16), b.astype(BF16), NT, preferred_element_type=F32)


def _mm_tn(a, b):
    return lax.dot_general(a.astype(BF16), b.astype(BF16), TN, preferred_element_type=F32)


def _split3(x):
    x1 = x.astype(BF16)
    r1 = x - x1.astype(F32)
    x2 = r1.astype(BF16)
    return x1, x2, (r1 - x2.astype(F32)).astype(BF16)


def _mm_01_left(m01, x):
    return sum(jnp.dot(m01, t, preferred_element_type=F32) for t in _split3(x))


def _mm_01_right(x, m01):
    return sum(jnp.dot(t, m01, preferred_element_type=F32) for t in _split3(x))


def _split2(a):
    hi = a.astype(BF16)
    return hi, (a - hi.astype(F32)).astype(BF16)


def _mm_s(a, b):
    return (jnp.dot(a[0], b[0], preferred_element_type=F32) + jnp.dot(a[0], b[1], preferred_element_type=F32)
            + jnp.dot(a[1], b[0], preferred_element_type=F32))


def _sigmoid(x):
    return 1.0 / (1.0 + jnp.exp(-x))


def _silu(x):
    return x * _sigmoid(x)


def _softplus(x):
    return jnp.maximum(x, 0.0) + jnp.log1p(jnp.exp(-jnp.abs(x)))


def _log_sigmoid(x):
    return jnp.minimum(x, 0.0) - jnp.log1p(jnp.exp(-jnp.abs(x)))


def _layer_norm(z, g, b):
    mu = jnp.mean(z, axis=-1, keepdims=True)
    d = z - mu
    var = jnp.mean(d * d, axis=-1, keepdims=True)
    return d * lax.rsqrt(var + LN_EPS) * g + b


def _iota2(shape, axis):
    return lax.broadcasted_iota(jnp.int32, shape, axis)


def _proj_kernel(x_ref, w_ref, o_ref):
    o_ref[...] = jnp.dot(x_ref[...].astype(BF16), w_ref[...], preferred_element_type=F32)


def _proj(x, w, tm):
    t, k = x.shape
    n = w.shape[1]
    return pl.pallas_call(
        _proj_kernel,
        grid=(t // tm,),
        in_specs=[pl.BlockSpec((tm, k), lambda i: (i, 0)), pl.BlockSpec((k, n), lambda i: (0, 0))],
        out_specs=pl.BlockSpec((tm, n), lambda i: (i, 0)),
        out_shape=jax.ShapeDtypeStruct((t, n), F32),
        compiler_params=_cp(("parallel",)),
        name="proj_a",
    )(x, w)


def _fox_proj_kernel(x_ref, w_ref, bf_ref, k_ref, v_ref, lf_ref, lft_ref, *q_refs, attn_copies):
    p = jnp.dot(x_ref[...].astype(BF16), w_ref[...], preferred_element_type=F32)
    q = p[:, 0:D]
    k = p[:, D:2 * D]
    v = p[:, 2 * D:3 * D]
    k_ref[...] = k
    v_ref[...] = v
    lf = _log_sigmoid(p[:, 3 * D:3 * D + LANES] + bf_ref[...])
    lf_ref[...] = lf[:, 0:FH]
    lft_ref[...] = lf.T[0:FH, :]
    if attn_copies:
        qt_ref, kb_ref, vt_ref = q_refs
        kb_ref[...] = k.astype(BF16)
        qs = q * (FHD ** -0.5 * LOG2E)
        for h in range(FH):
            qt_ref[0, h] = qs[:, h * FHD:(h + 1) * FHD].T.astype(BF16)
            vt_ref[0, h] = v[:, h * FHD:(h + 1) * FHD].T.astype(BF16)
    else:
        q_refs[0][...] = q


def _fox_proj(x, w, bf_row, tm, attn_copies):
    t = x.shape[0]
    row = lambda i: (i, 0)
    out_specs = [pl.BlockSpec((tm, D), row), pl.BlockSpec((tm, D), row),
                 pl.BlockSpec((tm, FH), row), pl.BlockSpec((FH, tm), lambda i: (0, i))]
    out_shape = [jax.ShapeDtypeStruct((t, D), F32), jax.ShapeDtypeStruct((t, D), F32),
                 jax.ShapeDtypeStruct((t, FH), F32), jax.ShapeDtypeStruct((FH, t), F32)]
    if attn_copies:
        tile = pl.BlockSpec((1, FH, FHD, tm), lambda i: (i, 0, 0, 0))
        tiles = jax.ShapeDtypeStruct((t // tm, FH, FHD, tm), BF16)
        out_specs += [tile, pl.BlockSpec((tm, D), row), tile]
        out_shape += [tiles, jax.ShapeDtypeStruct((t, D), BF16), tiles]
    else:
        out_specs += [pl.BlockSpec((tm, D), row)]
        out_shape += [jax.ShapeDtypeStruct((t, D), F32)]
    return pl.pallas_call(
        functools.partial(_fox_proj_kernel, attn_copies=attn_copies),
        grid=(t // tm,),
        in_specs=[pl.BlockSpec((tm, D), row), pl.BlockSpec((D, C_COLS_PAD), lambda i: (0, 0)),
                  pl.BlockSpec((1, LANES), lambda i: (0, 0))],
        out_specs=out_specs,
        out_shape=out_shape,
        compiler_params=_cp(("parallel",)),
        name="proj_c",
    )(x, w, bf_row)


def _outproj_ln_kernel(a_ref, x_ref, w_ref, g_ref, b_ref, o_ref):
    h = jnp.dot(a_ref[...].astype(BF16), w_ref[...], preferred_element_type=F32)
    o_ref[...] = _layer_norm(ALPHA * x_ref[...] + h, g_ref[...], b_ref[...])


def _outproj_ln(a, x, w, g, b, tm):
    t = x.shape[0]
    row = lambda i: (i, 0)
    fixed = lambda i: (0, 0)
    return pl.pallas_call(
        _outproj_ln_kernel,
        grid=(t // tm,),
        in_specs=[pl.BlockSpec((tm, D), row), pl.BlockSpec((tm, D), row), pl.BlockSpec((D, D), fixed),
                  pl.BlockSpec((1, D), fixed), pl.BlockSpec((1, D), fixed)],
        out_specs=pl.BlockSpec((tm, D), row),
        out_shape=jax.ShapeDtypeStruct((t, D), F32),
        compiler_params=_cp(("parallel",)),
        name="outproj_ln",
    )(a, x, w, g, b)


def _unit_lower_inverses(a_list, ri, ci):
    eye = (ri == ci).astype(F32)
    blk = (ri >> 4) == (ci >> 4)
    d = [_split2(jnp.where(blk, a, 0.0)) for a in a_list]
    d2f = [_mm_s(x, x) for x in d]
    d2 = [_split2(x) for x in d2f]
    d4f = [_mm_s(x, x) for x in d2]
    d4 = [_split2(x) for x in d4f]
    d8 = [_split2(_mm_s(x, x)) for x in d4]
    xs = [eye - jnp.where(blk, a, 0.0) for a in a_list]
    for pw in (d2, d4, d8):
        xs = [x + _mm_s(_split2(x), p) for x, p in zip(xs, pw)]
    for s in (5, 6, 7):
        msk = ((ri >> s) == (ci >> s)) & ((ri >> (s - 1)) != (ci >> (s - 1)))
        es = [_split2(jnp.where(msk, a, 0.0)) for a in a_list]
        xsp = [_split2(x) for x in xs]
        xe = [_split2(_mm_s(x, e)) for x, e in zip(xsp, es)]
        xs = [x - _mm_s(y, xp) for x, y, xp in zip(xs, xe, xsp)]
    return xs


def _head_gate_norm(out, g_row, gate):
    return out * lax.rsqrt(jnp.mean(out * out, axis=-1, keepdims=True) + NORM_EPS) * g_row * _silu(gate)


def _mixer_a_prompt_kernel(p_ref, cos_ref, sin_ref, cw_ref, alog_ref, dt_ref, rg_ref, gg_ref,
                           mix_ref, ret_ref, gdn_ref, conv_ref, ext_ref, *, nb):
    n = pl.program_id(0)
    c = CHUNK

    @pl.when(n == 0)
    def _init():
        ret_ref[...] = jnp.zeros_like(ret_ref)
        gdn_ref[...] = jnp.zeros_like(gdn_ref)
        ext_ref[:, 0:8, :] = jnp.zeros((nb, 8, CONV_CH), F32)

    ri = _iota2((c, c), 0)
    ci = _iota2((c, c), 1)
    rf = ri.astype(F32)
    diff = rf - ci.astype(F32)
    cosv = cos_ref[...]
    sinv = sin_ref[...]

    lgs = [math.log1p(-(2.0 ** (-5.0 - h))) for h in range(RH)]
    decay = [jnp.where(diff >= 0, jnp.exp(jnp.maximum(diff, 0.0) * lg), 0.0) for lg in lgs]
    inner = [jnp.exp((rf + 1.0) * lg) for lg in lgs]
    kdec = [jnp.exp((c - 1.0 - rf) * lg) for lg in lgs]
    r_units = [(b, h) for b in range(nb) for h in range(RH)]
    r_q, r_k = [], []
    for b, h in r_units:
        rq = p_ref[b, :, h * RDK:(h + 1) * RDK]
        rk = p_ref[b, :, RH * RDK + h * RDK:RH * RDK + (h + 1) * RDK]
        r_q.append(rq * cosv + pltpu.roll(rq, RDK // 2, 1) * sinv)
        r_k.append((rk * cosv + pltpu.roll(rk, RDK // 2, 1) * sinv) * (RDK ** -0.5))
    r_v = [p_ref[b, :, 2 * RH * RDK + h * RDV:2 * RH * RDK + (h + 1) * RDV] for b, h in r_units]
    r_s = [ret_ref[b, h] for b, h in r_units]
    r_scores = [_mm_nt(q, k) * decay[h] for q, k, (_, h) in zip(r_q, r_k, r_units)]
    r_state_out = [_mm(q * inner[h], s) for q, s, (_, h) in zip(r_q, r_s, r_units)]
    r_new = [_mm_tn(k * kdec[h], v) for k, v, (_, h) in zip(r_k, r_v, r_units)]
    r_out = [_mm(sc, v) + so for sc, v, so in zip(r_scores, r_v, r_state_out)]
    for (b, h), s, new, out in zip(r_units, r_s, r_new, r_out):
        ret_ref[b, h] = math.exp(c * lgs[h]) * s + new
        rgate = p_ref[b, :, 2 * RH * RDK + RH * RDV + h * RDV:2 * RH * RDK + RH * RDV + (h + 1) * RDV]
        mix_ref[b, :, h * RDV:(h + 1) * RDV] = _head_gate_norm(out, rg_ref[...], rgate)

    g_off = 2 * RH * RDK + 2 * RH * RDV
    z_off = g_off + CONV_CH
    incl = ri >= ci
    strict = ri > ci
    tril = incl.astype(BF16)
    units = []
    for b in range(nb):
        ext_ref[b, 8:8 + c, :] = p_ref[b, :, g_off:g_off + CONV_CH]
        y = ext_ref[b, 5:5 + c, :] * cw_ref[0:1, :]
        for i in range(1, CONV_W):
            y = y + ext_ref[b, 5 + i:5 + i + c, :] * cw_ref[i:i + 1, :]
        conv_ref[b] = ext_ref[b, c + 5:c + 8, :]
        ext_ref[b, 0:8, :] = ext_ref[b, c:c + 8, :]
        y = _silu(y)
        tail = p_ref[b, :, z_off + GH * GDV:z_off + GH * GDV + LANES]
        g_all = -jnp.exp(alog_ref[...]) * _softplus(tail + dt_ref[...])
        beta_all = _sigmoid(tail)
        for h in range(GH):
            qh = y[:, h * GDK:(h + 1) * GDK]
            kh = y[:, GH * GDK + h * GDK:GH * GDK + (h + 1) * GDK]
            vh = y[:, 2 * GH * GDK + h * GDV:2 * GH * GDK + (h + 1) * GDV]
            qh = qh * lax.rsqrt(jnp.sum(qh * qh, axis=-1, keepdims=True) + NORM_EPS) * (GDK ** -0.5)
            kh = kh * lax.rsqrt(jnp.sum(kh * kh, axis=-1, keepdims=True) + NORM_EPS)
            gb = jnp.broadcast_to(g_all[:, h:h + 1], (c, c))
            bcol = beta_all[:, GH + h:GH + h + 1]
            units.append((b, h, qh, kh, vh, gb, bcol))

    gcs = [_mm_01_left(tril, u[5]) for u in units]
    decs = []
    for gc in gcs:
        dmat = gc - gc.T
        decs.append(jnp.where(incl, jnp.exp(jnp.where(incl, dmat, 0.0)), 0.0))
    kbs = [u[3] * u[6] for u in units]
    a_list = [jnp.where(strict, _mm_nt(kb, u[3]) * dec, 0.0) for kb, u, dec in zip(kbs, units, decs)]
    ts = [t.astype(BF16) for t in _unit_lower_inverses(a_list, ri, ci)]
    egcs = [jnp.exp(gc) for gc in gcs]
    us = [jnp.dot(t, (u[4] * u[6]).astype(BF16), preferred_element_type=F32) for t, u in zip(ts, units)]
    ws = [jnp.dot(t, (kb * egc).astype(BF16), preferred_element_type=F32) for t, kb, egc in zip(ts, kbs, egcs)]
    attns = [_mm_nt(u[2], u[3]) * dec for u, dec in zip(units, decs)]
    ss = [gdn_ref[u[0], u[1]] for u in units]
    v_news = [uu - _mm(w, s) for uu, w, s in zip(us, ws, ss)]
    outs = [_mm(u[2] * egc, s) + _mm(attn, vn) for u, egc, s, attn, vn in zip(units, egcs, ss, attns, v_news)]
    for u, gc, s, vn, out in zip(units, gcs, ss, v_news, outs):
        b, h = u[0], u[1]
        g_last = gc[c - 1:c, :]
        gdn_ref[b, h] = jnp.exp(g_last) * s + _mm_tn(u[3] * jnp.exp(g_last - gc), vn)
        gz = p_ref[b, :, z_off + h * GDV:z_off + (h + 1) * GDV]
        mix_ref[b, :, RH * RDV + h * GDV:RH * RDV + (h + 1) * GDV] = _head_gate_norm(out, gg_ref[...], gz)


def _mixer_a_prompt(proj, cos2, sin2, conv_w, alog_row, dt_row, rg_row, gg_row):
    b, l, _ = proj.shape
    fixed = lambda n: (0, 0)
    return pl.pallas_call(
        functools.partial(_mixer_a_prompt_kernel, nb=b),
        grid=(l // CHUNK,),
        in_specs=[pl.BlockSpec((b, CHUNK, A_COLS_PAD), lambda n: (0, n, 0)),
                  pl.BlockSpec((CHUNK, LANES), lambda n: (n, 0)),
                  pl.BlockSpec((CHUNK, LANES), lambda n: (n, 0)),
                  pl.BlockSpec((CONV_W, CONV_CH), fixed),
                  pl.BlockSpec((1, LANES), fixed), pl.BlockSpec((1, LANES), fixed),
                  pl.BlockSpec((1, LANES), fixed), pl.BlockSpec((1, LANES), fixed)],
        out_specs=[pl.BlockSpec((b, CHUNK, D), lambda n: (0, n, 0)),
                   pl.BlockSpec((b, RH, RDK, RDV), lambda n: (0, 0, 0, 0)),
                   pl.BlockSpec((b, GH, GDK, GDV), lambda n: (0, 0, 0, 0)),
                   pl.BlockSpec((b, CONV_W - 1, CONV_CH), lambda n: (0, 0, 0))],
        out_shape=[jax.ShapeDtypeStruct((b, l, D), F32),
                   jax.ShapeDtypeStruct((b, RH, RDK, RDV), F32),
                   jax.ShapeDtypeStruct((b, GH, GDK, GDV), F32),
                   jax.ShapeDtypeStruct((b, CONV_W - 1, CONV_CH), F32)],
        scratch_shapes=[pltpu.VMEM((b, CHUNK + 8, CONV_CH), F32)],
        compiler_params=_cp(("arbitrary",)),
        name="mixer_a_prompt",
    )(proj, cos2, sin2, conv_w, alog_row, dt_row, rg_row, gg_row)


def _mixer_a_sample_kernel(p_ref, cos_ref, sin_ref, cw_ref, alog_ref, dt_ref, rg_ref, gg_ref,
                           rs_ref, gs_ref, cs_ref, mix_ref, ret_ref, gdn_ref, conv_ref):
    eye = _iota2((LANES, LANES), 0) == _iota2((LANES, LANES), 1)

    def col(v):
        return jnp.sum(jnp.where(eye, jnp.broadcast_to(v, (LANES, LANES)), 0.0), axis=1, keepdims=True)

    def vec_mat(c, s):
        return jnp.sum(c * s, axis=0, keepdims=True)

    cosv = cos_ref[...]
    sinv = sin_ref[...]
    rq = [p_ref[0, :, h * RDK:(h + 1) * RDK] for h in range(RH)]
    rk = [p_ref[0, :, RH * RDK + h * RDK:RH * RDK + (h + 1) * RDK] for h in range(RH)]
    rv = [p_ref[0, :, 2 * RH * RDK + h * RDV:2 * RH * RDK + (h + 1) * RDV] for h in range(RH)]
    rq = [x * cosv + pltpu.roll(x, RDK // 2, 1) * sinv for x in rq]
    rk = [(x * cosv + pltpu.roll(x, RDK // 2, 1) * sinv) * (RDK ** -0.5) for x in rk]

    g_off = 2 * RH * RDK + 2 * RH * RDV
    gq_all = p_ref[0, :, g_off:g_off + CONV_CH]
    cb = cs_ref[0]
    y = cb[0:1, :] * cw_ref[0:1, :] + cb[1:2, :] * cw_ref[1:2, :] + cb[2:3, :] * cw_ref[2:3, :] \
        + gq_all * cw_ref[3:4, :]
    conv_ref[0, 0:2, :] = cb[1:3, :]
    conv_ref[0, 2:3, :] = gq_all
    y = _silu(y)
    z_off = g_off + CONV_CH
    tail = p_ref[0, :, z_off + GH * GDV:z_off + GH * GDV + LANES]
    g_all = -jnp.exp(alog_ref[...]) * _softplus(tail + dt_ref[...])
    beta_all = _sigmoid(tail)
    gq = [y[:, h * GDK:(h + 1) * GDK] for h in range(GH)]
    gk = [y[:, GH * GDK + h * GDK:GH * GDK + (h + 1) * GDK] for h in range(GH)]
    gv = [y[:, 2 * GH * GDK + h * GDV:2 * GH * GDK + (h + 1) * GDV] for h in range(GH)]
    gq = [x * lax.rsqrt(jnp.sum(x * x, axis=-1, keepdims=True) + NORM_EPS) * (GDK ** -0.5) for x in gq]
    gk = [x * lax.rsqrt(jnp.sum(x * x, axis=-1, keepdims=True) + NORM_EPS) for x in gk]
    eg = [jnp.exp(g_all[:, h:h + 1]) for h in range(GH)]
    beta = [beta_all[:, GH + h:GH + h + 1] for h in range(GH)]

    q_cols = [col(x) for x in rq + gq]
    k_cols = [col(x) for x in rk + gk]
    r_state = [rs_ref[0, h] for h in range(RH)]
    g_state = [gs_ref[0, h] for h in range(GH)]
    qs = [vec_mat(c, s) for c, s in zip(q_cols, r_state + g_state)]
    ws = [vec_mat(k_cols[RH + h] * (beta[h] * eg[h]), g_state[h]) for h in range(GH)]
    r_qk = [jnp.sum(a * b, axis=-1, keepdims=True) for a, b in zip(rq, rk)]
    g_qk = [jnp.sum(a * b, axis=-1, keepdims=True) for a, b in zip(gq, gk)]
    for h in range(RH):
        gamma = 1.0 - 2.0 ** (-5.0 - h)
        out = r_qk[h] * rv[h] + gamma * qs[h]
        ret_ref[0, h] = gamma * r_state[h] + k_cols[h] * rv[h]
        rgate = p_ref[0, :, 2 * RH * RDK + RH * RDV + h * RDV:2 * RH * RDK + RH * RDV + (h + 1) * RDV]
        mix_ref[0, :, h * RDV:(h + 1) * RDV] = _head_gate_norm(out, rg_ref[...], rgate)
    for h in range(GH):
        v_new = gv[h] * beta[h] - ws[h]
        out = eg[h] * qs[RH + h] + g_qk[h] * v_new
        gdn_ref[0, h] = eg[h] * g_state[h] + k_cols[RH + h] * v_new
        gz = p_ref[0, :, z_off + h * GDV:z_off + (h + 1) * GDV]
        mix_ref[0, :, RH * RDV + h * GDV:RH * RDV + (h + 1) * GDV] = _head_gate_norm(out, gg_ref[...], gz)


def _mixer_a_sample(proj, cos2, sin2, conv_w, alog_row, dt_row, rg_row, gg_row, ret_s, gdn_s, conv_s):
    b = proj.shape[0]
    fixed = lambda i: (0, 0)
    st = lambda i: (i, 0, 0, 0)
    return pl.pallas_call(
        _mixer_a_sample_kernel,
        grid=(b,),
        in_specs=[pl.BlockSpec((1, 1, A_COLS_PAD), lambda i: (i, 0, 0)),
                  pl.BlockSpec((1, LANES), fixed), pl.BlockSpec((1, LANES), fixed),
                  pl.BlockSpec((CONV_W, CONV_CH), fixed),
                  pl.BlockSpec((1, LANES), fixed), pl.BlockSpec((1, LANES), fixed),
                  pl.BlockSpec((1, LANES), fixed), pl.BlockSpec((1, LANES), fixed),
                  pl.BlockSpec((1, RH, RDK, RDV), st), pl.BlockSpec((1, GH, GDK, GDV), st),
                  pl.BlockSpec((1, CONV_W - 1, CONV_CH), lambda i: (i, 0, 0))],
        out_specs=[pl.BlockSpec((1, 1, D), lambda i: (i, 0, 0)),
                   pl.BlockSpec((1, RH, RDK, RDV), st), pl.BlockSpec((1, GH, GDK, GDV), st),
                   pl.BlockSpec((1, CONV_W - 1, CONV_CH), lambda i: (i, 0, 0))],
        out_shape=[jax.ShapeDtypeStruct((b, 1, D), F32),
                   jax.ShapeDtypeStruct((b, RH, RDK, RDV), F32),
                   jax.ShapeDtypeStruct((b, GH, GDK, GDV), F32),
                   jax.ShapeDtypeStruct((b, CONV_W - 1, CONV_CH), F32)],
        compiler_params=_cp(("parallel",)),
        name="mixer_a_sample",
    )(proj, cos2, sin2, conv_w, alog_row, dt_row, rg_row, gg_row, ret_s, gdn_s, conv_s)


def _fox_cumsum_kernel(x_ref, o_ref):
    n = x_ref.shape[1] // LANES
    upper = (_iota2((LANES, LANES), 0) <= _iota2((LANES, LANES), 1)).astype(BF16)
    carry = jnp.zeros((FH, 1), F32)
    for c in range(n):
        p = _mm_01_right(x_ref[:, c * LANES:(c + 1) * LANES], upper) + carry
        o_ref[0, :, c * LANES:(c + 1) * LANES] = p
        carry = p[:, LANES - 1:LANES]


def _fox_cumsum(lft, b, l):
    return pl.pallas_call(
        _fox_cumsum_kernel,
        grid=(b,),
        in_specs=[pl.BlockSpec((FH, l), lambda i: (0, i))],
        out_specs=pl.BlockSpec((1, FH, l), lambda i: (i, 0, 0)),
        out_shape=jax.ShapeDtypeStruct((b, FH, l), F32),
        compiler_params=_cp(("parallel",)),
        name="fox_cumsum",
    )(lft)


def _fox_flash_kernel(q_ref, k_ref, v_ref, ck_ref, o_ref, ka_ref, sa_ref, sb_ref, *, tq, tk):
    assert tq == 2 * tk and tk % LANES == 0
    qi = pl.program_id(2)
    nk = k_ref.shape[1] // tk

    @pl.when(qi == 0)
    def _augment_keys():
        eye = _iota2((LANES, LANES), 0) == _iota2((LANES, LANES), 1)
        lane = _iota2((LANES, LANES), 1)
        for j in range(nk):
            ka_ref[j * tk:(j + 1) * tk, 0:FHD] = k_ref[0, j * tk:(j + 1) * tk, :]
            for c in range(tk // LANES):
                row = jnp.broadcast_to(ck_ref[0, j:j + 1, c * LANES:(c + 1) * LANES], (LANES, LANES))
                col = jnp.sum(jnp.where(eye, row, 0.0), axis=1, keepdims=True) * LOG2E
                c1, c2, c3 = (t.astype(F32) for t in _split3(col))
                blk = jnp.where(lane == 0, c1, jnp.where(lane == 1, c2, jnp.where(lane == 2, c3, 0.0)))
                r0 = j * tk + c * LANES
                ka_ref[r0:r0 + LANES, FHD:2 * FHD] = blk.astype(BF16)

    sub = _iota2((FHD, tq), 0)
    minus_ones = jnp.where(sub < 3, -1.0, 0.0).astype(BF16)
    qa = jnp.concatenate([q_ref[0, 0], q_ref[1, 0]], axis=1)
    qa = jnp.concatenate([qa, minus_ones], axis=0)

    def scores(j, s_ref):
        ka = ka_ref[pl.ds(pl.multiple_of(j * tk, tk), tk), :]
        s_ref[...] = jnp.dot(ka, qa, preferred_element_type=F32)

    ones_rows = jnp.ones((16, tk), BF16)

    def update(j, s_ref, carry, masked):
        m_old, acc = carry
        s = s_ref[...]
        if masked:
            kpos = j * tk + _iota2((tk, tq), 0)
            qpos = qi * tq + _iota2((tk, tq), 1)
            s = jnp.where(kpos <= qpos, s, -jnp.inf)
        m_new = jnp.maximum(m_old, jnp.max(s, axis=0, keepdims=True))
        alpha = jnp.exp2(m_old - m_new)
        p = jnp.exp2(s - m_new).astype(BF16)
        v_aug = jnp.concatenate([v_ref[j, 0], ones_rows], axis=0)
        return m_new, alpha * acc + jnp.dot(v_aug, p, preferred_element_type=F32)

    def two_blocks(i, carry):
        scores(2 * i + 1, sb_ref)
        carry = update(2 * i, sa_ref, carry, False)
        scores(2 * i + 2, sa_ref)
        return update(2 * i + 1, sb_ref, carry, False)

    scores(0, sa_ref)
    init = (jnp.full((1, tq), -jnp.inf, F32), jnp.zeros((FHD + 16, tq), F32))
    carry = lax.fori_loop(0, qi, two_blocks, init)
    scores(2 * qi + 1, sb_ref)
    carry = update(2 * qi, sa_ref, carry, True)
    carry = update(2 * qi + 1, sb_ref, carry, True)
    acc = carry[1]
    o_ref[0] = (acc[0:FHD] / acc[FHD:FHD + 1]).T


def _fox_flash(qt, kb, vt, cum, tq, tk):
    b, l, _ = kb.shape
    nk = l // tk
    return pl.pallas_call(
        functools.partial(_fox_flash_kernel, tq=tq, tk=tk),
        grid=(b, FH, l // tq),
        in_specs=[pl.BlockSpec((tq // tk, 1, FHD, tk), lambda i, h, qi: (i * (l // tq) + qi, h, 0, 0)),
                  pl.BlockSpec((1, l, FHD), lambda i, h, qi: (i, 0, h)),
                  pl.BlockSpec((nk, 1, FHD, tk), lambda i, h, qi: (i, h, 0, 0)),
                  pl.BlockSpec((1, nk, tk), lambda i, h, qi: (i * FH + h, 0, 0))],
        out_specs=pl.BlockSpec((1, tq, FHD), lambda i, h, qi: (i, qi, h)),
        out_shape=jax.ShapeDtypeStruct((b, l, D), F32),
        scratch_shapes=[pltpu.VMEM((l, 2 * FHD), BF16), pltpu.VMEM((tk, tq), F32), pltpu.VMEM((tk, tq), F32)],
        compiler_params=_cp(("arbitrary", "arbitrary", "arbitrary")),
        name="fox_flash",
    )(qt, kb, vt, cum)


def _fox_sample_kernel(pt_ref, q_ref, kn_ref, vn_ref, lfn_ref, *rest):
    npp = PAGES_PER_STEP
    k_refs, v_refs, lf_refs = rest[0:npp], rest[npp:2 * npp], rest[2 * npp:3 * npp]
    o_ref, m_ref, l_ref, acc_ref, carry_ref = rest[3 * npp:]
    j = pl.program_id(1)
    scale = FHD ** -0.5
    rows = PAGE * FH

    @pl.when(j == 0)
    def _init():
        m_ref[...] = jnp.full(m_ref.shape, -jnp.inf, F32)
        l_ref[...] = jnp.zeros(l_ref.shape, F32)
        acc_ref[...] = jnp.zeros(acc_ref.shape, F32)
        carry_ref[...] = jnp.zeros(carry_ref.shape, F32)

    q = q_ref[0]
    qb = (q * scale).astype(BF16)
    own = (_iota2((FH, rows), 1) & (FH - 1)) == _iota2((FH, rows), 0)
    nch = rows // LANES
    ci = _iota2((LANES, 2 * LANES), 1)
    pre_tot = ((_iota2((LANES, 2 * LANES), 0) <= ci) | (ci >= LANES)).astype(BF16)
    stacked, s_pages = [], []
    for pg in range(npp):
        kp = k_refs[pg][0].reshape(rows, FHD).astype(BF16)
        s_pages.append(lax.dot_general(qb, kp, NT, preferred_element_type=F32))
        lf = jnp.where(own, jnp.broadcast_to(lf_refs[pg][0], (FH, rows)), 0.0)
        stacked += [lf[:, c * LANES:(c + 1) * LANES] for c in range(nch)]
    pt = _mm_01_right(jnp.concatenate(stacked, axis=0), pre_tot)
    carry = carry_ref[...]
    logits = []
    for pg in range(npp):
        cums = []
        for c in range(nch):
            r0 = (pg * nch + c) * FH
            cums.append(pt[r0:r0 + FH, 0:LANES] + carry)
            carry = carry + pt[r0:r0 + FH, LANES:2 * LANES]
        logits.append(jnp.where(own, s_pages[pg] - jnp.concatenate(cums, axis=1), -jnp.inf))
    carry_ref[...] = carry
    m_old = m_ref[...]
    m_new = m_old
    for lg in logits:
        m_new = jnp.maximum(m_new, jnp.max(lg, axis=1, keepdims=True))
    alpha = jnp.exp(m_old - m_new)
    l_new = alpha * l_ref[...]
    acc = alpha * acc_ref[...]
    for pg in range(npp):
        p = jnp.exp(logits[pg] - m_new)
        l_new = l_new + jnp.sum(p, axis=1, keepdims=True)
        vp = v_refs[pg][0].reshape(rows, FHD).astype(BF16)
        acc = acc + jnp.dot(p.astype(BF16), vp, preferred_element_type=F32)
    m_ref[...] = m_new
    l_ref[...] = l_new
    acc_ref[...] = acc

    @pl.when(j == pl.num_programs(1) - 1)
    def _fin():
        s_new = jnp.sum(q * kn_ref[0], axis=1, keepdims=True) * scale - (carry[:, 0:1] + lfn_ref[0])
        m2 = jnp.maximum(m_new, s_new)
        a2 = jnp.exp(m_new - m2)
        p2 = jnp.exp(s_new - m2)
        o_ref[0] = (a2 * acc + p2 * vn_ref[0]) / (a2 * l_new + p2)


def _fox_sample(page_table, q3, k3, v3, lf3, k_pool, v_pool, lf_pool):
    b = q3.shape[0]
    npg = page_table.shape[1]
    npp = PAGES_PER_STEP
    assert npg % npp == 0
    pt = page_table.reshape(-1)
    tok = lambda i, j, pt_ref: (i, 0, 0)

    def page(r, nd):
        return lambda i, j, pt_ref: (pt_ref[i * npg + j * npp + r],) + (0,) * nd

    grid_spec = pltpu.PrefetchScalarGridSpec(
        num_scalar_prefetch=1,
        grid=(b, npg // npp),
        in_specs=[pl.BlockSpec((1, FH, FHD), tok), pl.BlockSpec((1, FH, FHD), tok), pl.BlockSpec((1, FH, FHD), tok),
                  pl.BlockSpec((1, FH, 1), tok)]
        + [pl.BlockSpec((1, PAGE, FH, FHD), page(r, 3)) for r in range(npp)]
        + [pl.BlockSpec((1, PAGE, FH, FHD), page(r, 3)) for r in range(npp)]
        + [pl.BlockSpec((1, 1, PAGE * FH), page(r, 2)) for r in range(npp)],
        out_specs=pl.BlockSpec((1, FH, FHD), tok),
        scratch_shapes=[pltpu.VMEM((FH, 1), F32), pltpu.VMEM((FH, 1), F32), pltpu.VMEM((FH, FHD), F32),
                        pltpu.VMEM((FH, LANES), F32)],
    )
    lf_flat =lf_pool.reshape(lf_pool.shape[0], 1, PAGE * FH)
    return pl.pallas_call(
        _fox_sample_kernel,
        grid_spec=grid_spec,
        out_shape=jax.ShapeDtypeStruct((b, FH, FHD), F32),
        compiler_params=_cp(("parallel", "arbitrary")),
        name="fox_sample",
    )(pt, q3, k3, v3, lf3, *([k_pool] * npp), *([v_pool] * npp), *([lf_flat] * npp))


_CAND_BLOCKS = [(a, 16 if a == 0 else 8, 16 // (a + 1)) for a in range(8)]


_MARK0 = -(2.0 ** 127)
_MARK_STEP = 2.0 ** 120
_MARK_LIMIT = -(2.0 ** 126)


def _top16(s, iota):
    rank = jnp.full(s.shape, float(PTOPK), F32)
    vals = jnp.zeros((PTOPK, s.shape[1]), F32)
    i16 = _iota2((PTOPK, s.shape[1]), 0)
    for r in range(PTOPK):
        m = jnp.max(s, axis=0, keepdims=True)
        sel = iota == jnp.min(jnp.where(s == m, iota, 1e9), axis=0, keepdims=True)
        rank = jnp.where(sel, float(r), rank)
        s = jnp.where(sel, -jnp.inf, s)
        vals = jnp.where(i16 == r, m, vals)
    return rank, vals


def _top16_fast(s):
    lowest = jnp.min(s)
    vals = jnp.zeros((PTOPK, s.shape[1]), F32)
    i16 = _iota2((PTOPK, s.shape[1]), 0)
    for r in range(PTOPK):
        m = jnp.max(s, axis=0, keepdims=True)
        s = jnp.where(s == m, _MARK0 + r * _MARK_STEP, s)
        vals = jnp.where(i16 == r, m, vals)
    marked = s < _MARK_LIMIT
    rank = jnp.where(marked, (s - _MARK0) * (1.0 / _MARK_STEP), float(PTOPK))
    taken = jnp.sum(jnp.where(marked, 1.0, 0.0), axis=0, keepdims=True)
    return rank, vals, _not_16(taken) | (lowest <= _MARK_LIMIT)


def _pick16(cand, cidx, exact):
    picked = jnp.zeros(cand.shape, F32)
    for _ in range(PTOPK):
        m = jnp.max(cand, axis=0, keepdims=True)
        sel = cand == m
        if exact:
            sel = cidx == jnp.min(jnp.where(sel, cidx, 1e9), axis=0, keepdims=True)
        picked = jnp.where(sel, 1.0, picked)
        cand = jnp.where(sel, -jnp.inf, cand)
    return picked


def _not_16(count):
    return jnp.max(jnp.abs(count - float(PTOPK))) > 0.5


def _route_head(s_t):
    s1 = s_t[0:NKEYS]
    s2 = s_t[NKEYS:2 * NKEYS]
    w = s_t.shape[1]
    iota = _iota2((NKEYS, w), 0).astype(F32)
    rank1, v1, redo1 = _top16_fast(s1)
    rank2, v2, redo2 = _top16_fast(s2)
    rank1, v1, rank2, v2 = lax.cond(redo1 | redo2, lambda: _top16(s1, iota) + _top16(s2, iota),
                                    lambda: (rank1, v1, rank2, v2))
    e1 = jnp.exp(v1 - v1[0:1])
    e2 = jnp.exp(v2 - v2[0:1])
    i8 = _iota2((8, w), 0)
    i8f = i8.astype(F32)
    i16f = _iota2((16, w), 0).astype(F32)
    cand, cidx, cprob = [], [], []
    for a, rows, valid in _CAND_BLOCKS:
        c = v1[a:a + 1] + v2[0:rows]
        pr = e1[a:a + 1] * e2[0:rows]
        if rows == 16:
            ix = i16f
        else:
            ix = i8f + float(a * PTOPK)
            c = jnp.where(i8 < valid, c, -jnp.inf)
        cand.append(c)
        cidx.append(ix)
        cprob.append(pr)
    cand.append(v1[8:16] + v2[0:1])
    cidx.append((i8f + 8.0) * float(PTOPK))
    cprob.append(e1[8:16] * e2[0:1])
    cand = jnp.concatenate(cand, axis=0)
    cidx = jnp.concatenate(cidx, axis=0)
    cprob = jnp.concatenate(cprob, axis=0)
    picked = _pick16(cand, cidx, False)
    picked = lax.cond(_not_16(jnp.sum(picked, axis=0, keepdims=True)),
                      lambda: _pick16(cand, cidx, True), lambda: picked)
    z = jnp.sum(picked * cprob, axis=0, keepdims=True)
    n_low = jnp.zeros((8, w), F32)
    off = 0
    for a, rows, _ in _CAND_BLOCKS:
        cnt = jnp.sum(picked[off:off + rows], axis=0, keepdims=True)
        n_low = jnp.where(i8 == a, cnt, n_low)
        off += rows
    n16 = jnp.concatenate([n_low, picked[off:off + 8]], axis=0)
    n1d = jnp.zeros((NKEYS, w), F32)
    for a in range(PTOPK):
        n1d = jnp.where(rank1 == float(a), n16[a:a + 1], n1d)
    e1d = jnp.exp(s1 - v1[0:1]) * (GELU_GATE_SCALE / z)
    e2d = jnp.exp(s2 - v2[0:1])
    return rank2, e2d, n1d, e1d


def _peer_kernel(x_ref, wq_ref, keys_ref, u0_ref, un_ref, vt_ref, g_ref, b_ref, o_ref,
                 xb_ref, q_ref, r2_ref, e2_ref, n1_ref, e1_ref, ha_ref, hb_ref, at_ref, acc_ref, *, tm, te, rw):
    j = pl.program_id(1)
    nlg = tm // LANES
    nsub = te // NKEYS

    @pl.when(j == 0)
    def _route():
        xb = x_ref[...].astype(BF16)
        xb_ref[...] = xb
        q = jnp.dot(xb, wq_ref[...], preferred_element_type=F32)
        for h in range(PH):
            q_ref[h] = q[:, h * LANES:(h + 1) * LANES].astype(BF16)
        acc_ref[...] = jnp.zeros(acc_ref.shape, F32)
        ha_ref[:, 0:tm] = lax.dot_general(u0_ref[...], xb, NT, preferred_element_type=F32)

        def body(h, carry):
            for w0 in range(0, tm, rw):
                s_t = lax.dot_general(keys_ref[...], q_ref[h, w0:w0 + rw, :], NT,
                                      preferred_element_type=F32)
                r2, e2, n1, e1 = _route_head(s_t)
                for g in range(rw // LANES):
                    lg = slice(g * LANES, (g + 1) * LANES)
                    r2_ref[h, w0 // LANES + g] = r2[:, lg].astype(GATE_DT)
                    e2_ref[h, w0 // LANES + g] = e2[:, lg].astype(GATE_DT)
                    n1_ref[h, w0 // LANES + g] = n1[:, lg]
                    e1_ref[h, w0 // LANES + g] = e1[:, lg]
            return carry

        lax.fori_loop(0, PH, body, 0)

    def step(h_cur_ref, h_next_ref):
        h_next_ref[:, 0:tm] = lax.dot_general(un_ref[...], xb_ref[...], NT, preferred_element_type=F32)
        zero = jnp.zeros((), GATE_DT)
        pair = 2
        for pp in range(nsub // pair):
            for g in range(nlg):
                lanes = slice(g * LANES, (g + 1) * LANES)
                gates = [jnp.zeros((NKEYS, LANES), GATE_DT) for _ in range(pair)]
                for h in range(PH):
                    r2 = r2_ref[h, g]
                    e2 = e2_ref[h, g]
                    for ii in range(pair):
                        i1 = j * nsub + pp * pair + ii
                        n_row = n1_ref[h, g, pl.ds(i1, 1), :].astype(GATE_DT)
                        e_row = e1_ref[h, g, pl.ds(i1, 1), :].astype(GATE_DT)
                        gates[ii] = gates[ii] + jnp.where(r2 < n_row, e2, zero) * e_row
                for ii in range(pair):
                    r0 = (pp * pair + ii) * NKEYS
                    hb = h_cur_ref[r0:r0 + NKEYS, lanes]
                    act = hb * (1.0 + lax.erf(hb))
                    at_ref[r0:r0 + NKEYS, lanes] = (act.astype(GATE_DT) * gates[ii]).astype(BF16)
        acc_ref[...] += lax.dot_general(vt_ref[0], at_ref[:, 0:tm], TN, preferred_element_type=F32)

    @pl.when(j % 2 == 0)
    def _even():
        step(ha_ref, hb_ref)

    @pl.when(j % 2 == 1)
    def _odd():
        step(hb_ref, ha_ref)

    @pl.when(j == pl.num_programs(1) - 1)
    def _fin():
        o_ref[...] = _layer_norm(ALPHA * x_ref[...] + acc_ref[...].T, g_ref[...], b_ref[...])


def _peer(x, wq, keys, u, vt, g, b, tm, te):
    t = x.shape[0]
    nj = NEXP // te
    fixed = lambda i, j: (0, 0)
    row_pad = LANES if (tm // LANES) % 4 == 0 else 0
    return pl.pallas_call(
        functools.partial(_peer_kernel, tm=tm, te=te, rw=tm),
        grid=(t // tm, nj),
        in_specs=[pl.BlockSpec((tm, D), lambda i, j: (i, 0)),
                  pl.BlockSpec((D, PH * LANES), fixed),
                  pl.BlockSpec((2 * NKEYS, LANES), fixed),
                  pl.BlockSpec((te, D), fixed),
                  pl.BlockSpec((te, D), lambda i, j: (jnp.minimum(j + 1, nj - 1), 0)),
                  pl.BlockSpec((1, te, D), lambda i, j: (j, 0, 0)),
                  pl.BlockSpec((1, D), fixed), pl.BlockSpec((1, D), fixed)],
        out_specs=pl.BlockSpec((tm, D), lambda i, j: (i, 0)),
        out_shape=jax.ShapeDtypeStruct((t, D), F32),
        scratch_shapes=[pltpu.VMEM((tm, D), BF16),
                        pltpu.VMEM((PH, tm, LANES), BF16),
                        pltpu.VMEM((PH, tm // LANES, NKEYS, LANES), GATE_DT),
                        pltpu.VMEM((PH, tm // LANES, NKEYS, LANES), GATE_DT),
                        pltpu.VMEM((PH, tm // LANES, NKEYS, LANES), F32),
                        pltpu.VMEM((PH, tm // LANES, NKEYS, LANES), F32),
                        pltpu.VMEM((te, tm + row_pad), F32),
                        pltpu.VMEM((te, tm + row_pad), F32),
                        pltpu.VMEM((te, tm + row_pad), BF16),
                        pltpu.VMEM((D, tm), F32)],
        compiler_params=_cp(("parallel", "arbitrary")),
        name="peer",
    )(x, wq, keys, u, u, vt, g, b)


def _rope_tables(pos):
    half = RDK // 2
    inv = ROPE_BASE ** (-jnp.arange(half, dtype=F32) / half)
    ang = pos.astype(F32)[:, None] * inv[None, :]
    cos, sin = jnp.cos(ang), jnp.sin(ang)
    return jnp.concatenate([cos, cos], -1), jnp.concatenate([-sin, sin], -1)


def _pad_lanes(v, width=LANES, offset=0):
    return jnp.zeros((1, width), F32).at[0, offset:offset + v.shape[0]].set(v.astype(F32))


def _peer_params(wq, sub_keys, u_tab, v_tab, te):
    half = sub_keys.shape[-1]
    keys = jnp.zeros((2 * NKEYS, LANES), F32)
    keys = keys.at[0:NKEYS, 0:half].set(sub_keys[0]).at[NKEYS:, half:2 * half].set(sub_keys[1])
    vt = v_tab.astype(BF16).reshape(NEXP // te, te, D)
    return wq.astype(BF16), keys.astype(BF16), (u_tab * (2.0 ** -0.5)).astype(BF16), vt


def _tile(t, pref):
    return pref if t % pref == 0 else LANES


def kernel(x_prompt, x_sample, state_ret, state_gdn, state_gdn_conv, cache_fox_k, cache_fox_v, cache_fox_logf,
           page_table, w_in_a, ret_norm_g, gdn_a_log, gdn_dt_bias, gdn_conv_w, gdn_norm_g, w_out_a, w_in_c,
           fox_b_f, w_out_c, peer_wq, peer_sub_keys, peer_u, peer_v, ln_g, ln_b):
    bp, lp, _ = x_prompt.shape
    bs = x_sample.shape[0]
    tp = bp * lp
    xp = x_prompt.reshape(tp, D)
    xs = x_sample.reshape(bs, D)
    tmp, tms = _tile(tp, 512), _tile(bs, 512)
    te = 1024
    ln = lambda layer, k: (ln_g[layer, k].reshape(1, D), ln_b[layer, k].reshape(1, D))

    w_in = jnp.pad(w_in_a[0], ((0, 0), (0, A_COLS_PAD - A_COLS))).astype(BF16)
    cos_p, sin_p = _rope_tables(jnp.arange(lp))
    cos_s, sin_s = _rope_tables(PAST + jnp.arange(1))
    alog, dtb = _pad_lanes(gdn_a_log[0]), _pad_lanes(gdn_dt_bias[0])
    rg, gg = ret_norm_g[0].reshape(1, RDV), gdn_norm_g[0].reshape(1, GDV)
    proj_p = _proj(xp, w_in, tmp).reshape(bp, lp, A_COLS_PAD)
    proj_s = _proj(xs, w_in, tms).reshape(bs, 1, A_COLS_PAD)
    mix_p, ret_p, gdn_p, conv_p = _mixer_a_prompt(proj_p, cos_p, sin_p, gdn_conv_w[0], alog, dtb, rg, gg)
    mix_s, ret_s, gdn_s, conv_s = _mixer_a_sample(proj_s, cos_s, sin_s, gdn_conv_w[0], alog, dtb, rg, gg,
                                                  state_ret[0], state_gdn[0], state_gdn_conv[0])
    w_out = w_out_a[0].astype(BF16)
    g0, b0 = ln(0, 0)
    xp = _outproj_ln(mix_p.reshape(tp, D), xp, w_out, g0, b0, tmp)
    xs = _outproj_ln(mix_s.reshape(bs, D), xs, w_out, g0, b0, tms)
    pw = _peer_params(peer_wq[0], peer_sub_keys[0], peer_u[0], peer_v[0], te)
    g1, b1 = ln(0, 1)
    xp = _peer(xp, *pw, g1, b1, tmp, te)
    xs = _peer(xs, *pw, g1, b1, tms, te)

    w_c = jnp.pad(w_in_c[0], ((0, 0), (0, C_COLS_PAD - w_in_c.shape[-1]))).astype(BF16)
    bf_row = _pad_lanes(fox_b_f[0])
    kp, vp, lfp, lftp, qtp, kbp, vtp = _fox_proj(xp, w_c, bf_row, tmp, True)
    ks, vs, lfs, _, qs = _fox_proj(xs, w_c, bf_row, tms, False)
    tk = tmp
    tq = 2 * tk
    cum = _fox_cumsum(lftp, bp, lp).reshape(bp * FH, lp // tk, tk)
    op = _fox_flash(qtp, kbp.reshape(bp, lp, D), vtp, cum, tq, tk)
    os_ = _fox_sample(page_table, qs.reshape(bs, FH, FHD), ks.reshape(bs, FH, FHD), vs.reshape(bs, FH, FHD),
                      lfs.reshape(bs, FH, 1), cache_fox_k[0], cache_fox_v[0], cache_fox_logf[0])
    w_oc = w_out_c[0].astype(BF16)
    g2, b2 = ln(1, 0)
    xp = _outproj_ln(op.reshape(tp, D), xp, w_oc, g2, b2, tmp)
    xs = _outproj_ln(os_.reshape(bs, D), xs, w_oc, g2, b2, tms)
    pw = _peer_params(peer_wq[1], peer_sub_keys[1], peer_u[1], peer_v[1], te)
    g3, b3 = ln(1, 1)
    xp = _peer(xp, *pw, g3, b3, tmp, te)
    xs = _peer(xs, *pw, g3, b3, tms, te)

    return (xp.reshape(bp, lp, D), xs.reshape(bs, 1, D),
            ret_p[None], ret_s[None], gdn_p[None], gdn_s[None], conv_p[None], conv_s[None],
            kp.reshape(1, bp, lp, FH, FHD), ks.reshape(1, bs, 1, FH, FHD),
            vp.reshape(1, bp, lp, FH, FHD), vs.reshape(1, bs, 1, FH, FHD),
            lfp.reshape(1, bp, lp, FH), lfs.reshape(1, bs, 1, FH))
```
